```python
import math
import jax, jax.numpy as jnp
from jax import lax
import numpy as np

D_MODEL = 1024
BATCH = 4
SEQ = 4096
DEPTH = 1
DEC_BATCH = 128
DEC_SEQ = 8
PAST_LEN = 2048
PAGE_SIZE = 128

M_HEADS = 4
M_DIM = 128
M_WIDTH = M_HEADS * M_DIM
M_CHUNK = 64
FORGET_BIAS = 3.0
A_HEADS = 8
A_KV_HEADS = 2
A_GROUP = A_HEADS // A_KV_HEADS
A_DIM = 64
A_WIDTH = A_HEADS * A_DIM
KV_WIDTH = A_KV_HEADS * A_DIM
MIX_WIDTH = M_WIDTH + A_WIDTH
CMP_LEN = 32
CMP_STRIDE = 16
SEL_LEN = 64
SEL_TOP = 16
WINDOW = 512
Q_BLOCK = 128
FORCE_SCORE = 1.0e4
ROPE_THETA = 500000.0
ROPE_DIM = A_DIM // 4
N_EXPERTS = 32
TOP_K = 4
D_FF = D_MODEL
SWIGLU_LIMIT = 7.0
SWIGLU_ALPHA = 1.702
MOE_BLOCK = 128
DN_ALPHA = (2 * DEPTH) ** 0.25
DN_BETA = (8 * DEPTH) ** -0.25
LN_EPS = 1e-5
IN_SIZES = (M_WIDTH, M_WIDTH, M_WIDTH, M_WIDTH, M_HEADS, M_HEADS, A_WIDTH,
            KV_WIDTH, KV_WIDTH, KV_WIDTH, KV_WIDTH, KV_WIDTH, KV_WIDTH, 3 * A_HEADS)
N_IN = sum(IN_SIZES)

kernel_name = "hymba_mlstm_nsa_moe_step"


def _layer_norm(x, g=None, b=None):
    xf = x.astype(jnp.float32)
    mu = jnp.mean(xf, -1, keepdims=True)
    var = jnp.mean(jnp.square(xf - mu), -1, keepdims=True)
    y = (xf - mu) * lax.rsqrt(var + LN_EPS)
    if g is not None:
        y = y * g.astype(jnp.float32) + b.astype(jnp.float32)
    return y.astype(x.dtype)


def _rope(x, pos):
    inv = ROPE_THETA ** (-jnp.arange(0, ROPE_DIM, 2, dtype=jnp.float32) / ROPE_DIM)
    ang = pos.astype(jnp.float32)[:, None] * inv[None, :]
    cos, sin = jnp.cos(ang)[:, None, :], jnp.sin(ang)[:, None, :]
    xr = x[..., :ROPE_DIM].astype(jnp.float32)
    x1, x2 = xr[..., :ROPE_DIM // 2], xr[..., ROPE_DIM // 2:]
    rot = jnp.concatenate([x1 * cos - x2 * sin, x2 * cos + x1 * sin], -1).astype(x.dtype)
    return jnp.concatenate([rot, x[..., ROPE_DIM:]], -1)


def _masked_softmax(s, valid):
    s = jnp.where(valid, s, -jnp.inf)
    m = jnp.max(s, -1, keepdims=True)
    m = jnp.where(jnp.isfinite(m), m, 0.0)
    e = jnp.exp(s - m)
    return e / jnp.maximum(jnp.sum(e, -1, keepdims=True), jnp.finfo(jnp.float32).tiny)


def _pre(x, c, pos, w_ada, b_ada, w_in, b_in):
    B, T, D = x.shape
    mod = (c @ w_ada + b_ada).reshape(B, 6, 1, D)
    h = _layer_norm(x) * (1 + mod[:, 1]) + mod[:, 0]
    z = h @ w_in + b_in
    cuts = np.cumsum(IN_SIZES)[:-1].tolist()
    qm, km, vm, om, im, fm, qa, kc, vc, ks, vs, kw, vw, ga = jnp.split(z, cuts, axis=-1)
    hm = lambda t: t.reshape(B, T, M_HEADS, M_DIM)
    hk = lambda t: t.reshape(B, T, A_KV_HEADS, A_DIM)
    qa = _rope(qa.reshape(B, T, A_HEADS, A_DIM), pos)
    cmp_kv = jnp.stack([_rope(hk(kc), pos), hk(vc)], axis=2)
    sel_kv = jnp.stack([_rope(hk(ks), pos), hk(vs)], axis=2)
    win_kv = jnp.stack([_rope(hk(kw), pos), hk(vw)], axis=2)
    return mod, (hm(qm), hm(km), hm(vm), om, im, fm, qa, cmp_kv, sel_kv, win_kv,
                 ga.reshape(B, T, A_HEADS, 3))


def _mlstm(q, k, v, i_pre, f_pre, C0, n0, m0):
    B, T, H, d = q.shape
    L = math.gcd(T, M_CHUNK)
    nc = T // L
    f32 = jnp.float32

    def chunks(a):
        return jnp.moveaxis(a.astype(f32).reshape(B, nc, L, *a.shape[2:]), 1, 0)

    xs = (chunks(q), chunks(k * d ** -0.5), chunks(v), chunks(i_pre),
          chunks(jax.nn.log_sigmoid(f_pre.astype(f32))))
    causal = jnp.tril(jnp.ones((L, L), bool))[None, :, :, None]

    def step(carry, xc):
        C, n, m = carry
        qc, kc, vc, ic, fc = xc
        b = jnp.cumsum(fc, axis=1)
        D = jnp.where(causal, b[:, :, None] - b[:, None] + ic[:, None], -jnp.inf)
        m_t = jnp.maximum(b + m[:, None], jnp.max(D, axis=2))
        inter = jnp.exp(b + m[:, None] - m_t)
        A = jnp.exp(D - m_t[:, :, None]) * jnp.einsum('bthd,bshd->btsh', qc, kc)
        num = inter[..., None] * jnp.einsum('bhvk,bthk->bthv', C, qc) + jnp.einsum('btsh,bshv->bthv', A, vc)
        den = inter * jnp.einsum('bhk,bthk->bth', n, qc) + jnp.sum(A, axis=2)
        h = num / jnp.maximum(jnp.abs(den), jnp.exp(-m_t))[..., None]
        m_new = m_t[:, -1]
        w_src = jnp.exp(b[:, -1:] - b + ic - m_new[:, None])
        w_old = jnp.exp(b[:, -1] + m - m_new)
        C = w_old[..., None, None] * C + jnp.einsum('bsh,bshv,bshk->bhvk', w_src, vc, kc)
        n = w_old[..., None] * n + jnp.einsum('bsh,bshk->bhk', w_src, kc)
        return (C, n, m_new), h

    (C, n, m), h = lax.scan(step, (C0.astype(f32), n0.astype(f32), m0.astype(f32)), xs)
    h = jnp.moveaxis(h, 0, 1).reshape(B, T, H, d)
    dt = q.dtype
    return h.astype(dt), C.astype(dt), n.astype(dt), m.astype(dt)


def _mlstm_out(h, o_pre, g):
    B, T = h.shape[:2]
    hn = _layer_norm(h) * g.reshape(M_HEADS, M_DIM)
    return hn.reshape(B, T, M_WIDTH) * jax.nn.sigmoid(o_pre)


def _compress(rows, pe, w1, w2):
    B, L = rows.shape[:2]
    n_cmp = (L - CMP_LEN) // CMP_STRIDE + 1
    idx = jnp.arange(n_cmp)[:, None] * CMP_STRIDE + jnp.arange(CMP_LEN)[None, :]
    blk = rows[:, idx] + pe[:, None, :]
    flat = blk.transpose(0, 1, 3, 2, 4).reshape(B, n_cmp, A_KV_HEADS, CMP_LEN * A_DIM)
    return jax.nn.gelu(flat @ w1) @ w2


def _sel_blocks(rows):
    B, L = rows.shape[:2]
    n_sel = -(-L // SEL_LEN)
    rows = jnp.pad(rows, ((0, 0), (0, n_sel * SEL_LEN - L), (0, 0), (0, 0)))
    return rows.reshape(B, n_sel, SEL_LEN, A_KV_HEADS, A_DIM).transpose(0, 3, 1, 2, 4)


def _nsa_branches(cmp_kv, sel_kv, cmp_params):
    pe_k, w1_k, w2_k, pe_v, w1_v, w2_v = cmp_params
    return (_compress(cmp_kv[:, :, 0], pe_k, w1_k, w2_k), _compress(cmp_kv[:, :, 1], pe_v, w1_v, w2_v),
            _sel_blocks(sel_kv[:, :, 0]), _sel_blocks(sel_kv[:, :, 1]))


def _nsa_core(q, gates, q_pos, kcmp, vcmp, ksb, vsb, kw, vw, kw_pos):
    B, Tq = q.shape[:2]
    f32 = jnp.float32
    qg = q.reshape(B, Tq, A_KV_HEADS, A_GROUP, A_DIM)
    scale = A_DIM ** -0.5
    n_cmp, n_sel = kcmp.shape[1], ksb.shape[2]
    s_c = jnp.einsum('btgrd,bngd->bgrtn', qg, kcmp).astype(f32) * scale
    cmp_end = jnp.arange(n_cmp) * CMP_STRIDE + (CMP_LEN - 1)
    p_c = _masked_softmax(s_c, cmp_end[None, :] <= q_pos[:, None])
    o_c = jnp.einsum('bgrtn,bngd->btgrd', p_c.astype(q.dtype), vcmp)
    ci = jnp.arange(n_cmp)[:, None] * CMP_STRIDE
    sj = jnp.arange(n_sel)[None, :] * SEL_LEN
    overlap = ((ci < sj + SEL_LEN) & (ci + CMP_LEN > sj)).astype(f32)
    score = jnp.einsum('bgrtn,nj->bgtj', p_c, overlap)
    j = jnp.arange(n_sel)[None, :]
    cur = (q_pos // SEL_LEN)[:, None]
    forced = (j == 0) | (j == cur) | (j == cur - 1)
    score = jnp.where(forced, FORCE_SCORE, score)
    score = jnp.where(j * SEL_LEN <= q_pos[:, None], score, -1.0)
    _, idx = lax.top_k(score, min(SEL_TOP, n_sel))
    bi = jnp.arange(B)[:, None, None, None]
    gi = jnp.arange(A_KV_HEADS)[None, :, None, None]
    ks, vs = ksb[bi, gi, idx], vsb[bi, gi, idx]
    k_pos = idx[..., None] * SEL_LEN + jnp.arange(SEL_LEN)
    valid = (k_pos <= q_pos[:, None, None])[:, :, None]
    s_s = jnp.einsum('btgrd,bgtnld->bgrtnl', qg, ks).astype(f32) * scale
    shp = s_s.shape
    p_s = _masked_softmax(s_s.reshape(*shp[:4], -1),
                          jnp.broadcast_to(valid, shp).reshape(*shp[:4], -1)).reshape(shp)
    o_s = jnp.einsum('bgrtnl,bgtnld->btgrd', p_s.astype(q.dtype), vs)
    s_w = jnp.einsum('btgrd,bkgd->bgrtk', qg, kw).astype(f32) * scale
    dist = q_pos[:, None] - kw_pos[None, :]
    p_w = _masked_softmax(s_w, (dist >= 0) & (dist < WINDOW) & (kw_pos[None, :] >= 0))
    o_w = jnp.einsum('bgrtk,bkgd->btgrd', p_w.astype(q.dtype), vw)
    g = jax.nn.sigmoid(gates.astype(f32)).reshape(B, Tq, A_KV_HEADS, A_GROUP, 3).astype(q.dtype)
    o = g[..., 0:1] * o_c + g[..., 1:2] * o_s + g[..., 2:3] * o_w
    return o.reshape(B, Tq, A_WIDTH)


def _nsa_prompt(q, gates, cmp_kv, sel_kv, win_kv, cmp_params):
    B, S = q.shape[:2]
    kcmp, vcmp, ksb, vsb = _nsa_branches(cmp_kv, sel_kv, cmp_params)
    win_pad = jnp.pad(win_kv, ((0, 0), (WINDOW, 0), (0, 0), (0, 0), (0, 0)))
    nb = S // Q_BLOCK
    qb = jnp.moveaxis(q.reshape(B, nb, Q_BLOCK, A_HEADS, A_DIM), 1, 0)
    gb = jnp.moveaxis(gates.reshape(B, nb, Q_BLOCK, A_HEADS, 3), 1, 0)

    def block(args):
        qi, gi, s0 = args
        w = lax.dynamic_slice_in_dim(win_pad, s0, WINDOW + Q_BLOCK, axis=1)
        return _nsa_core(qi, gi, s0 + jnp.arange(Q_BLOCK), kcmp, vcmp, ksb, vsb,
                         w[:, :, 0], w[:, :, 1], s0 - WINDOW + jnp.arange(WINDOW + Q_BLOCK))

    out = lax.map(block, (qb, gb, jnp.arange(nb) * Q_BLOCK))
    return jnp.moveaxis(out, 0, 1).reshape(B, S, A_WIDTH)


def _gather_pages(pool, page_table):
    db, n_pages = page_table.shape
    return pool[page_table].reshape(db, n_pages * pool.shape[1], *pool.shape[2:])


def _moe(h, w_router, b_router, w_gu, b_gu, w_down, b_down):
    T, D = h.shape
    logits = (h @ w_router + b_router).astype(jnp.float32)
    top_val, top_idx = lax.top_k(logits, TOP_K)
    gate_w = jax.nn.softmax(top_val, axis=-1).astype(h.dtype)
    n_slots = T * TOP_K
    expert = top_idx.reshape(-1)
    order = jnp.argsort(expert)
    sorted_e = expert[order]
    counts = jnp.zeros((N_EXPERTS,), jnp.int32).at[expert].add(1)
    padded = (counts + MOE_BLOCK - 1) // MOE_BLOCK * MOE_BLOCK
    pad_end = jnp.cumsum(padded)
    pad_start = pad_end - padded
    grp_start = jnp.cumsum(counts) - counts
    dest_sorted = pad_start[sorted_e] + jnp.arange(n_slots, dtype=jnp.int32) - grp_start[sorted_e]
    dest = jnp.zeros((n_slots,), jnp.int32).at[order].set(dest_sorted)
    n_blocks = -(-n_slots // MOE_BLOCK) + N_EXPERTS
    src = jnp.zeros((n_blocks * MOE_BLOCK,), jnp.int32).at[dest].set(jnp.arange(n_slots, dtype=jnp.int32) // TOP_K)
    xb = h[src].reshape(n_blocks, MOE_BLOCK, D)
    blk_e = jnp.minimum(jnp.searchsorted(pad_end, jnp.arange(n_blocks) * MOE_BLOCK, side='right'), N_EXPERTS - 1)

    def expert_block(args):
        xe, e = args
        gu = xe @ w_gu[e] + b_gu[e]
        g = jnp.minimum(gu[:, :D_FF], SWIGLU_LIMIT)
        u = jnp.clip(gu[:, D_FF:], -SWIGLU_LIMIT, SWIGLU_LIMIT)
        return ((u + 1) * (g * jax.nn.sigmoid(SWIGLU_ALPHA * g))) @ w_down[e] + b_down[e]

    yb = lax.map(expert_block, (xb, blk_e)).reshape(n_blocks * MOE_BLOCK, D)
    return jnp.einsum('tk,tkd->td', gate_w, yb[dest].reshape(T, TOP_K, D))


def _post(x, mix, mod, ln1_g, ln1_b, ln2_g, ln2_b, moe_params):
    x1 = _layer_norm(DN_ALPHA * x + mod[:, 2] * mix, ln1_g, ln1_b)
    h2 = _layer_norm(x1) * (1 + mod[:, 4]) + mod[:, 3]
    B, T, D = h2.shape
    f = _moe(h2.reshape(B * T, D), *moe_params).reshape(B, T, D)
    return _layer_norm(DN_ALPHA * x1 + mod[:, 5] * f, ln2_g, ln2_b)


def setup_inputs(seed: int = 0) -> dict:
    key = jax.random.key(seed)
    ks = jax.random.split(key, 40)
    nrm = lambda k, shape, s: jax.random.normal(k, shape, jnp.float32) * s
    n_pages = PAST_LEN // PAGE_SIZE
    n_pool = (5 * DEC_BATCH * n_pages + 3) // 4
    wbuf = min(WINDOW, PAST_LEN)
    page_table = jax.random.permutation(ks[0], n_pool)[:DEC_BATCH * n_pages].reshape(DEC_BATCH, n_pages).astype(jnp.int32)
    f_off = 4 * M_WIDTH + M_HEADS
    b_in = nrm(ks[1], (DEPTH, N_IN), 0.02).at[:, f_off:f_off + M_HEADS].add(FORGET_BIAS)
    kv_page = (DEPTH, n_pool, PAGE_SIZE, 2, A_KV_HEADS, A_DIM)
    return {
        "x_prompt": nrm(ks[2], (BATCH, SEQ, D_MODEL), 1.0),
        "x_sample": nrm(ks[3], (DEC_BATCH, DEC_SEQ, D_MODEL), 1.0),
        "cache_cmp": nrm(ks[4], kv_page, 1.0),
        "cache_sel": nrm(ks[5], kv_page, 1.0),
        "state_win": nrm(ks[6], (DEPTH, DEC_BATCH, wbuf, 2, A_KV_HEADS, A_DIM), 1.0),
        "state_C": nrm(ks[7], (DEPTH, DEC_BATCH, M_HEADS, M_DIM, M_DIM), 0.1),
        "state_n": nrm(ks[8], (DEPTH, DEC_BATCH, M_HEADS, M_DIM), 0.5),
        "state_m": nrm(ks[9], (DEPTH, DEC_BATCH, M_HEADS), 1.0),
        "page_table": page_table,
        "c_prompt": nrm(ks[10], (BATCH, D_MODEL), 1.0),
        "c_sample": nrm(ks[11], (DEC_BATCH, D_MODEL), 1.0),
        "w_ada": nrm(ks[12], (DEPTH, D_MODEL, 6 * D_MODEL), 0.5 * D_MODEL ** -0.5),
        "b_ada": nrm(ks[13], (DEPTH, 6 * D_MODEL), 0.02),
        "w_in": nrm(ks[14], (DEPTH, D_MODEL, N_IN), D_MODEL ** -0.5),
        "b_in": b_in,
        "m_norm_g": 1.0 + nrm(ks[15], (DEPTH, M_WIDTH), 0.02),
        "cmp_pe_k": nrm(ks[16], (DEPTH, CMP_LEN, A_DIM), 0.02),
        "cmp_w1_k": nrm(ks[17], (DEPTH, CMP_LEN * A_DIM, A_DIM), (CMP_LEN * A_DIM) ** -0.5),
        "cmp_w2_k": nrm(ks[18], (DEPTH, A_DIM, A_DIM), A_DIM ** -0.5),
        "cmp_pe_v": nrm(ks[19], (DEPTH, CMP_LEN, A_DIM), 0.02),
        "cmp_w1_v": nrm(ks[20], (DEPTH, CMP_LEN * A_DIM, A_DIM), (CMP_LEN * A_DIM) ** -0.5),
        "cmp_w2_v": nrm(ks[21], (DEPTH, A_DIM, A_DIM), A_DIM ** -0.5),
        "w_out": nrm(ks[22], (DEPTH, MIX_WIDTH, D_MODEL), DN_BETA * MIX_WIDTH ** -0.5),
        "ln1_g": 1.0 + nrm(ks[23], (DEPTH, D_MODEL), 0.02),
        "ln1_b": nrm(ks[24], (DEPTH, D_MODEL), 0.02),
        "w_router": nrm(ks[25], (DEPTH, D_MODEL, N_EXPERTS), D_MODEL ** -0.5),
        "b_router": nrm(ks[26], (DEPTH, N_EXPERTS), 0.01),
        "w_gu": nrm(ks[27], (DEPTH, N_EXPERTS, D_MODEL, 2 * D_FF), D_MODEL ** -0.5),
        "b_gu": nrm(ks[28], (DEPTH, N_EXPERTS, 2 * D_FF), 0.02),
        "w_down": nrm(ks[29], (DEPTH, N_EXPERTS, D_FF, D_MODEL), DN_BETA * D_FF ** -0.5),
        "b_down": nrm(ks[30], (DEPTH, N_EXPERTS, D_MODEL), 0.02),
        "ln2_g": 1.0 + nrm(ks[31], (DEPTH, D_MODEL), 0.02),
        "ln2_b": nrm(ks[32], (DEPTH, D_MODEL), 0.02),
    }


def reference(x_prompt, x_sample, cache_cmp, cache_sel, state_win, state_C, state_n, state_m, page_table,
              c_prompt, c_sample, w_ada, b_ada, w_in, b_in, m_norm_g, cmp_pe_k, cmp_w1_k, cmp_w2_k,
              cmp_pe_v, cmp_w1_v, cmp_w2_v, w_out, ln1_g, ln1_b, w_router, b_router, w_gu, b_gu,
              w_down, b_down, ln2_g, ln2_b):
    B, S, _ = x_prompt.shape
    DB, T, _ = x_sample.shape
    past_len = page_table.shape[1] * cache_cmp.shape[2]
    wbuf = state_win.shape[2]
    pos_p = jnp.arange(S)
    pos_s = past_len + jnp.arange(T)
    y_prompt, y_sample = x_prompt, x_sample
    l_cmp_p, l_sel_p, l_win_p, l_C_p, l_n_p, l_m_p = [], [], [], [], [], []
    l_cmp_s, l_sel_s, l_win_s, l_C_s, l_n_s, l_m_s = [], [], [], [], [], []
    for l in range(DEPTH):
        cmp_l = (cmp_pe_k[l], cmp_w1_k[l], cmp_w2_k[l], cmp_pe_v[l], cmp_w1_v[l], cmp_w2_v[l])
        moe_l = (w_router[l], b_router[l], w_gu[l], b_gu[l], w_down[l], b_down[l])
        mod, (qm, km, vm, om, im, fm, qa, cmp_kv, sel_kv, win_kv, ga) = _pre(
            y_prompt, c_prompt, pos_p, w_ada[l], b_ada[l], w_in[l], b_in[l])
        zero_c = jnp.zeros((B, M_HEADS, M_DIM, M_DIM), y_prompt.dtype)
        hm, C_p, n_p, m_p = _mlstm(qm, km, vm, im, fm, zero_c, zero_c[..., 0], zero_c[..., 0, 0])
        ma = _nsa_prompt(qa, ga, cmp_kv, sel_kv, win_kv, cmp_l)
        mix = jnp.concatenate([_mlstm_out(hm, om, m_norm_g[l]), ma], axis=-1) @ w_out[l]
        y_prompt = _post(y_prompt, mix, mod, ln1_g[l], ln1_b[l], ln2_g[l], ln2_b[l], moe_l)
        l_cmp_p.append(cmp_kv)
        l_sel_p.append(sel_kv)
        l_win_p.append(win_kv[:, -min(WINDOW, S):])
        l_C_p.append(C_p)
        l_n_p.append(n_p)
        l_m_p.append(m_p)
        mod, (qm, km, vm, om, im, fm, qa, cmp_kv, sel_kv, win_kv, ga) = _pre(
            y_sample, c_sample, pos_s, w_ada[l], b_ada[l], w_in[l], b_in[l])
        hm, C_s, n_s, m_s = _mlstm(qm, km, vm, im, fm, state_C[l], state_n[l], state_m[l])
        full_cmp = jnp.concatenate([_gather_pages(cache_cmp[l], page_table), cmp_kv], axis=1)
        full_sel = jnp.concatenate([_gather_pages(cache_sel[l], page_table), sel_kv], axis=1)
        kcmp, vcmp, ksb, vsb = _nsa_branches(full_cmp, full_sel, cmp_l)
        win_all = jnp.concatenate([state_win[l], win_kv], axis=1)
        ma = _nsa_core(qa, ga, pos_s, kcmp, vcmp, ksb, vsb, win_all[:, :, 0], win_all[:, :, 1],
                       past_len - wbuf + jnp.arange(wbuf + T))
        mix = jnp.concatenate([_mlstm_out(hm, om, m_norm_g[l]), ma], axis=-1) @ w_out[l]
        y_sample = _post(y_sample, mix, mod, ln1_g[l], ln1_b[l], ln2_g[l], ln2_b[l], moe_l)
        l_cmp_s.append(cmp_kv)
        l_sel_s.append(sel_kv)
        l_win_s.append(win_all[:, -wbuf:])
        l_C_s.append(C_s)
        l_n_s.append(n_s)
        l_m_s.append(m_s)
    new_cmp_p, new_sel_p, new_win_p = jnp.stack(l_cmp_p), jnp.stack(l_sel_p), jnp.stack(l_win_p)
    new_C_p, new_n_p, new_m_p = jnp.stack(l_C_p), jnp.stack(l_n_p), jnp.stack(l_m_p)
    new_cmp_s, new_sel_s, new_win_s = jnp.stack(l_cmp_s), jnp.stack(l_sel_s), jnp.stack(l_win_s)
    new_C_s, new_n_s, new_m_s = jnp.stack(l_C_s), jnp.stack(l_n_s), jnp.stack(l_m_s)
    return (y_prompt, y_sample, new_cmp_p, new_sel_p, new_win_p, new_C_p, new_n_p, new_m_p,
            new_cmp_s, new_sel_s, new_win_s, new_C_s, new_n_s, new_m_s)
```

```python
import functools
import math

import numpy as np
import jax
import jax.numpy as jnp
from jax import lax
from jax.experimental import pallas as pl
from jax.experimental.pallas import tpu as pltpu

F32 = jnp.float32
BF16 = jnp.bfloat16

M_HEADS = 4
M_DIM = 128
M_WIDTH = M_HEADS * M_DIM
M_CHUNK = 64
A_HEADS = 8
A_KV_HEADS = 2
A_GROUP = A_HEADS // A_KV_HEADS
A_DIM = 64
A_WIDTH = A_HEADS * A_DIM
KV_WIDTH = A_KV_HEADS * A_DIM
CMP_LEN = 32
CMP_STRIDE = 16
SEL_LEN = 64
SEL_TOP = 16
WINDOW = 512
Q_BLOCK = 128
FORCE_SCORE = 1.0e4
ROPE_THETA = 500000.0
ROPE_DIM = A_DIM // 4
TOP_K = 4
SWIGLU_LIMIT = 7.0
SWIGLU_ALPHA = 1.702
LN_EPS = 1e-5
LANES = 128
MOE_ROWS = 256
VMEM_LIMIT = 56 * 1024 * 1024

_O_MQ = 0
_O_IF = 4 * M_WIDTH
_O_QA = _O_IF + 2 * M_HEADS
_O_KV = _O_QA + A_WIDTH
_O_GA = _O_KV + 6 * KV_WIDTH
_N_IN = _O_GA + 3 * A_HEADS
_R_QA = 4 * M_WIDTH
_R_KV = _R_QA + A_WIDTH
_R_GT = _R_KV + 6 * KV_WIDTH
_R_END = _R_GT + LANES


def _cparams(sem):
    return pltpu.CompilerParams(dimension_semantics=sem, vmem_limit_bytes=VMEM_LIMIT)


def _ln_core(x):
    mu = jnp.mean(x, axis=-1, keepdims=True)
    xc = x - mu
    var = jnp.mean(xc * xc, axis=-1, keepdims=True)
    return xc * lax.rsqrt(var + LN_EPS)


def _ada_kernel(c_ref, w_ref, b_ref, o_ref):
    o_ref[...] = jnp.dot(c_ref[...].astype(BF16), w_ref[...].astype(BF16),
                         preferred_element_type=F32) + b_ref[...]


def _ada(c, w_ada, b_ada):
    n, d = c.shape
    cols = w_ada.shape[1]
    return pl.pallas_call(
        _ada_kernel,
        grid=(cols // d,),
        in_specs=[pl.BlockSpec((n, d), lambda j: (0, 0)),
                  pl.BlockSpec((d, d), lambda j: (0, j)),
                  pl.BlockSpec((1, d), lambda j: (0, j))],
        out_specs=pl.BlockSpec((n, d), lambda j: (0, j)),
        out_shape=jax.ShapeDtypeStruct((n, cols), F32),
        compiler_params=_cparams(("arbitrary",)),
        name="ada",
    )(c, w_ada, b_ada.reshape(1, cols))


def _rope_apply(v, cos, sa, sb):
    reps = v.shape[1] // LANES
    tile = lambda t: t if reps == 1 else jnp.concatenate([t] * reps, axis=1)
    w = v.shape[1]
    return (v * tile(cos) + pltpu.roll(v, w - ROPE_DIM // 2, 1) * tile(sa)
            + pltpu.roll(v, ROPE_DIM // 2, 1) * tile(sb))


def _pre_kernel(x_ref, mod_ref, cos_ref, sa_ref, sb_ref, w_ref, b_ref,
                mq_ref, gt_ref, qa_ref, cmp_ref, sel_ref, win_ref):
    bb, tt, d = x_ref.shape
    mod = mod_ref[...]
    h = _ln_core(x_ref[...]) * (1.0 + mod[:, 1:2, :]) + mod[:, 0:1, :]
    h = h.reshape(bb * tt, d).astype(BF16)
    z = jnp.dot(h, w_ref[...], preferred_element_type=F32) + b_ref[...]
    cos, sa, sb = cos_ref[...], sa_ref[...], sb_ref[...]
    mq_ref[...] = z[:, :_R_QA]
    gt_ref[...] = z[:, _R_GT:_R_END]
    qa_ref[...] = _rope_apply(z[:, _R_QA:_R_KV], cos, sa, sb)
    for n, ref in enumerate((cmp_ref, sel_ref, win_ref)):
        o = _R_KV + 2 * KV_WIDTH * n
        ref[:, :KV_WIDTH] = _rope_apply(z[:, o:o + KV_WIDTH], cos, sa, sb)
        ref[:, KV_WIDTH:] = z[:, o + KV_WIDTH:o + 2 * KV_WIDTH]


def _rope_tables(pos):
    half = ROPE_DIM // 2
    inv = ROPE_THETA ** (-jnp.arange(0, ROPE_DIM, 2, dtype=F32) / ROPE_DIM)
    ang = pos.astype(F32)[:, None] * inv[None, :]
    cos, sin = jnp.cos(ang), jnp.sin(ang)
    n = pos.shape[0]
    one = jnp.ones((n, A_DIM - ROPE_DIM), F32)
    zero = jnp.zeros((n, A_DIM - ROPE_DIM), F32)
    zh = jnp.zeros((n, half), F32)
    cos_t = jnp.concatenate([cos, cos, one], axis=1)
    sa_t = jnp.concatenate([-sin, zh, zero], axis=1)
    sb_t = jnp.concatenate([zh, sin, zero], axis=1)
    two = lambda t: jnp.concatenate([t, t], axis=1)
    return two(cos_t), two(sa_t), two(sb_t)


def _prep_w_in(w_in, b_in):
    pad = LANES - 2 * M_HEADS - 3 * A_HEADS
    cat = lambda a: jnp.concatenate(
        [a[..., _O_MQ:_O_IF], a[..., _O_QA:_O_GA], a[..., _O_IF:_O_QA], a[..., _O_GA:_N_IN],
         jnp.zeros(a.shape[:-1] + (pad,), a.dtype)], axis=-1)
    return cat(w_in).astype(BF16), cat(b_in[None, :])


def _pre(x, mod, pos, w_r, b_r, bb, tt):
    B, T, d = x.shape
    nt = T // tt
    rows = bb * tt
    cos, sa, sb = _rope_tables(pos)
    if bb > 1:
        cos, sa, sb = (jnp.tile(t, (bb, 1)) for t in (cos, sa, sb))
    n_tok = B * T
    tab = pl.BlockSpec((rows, LANES), lambda i, j: (j, 0))
    row = lambda w: pl.BlockSpec((rows, w), lambda i, j: (i * nt + j, 0))
    widths = (_R_QA, LANES, A_WIDTH, 2 * KV_WIDTH, 2 * KV_WIDTH, 2 * KV_WIDTH)
    return pl.pallas_call(
        _pre_kernel,
        grid=(B // bb, nt),
        in_specs=[pl.BlockSpec((bb, tt, d), lambda i, j: (i, j, 0)),
                  pl.BlockSpec((bb, 6, d), lambda i, j: (i, 0, 0)),
                  tab, tab, tab,
                  pl.BlockSpec((d, _R_END), lambda i, j: (0, 0)),
                  pl.BlockSpec((1, _R_END), lambda i, j: (0, 0))],
        out_specs=[row(w) for w in widths],
        out_shape=[jax.ShapeDtypeStruct((n_tok, w), F32) for w in widths],
        compiler_params=_cparams(("parallel", "arbitrary")),
        name="pre",
    )(x, mod, cos, sa, sb, w_r, b_r)


def _post1_kernel(mo_ref, ao_ref, x_ref, mod_ref, wo_ref, g_ref, b_ref, wr_ref, br_ref,
                  x1_ref, h2_ref, lg_ref, *, alpha):
    bb, tt, d = x_ref.shape
    mod = mod_ref[...]
    mixin = jnp.concatenate([mo_ref[...], ao_ref[...]], axis=1).astype(BF16)
    mix = jnp.dot(mixin, wo_ref[...], preferred_element_type=F32).reshape(bb, tt, d)
    x1 = _ln_core(alpha * x_ref[...] + mod[:, 2:3, :] * mix) * g_ref[...] + b_ref[...]
    h2 = _ln_core(x1) * (1.0 + mod[:, 4:5, :]) + mod[:, 3:4, :]
    x1_ref[...] = x1
    h2f = h2.reshape(bb * tt, d)
    h2_ref[...] = h2f
    lg_ref[...] = jnp.dot(h2f, wr_ref[...], preferred_element_type=F32,
                          precision=lax.Precision.HIGHEST) + br_ref[...]


def _post1(mo, ao, x, mod, w_out, ln_g, ln_b, w_router, b_router, bb, tt, alpha):
    B, T, d = x.shape
    nt = T // tt
    rows = bb * tt
    ne = w_router.shape[1]
    wr = jnp.pad(w_router, ((0, 0), (0, LANES - ne)))
    br = jnp.pad(b_router, (0, LANES - ne), constant_values=-jnp.inf).reshape(1, LANES)
    full = lambda *s: pl.BlockSpec(s, lambda i, j: (0,) * len(s))
    row = lambda w: pl.BlockSpec((rows, w), lambda i, j: (i * nt + j, 0))
    return pl.pallas_call(
        functools.partial(_post1_kernel, alpha=alpha),
        grid=(B // bb, nt),
        in_specs=[row(M_WIDTH), row(A_WIDTH),
                  pl.BlockSpec((bb, tt, d), lambda i, j: (i, j, 0)),
                  pl.BlockSpec((bb, 6, d), lambda i, j: (i, 0, 0)),
                  full(M_WIDTH + A_WIDTH, d), full(1, d), full(1, d), full(d, LANES), full(1, LANES)],
        out_specs=[pl.BlockSpec((bb, tt, d), lambda i, j: (i, j, 0)), row(d), row(LANES)],
        out_shape=[jax.ShapeDtypeStruct((B, T, d), F32),
                   jax.ShapeDtypeStruct((B * T, d), F32),
                   jax.ShapeDtypeStruct((B * T, LANES), F32)],
        compiler_params=_cparams(("parallel", "arbitrary")),
        name="post1",
    )(mo, ao, x, mod, w_out.astype(BF16), ln_g.reshape(1, d), ln_b.reshape(1, d), wr, br)


def _ffn_kernel(be_ref, nu_ref, x_ref, wgu_ref, bgu_ref, wd_ref, bd_ref, y_ref):
    i = pl.program_id(0)
    dff = wd_ref.shape[1]

    @pl.when(i < nu_ref[0])
    def _():
        gu = jnp.dot(x_ref[...], wgu_ref[0], preferred_element_type=F32) + bgu_ref[0]
        g = jnp.minimum(gu[:, :dff], SWIGLU_LIMIT)
        u = jnp.clip(gu[:, dff:], -SWIGLU_LIMIT, SWIGLU_LIMIT)
        act = (u + 1.0) * (g * jax.nn.sigmoid(SWIGLU_ALPHA * g))
        y_ref[...] = jnp.dot(act.astype(BF16), wd_ref[0], preferred_element_type=F32) + bd_ref[0]

    @pl.when(i >= nu_ref[0])
    def _():
        y_ref[...] = jnp.zeros_like(y_ref)


def _ffn(xb, blk_e, n_used, w_gu, b_gu, w_down, b_down):
    rows, d = xb.shape
    ne, _, f2 = w_gu.shape
    dff = w_down.shape[1]
    nb = rows // MOE_ROWS
    grid_spec = pltpu.PrefetchScalarGridSpec(
        num_scalar_prefetch=2,
        grid=(nb,),
        in_specs=[pl.BlockSpec((MOE_ROWS, d), lambda i, be, nu: (i, 0)),
                  pl.BlockSpec((1, d, f2), lambda i, be, nu: (be[i], 0, 0)),
                  pl.BlockSpec((1, 1, f2), lambda i, be, nu: (be[i], 0, 0)),
                  pl.BlockSpec((1, dff, d), lambda i, be, nu: (be[i], 0, 0)),
                  pl.BlockSpec((1, 1, d), lambda i, be, nu: (be[i], 0, 0))],
        out_specs=pl.BlockSpec((MOE_ROWS, d), lambda i, be, nu: (i, 0)),
    )
    return pl.pallas_call(
        _ffn_kernel,
        grid_spec=grid_spec,
        out_shape=jax.ShapeDtypeStruct((rows, d), F32),
        compiler_params=_cparams(("arbitrary",)),
        name="ffn",
    )(blk_e, n_used, xb, w_gu.astype(BF16), b_gu.reshape(ne, 1, f2),
      w_down.astype(BF16), b_down.reshape(ne, 1, d))


def _post2_kernel(yg_ref, gw_ref, x1_ref, mod_ref, g_ref, b_ref, y_ref, *, alpha):
    gw = gw_ref[...]
    f = yg_ref[0] * gw[:, :, 0:1]
    for k in range(1, TOP_K):
        f = f + yg_ref[k] * gw[:, :, k:k + 1]
    y = alpha * x1_ref[...] + mod_ref[...][:, 5:6, :] * f
    y_ref[...] = _ln_core(y) * g_ref[...] + b_ref[...]


def _post2(yg, gw, x1, mod, ln_g, ln_b, bb, tt, alpha):
    B, T, d = x1.shape
    full = lambda *s: pl.BlockSpec(s, lambda i, j: (0,) * len(s))
    return pl.pallas_call(
        functools.partial(_post2_kernel, alpha=alpha),
        grid=(B // bb, T // tt),
        in_specs=[pl.BlockSpec((TOP_K, bb, tt, d), lambda i, j: (0, i, j, 0)),
                  pl.BlockSpec((bb, tt, TOP_K), lambda i, j: (i, j, 0)),
                  pl.BlockSpec((bb, tt, d), lambda i, j: (i, j, 0)),
                  pl.BlockSpec((bb, 6, d), lambda i, j: (i, 0, 0)),
                  full(1, d), full(1, d)],
        out_specs=pl.BlockSpec((bb, tt, d), lambda i, j: (i, j, 0)),
        out_shape=jax.ShapeDtypeStruct((B, T, d), F32),
        compiler_params=_cparams(("parallel", "arbitrary")),
        name="post2",
    )(yg, gw, x1, mod, ln_g.reshape(1, d), ln_b.reshape(1, d))


def _route(logits, n_experts):
    top_val, top_idx = lax.top_k(logits, TOP_K)
    gate_w = jax.nn.softmax(top_val, axis=-1)
    T = logits.shape[0]
    n_slots = T * TOP_K
    expert = top_idx.reshape(-1).astype(jnp.int32)
    order = jnp.argsort(expert)
    sorted_e = expert[order]
    counts = jnp.zeros((n_experts,), jnp.int32).at[expert].add(1)
    padded = (counts + MOE_ROWS - 1) // MOE_ROWS * MOE_ROWS
    pad_end = jnp.cumsum(padded)
    pad_start = pad_end - padded
    grp_start = jnp.cumsum(counts) - counts
    dest_sorted = pad_start[sorted_e] + jnp.arange(n_slots, dtype=jnp.int32) - grp_start[sorted_e]
    dest = jnp.zeros((n_slots,), jnp.int32).at[order].set(dest_sorted)
    n_blocks = -(-n_slots // MOE_ROWS) + n_experts
    src = jnp.zeros((n_blocks * MOE_ROWS,), jnp.int32).at[dest].set(
        jnp.arange(n_slots, dtype=jnp.int32) // TOP_K)
    n_used = (pad_end[-1] // MOE_ROWS).astype(jnp.int32)
    blk = jnp.minimum(jnp.arange(n_blocks, dtype=jnp.int32), n_used - 1) * MOE_ROWS
    blk_e = jnp.minimum(jnp.searchsorted(pad_end, blk, side='right'), n_experts - 1).astype(jnp.int32)
    return gate_w, dest, src, blk_e, n_used.reshape(1)


def _masked_softmax(s, valid):
    s = jnp.where(valid, s, -jnp.inf)
    m = jnp.max(s, -1, keepdims=True)
    m = jnp.where(jnp.isfinite(m), m, 0.0)
    e = jnp.exp(s - m)
    return e / jnp.maximum(jnp.sum(e, -1, keepdims=True), jnp.finfo(jnp.float32).tiny)


def _layer_norm(x):
    return _ln_core(x)


def _mlstm(q, k, v, i_pre, f_pre, C0, n0, m0):
    B, T, H, d = q.shape
    L = math.gcd(T, M_CHUNK)
    nc = T // L

    def chunks(a):
        return jnp.moveaxis(a.reshape(B, nc, L, *a.shape[2:]), 1, 0)

    xs = (chunks(q), chunks(k * d ** -0.5), chunks(v), chunks(i_pre), chunks(jax.nn.log_sigmoid(f_pre)))
    causal = jnp.tril(jnp.ones((L, L), bool))[None, :, :, None]

    def step(carry, xc):
        C, n, m = carry
        qc, kc, vc, ic, fc = xc
        b = jnp.cumsum(fc, axis=1)
        D = jnp.where(causal, b[:, :, None] - b[:, None] + ic[:, None], -jnp.inf)
        m_t = jnp.maximum(b + m[:, None], jnp.max(D, axis=2))
        inter = jnp.exp(b + m[:, None] - m_t)
        A = jnp.exp(D - m_t[:, :, None]) * jnp.einsum('bthd,bshd->btsh', qc, kc)
        num = inter[..., None] * jnp.einsum('bhvk,bthk->bthv', C, qc) + jnp.einsum('btsh,bshv->bthv', A, vc)
        den = inter * jnp.einsum('bhk,bthk->bth', n, qc) + jnp.sum(A, axis=2)
        h = num / jnp.maximum(jnp.abs(den), jnp.exp(-m_t))[..., None]
        m_new = m_t[:, -1]
        w_src = jnp.exp(b[:, -1:] - b + ic - m_new[:, None])
        w_old = jnp.exp(b[:, -1] + m - m_new)
        C = w_old[..., None, None] * C + jnp.einsum('bsh,bshv,bshk->bhvk', w_src, vc, kc)
        n = w_old[..., None] * n + jnp.einsum('bsh,bshk->bhk', w_src, kc)
        return (C, n, m_new), h

    (C, n, m), h = lax.scan(step, (C0, n0, m0), xs)
    h = jnp.moveaxis(h, 0, 1).reshape(B, T, H, d)
    return h, C, n, m


def _mlstm_out(h, o_pre, g):
    B, T = h.shape[:2]
    hn = _layer_norm(h) * g.reshape(M_HEADS, M_DIM)
    return hn.reshape(B, T, M_WIDTH) * jax.nn.sigmoid(o_pre)


def _compress(rows, pe, w1, w2):
    B, L = rows.shape[:2]
    n_cmp = (L - CMP_LEN) // CMP_STRIDE + 1
    idx = jnp.arange(n_cmp)[:, None] * CMP_STRIDE + jnp.arange(CMP_LEN)[None, :]
    blk = rows[:, idx] + pe[:, None, :]
    flat = blk.transpose(0, 1, 3, 2, 4).reshape(B, n_cmp, A_KV_HEADS, CMP_LEN * A_DIM)
    return jax.nn.gelu(flat @ w1) @ w2


def _sel_blocks(rows):
    B, L = rows.shape[:2]
    n_sel = -(-L // SEL_LEN)
    rows = jnp.pad(rows, ((0, 0), (0, n_sel * SEL_LEN - L), (0, 0), (0, 0)))
    return rows.reshape(B, n_sel, SEL_LEN, A_KV_HEADS, A_DIM).transpose(0, 3, 1, 2, 4)


def _nsa_branches(cmp_kv, sel_kv, cmp_params):
    pe_k, w1_k, w2_k, pe_v, w1_v, w2_v = cmp_params
    return (_compress(cmp_kv[:, :, 0], pe_k, w1_k, w2_k), _compress(cmp_kv[:, :, 1], pe_v, w1_v, w2_v),
            _sel_blocks(sel_kv[:, :, 0]), _sel_blocks(sel_kv[:, :, 1]))


def _nsa_core(q, gates, q_pos, kcmp, vcmp, ksb, vsb, kw, vw, kw_pos):
    B, Tq = q.shape[:2]
    qg = q.reshape(B, Tq, A_KV_HEADS, A_GROUP, A_DIM)
    scale = A_DIM ** -0.5
    n_cmp, n_sel = kcmp.shape[1], ksb.shape[2]
    s_c = jnp.einsum('btgrd,bngd->bgrtn', qg, kcmp) * scale
    cmp_end = jnp.arange(n_cmp) * CMP_STRIDE + (CMP_LEN - 1)
    p_c = _masked_softmax(s_c, cmp_end[None, :] <= q_pos[:, None])
    o_c = jnp.einsum('bgrtn,bngd->btgrd', p_c, vcmp)
    ci = jnp.arange(n_cmp)[:, None] * CMP_STRIDE
    sj = jnp.arange(n_sel)[None, :] * SEL_LEN
    overlap = ((ci < sj + SEL_LEN) & (ci + CMP_LEN > sj)).astype(F32)
    score = jnp.einsum('bgrtn,nj->bgtj', p_c, overlap)
    j = jnp.arange(n_sel)[None, :]
    cur = (q_pos // SEL_LEN)[:, None]
    forced = (j == 0) | (j == cur) | (j == cur - 1)
    score = jnp.where(forced, FORCE_SCORE, score)
    score = jnp.where(j * SEL_LEN <= q_pos[:, None], score, -1.0)
    _, idx = lax.top_k(score, min(SEL_TOP, n_sel))
    bi = jnp.arange(B)[:, None, None, None]
    gi = jnp.arange(A_KV_HEADS)[None, :, None, None]
    ks, vs = ksb[bi, gi, idx], vsb[bi, gi, idx]
    k_pos = idx[..., None] * SEL_LEN + jnp.arange(SEL_LEN)
    valid = (k_pos <= q_pos[:, None, None])[:, :, None]
    s_s = jnp.einsum('btgrd,bgtnld->bgrtnl', qg, ks) * scale
    shp = s_s.shape
    p_s = _masked_softmax(s_s.reshape(*shp[:4], -1),
                          jnp.broadcast_to(valid, shp).reshape(*shp[:4], -1)).reshape(shp)
    o_s = jnp.einsum('bgrtnl,bgtnld->btgrd', p_s, vs)
    s_w = jnp.einsum('btgrd,bkgd->bgrtk', qg, kw) * scale
    dist = q_pos[:, None] - kw_pos[None, :]
    p_w = _masked_softmax(s_w, (dist >= 0) & (dist < WINDOW) & (kw_pos[None, :] >= 0))
    o_w = jnp.einsum('bgrtk,bkgd->btgrd', p_w, vw)
    g = jax.nn.sigmoid(gates).reshape(B, Tq, A_KV_HEADS, A_GROUP, 3)
    o = g[..., 0:1] * o_c + g[..., 1:2] * o_s + g[..., 2:3] * o_w
    return o.reshape(B, Tq, A_WIDTH)


def _nsa_prompt(q, gates, cmp_kv, sel_kv, win_kv, cmp_params):
    B, S = q.shape[:2]
    kcmp, vcmp, ksb, vsb = _nsa_branches(cmp_kv, sel_kv, cmp_params)
    win_pad = jnp.pad(win_kv, ((0, 0), (WINDOW, 0), (0, 0), (0, 0), (0, 0)))
    nb = S // Q_BLOCK
    qb = jnp.moveaxis(q.reshape(B, nb, Q_BLOCK, A_HEADS, A_DIM), 1, 0)
    gb = jnp.moveaxis(gates.reshape(B, nb, Q_BLOCK, A_HEADS, 3), 1, 0)

    def block(args):
        qi, gi, s0 = args
        w = lax.dynamic_slice_in_dim(win_pad, s0, WINDOW + Q_BLOCK, axis=1)
        return _nsa_core(qi, gi, s0 + jnp.arange(Q_BLOCK), kcmp, vcmp, ksb, vsb,
                         w[:, :, 0], w[:, :, 1], s0 - WINDOW + jnp.arange(WINDOW + Q_BLOCK))

    out = lax.map(block, (qb, gb, jnp.arange(nb) * Q_BLOCK))
    return jnp.moveaxis(out, 0, 1).reshape(B, S, A_WIDTH)


def _split_pre(outs, B, T):
    mq, gt, qa, cmp_kv, sel_kv, win_kv = outs
    mq = mq.reshape(B, T, 4, M_HEADS, M_DIM)
    kv = lambda a: a.reshape(B, T, 2, A_KV_HEADS, A_DIM)
    gt = gt.reshape(B, T, LANES)
    return (mq[:, :, 0], mq[:, :, 1], mq[:, :, 2], mq[:, :, 3].reshape(B, T, M_WIDTH),
            gt[..., :M_HEADS], gt[..., M_HEADS:2 * M_HEADS],
            qa.reshape(B, T, A_HEADS, A_DIM), kv(cmp_kv), kv(sel_kv), kv(win_kv),
            gt[..., 2 * M_HEADS:2 * M_HEADS + 3 * A_HEADS].reshape(B, T, A_HEADS, 3))


def _moe_and_out(h2, logits, x1, mod, moe_w, ln_g, ln_b, bb, tt, alpha, n_experts):
    w_gu, b_gu, w_down, b_down = moe_w
    B, T, d = x1.shape
    gate_w, dest, src, blk_e, n_used = _route(logits, n_experts)
    xb = h2[src].astype(BF16)
    yb = _ffn(xb, blk_e, n_used, w_gu, b_gu, w_down, b_down)
    yg = yb[dest.reshape(B * T, TOP_K).T].reshape(TOP_K, B, T, d)
    return _post2(yg, gate_w.reshape(B, T, TOP_K), x1, mod, ln_g, ln_b, bb, tt, alpha)


def kernel(x_prompt, x_sample, cache_cmp, cache_sel, state_win, state_C, state_n, state_m, page_table,
           c_prompt, c_sample, w_ada, b_ada, w_in, b_in, m_norm_g, cmp_pe_k, cmp_w1_k, cmp_w2_k,
           cmp_pe_v, cmp_w1_v, cmp_w2_v, w_out, ln1_g, ln1_b, w_router, b_router, w_gu, b_gu,
           w_down, b_down, ln2_g, ln2_b):
    B, S, d = x_prompt.shape
    DB, T, _ = x_sample.shape
    depth = w_ada.shape[0]
    n_experts = w_router.shape[-1]
    alpha = (2 * depth) ** 0.25
    past_len = page_table.shape[1] * cache_cmp.shape[2]
    wbuf = state_win.shape[2]
    pos_p = jnp.arange(S)
    pos_s = past_len + jnp.arange(T)
    tt_p = min(S, 256)
    bb_s = min(DB, max(1, 256 // T))
    y_prompt, y_sample = x_prompt, x_sample
    outs = [[] for _ in range(12)]
    for l in range(depth):
        cmp_l = (cmp_pe_k[l], cmp_w1_k[l], cmp_w2_k[l], cmp_pe_v[l], cmp_w1_v[l], cmp_w2_v[l])
        moe_w = (w_gu[l], b_gu[l], w_down[l], b_down[l])
        w_r, b_r = _prep_w_in(w_in[l], b_in[l])
        c_all = jnp.concatenate([c_prompt, c_sample], axis=0)
        n_c = c_all.shape[0]
        c_all = jnp.pad(c_all, ((0, -n_c % 8), (0, 0)))
        mod_all = _ada(c_all, w_ada[l], b_ada[l]).reshape(-1, 6, d)
        mod_p, mod_s = mod_all[:B], mod_all[B:B + DB]

        pre = _pre(y_prompt, mod_p, pos_p, w_r, b_r, 1, tt_p)
        qm, km, vm, om, im, fm, qa, cmp_kv, sel_kv, win_kv, ga = _split_pre(pre, B, S)
        zc = jnp.zeros((B, M_HEADS, M_DIM, M_DIM), F32)
        hm, C_p, n_p, m_p = _mlstm(qm, km, vm, im, fm, zc, zc[..., 0], zc[..., 0, 0])
        ma = _nsa_prompt(qa, ga, cmp_kv, sel_kv, win_kv, cmp_l)
        mo = _mlstm_out(hm, om, m_norm_g[l])
        x1, h2, lg = _post1(mo.reshape(B * S, M_WIDTH), ma.reshape(B * S, A_WIDTH), y_prompt, mod_p,
                            w_out[l], ln1_g[l], ln1_b[l], w_router[l], b_router[l], 1, tt_p, alpha)
        y_prompt = _moe_and_out(h2, lg, x1, mod_p, moe_w, ln2_g[l], ln2_b[l], 1, tt_p, alpha, n_experts)
        for lst, v in zip(outs[:6], (cmp_kv, sel_kv, win_kv[:, -min(WINDOW, S):], C_p, n_p, m_p)):
            lst.append(v)

        pre = _pre(y_sample, mod_s, pos_s, w_r, b_r, bb_s, T)
        qm, km, vm, om, im, fm, qa, cmp_kv, sel_kv, win_kv, ga = _split_pre(pre, DB, T)
        hm, C_s, n_s, m_s = _mlstm(qm, km, vm, im, fm, state_C[l], state_n[l], state_m[l])
        gather = lambda pool: pool[page_table].reshape(DB, past_len, *pool.shape[2:])
        full_cmp = jnp.concatenate([gather(cache_cmp[l]), cmp_kv], axis=1)
        full_sel = jnp.concatenate([gather(cache_sel[l]), sel_kv], axis=1)
        kcmp, vcmp, ksb, vsb = _nsa_branches(full_cmp, full_sel, cmp_l)
        win_all = jnp.concatenate([state_win[l], win_kv], axis=1)
        ma = _nsa_core(qa, ga, pos_s, kcmp, vcmp, ksb, vsb, win_all[:, :, 0], win_all[:, :, 1],
                       past_len - wbuf + jnp.arange(wbuf + T))
        mo = _mlstm_out(hm, om, m_norm_g[l])
        x1, h2, lg = _post1(mo.reshape(DB * T, M_WIDTH), ma.reshape(DB * T, A_WIDTH), y_sample, mod_s,
                            w_out[l], ln1_g[l], ln1_b[l], w_router[l], b_router[l], bb_s, T, alpha)
        y_sample = _moe_and_out(h2, lg, x1, mod_s, moe_w, ln2_g[l], ln2_b[l], bb_s, T, alpha, n_experts)
        for lst, v in zip(outs[6:], (cmp_kv, sel_kv, win_all[:, -wbuf:], C_s, n_s, m_s)):
            lst.append(v)
    st = [jnp.stack(v) for v in outs]
    return (y_prompt, y_sample, *st)
```

```python
import functools
import math

import numpy as np
import jax
import jax.numpy as jnp
from jax import lax
from jax.experimental import pallas as pl
from jax.experimental.pallas import tpu as pltpu

F32 = jnp.float32
BF16 = jnp.bfloat16

M_HEADS = 4
M_DIM = 128
M_WIDTH = M_HEADS * M_DIM
M_CHUNK = 64
A_HEADS = 8
A_KV_HEADS = 2
A_GROUP = A_HEADS // A_KV_HEADS
A_DIM = 64
A_WIDTH = A_HEADS * A_DIM
KV_WIDTH = A_KV_HEADS * A_DIM
CMP_LEN = 32
CMP_STRIDE = 16
SEL_LEN = 64
SEL_TOP = 16
WINDOW = 512
Q_BLOCK = 128
FORCE_SCORE = 1.0e4
ROPE_THETA = 500000.0
ROPE_DIM = A_DIM // 4
TOP_K = 4
SWIGLU_LIMIT = 7.0
SWIGLU_ALPHA = 1.702
LN_EPS = 1e-5
LANES = 128
MOE_ROWS = 256
VMEM_LIMIT = 56 * 1024 * 1024

_O_MQ = 0
_O_IF = 4 * M_WIDTH
_O_QA = _O_IF + 2 * M_HEADS
_O_KV = _O_QA + A_WIDTH
_O_GA = _O_KV + 6 * KV_WIDTH
_N_IN = _O_GA + 3 * A_HEADS
_R_QA = 4 * M_WIDTH
_R_KV = _R_QA + A_WIDTH
_R_GT = _R_KV + 6 * KV_WIDTH
_R_END = _R_GT + LANES


def _cparams(sem):
    return pltpu.CompilerParams(dimension_semantics=sem, vmem_limit_bytes=VMEM_LIMIT)


def _ln_core(x):
    mu = jnp.mean(x, axis=-1, keepdims=True)
    xc = x - mu
    var = jnp.mean(xc * xc, axis=-1, keepdims=True)
    return xc * lax.rsqrt(var + LN_EPS)


def _ada_kernel(c_ref, w_ref, b_ref, o_ref):
    o_ref[...] = jnp.dot(c_ref[...].astype(BF16), w_ref[...].astype(BF16),
                         preferred_element_type=F32) + b_ref[...]


def _ada(c, w_ada, b_ada):
    n, d = c.shape
    cols = w_ada.shape[1]
    return pl.pallas_call(
        _ada_kernel,
        grid=(cols // d,),
        in_specs=[pl.BlockSpec((n, d), lambda j: (0, 0)),
                  pl.BlockSpec((d, d), lambda j: (0, j)),
                  pl.BlockSpec((1, d), lambda j: (0, j))],
        out_specs=pl.BlockSpec((n, d), lambda j: (0, j)),
        out_shape=jax.ShapeDtypeStruct((n, cols), F32),
        compiler_params=_cparams(("arbitrary",)),
        name="ada",
    )(c, w_ada, b_ada.reshape(1, cols))


def _rope_apply(v, cos, sa, sb):
    reps = v.shape[1] // LANES
    tile = lambda t: t if reps == 1 else jnp.concatenate([t] * reps, axis=1)
    w = v.shape[1]
    return (v * tile(cos) + pltpu.roll(v, w - ROPE_DIM // 2, 1) * tile(sa)
            + pltpu.roll(v, ROPE_DIM // 2, 1) * tile(sb))


def _pre_kernel(x_ref, mod_ref, cos_ref, sa_ref, sb_ref, w_ref, b_ref,
                mq_ref, gt_ref, qa_ref, cmp_ref, sel_ref, win_ref):
    bb, tt, d = x_ref.shape
    mod = mod_ref[...]
    h = _ln_core(x_ref[...]) * (1.0 + mod[:, 1:2, :]) + mod[:, 0:1, :]
    h = h.reshape(bb * tt, d).astype(BF16)
    z = jnp.dot(h, w_ref[...], preferred_element_type=F32) + b_ref[...]
    cos, sa, sb = cos_ref[...], sa_ref[...], sb_ref[...]
    mq_ref[...] = z[:, :_R_QA]
    gt_ref[...] = z[:, _R_GT:_R_END]
    qa_ref[...] = _rope_apply(z[:, _R_QA:_R_KV], cos, sa, sb)
    for n, ref in enumerate((cmp_ref, sel_ref, win_ref)):
        o = _R_KV + 2 * KV_WIDTH * n
        ref[:, :KV_WIDTH] = _rope_apply(z[:, o:o + KV_WIDTH], cos, sa, sb)
        ref[:, KV_WIDTH:] = z[:, o + KV_WIDTH:o + 2 * KV_WIDTH]


def _rope_tables(pos):
    half = ROPE_DIM // 2
    inv = ROPE_THETA ** (-jnp.arange(0, ROPE_DIM, 2, dtype=F32) / ROPE_DIM)
    ang = pos.astype(F32)[:, None] * inv[None, :]
    cos, sin = jnp.cos(ang), jnp.sin(ang)
    n = pos.shape[0]
    one = jnp.ones((n, A_DIM - ROPE_DIM), F32)
    zero = jnp.zeros((n, A_DIM - ROPE_DIM), F32)
    zh = jnp.zeros((n, half), F32)
    cos_t = jnp.concatenate([cos, cos, one], axis=1)
    sa_t = jnp.concatenate([-sin, zh, zero], axis=1)
    sb_t = jnp.concatenate([zh, sin, zero], axis=1)
    two = lambda t: jnp.concatenate([t, t], axis=1)
    return two(cos_t), two(sa_t), two(sb_t)


def _prep_w_in(w_in, b_in):
    pad = LANES - 2 * M_HEADS - 3 * A_HEADS
    cat = lambda a: jnp.concatenate(
        [a[..., _O_MQ:_O_IF], a[..., _O_QA:_O_GA], a[..., _O_IF:_O_QA], a[..., _O_GA:_N_IN],
         jnp.zeros(a.shape[:-1] + (pad,), a.dtype)], axis=-1)
    return cat(w_in).astype(BF16), cat(b_in[None, :])


def _pre(x, mod, pos, w_r, b_r, bb, tt):
    B, T, d = x.shape
    nt = T // tt
    rows = bb * tt
    cos, sa, sb = _rope_tables(pos)
    if bb > 1:
        cos, sa, sb = (jnp.tile(t, (bb, 1)) for t in (cos, sa, sb))
    n_tok = B * T
    tab = pl.BlockSpec((rows, LANES), lambda i, j: (j, 0))
    row = lambda w: pl.BlockSpec((rows, w), lambda i, j: (i * nt + j, 0))
    widths = (_R_QA, LANES, A_WIDTH, 2 * KV_WIDTH, 2 * KV_WIDTH, 2 * KV_WIDTH)
    return pl.pallas_call(
        _pre_kernel,
        grid=(B // bb, nt),
        in_specs=[pl.BlockSpec((bb, tt, d), lambda i, j: (i, j, 0)),
                  pl.BlockSpec((bb, 6, d), lambda i, j: (i, 0, 0)),
                  tab, tab, tab,
                  pl.BlockSpec((d, _R_END), lambda i, j: (0, 0)),
                  pl.BlockSpec((1, _R_END), lambda i, j: (0, 0))],
        out_specs=[row(w) for w in widths],
        out_shape=[jax.ShapeDtypeStruct((n_tok, w), F32) for w in widths],
        compiler_params=_cparams(("parallel", "arbitrary")),
        name="pre",
    )(x, mod, cos, sa, sb, w_r, b_r)


def _post1_kernel(mo_ref, ao_ref, x_ref, mod_ref, wo_ref, g_ref, b_ref, wr_ref, br_ref,
                  x1_ref, h2_ref, lg_ref, *, alpha):
    bb, tt, d = x_ref.shape
    mod = mod_ref[...]
    mixin = jnp.concatenate([mo_ref[...], ao_ref[...]], axis=1).astype(BF16)
    mix = jnp.dot(mixin, wo_ref[...], preferred_element_type=F32).reshape(bb, tt, d)
    x1 = _ln_core(alpha * x_ref[...] + mod[:, 2:3, :] * mix) * g_ref[...] + b_ref[...]
    h2 = _ln_core(x1) * (1.0 + mod[:, 4:5, :]) + mod[:, 3:4, :]
    x1_ref[...] = x1
    h2f = h2.reshape(bb * tt, d)
    h2_ref[...] = h2f
    lg_ref[...] = jnp.dot(h2f, wr_ref[...], preferred_element_type=F32,
                          precision=lax.Precision.HIGHEST) + br_ref[...]


def _post1(mo, ao, x, mod, w_out, ln_g, ln_b, w_router, b_router, bb, tt, alpha):
    B, T, d = x.shape
    nt = T // tt
    rows = bb * tt
    ne = w_router.shape[1]
    wr = jnp.pad(w_router, ((0, 0), (0, LANES - ne)))
    br = jnp.pad(b_router, (0, LANES - ne), constant_values=-jnp.inf).reshape(1, LANES)
    full = lambda *s: pl.BlockSpec(s, lambda i, j: (0,) * len(s))
    row = lambda w: pl.BlockSpec((rows, w), lambda i, j: (i * nt + j, 0))
    return pl.pallas_call(
        functools.partial(_post1_kernel, alpha=alpha),
        grid=(B // bb, nt),
        in_specs=[row(M_WIDTH), row(A_WIDTH),
                  pl.BlockSpec((bb, tt, d), lambda i, j: (i, j, 0)),
                  pl.BlockSpec((bb, 6, d), lambda i, j: (i, 0, 0)),
                  full(M_WIDTH + A_WIDTH, d), full(1, d), full(1, d), full(d, LANES), full(1, LANES)],
        out_specs=[pl.BlockSpec((bb, tt, d), lambda i, j: (i, j, 0)), row(d), row(LANES)],
        out_shape=[jax.ShapeDtypeStruct((B, T, d), F32),
                   jax.ShapeDtypeStruct((B * T, d), F32),
                   jax.ShapeDtypeStruct((B * T, LANES), F32)],
        compiler_params=_cparams(("parallel", "arbitrary")),
        name="post1",
    )(mo, ao, x, mod, w_out.astype(BF16), ln_g.reshape(1, d), ln_b.reshape(1, d), wr, br)


def _ffn_kernel(be_ref, nu_ref, x_ref, wgu_ref, bgu_ref, wd_ref, bd_ref, y_ref):
    i = pl.program_id(0)
    dff = wd_ref.shape[1]

    @pl.when(i < nu_ref[0])
    def _():
        gu = jnp.dot(x_ref[...], wgu_ref[0], preferred_element_type=F32) + bgu_ref[0]
        g = jnp.minimum(gu[:, :dff], SWIGLU_LIMIT)
        u = jnp.clip(gu[:, dff:], -SWIGLU_LIMIT, SWIGLU_LIMIT)
        act = (u + 1.0) * (g * jax.nn.sigmoid(SWIGLU_ALPHA * g))
        y_ref[...] = jnp.dot(act.astype(BF16), wd_ref[0], preferred_element_type=F32) + bd_ref[0]

    @pl.when(i >= nu_ref[0])
    def _():
        y_ref[...] = jnp.zeros_like(y_ref)


def _ffn(xb, blk_e, n_used, w_gu, b_gu, w_down, b_down):
    rows, d = xb.shape
    ne, _, f2 = w_gu.shape
    dff = w_down.shape[1]
    nb = rows // MOE_ROWS
    grid_spec = pltpu.PrefetchScalarGridSpec(
        num_scalar_prefetch=2,
        grid=(nb,),
        in_specs=[pl.BlockSpec((MOE_ROWS, d), lambda i, be, nu: (i, 0)),
                  pl.BlockSpec((1, d, f2), lambda i, be, nu: (be[i], 0, 0)),
                  pl.BlockSpec((1, 1, f2), lambda i, be, nu: (be[i], 0, 0)),
                  pl.BlockSpec((1, dff, d), lambda i, be, nu: (be[i], 0, 0)),
                  pl.BlockSpec((1, 1, d), lambda i, be, nu: (be[i], 0, 0))],
        out_specs=pl.BlockSpec((MOE_ROWS, d), lambda i, be, nu: (i, 0)),
    )
    return pl.pallas_call(
        _ffn_kernel,
        grid_spec=grid_spec,
        out_shape=jax.ShapeDtypeStruct((rows, d), F32),
        compiler_params=_cparams(("arbitrary",)),
        name="ffn",
    )(blk_e, n_used, xb, w_gu.astype(BF16), b_gu.reshape(ne, 1, f2),
      w_down.astype(BF16), b_down.reshape(ne, 1, d))


def _post2_kernel(yg_ref, gw_ref, x1_ref, mod_ref, g_ref, b_ref, y_ref, *, alpha):
    gw = gw_ref[...]
    f = yg_ref[0] * gw[:, :, 0:1]
    for k in range(1, TOP_K):
        f = f + yg_ref[k] * gw[:, :, k:k + 1]
    y = alpha * x1_ref[...] + mod_ref[...][:, 5:6, :] * f
    y_ref[...] = _ln_core(y) * g_ref[...] + b_ref[...]


def _post2(yg, gw, x1, mod, ln_g, ln_b, bb, tt, alpha):
    B, T, d = x1.shape
    full = lambda *s: pl.BlockSpec(s, lambda i, j: (0,) * len(s))
    return pl.pallas_call(
        functools.partial(_post2_kernel, alpha=alpha),
        grid=(B // bb, T // tt),
        in_specs=[pl.BlockSpec((TOP_K, bb, tt, d), lambda i, j: (0, i, j, 0)),
                  pl.BlockSpec((bb, tt, TOP_K), lambda i, j: (i, j, 0)),
                  pl.BlockSpec((bb, tt, d), lambda i, j: (i, j, 0)),
                  pl.BlockSpec((bb, 6, d), lambda i, j: (i, 0, 0)),
                  full(1, d), full(1, d)],
        out_specs=pl.BlockSpec((bb, tt, d), lambda i, j: (i, j, 0)),
        out_shape=jax.ShapeDtypeStruct((B, T, d), F32),
        compiler_params=_cparams(("parallel", "arbitrary")),
        name="post2",
    )(yg, gw, x1, mod, ln_g.reshape(1, d), ln_b.reshape(1, d))


def _route(logits, n_experts):
    top_val, top_idx = lax.top_k(logits, TOP_K)
    gate_w = jax.nn.softmax(top_val, axis=-1)
    T = logits.shape[0]
    n_slots = T * TOP_K
    expert = top_idx.reshape(-1).astype(jnp.int32)
    order = jnp.argsort(expert)
    sorted_e = expert[order]
    counts = jnp.zeros((n_experts,), jnp.int32).at[expert].add(1)
    padded = (counts + MOE_ROWS - 1) // MOE_ROWS * MOE_ROWS
    pad_end = jnp.cumsum(padded)
    pad_start = pad_end - padded
    grp_start = jnp.cumsum(counts) - counts
    dest_sorted = pad_start[sorted_e] + jnp.arange(n_slots, dtype=jnp.int32) - grp_start[sorted_e]
    dest = jnp.zeros((n_slots,), jnp.int32).at[order].set(dest_sorted)
    n_blocks = -(-n_slots // MOE_ROWS) + n_experts
    src = jnp.zeros((n_blocks * MOE_ROWS,), jnp.int32).at[dest].set(
        jnp.arange(n_slots, dtype=jnp.int32) // TOP_K)
    n_used = (pad_end[-1] // MOE_ROWS).astype(jnp.int32)
    blk = jnp.minimum(jnp.arange(n_blocks, dtype=jnp.int32), n_used - 1) * MOE_ROWS
    blk_e = jnp.minimum(jnp.searchsorted(pad_end, blk, side='right'), n_experts - 1).astype(jnp.int32)
    return gate_w, dest, src, blk_e, n_used.reshape(1)


def _masked_softmax(s, valid):
    s = jnp.where(valid, s, -jnp.inf)
    m = jnp.max(s, -1, keepdims=True)
    m = jnp.where(jnp.isfinite(m), m, 0.0)
    e = jnp.exp(s - m)
    return e / jnp.maximum(jnp.sum(e, -1, keepdims=True), jnp.finfo(jnp.float32).tiny)


def _layer_norm(x):
    return _ln_core(x)


def _mlstm(q, k, v, i_pre, f_pre, C0, n0, m0):
    B, T, H, d = q.shape
    L = math.gcd(T, M_CHUNK)
    nc = T // L

    def chunks(a):
        return jnp.moveaxis(a.reshape(B, nc, L, *a.shape[2:]), 1, 0)

    xs = (chunks(q), chunks(k * d ** -0.5), chunks(v), chunks(i_pre), chunks(jax.nn.log_sigmoid(f_pre)))
    causal = jnp.tril(jnp.ones((L, L), bool))[None, :, :, None]

    def step(carry, xc):
        C, n, m = carry
        qc, kc, vc, ic, fc = xc
        b = jnp.cumsum(fc, axis=1)
        D = jnp.where(causal, b[:, :, None] - b[:, None] + ic[:, None], -jnp.inf)
        m_t = jnp.maximum(b + m[:, None], jnp.max(D, axis=2))
        inter = jnp.exp(b + m[:, None] - m_t)
        A = jnp.exp(D - m_t[:, :, None]) * jnp.einsum('bthd,bshd->btsh', qc, kc)
        num = inter[..., None] * jnp.einsum('bhvk,bthk->bthv', C, qc) + jnp.einsum('btsh,bshv->bthv', A, vc)
        den = inter * jnp.einsum('bhk,bthk->bth', n, qc) + jnp.sum(A, axis=2)
        h = num / jnp.maximum(jnp.abs(den), jnp.exp(-m_t))[..., None]
        m_new = m_t[:, -1]
        w_src = jnp.exp(b[:, -1:] - b + ic - m_new[:, None])
        w_old = jnp.exp(b[:, -1] + m - m_new)
        C = w_old[..., None, None] * C + jnp.einsum('bsh,bshv,bshk->bhvk', w_src, vc, kc)
        n = w_old[..., None] * n + jnp.einsum('bsh,bshk->bhk', w_src, kc)
        return (C, n, m_new), h

    (C, n, m), h = lax.scan(step, (C0, n0, m0), xs)
    h = jnp.moveaxis(h, 0, 1).reshape(B, T, H, d)
    return h, C, n, m


def _mlstm_out(h, o_pre, g):
    B, T = h.shape[:2]
    hn = _layer_norm(h) * g.reshape(M_HEADS, M_DIM)
    return hn.reshape(B, T, M_WIDTH) * jax.nn.sigmoid(o_pre)


def _compress(rows, pe, w1, w2):
    B, L = rows.shape[:2]
    n_cmp = (L - CMP_LEN) // CMP_STRIDE + 1
    idx = jnp.arange(n_cmp)[:, None] * CMP_STRIDE + jnp.arange(CMP_LEN)[None, :]
    blk = rows[:, idx] + pe[:, None, :]
    flat = blk.transpose(0, 1, 3, 2, 4).reshape(B, n_cmp, A_KV_HEADS, CMP_LEN * A_DIM)
    return jax.nn.gelu(flat @ w1) @ w2


def _sel_blocks(rows):
    B, L = rows.shape[:2]
    n_sel = -(-L // SEL_LEN)
    rows = jnp.pad(rows, ((0, 0), (0, n_sel * SEL_LEN - L), (0, 0), (0, 0)))
    return rows.reshape(B, n_sel, SEL_LEN, A_KV_HEADS, A_DIM).transpose(0, 3, 1, 2, 4)


def _nsa_branches(cmp_kv, sel_kv, cmp_params):
    pe_k, w1_k, w2_k, pe_v, w1_v, w2_v = cmp_params
    return (_compress(cmp_kv[:, :, 0], pe_k, w1_k, w2_k), _compress(cmp_kv[:, :, 1], pe_v, w1_v, w2_v),
            _sel_blocks(sel_kv[:, :, 0]), _sel_blocks(sel_kv[:, :, 1]))


def _nsa_core(q, gates, q_pos, kcmp, vcmp, ksb, vsb, kw, vw, kw_pos):
    B, Tq = q.shape[:2]
    qg = q.reshape(B, Tq, A_KV_HEADS, A_GROUP, A_DIM)
    scale = A_DIM ** -0.5
    n_cmp, n_sel = kcmp.shape[1], ksb.shape[2]
    s_c = jnp.einsum('btgrd,bngd->bgrtn', qg, kcmp) * scale
    cmp_end = jnp.arange(n_cmp) * CMP_STRIDE + (CMP_LEN - 1)
    p_c = _masked_softmax(s_c, cmp_end[None, :] <= q_pos[:, None])
    o_c = jnp.einsum('bgrtn,bngd->btgrd', p_c, vcmp)
    ci = jnp.arange(n_cmp)[:, None] * CMP_STRIDE
    sj = jnp.arange(n_sel)[None, :] * SEL_LEN
    overlap = ((ci < sj + SEL_LEN) & (ci + CMP_LEN > sj)).astype(F32)
    score = jnp.einsum('bgrtn,nj->bgtj', p_c, overlap)
    j = jnp.arange(n_sel)[None, :]
    cur = (q_pos // SEL_LEN)[:, None]
    forced = (j == 0) | (j == cur) | (j == cur - 1)
    score = jnp.where(forced, FORCE_SCORE, score)
    score = jnp.where(j * SEL_LEN <= q_pos[:, None], score, -1.0)
    _, idx = lax.top_k(score, min(SEL_TOP, n_sel))
    bi = jnp.arange(B)[:, None, None, None]
    gi = jnp.arange(A_KV_HEADS)[None, :, None, None]
    ks, vs = ksb[bi, gi, idx], vsb[bi, gi, idx]
    k_pos = idx[..., None] * SEL_LEN + jnp.arange(SEL_LEN)
    valid = (k_pos <= q_pos[:, None, None])[:, :, None]
    s_s = jnp.einsum('btgrd,bgtnld->bgrtnl', qg, ks) * scale
    shp = s_s.shape
    p_s = _masked_softmax(s_s.reshape(*shp[:4], -1),
                          jnp.broadcast_to(valid, shp).reshape(*shp[:4], -1)).reshape(shp)
    o_s = jnp.einsum('bgrtnl,bgtnld->btgrd', p_s, vs)
    s_w = jnp.einsum('btgrd,bkgd->bgrtk', qg, kw) * scale
    dist = q_pos[:, None] - kw_pos[None, :]
    p_w = _masked_softmax(s_w, (dist >= 0) & (dist < WINDOW) & (kw_pos[None, :] >= 0))
    o_w = jnp.einsum('bgrtk,bkgd->btgrd', p_w, vw)
    g = jax.nn.sigmoid(gates).reshape(B, Tq, A_KV_HEADS, A_GROUP, 3)
    o = g[..., 0:1] * o_c + g[..., 1:2] * o_s + g[..., 2:3] * o_w
    return o.reshape(B, Tq, A_WIDTH)


def _nsa_prompt(q, gates, cmp_kv, sel_kv, win_kv, cmp_params):
    B, S = q.shape[:2]
    kcmp, vcmp, ksb, vsb = _nsa_branches(cmp_kv, sel_kv, cmp_params)
    win_pad = jnp.pad(win_kv, ((0, 0), (WINDOW, 0), (0, 0), (0, 0), (0, 0)))
    nb = S // Q_BLOCK
    qb = jnp.moveaxis(q.reshape(B, nb, Q_BLOCK, A_HEADS, A_DIM), 1, 0)
    gb = jnp.moveaxis(gates.reshape(B, nb, Q_BLOCK, A_HEADS, 3), 1, 0)

    def block(args):
        qi, gi, s0 = args
        w = lax.dynamic_slice_in_dim(win_pad, s0, WINDOW + Q_BLOCK, axis=1)
        return _nsa_core(qi, gi, s0 + jnp.arange(Q_BLOCK), kcmp, vcmp, ksb, vsb,
                         w[:, :, 0], w[:, :, 1], s0 - WINDOW + jnp.arange(WINDOW + Q_BLOCK))

    out = lax.map(block, (qb, gb, jnp.arange(nb) * Q_BLOCK))
    return jnp.moveaxis(out, 0, 1).reshape(B, S, A_WIDTH)


def _gelu_tanh(x):
    return x * (0.5 * (1.0 + jnp.tanh(math.sqrt(2.0 / math.pi) * (x + 0.044715 * (x * x * x)))))


def _compress_kernel(x_ref, pe_ref, w1_ref, w2_ref, o_ref, *, n_cmp):
    nh = x_ref.shape[1]
    w = 2 * KV_WIDTH
    half = CMP_LEN // 2
    pa = jnp.zeros((nh, w), F32)
    pb = jnp.zeros((nh, w), F32)
    for j in range(half):
        xj = x_ref[0, :, j * w:(j + 1) * w]
        pa = pa + jnp.dot((xj + pe_ref[j:j + 1, :]).astype(BF16), w1_ref[j], preferred_element_type=F32)
        pb = pb + jnp.dot((xj + pe_ref[half + j:half + j + 1, :]).astype(BF16), w1_ref[half + j],
                          preferred_element_type=F32)
    hid = pa + pltpu.roll(pb, nh - 1, 0)
    y = jnp.dot(_gelu_tanh(hid).astype(BF16), w2_ref[...], preferred_element_type=F32)
    rows = lax.broadcasted_iota(jnp.int32, (nh, w), 0)
    o_ref[0] = jnp.where(rows < n_cmp, y, 0.0)


def _block_diag4(a, b, c, d):
    z = jnp.zeros_like(a)
    return jnp.concatenate([jnp.concatenate([a, z, z, z], -1), jnp.concatenate([z, b, z, z], -1),
                            jnp.concatenate([z, z, c, z], -1), jnp.concatenate([z, z, z, d], -1)], -2)


def _compress_weights(cmp_params):
    pe_k, w1_k, w2_k, pe_v, w1_v, w2_v = cmp_params
    pe4 = jnp.concatenate([pe_k, pe_k, pe_v, pe_v], axis=1)
    w1k = w1_k.reshape(CMP_LEN, A_DIM, A_DIM)
    w1v = w1_v.reshape(CMP_LEN, A_DIM, A_DIM)
    w1 = _block_diag4(w1k, w1k, w1v, w1v).astype(BF16)
    w2 = _block_diag4(w2_k, w2_k, w2_v, w2_v).astype(BF16)
    return pe4, w1, w2


def _compress_prompt(cmp_rows, cmp_w):
    B, L, w = cmp_rows.shape
    nh = L // CMP_STRIDE
    n_cmp = (L - CMP_LEN) // CMP_STRIDE + 1
    pe4, w1, w2 = cmp_w
    full = lambda *s: pl.BlockSpec(s, lambda b: (0,) * len(s))
    return pl.pallas_call(
        functools.partial(_compress_kernel, n_cmp=n_cmp),
        grid=(B,),
        in_specs=[pl.BlockSpec((1, nh, CMP_STRIDE * w), lambda b: (b, 0, 0)),
                  full(CMP_LEN, w), full(CMP_LEN, w, w), full(w, w)],
        out_specs=pl.BlockSpec((1, nh, w), lambda b: (b, 0, 0)),
        out_shape=jax.ShapeDtypeStruct((B, nh, w), F32),
        compiler_params=_cparams(("parallel",)),
        name="compress_prompt",
    )(cmp_rows.reshape(B, nh, CMP_STRIDE * w), pe4, w1, w2)


_NEG = -1e30
_NEG_SEL = -1e9
_SEL_CHUNK = 256


def _dot_nt(a, b):
    return lax.dot_general(a, b, (((1,), (1,)), ((), ())), preferred_element_type=F32)


def _softmax_rows(s, valid):
    s = jnp.where(valid[None], s, _NEG)
    m = jnp.max(s, axis=-1, keepdims=True)
    e = jnp.where(valid[None], jnp.exp(s - m), 0.0)
    return e / jnp.maximum(jnp.sum(e, axis=-1, keepdims=True), jnp.finfo(jnp.float32).tiny)


def _nsa_prompt_kernel(q_ref, gt_ref, kc_ref, vc_ref, sel_ref, win_ref, o_ref, *, seq):
    tq = Q_BLOCK
    nh = A_HEADS
    n_sel = seq // SEL_LEN
    s0 = pl.program_id(1) * tq
    lane = lax.broadcasted_iota(jnp.int32, (tq, LANES), 1)

    q = q_ref[0] * (A_DIM ** -0.5)
    qz = []
    for hd in range(nh):
        g = hd // A_GROUP
        tile = q[:, LANES * (hd // 2):LANES * (hd // 2 + 1)]
        if hd % 2 != g:
            tile = pltpu.roll(tile, A_DIM, 1)
        keep = (lane < A_DIM) if g == 0 else (lane >= A_DIM)
        qz.append(jnp.where(keep, tile, 0.0))
    qz = jnp.concatenate(qz, axis=0).astype(BF16)

    kc = kc_ref[0].astype(BF16)
    vc = vc_ref[0].astype(BF16)
    ncp = kc.shape[0]
    s_c = _dot_nt(qz, kc).reshape(nh, tq, ncp)
    tpos = s0 + lax.broadcasted_iota(jnp.int32, (tq, ncp), 0)
    nidx = lax.broadcasted_iota(jnp.int32, (tq, ncp), 1)
    p_c = _softmax_rows(s_c, nidx * CMP_STRIDE + (CMP_LEN - 1) <= tpos)
    o_c = jnp.dot(p_c.reshape(nh * tq, ncp).astype(BF16), vc, preferred_element_type=F32)

    jn = lax.broadcasted_iota(jnp.int32, (n_sel, ncp), 0) * SEL_LEN
    cn = lax.broadcasted_iota(jnp.int32, (n_sel, ncp), 1) * CMP_STRIDE
    ov_t = jnp.where((cn < jn + SEL_LEN) & (cn + CMP_LEN > jn), 1.0, 0.0).astype(BF16)
    jb = lax.broadcasted_iota(jnp.int32, (n_sel, tq), 0)
    tp = s0 + lax.broadcasted_iota(jnp.int32, (n_sel, tq), 1)
    cur = tp // SEL_LEN
    forced = (jb == 0) | (jb == cur) | (jb == cur - 1)
    bias = []
    for g in range(A_KV_HEADS):
        ps = p_c[g * A_GROUP]
        for r in range(1, A_GROUP):
            ps = ps + p_c[g * A_GROUP + r]
        hi = ps.astype(BF16)
        lo = (ps - hi.astype(F32)).astype(BF16)
        score = _dot_nt(ov_t, hi) + _dot_nt(ov_t, lo)
        score = jnp.where(forced, FORCE_SCORE, score)
        score = jnp.where(jb * SEL_LEN <= tp, score, -1.0)
        cnt = jnp.zeros((n_sel, tq), jnp.int32)
        for k in range(n_sel):
            rk = score[k:k + 1, :]
            ahead = (rk > score) | ((rk == score) & (jb > k))
            cnt = cnt + jnp.where(ahead, 1, 0)
        bias_t = jnp.where(cnt < min(SEL_TOP, n_sel), 0.0, _NEG_SEL)
        if n_sel < LANES:
            bias_t = jnp.concatenate([bias_t, jnp.zeros((LANES - n_sel, tq), F32)], axis=0)
        bias.append(bias_t.T.astype(BF16))
    q_aug = jnp.concatenate(
        [qz, jnp.concatenate([bias[hd // A_GROUP] for hd in range(nh)], axis=0)], axis=1)

    kc_n = _SEL_CHUNK

    def chunk(c, carry, causal):
        m, l, acc = carry
        k0 = pl.multiple_of(c * kc_n, kc_n)
        kv = sel_ref[0, pl.ds(k0, kc_n), :]
        kblk = (k0 + lax.broadcasted_iota(jnp.int32, (kc_n, LANES), 0)) // SEL_LEN
        onehot = jnp.where(kblk == lax.broadcasted_iota(jnp.int32, (kc_n, LANES), 1), 1.0, 0.0)
        k_aug = jnp.concatenate([kv[:, :KV_WIDTH], onehot], axis=1).astype(BF16)
        s = _dot_nt(q_aug, k_aug)
        if causal:
            kpos = k0 + lax.broadcasted_iota(jnp.int32, (tq, kc_n), 1)
            qpos = s0 + lax.broadcasted_iota(jnp.int32, (tq, kc_n), 0)
            s = jnp.where((kpos <= qpos)[None], s.reshape(nh, tq, kc_n), _NEG_SEL).reshape(nh * tq, kc_n)
        m_new = jnp.maximum(m, jnp.max(s, axis=-1, keepdims=True))
        a = jnp.exp(m - m_new)
        p = jnp.exp(s - m_new)
        l = a * l + jnp.sum(p, axis=-1, keepdims=True)
        acc = a * acc + jnp.dot(p.astype(BF16), kv[:, KV_WIDTH:].astype(BF16), preferred_element_type=F32)
        return m_new, l, acc

    n_ch = (s0 + tq - 1) // kc_n + 1
    init = (jnp.full((nh * tq, 1), _NEG, F32), jnp.zeros((nh * tq, 1), F32), jnp.zeros((nh * tq, LANES), F32))
    carry = lax.fori_loop(0, n_ch - 1, lambda c, cr: chunk(c, cr, False), init)
    _, l_s, acc_s = chunk(n_ch - 1, carry, True)
    o_s = acc_s / l_s

    wk = WINDOW + tq
    w0 = pl.multiple_of(jnp.clip(s0 - WINDOW, 0, seq - wk), tq)
    kvw = win_ref[0, pl.ds(w0, wk), :]
    s_w = _dot_nt(qz, kvw[:, :KV_WIDTH].astype(BF16)).reshape(nh, tq, wk)
    dist = (s0 + lax.broadcasted_iota(jnp.int32, (tq, wk), 0)) - (w0 + lax.broadcasted_iota(jnp.int32, (tq, wk), 1))
    p_w = _softmax_rows(s_w, (dist >= 0) & (dist < WINDOW))
    o_w = jnp.dot(p_w.reshape(nh * tq, wk).astype(BF16), kvw[:, KV_WIDTH:].astype(BF16),
                  preferred_element_type=F32)

    gs = jax.nn.sigmoid(gt_ref[0])
    tiles = []
    for pair in range(nh // 2):
        g = (2 * pair) // A_GROUP
        mixed = []
        for hd in (2 * pair, 2 * pair + 1):
            c = 2 * M_HEADS + 3 * hd
            r = slice(hd * tq, (hd + 1) * tq)
            mixed.append(gs[:, c:c + 1] * o_c[r] + gs[:, c + 1:c + 2] * o_s[r] + gs[:, c + 2:c + 3] * o_w[r])
        a, b = mixed
        if g == 0:
            tiles.append(jnp.where(lane < A_DIM, a, pltpu.roll(b, A_DIM, 1)))
        else:
            tiles.append(jnp.where(lane < A_DIM, pltpu.roll(a, A_DIM, 1), b))
    o_ref[0] = jnp.concatenate(tiles, axis=1)


def _nsa_prompt_call(qa, gt, comp, sel_kv, win_kv):
    B, S, _ = qa.shape
    assert S % _SEL_CHUNK == 0 and S >= WINDOW + Q_BLOCK and S // SEL_LEN <= LANES
    ncp = comp.shape[1]
    per_b = lambda r, w: pl.BlockSpec((1, r, w), lambda b, i: (b, 0, 0))
    return pl.pallas_call(
        functools.partial(_nsa_prompt_kernel, seq=S),
        grid=(B, S // Q_BLOCK),
        in_specs=[pl.BlockSpec((1, Q_BLOCK, A_WIDTH), lambda b, i: (b, i, 0)),
                  pl.BlockSpec((1, Q_BLOCK, LANES), lambda b, i: (b, i, 0)),
                  pl.BlockSpec((1, ncp, KV_WIDTH), lambda b, i: (b, 0, 0)),
                  pl.BlockSpec((1, ncp, KV_WIDTH), lambda b, i: (b, 0, 1)),
                  per_b(S, 2 * KV_WIDTH), per_b(S, 2 * KV_WIDTH)],
        out_specs=pl.BlockSpec((1, Q_BLOCK, A_WIDTH), lambda b, i: (b, i, 0)),
        out_shape=jax.ShapeDtypeStruct((B, S, A_WIDTH), F32),
        compiler_params=_cparams(("parallel", "arbitrary")),
        name="nsa_prompt",
    )(qa, gt, comp, comp, sel_kv, win_kv)


def _split_pre(outs, B, T):
    mq, gt, qa, cmp_kv, sel_kv, win_kv = outs
    mq = mq.reshape(B, T, 4, M_HEADS, M_DIM)
    kv = lambda a: a.reshape(B, T, 2, A_KV_HEADS, A_DIM)
    gt = gt.reshape(B, T, LANES)
    return (mq[:, :, 0], mq[:, :, 1], mq[:, :, 2], mq[:, :, 3].reshape(B, T, M_WIDTH),
            gt[..., :M_HEADS], gt[..., M_HEADS:2 * M_HEADS],
            qa.reshape(B, T, A_HEADS, A_DIM), kv(cmp_kv), kv(sel_kv), kv(win_kv),
            gt[..., 2 * M_HEADS:2 * M_HEADS + 3 * A_HEADS].reshape(B, T, A_HEADS, 3))


def _moe_and_out(h2, logits, x1, mod, moe_w, ln_g, ln_b, bb, tt, alpha, n_experts):
    w_gu, b_gu, w_down, b_down = moe_w
    B, T, d = x1.shape
    gate_w, dest, src, blk_e, n_used = _route(logits, n_experts)
    xb = h2[src].astype(BF16)
    yb = _ffn(xb, blk_e, n_used, w_gu, b_gu, w_down, b_down)
    yg = yb[dest.reshape(B * T, TOP_K).T].reshape(TOP_K, B, T, d)
    return _post2(yg, gate_w.reshape(B, T, TOP_K), x1, mod, ln_g, ln_b, bb, tt, alpha)


def kernel(x_prompt, x_sample, cache_cmp, cache_sel, state_win, state_C, state_n, state_m, page_table,
           c_prompt, c_sample, w_ada, b_ada, w_in, b_in, m_norm_g, cmp_pe_k, cmp_w1_k, cmp_w2_k,
           cmp_pe_v, cmp_w1_v, cmp_w2_v, w_out, ln1_g, ln1_b, w_router, b_router, w_gu, b_gu,
           w_down, b_down, ln2_g, ln2_b):
    B, S, d = x_prompt.shape
    DB, T, _ = x_sample.shape
    depth = w_ada.shape[0]
    n_experts = w_router.shape[-1]
    alpha = (2 * depth) ** 0.25
    past_len = page_table.shape[1] * cache_cmp.shape[2]
    wbuf = state_win.shape[2]
    pos_p = jnp.arange(S)
    pos_s = past_len + jnp.arange(T)
    tt_p = min(S, 256)
    bb_s = min(DB, max(1, 256 // T))
    y_prompt, y_sample = x_prompt, x_sample
    outs = [[] for _ in range(12)]
    for l in range(depth):
        cmp_l = (cmp_pe_k[l], cmp_w1_k[l], cmp_w2_k[l], cmp_pe_v[l], cmp_w1_v[l], cmp_w2_v[l])
        moe_w = (w_gu[l], b_gu[l], w_down[l], b_down[l])
        w_r, b_r = _prep_w_in(w_in[l], b_in[l])
        cmp_w = _compress_weights(cmp_l)
        c_all = jnp.concatenate([c_prompt, c_sample], axis=0)
        n_c = c_all.shape[0]
        c_all = jnp.pad(c_all, ((0, -n_c % 8), (0, 0)))
        mod_all = _ada(c_all, w_ada[l], b_ada[l]).reshape(-1, 6, d)
        mod_p, mod_s = mod_all[:B], mod_all[B:B + DB]

        pre = _pre(y_prompt, mod_p, pos_p, w_r, b_r, 1, tt_p)
        qm, km, vm, om, im, fm, qa, cmp_kv, sel_kv, win_kv, ga = _split_pre(pre, B, S)
        zc = jnp.zeros((B, M_HEADS, M_DIM, M_DIM), F32)
        hm, C_p, n_p, m_p = _mlstm(qm, km, vm, im, fm, zc, zc[..., 0], zc[..., 0, 0])
        r3 = lambda a: a.reshape(B, S, a.shape[-1])
        comp = _compress_prompt(r3(pre[3]), cmp_w)
        ma = _nsa_prompt_call(r3(pre[2]), r3(pre[1]), comp, r3(pre[4]), r3(pre[5]))
        mo = _mlstm_out(hm, om, m_norm_g[l])
        x1, h2, lg = _post1(mo.reshape(B * S, M_WIDTH), ma.reshape(B * S, A_WIDTH), y_prompt, mod_p,
                            w_out[l], ln1_g[l], ln1_b[l], w_router[l], b_router[l], 1, tt_p, alpha)
        y_prompt = _moe_and_out(h2, lg, x1, mod_p, moe_w, ln2_g[l], ln2_b[l], 1, tt_p, alpha, n_experts)
        for lst, v in zip(outs[:6], (cmp_kv, sel_kv, win_kv[:, -min(WINDOW, S):], C_p, n_p, m_p)):
            lst.append(v)

        pre = _pre(y_sample, mod_s, pos_s, w_r, b_r, bb_s, T)
        qm, km, vm, om, im, fm, qa, cmp_kv, sel_kv, win_kv, ga = _split_pre(pre, DB, T)
        hm, C_s, n_s, m_s = _mlstm(qm, km, vm, im, fm, state_C[l], state_n[l], state_m[l])
        gather = lambda pool: pool[page_table].reshape(DB, past_len, *pool.shape[2:])
        full_cmp = jnp.concatenate([gather(cache_cmp[l]), cmp_kv], axis=1)
        full_sel = jnp.concatenate([gather(cache_sel[l]), sel_kv], axis=1)
        kcmp, vcmp, ksb, vsb = _nsa_branches(full_cmp, full_sel, cmp_l)
        win_all = jnp.concatenate([state_win[l], win_kv], axis=1)
        ma = _nsa_core(qa, ga, pos_s, kcmp, vcmp, ksb, vsb, win_all[:, :, 0], win_all[:, :, 1],
                       past_len - wbuf + jnp.arange(wbuf + T))
        mo = _mlstm_out(hm, om, m_norm_g[l])
        x1, h2, lg = _post1(mo.reshape(DB * T, M_WIDTH), ma.reshape(DB * T, A_WIDTH), y_sample, mod_s,
                            w_out[l], ln1_g[l], ln1_b[l], w_router[l], b_router[l], bb_s, T, alpha)
        y_sample = _moe_and_out(h2, lg, x1, mod_s, moe_w, ln2_g[l], ln2_b[l], bb_s, T, alpha, n_experts)
        for lst, v in zip(outs[6:], (cmp_kv, sel_kv, win_all[:, -wbuf:], C_s, n_s, m_s)):
            lst.append(v)
    st = [jnp.stack(v) for v in outs]
    return (y_prompt, y_sample, *st)
```

```python
import functools
import math

import numpy as np
import jax
import jax.numpy as jnp
from jax import lax
from jax.experimental import pallas as pl
from jax.experimental.pallas import tpu as pltpu

F32 = jnp.float32
BF16 = jnp.bfloat16

M_HEADS = 4
M_DIM = 128
M_WIDTH = M_HEADS * M_DIM
M_CHUNK = 64
A_HEADS = 8
A_KV_HEADS = 2
A_GROUP = A_HEADS // A_KV_HEADS
A_DIM = 64
A_WIDTH = A_HEADS * A_DIM
KV_WIDTH = A_KV_HEADS * A_DIM
CMP_LEN = 32
CMP_STRIDE = 16
SEL_LEN = 64
SEL_TOP = 16
WINDOW = 512
Q_BLOCK = 128
FORCE_SCORE = 1.0e4
ROPE_THETA = 500000.0
ROPE_DIM = A_DIM // 4
TOP_K = 4
SWIGLU_LIMIT = 7.0
SWIGLU_ALPHA = 1.702
LN_EPS = 1e-5
LANES = 128
MOE_ROWS = 256
VMEM_LIMIT = 56 * 1024 * 1024

_O_MQ = 0
_O_IF = 4 * M_WIDTH
_O_QA = _O_IF + 2 * M_HEADS
_O_KV = _O_QA + A_WIDTH
_O_GA = _O_KV + 6 * KV_WIDTH
_N_IN = _O_GA + 3 * A_HEADS
_R_QA = 4 * M_WIDTH
_R_KV = _R_QA + A_WIDTH
_R_GT = _R_KV + 6 * KV_WIDTH
_R_END = _R_GT + LANES


def _cparams(sem):
    return pltpu.CompilerParams(dimension_semantics=sem, vmem_limit_bytes=VMEM_LIMIT)


def _ln_core(x):
    mu = jnp.mean(x, axis=-1, keepdims=True)
    xc = x - mu
    var = jnp.mean(xc * xc, axis=-1, keepdims=True)
    return xc * lax.rsqrt(var + LN_EPS)


def _ada_kernel(c_ref, w_ref, b_ref, o_ref):
    o_ref[...] = jnp.dot(c_ref[...].astype(BF16), w_ref[...].astype(BF16),
                         preferred_element_type=F32) + b_ref[...]


def _ada(c, w_ada, b_ada):
    n, d = c.shape
    cols = w_ada.shape[1]
    return pl.pallas_call(
        _ada_kernel,
        grid=(cols // d,),
        in_specs=[pl.BlockSpec((n, d), lambda j: (0, 0)),
                  pl.BlockSpec((d, d), lambda j: (0, j)),
                  pl.BlockSpec((1, d), lambda j: (0, j))],
        out_specs=pl.BlockSpec((n, d), lambda j: (0, j)),
        out_shape=jax.ShapeDtypeStruct((n, cols), F32),
        compiler_params=_cparams(("arbitrary",)),
        name="ada",
    )(c, w_ada, b_ada.reshape(1, cols))


def _rope_apply(v, cos, sa, sb):
    reps = v.shape[1] // LANES
    tile = lambda t: t if reps == 1 else jnp.concatenate([t] * reps, axis=1)
    w = v.shape[1]
    return (v * tile(cos) + pltpu.roll(v, w - ROPE_DIM // 2, 1) * tile(sa)
            + pltpu.roll(v, ROPE_DIM // 2, 1) * tile(sb))


def _pre_kernel(x_ref, mod_ref, cos_ref, sa_ref, sb_ref, w_ref, b_ref,
                mq_ref, gt_ref, qa_ref, cmp_ref, sel_ref, win_ref):
    bb, tt, d = x_ref.shape
    mod = mod_ref[...]
    h = _ln_core(x_ref[...]) * (1.0 + mod[:, 1:2, :]) + mod[:, 0:1, :]
    h = h.reshape(bb * tt, d).astype(BF16)
    z = jnp.dot(h, w_ref[...], preferred_element_type=F32) + b_ref[...]
    cos, sa, sb = cos_ref[...], sa_ref[...], sb_ref[...]
    mq_ref[...] = z[:, :_R_QA]
    gt_ref[...] = z[:, _R_GT:_R_END]
    qa_ref[...] = _rope_apply(z[:, _R_QA:_R_KV], cos, sa, sb)
    for n, ref in enumerate((cmp_ref, sel_ref, win_ref)):
        o = _R_KV + 2 * KV_WIDTH * n
        ref[:, :KV_WIDTH] = _rope_apply(z[:, o:o + KV_WIDTH], cos, sa, sb)
        ref[:, KV_WIDTH:] = z[:, o + KV_WIDTH:o + 2 * KV_WIDTH]


def _rope_tables(pos):
    half = ROPE_DIM // 2
    inv = ROPE_THETA ** (-jnp.arange(0, ROPE_DIM, 2, dtype=F32) / ROPE_DIM)
    ang = pos.astype(F32)[:, None] * inv[None, :]
    cos, sin = jnp.cos(ang), jnp.sin(ang)
    n = pos.shape[0]
    one = jnp.ones((n, A_DIM - ROPE_DIM), F32)
    zero = jnp.zeros((n, A_DIM - ROPE_DIM), F32)
    zh = jnp.zeros((n, half), F32)
    cos_t = jnp.concatenate([cos, cos, one], axis=1)
    sa_t = jnp.concatenate([-sin, zh, zero], axis=1)
    sb_t = jnp.concatenate([zh, sin, zero], axis=1)
    two = lambda t: jnp.concatenate([t, t], axis=1)
    return two(cos_t), two(sa_t), two(sb_t)


def _prep_w_in(w_in, b_in):
    pad = LANES - 2 * M_HEADS - 3 * A_HEADS
    cat = lambda a: jnp.concatenate(
        [a[..., _O_MQ:_O_IF], a[..., _O_QA:_O_GA], a[..., _O_IF:_O_QA], a[..., _O_GA:_N_IN],
         jnp.zeros(a.shape[:-1] + (pad,), a.dtype)], axis=-1)
    return cat(w_in).astype(BF16), cat(b_in[None, :])


def _pre(x, mod, pos, w_r, b_r, bb, tt):
    B, T, d = x.shape
    nt = T // tt
    rows = bb * tt
    cos, sa, sb = _rope_tables(pos)
    if bb > 1:
        cos, sa, sb = (jnp.tile(t, (bb, 1)) for t in (cos, sa, sb))
    n_tok = B * T
    tab = pl.BlockSpec((rows, LANES), lambda i, j: (j, 0))
    row = lambda w: pl.BlockSpec((rows, w), lambda i, j: (i * nt + j, 0))
    widths = (_R_QA, LANES, A_WIDTH, 2 * KV_WIDTH, 2 * KV_WIDTH, 2 * KV_WIDTH)
    return pl.pallas_call(
        _pre_kernel,
        grid=(B // bb, nt),
        in_specs=[pl.BlockSpec((bb, tt, d), lambda i, j: (i, j, 0)),
                  pl.BlockSpec((bb, 6, d), lambda i, j: (i, 0, 0)),
                  tab, tab, tab,
                  pl.BlockSpec((d, _R_END), lambda i, j: (0, 0)),
                  pl.BlockSpec((1, _R_END), lambda i, j: (0, 0))],
        out_specs=[row(w) for w in widths],
        out_shape=[jax.ShapeDtypeStruct((n_tok, w), F32) for w in widths],
        compiler_params=_cparams(("parallel", "arbitrary")),
        name="pre",
    )(x, mod, cos, sa, sb, w_r, b_r)


def _post1_kernel(mo_ref, ao_ref, x_ref, mod_ref, wo_ref, g_ref, b_ref, wr_ref, br_ref,
                  x1_ref, h2_ref, lg_ref, *, alpha):
    bb, tt, d = x_ref.shape
    mod = mod_ref[...]
    mixin = jnp.concatenate([mo_ref[...], ao_ref[...]], axis=1).astype(BF16)
    mix = jnp.dot(mixin, wo_ref[...], preferred_element_type=F32).reshape(bb, tt, d)
    x1 = _ln_core(alpha * x_ref[...] + mod[:, 2:3, :] * mix) * g_ref[...] + b_ref[...]
    h2 = _ln_core(x1) * (1.0 + mod[:, 4:5, :]) + mod[:, 3:4, :]
    x1_ref[...] = x1
    h2f = h2.reshape(bb * tt, d)
    h2_ref[...] = h2f
    lg_ref[...] = jnp.dot(h2f, wr_ref[...], preferred_element_type=F32,
                          precision=lax.Precision.HIGHEST) + br_ref[...]


def _post1(mo, ao, x, mod, w_out, ln_g, ln_b, w_router, b_router, bb, tt, alpha):
    B, T, d = x.shape
    nt = T // tt
    rows = bb * tt
    ne = w_router.shape[1]
    wr = jnp.pad(w_router, ((0, 0), (0, LANES - ne)))
    br = jnp.pad(b_router, (0, LANES - ne), constant_values=-jnp.inf).reshape(1, LANES)
    full = lambda *s: pl.BlockSpec(s, lambda i, j: (0,) * len(s))
    row = lambda w: pl.BlockSpec((rows, w), lambda i, j: (i * nt + j, 0))
    return pl.pallas_call(
        functools.partial(_post1_kernel, alpha=alpha),
        grid=(B // bb, nt),
        in_specs=[row(M_WIDTH), row(A_WIDTH),
                  pl.BlockSpec((bb, tt, d), lambda i, j: (i, j, 0)),
                  pl.BlockSpec((bb, 6, d), lambda i, j: (i, 0, 0)),
                  full(M_WIDTH + A_WIDTH, d), full(1, d), full(1, d), full(d, LANES), full(1, LANES)],
        out_specs=[pl.BlockSpec((bb, tt, d), lambda i, j: (i, j, 0)), row(d), row(LANES)],
        out_shape=[jax.ShapeDtypeStruct((B, T, d), F32),
                   jax.ShapeDtypeStruct((B * T, d), F32),
                   jax.ShapeDtypeStruct((B * T, LANES), F32)],
        compiler_params=_cparams(("parallel", "arbitrary")),
        name="post1",
    )(mo, ao, x, mod, w_out.astype(BF16), ln_g.reshape(1, d), ln_b.reshape(1, d), wr, br)


def _ffn_kernel(be_ref, nu_ref, x_ref, wgu_ref, bgu_ref, wd_ref, bd_ref, y_ref):
    i = pl.program_id(0)
    dff = wd_ref.shape[1]

    @pl.when(i < nu_ref[0])
    def _():
        gu = jnp.dot(x_ref[...], wgu_ref[0], preferred_element_type=F32) + bgu_ref[0]
        g = jnp.minimum(gu[:, :dff], SWIGLU_LIMIT)
        u = jnp.clip(gu[:, dff:], -SWIGLU_LIMIT, SWIGLU_LIMIT)
        act = (u + 1.0) * (g * jax.nn.sigmoid(SWIGLU_ALPHA * g))
        y_ref[...] = jnp.dot(act.astype(BF16), wd_ref[0], preferred_element_type=F32) + bd_ref[0]

    @pl.when(i >= nu_ref[0])
    def _():
        y_ref[...] = jnp.zeros_like(y_ref)


def _ffn(xb, blk_e, n_used, w_gu, b_gu, w_down, b_down):
    rows, d = xb.shape
    ne, _, f2 = w_gu.shape
    dff = w_down.shape[1]
    nb = rows // MOE_ROWS
    grid_spec = pltpu.PrefetchScalarGridSpec(
        num_scalar_prefetch=2,
        grid=(nb,),
        in_specs=[pl.BlockSpec((MOE_ROWS, d), lambda i, be, nu: (i, 0)),
                  pl.BlockSpec((1, d, f2), lambda i, be, nu: (be[i], 0, 0)),
                  pl.BlockSpec((1, 1, f2), lambda i, be, nu: (be[i], 0, 0)),
                  pl.BlockSpec((1, dff, d), lambda i, be, nu: (be[i], 0, 0)),
                  pl.BlockSpec((1, 1, d), lambda i, be, nu: (be[i], 0, 0))],
        out_specs=pl.BlockSpec((MOE_ROWS, d), lambda i, be, nu: (i, 0)),
    )
    return pl.pallas_call(
        _ffn_kernel,
        grid_spec=grid_spec,
        out_shape=jax.ShapeDtypeStruct((rows, d), F32),
        compiler_params=_cparams(("arbitrary",)),
        name="ffn",
    )(blk_e, n_used, xb, w_gu.astype(BF16), b_gu.reshape(ne, 1, f2),
      w_down.astype(BF16), b_down.reshape(ne, 1, d))


def _post2_kernel(yg_ref, gw_ref, x1_ref, mod_ref, g_ref, b_ref, y_ref, *, alpha):
    gw = gw_ref[...]
    f = yg_ref[0] * gw[:, :, 0:1]
    for k in range(1, TOP_K):
        f = f + yg_ref[k] * gw[:, :, k:k + 1]
    y = alpha * x1_ref[...] + mod_ref[...][:, 5:6, :] * f
    y_ref[...] = _ln_core(y) * g_ref[...] + b_ref[...]


def _post2(yg, gw, x1, mod, ln_g, ln_b, bb, tt, alpha):
    B, T, d = x1.shape
    full = lambda *s: pl.BlockSpec(s, lambda i, j: (0,) * len(s))
    return pl.pallas_call(
        functools.partial(_post2_kernel, alpha=alpha),
        grid=(B // bb, T // tt),
        in_specs=[pl.BlockSpec((TOP_K, bb, tt, d), lambda i, j: (0, i, j, 0)),
                  pl.BlockSpec((bb, tt, TOP_K), lambda i, j: (i, j, 0)),
                  pl.BlockSpec((bb, tt, d), lambda i, j: (i, j, 0)),
                  pl.BlockSpec((bb, 6, d), lambda i, j: (i, 0, 0)),
                  full(1, d), full(1, d)],
        out_specs=pl.BlockSpec((bb, tt, d), lambda i, j: (i, j, 0)),
        out_shape=jax.ShapeDtypeStruct((B, T, d), F32),
        compiler_params=_cparams(("parallel", "arbitrary")),
        name="post2",
    )(yg, gw, x1, mod, ln_g.reshape(1, d), ln_b.reshape(1, d))


def _route(logits, n_experts):
    top_val, top_idx = lax.top_k(logits, TOP_K)
    gate_w = jax.nn.softmax(top_val, axis=-1)
    T = logits.shape[0]
    n_slots = T * TOP_K
    expert = top_idx.reshape(-1).astype(jnp.int32)
    order = jnp.argsort(expert)
    sorted_e = expert[order]
    counts = jnp.zeros((n_experts,), jnp.int32).at[expert].add(1)
    padded = (counts + MOE_ROWS - 1) // MOE_ROWS * MOE_ROWS
    pad_end = jnp.cumsum(padded)
    pad_start = pad_end - padded
    grp_start = jnp.cumsum(counts) - counts
    dest_sorted = pad_start[sorted_e] + jnp.arange(n_slots, dtype=jnp.int32) - grp_start[sorted_e]
    dest = jnp.zeros((n_slots,), jnp.int32).at[order].set(dest_sorted)
    n_blocks = -(-n_slots // MOE_ROWS) + n_experts
    src = jnp.zeros((n_blocks * MOE_ROWS,), jnp.int32).at[dest].set(
        jnp.arange(n_slots, dtype=jnp.int32) // TOP_K)
    n_used = (pad_end[-1] // MOE_ROWS).astype(jnp.int32)
    blk = jnp.minimum(jnp.arange(n_blocks, dtype=jnp.int32), n_used - 1) * MOE_ROWS
    blk_e = jnp.minimum(jnp.searchsorted(pad_end, blk, side='right'), n_experts - 1).astype(jnp.int32)
    return gate_w, dest, src, blk_e, n_used.reshape(1)


def _log_sigmoid(x):
    return jnp.minimum(x, 0.0) - jnp.log(1.0 + jnp.exp(-jnp.abs(x)))


def _mlstm_kernel(mq_ref, gt_ref, gtt_ref, c0_ref, n0_ref, m0_ref, g_ref, mo_ref, c_ref, n_ref, m_ref):
    bb, L, _ = mq_ref.shape
    d = M_DIM

    @pl.when(pl.program_id(1) == 0)
    def _():
        c_ref[...] = c0_ref[...]
        n_ref[...] = n0_ref[...]
        m_ref[...] = m0_ref[...]

    tt = lax.broadcasted_iota(jnp.int32, (L, L), 0)
    ss = lax.broadcasted_iota(jnp.int32, (L, L), 1)
    causal = ss <= tt
    for s in range(bb):
        gcol = gt_ref[s]
        grow = gtt_ref[s, 0]
        for h in range(M_HEADS):
            q = mq_ref[s, :, h * d:(h + 1) * d]
            k = mq_ref[s, :, M_WIDTH + h * d:M_WIDTH + (h + 1) * d] * (d ** -0.5)
            v = mq_ref[s, :, 2 * M_WIDTH + h * d:2 * M_WIDTH + (h + 1) * d]
            o_pre = mq_ref[s, :, 3 * M_WIDTH + h * d:3 * M_WIDTH + (h + 1) * d]
            i_c = gcol[:, h:h + 1]
            lf_c = _log_sigmoid(gcol[:, M_HEADS + h:M_HEADS + h + 1])
            i_r = grow[h:h + 1, :]
            lf_r = _log_sigmoid(grow[M_HEADS + h:M_HEADS + h + 1, :])
            b_c = jnp.sum(jnp.where(causal, lf_r, 0.0), axis=1, keepdims=True)
            b_r = jnp.sum(jnp.where(causal, 0.0, lf_c), axis=0, keepdims=True) + lf_r
            c_old = c_ref[s, h]
            n_old = n_ref[s, h:h + 1, :]
            m_old = m_ref[s, :, h:h + 1]
            dmat = jnp.where(causal, b_c - b_r + i_r, _NEG)
            m_t = jnp.maximum(b_c + m_old, jnp.max(dmat, axis=1, keepdims=True))
            inter = jnp.exp(b_c + m_old - m_t)
            qb, kb, vb = q.astype(BF16), k.astype(BF16), v.astype(BF16)
            a = jnp.exp(dmat - m_t) * _dot_nt(qb, kb)
            num = inter * _dot_nt(qb, c_old.astype(BF16)) + jnp.dot(a.astype(BF16), vb, preferred_element_type=F32)
            den = inter * jnp.sum(q * n_old, axis=1, keepdims=True) + jnp.sum(a, axis=1, keepdims=True)
            hv = num / jnp.maximum(jnp.abs(den), jnp.exp(-m_t))
            m_new = m_t[L - 1:L, :]
            b_last = b_c[L - 1:L, :]
            w_src = jnp.exp(b_last - b_c + i_c - m_new)
            w_old = jnp.exp(b_last + m_old - m_new)
            c_ref[s, h] = w_old * c_old + lax.dot_general(
                (w_src * v).astype(BF16), kb, (((0,), (0,)), ((), ())), preferred_element_type=F32)
            n_ref[s, h:h + 1, :] = w_old * n_old + jnp.sum(w_src * k, axis=0, keepdims=True)
            m_ref[s, :, h:h + 1] = m_new
            mo_ref[s, :, h * d:(h + 1) * d] = (_ln_core(hv) * g_ref[:, h * d:(h + 1) * d]) * jax.nn.sigmoid(o_pre)


def _mlstm_call(mq, gt, C0, n0, m0, g, bb):
    B, T, _ = mq.shape
    L = math.gcd(T, M_CHUNK)
    nc = T // L
    gtt = gt[:, :, :2 * M_HEADS].reshape(B, nc, L, 2 * M_HEADS).transpose(0, 1, 3, 2)
    st4 = pl.BlockSpec((bb, M_HEADS, M_DIM, M_DIM), lambda i, c: (i, 0, 0, 0))
    st3 = pl.BlockSpec((bb, M_HEADS, M_DIM), lambda i, c: (i, 0, 0))
    st2 = pl.BlockSpec((bb, 1, M_HEADS), lambda i, c: (i, 0, 0))
    mo, C, n, m = pl.pallas_call(
        _mlstm_kernel,
        grid=(B // bb, nc),
        in_specs=[pl.BlockSpec((bb, L, 4 * M_WIDTH), lambda i, c: (i, c, 0)),
                  pl.BlockSpec((bb, L, LANES), lambda i, c: (i, c, 0)),
                  pl.BlockSpec((bb, 1, 2 * M_HEADS, L), lambda i, c: (i, c, 0, 0)),
                  st4, st3, st2,
                  pl.BlockSpec((1, M_WIDTH), lambda i, c: (0, 0))],
        out_specs=[pl.BlockSpec((bb, L, M_WIDTH), lambda i, c: (i, c, 0)), st4, st3, st2],
        out_shape=[jax.ShapeDtypeStruct((B, T, M_WIDTH), F32),
                   jax.ShapeDtypeStruct(C0.shape, F32), jax.ShapeDtypeStruct(n0.shape, F32),
                   jax.ShapeDtypeStruct((B, 1, M_HEADS), F32)],
        compiler_params=_cparams(("parallel", "arbitrary")),
        name="mlstm",
    )(mq, gt, gtt, C0, n0, m0.reshape(B, 1, M_HEADS), g.reshape(1, M_WIDTH))
    return mo, C, n, m.reshape(B, M_HEADS)


def _gelu_tanh(x):
    return x * (0.5 * (1.0 + jnp.tanh(math.sqrt(2.0 / math.pi) * (x + 0.044715 * (x * x * x)))))


def _compress_body(load, pe_ref, w1_ref, w2_ref, rows_total, nh, n_cmp):
    w = 2 * KV_WIDTH
    half = CMP_LEN // 2
    pa = jnp.zeros((rows_total, w), F32)
    pb = jnp.zeros((rows_total, w), F32)
    for j in range(half):
        xj = load(j)
        pa = pa + jnp.dot((xj + pe_ref[j:j + 1, :]).astype(BF16), w1_ref[j], preferred_element_type=F32)
        pb = pb + jnp.dot((xj + pe_ref[half + j:half + j + 1, :]).astype(BF16), w1_ref[half + j],
                          preferred_element_type=F32)
    hid = pa + pltpu.roll(pb, rows_total - 1, 0)
    y = jnp.dot(_gelu_tanh(hid).astype(BF16), w2_ref[...], preferred_element_type=F32)
    rows = lax.broadcasted_iota(jnp.int32, (rows_total, w), 0)
    assert nh & (nh - 1) == 0
    return jnp.where((rows & (nh - 1)) < n_cmp, y, 0.0)


def _compress_kernel(x_ref, pe_ref, w1_ref, w2_ref, o_ref, *, n_cmp):
    nh = x_ref.shape[1]
    w = 2 * KV_WIDTH
    o_ref[0] = _compress_body(lambda j: x_ref[0, :, j * w:(j + 1) * w], pe_ref, w1_ref, w2_ref, nh, nh, n_cmp)


def _compress_sample_kernel(pt_ref, pool_ref, pe_ref, w1_ref, w2_ref, o_ref, xbuf, sem, *, seqs, n_pages, n_cmp):
    i = pl.program_id(0)
    slot = i % 2
    rpp = pool_ref.shape[1]
    nh = n_pages * rpp
    w = 2 * KV_WIDTH

    def copies(step, sl):
        return [pltpu.make_async_copy(pool_ref.at[pt_ref[step * seqs + s, p]],
                                      xbuf.at[sl, pl.ds((s * n_pages + p) * rpp, rpp), :], sem.at[sl])
                for s in range(seqs) for p in range(n_pages)]

    @pl.when(i == 0)
    def _():
        for cp in copies(0, 0):
            cp.start()

    @pl.when(i + 1 < pl.num_programs(0))
    def _():
        for cp in copies(i + 1, 1 - slot):
            cp.start()

    for cp in copies(i, slot):
        cp.wait()
    o_ref[...] = _compress_body(lambda j: xbuf[slot, :, j * w:(j + 1) * w], pe_ref, w1_ref, w2_ref,
                                seqs * nh, nh, n_cmp)


def _compress_sample(pool, page_table, cmp_w, n_cmp, seqs):
    n_pool, page, w = pool.shape
    DB, n_pages = page_table.shape
    rpp = page // CMP_STRIDE
    nh = n_pages * rpp
    pe4, w1, w2 = cmp_w
    full = lambda *s: pl.BlockSpec(s, lambda i, pt: (0,) * len(s))
    grid_spec = pltpu.PrefetchScalarGridSpec(
        num_scalar_prefetch=1,
        grid=(DB // seqs,),
        in_specs=[pl.BlockSpec(memory_space=pl.ANY), full(CMP_LEN, w), full(CMP_LEN, w, w), full(w, w)],
        out_specs=pl.BlockSpec((seqs * nh, w), lambda i, pt: (i, 0)),
        scratch_shapes=[pltpu.VMEM((2, seqs * nh, CMP_STRIDE * w), F32), pltpu.SemaphoreType.DMA((2,))],
    )
    return pl.pallas_call(
        functools.partial(_compress_sample_kernel, seqs=seqs, n_pages=n_pages, n_cmp=n_cmp),
        grid_spec=grid_spec,
        out_shape=jax.ShapeDtypeStruct((DB * nh, w), F32),
        compiler_params=_cparams(("arbitrary",)),
        name="compress_sample",
    )(page_table, pool.reshape(n_pool, rpp, CMP_STRIDE * w), pe4, w1, w2)


def _block_diag4(a, b, c, d):
    z = jnp.zeros_like(a)
    return jnp.concatenate([jnp.concatenate([a, z, z, z], -1), jnp.concatenate([z, b, z, z], -1),
                            jnp.concatenate([z, z, c, z], -1), jnp.concatenate([z, z, z, d], -1)], -2)


def _compress_weights(cmp_params):
    pe_k, w1_k, w2_k, pe_v, w1_v, w2_v = cmp_params
    pe4 = jnp.concatenate([pe_k, pe_k, pe_v, pe_v], axis=1)
    w1k = w1_k.reshape(CMP_LEN, A_DIM, A_DIM)
    w1v = w1_v.reshape(CMP_LEN, A_DIM, A_DIM)
    w1 = _block_diag4(w1k, w1k, w1v, w1v).astype(BF16)
    w2 = _block_diag4(w2_k, w2_k, w2_v, w2_v).astype(BF16)
    return pe4, w1, w2


def _compress_prompt(cmp_rows, cmp_w):
    B, L, w = cmp_rows.shape
    nh = L // CMP_STRIDE
    n_cmp = (L - CMP_LEN) // CMP_STRIDE + 1
    pe4, w1, w2 = cmp_w
    full = lambda *s: pl.BlockSpec(s, lambda b: (0,) * len(s))
    return pl.pallas_call(
        functools.partial(_compress_kernel, n_cmp=n_cmp),
        grid=(B,),
        in_specs=[pl.BlockSpec((1, nh, CMP_STRIDE * w), lambda b: (b, 0, 0)),
                  full(CMP_LEN, w), full(CMP_LEN, w, w), full(w, w)],
        out_specs=pl.BlockSpec((1, nh, w), lambda b: (b, 0, 0)),
        out_shape=jax.ShapeDtypeStruct((B, nh, w), F32),
        compiler_params=_cparams(("parallel",)),
        name="compress_prompt",
    )(cmp_rows.reshape(B, nh, CMP_STRIDE * w), pe4, w1, w2)


_NEG = -1e30
_NEG_SEL = -1e9
_SEL_CHUNK = 256


def _dot_nt(a, b):
    return lax.dot_general(a, b, (((1,), (1,)), ((), ())), preferred_element_type=F32)


def _softmax_rows(s, valid):
    s = jnp.where(valid[None], s, _NEG)
    m = jnp.max(s, axis=-1, keepdims=True)
    e = jnp.where(valid[None], jnp.exp(s - m), 0.0)
    return e / jnp.maximum(jnp.sum(e, axis=-1, keepdims=True), jnp.finfo(jnp.float32).tiny)


def _heads_to_rows(q):
    tq = q.shape[0]
    lane = lax.broadcasted_iota(jnp.int32, (tq, LANES), 1)
    q = q * (A_DIM ** -0.5)
    rows = []
    for hd in range(A_HEADS):
        g = hd // A_GROUP
        tile = q[:, LANES * (hd // 2):LANES * (hd // 2 + 1)]
        if hd % 2 != g:
            tile = pltpu.roll(tile, A_DIM, 1)
        keep = (lane < A_DIM) if g == 0 else (lane >= A_DIM)
        rows.append(jnp.where(keep, tile, 0.0))
    return jnp.concatenate(rows, axis=0).astype(BF16)


def _gate_rows_to_heads(gt, o_c, o_s, o_w):
    tq = gt.shape[0]
    lane = lax.broadcasted_iota(jnp.int32, (tq, LANES), 1)
    gs = jax.nn.sigmoid(gt)
    tiles = []
    for pair in range(A_HEADS // 2):
        g = (2 * pair) // A_GROUP
        mixed = []
        for hd in (2 * pair, 2 * pair + 1):
            c = 2 * M_HEADS + 3 * hd
            r = slice(hd * tq, (hd + 1) * tq)
            mixed.append(gs[:, c:c + 1] * o_c[r] + gs[:, c + 1:c + 2] * o_s[r] + gs[:, c + 2:c + 3] * o_w[r])
        a, b = mixed
        if g == 0:
            tiles.append(jnp.where(lane < A_DIM, a, pltpu.roll(b, A_DIM, 1)))
        else:
            tiles.append(jnp.where(lane < A_DIM, pltpu.roll(a, A_DIM, 1), b))
    return jnp.concatenate(tiles, axis=1)


def _attend_two(s_a, ok_a, v_a, s_b, ok_b, v_b):
    nh, tq = s_a.shape[:2]
    if ok_a is not None:
        s_a = jnp.where(ok_a[None], s_a, _NEG)
    s_b = jnp.where(ok_b[None], s_b, _NEG)
    m = jnp.maximum(jnp.max(s_a, axis=-1, keepdims=True), jnp.max(s_b, axis=-1, keepdims=True))
    e_a = jnp.exp(s_a - m)
    e_b = jnp.exp(s_b - m)
    l = jnp.sum(e_a, axis=-1, keepdims=True) + jnp.sum(e_b, axis=-1, keepdims=True)
    o = (jnp.dot(e_a.reshape(nh * tq, -1).astype(BF16), v_a, preferred_element_type=F32)
         + jnp.dot(e_b.reshape(nh * tq, -1).astype(BF16), v_b, preferred_element_type=F32))
    return o / l.reshape(nh * tq, 1)


def _nsa_sample_kernel(pt_ref, q_ref, gt_ref, kc_ref, vc_ref, seln_ref, wst_ref, wnew_ref, oh_ref, pool_ref,
                       o_ref, nwin_ref, selbuf, sem, *, past):
    b = pl.program_id(0)
    slot = b % 2
    tq = q_ref.shape[0]
    nh = A_HEADS
    n_pages = pt_ref.shape[1]
    page = pool_ref.shape[1]
    wbuf = wst_ref.shape[1]
    n_sel = -(-(past + tq) // SEL_LEN)

    def copies(seq, sl):
        return [pltpu.make_async_copy(pool_ref.at[pt_ref[seq, p]], selbuf.at[sl, pl.ds(p * page, page), :],
                                      sem.at[sl]) for p in range(n_pages)]

    @pl.when(b == 0)
    def _():
        for cp in copies(0, 0):
            cp.start()

    @pl.when(b + 1 < pl.num_programs(0))
    def _():
        for cp in copies(b + 1, 1 - slot):
            cp.start()

    qz = _heads_to_rows(q_ref[...])
    tpos = past + lax.broadcasted_iota(jnp.int32, (tq, LANES), 0)

    kc = kc_ref[...].astype(BF16)
    vc = vc_ref[...].astype(BF16)
    ncp = kc.shape[0]
    s_c = _dot_nt(qz, kc).reshape(nh, tq, ncp)
    tp_c = past + lax.broadcasted_iota(jnp.int32, (tq, ncp), 0)
    nidx = lax.broadcasted_iota(jnp.int32, (tq, ncp), 1)
    p_c = _softmax_rows(s_c, nidx * CMP_STRIDE + (CMP_LEN - 1) <= tp_c)
    o_c = jnp.dot(p_c.reshape(nh * tq, ncp).astype(BF16), vc, preferred_element_type=F32)

    cn = lax.broadcasted_iota(jnp.int32, (ncp, LANES), 0) * CMP_STRIDE
    jn = lax.broadcasted_iota(jnp.int32, (ncp, LANES), 1) * SEL_LEN
    ov = jnp.where((cn < jn + SEL_LEN) & (cn + CMP_LEN > jn), 1.0, 0.0).astype(BF16)
    jb = lax.broadcasted_iota(jnp.int32, (tq, LANES), 1)
    cur = tpos // SEL_LEN
    forced = (jb == 0) | (jb == cur) | (jb == cur - 1)
    bias = []
    for g in range(A_KV_HEADS):
        ps = p_c[g * A_GROUP]
        for r in range(1, A_GROUP):
            ps = ps + p_c[g * A_GROUP + r]
        hi = ps.astype(BF16)
        lo = (ps - hi.astype(F32)).astype(BF16)
        score = (jnp.dot(hi, ov, preferred_element_type=F32) + jnp.dot(lo, ov, preferred_element_type=F32))
        score = jnp.where(forced, FORCE_SCORE, score)
        score = jnp.where(jb * SEL_LEN <= tpos, score, -1.0)
        cnt = jnp.zeros((tq, LANES), jnp.int32)
        for k in range(n_sel):
            sk = score[:, k:k + 1]
            ahead = (sk > score) | ((sk == score) & (jb > k))
            cnt = cnt + jnp.where(ahead, 1, 0)
        bias.append(jnp.where(cnt < min(SEL_TOP, n_sel), 0.0, _NEG_SEL).astype(BF16))
    q_aug = jnp.concatenate(
        [qz, jnp.concatenate([bias[hd // A_GROUP] for hd in range(nh)], axis=0)], axis=1)

    tw = lax.broadcasted_iota(jnp.int32, (tq, wbuf), 0)
    iw = lax.broadcasted_iota(jnp.int32, (tq, wbuf), 1)
    tn = lax.broadcasted_iota(jnp.int32, (tq, tq), 0)
    un = lax.broadcasted_iota(jnp.int32, (tq, tq), 1)
    wst = wst_ref[0]
    wnew = wnew_ref[...]
    s_wa = _dot_nt(qz, wst[:, :KV_WIDTH].astype(BF16)).reshape(nh, tq, wbuf)
    s_wb = _dot_nt(qz, wnew[:, :KV_WIDTH].astype(BF16)).reshape(nh, tq, tq)
    o_w = _attend_two(s_wa, (wbuf + tw - iw < WINDOW), wst[:, KV_WIDTH:].astype(BF16),
                      s_wb, un <= tn, wnew[:, KV_WIDTH:].astype(BF16))
    nwin_ref[0, :wbuf - tq, :] = wst[tq:, :]
    nwin_ref[0, wbuf - tq:, :] = wnew

    for cp in copies(b, slot):
        cp.wait()
    kv = selbuf[slot]
    seln = seln_ref[...]
    k_aug = jnp.concatenate([kv[:, :KV_WIDTH].astype(BF16), oh_ref[...]], axis=1)
    nblk = (past + lax.broadcasted_iota(jnp.int32, (tq, LANES), 0)) // SEL_LEN
    oh_new = jnp.where(nblk == lax.broadcasted_iota(jnp.int32, (tq, LANES), 1), 1.0, 0.0)
    kn_aug = jnp.concatenate([seln[:, :KV_WIDTH], oh_new], axis=1).astype(BF16)
    s_sa = _dot_nt(q_aug, k_aug).reshape(nh, tq, past)
    s_sb = _dot_nt(q_aug, kn_aug).reshape(nh, tq, tq)
    o_s = _attend_two(s_sa, None, kv[:, KV_WIDTH:].astype(BF16), s_sb, un <= tn, seln[:, KV_WIDTH:].astype(BF16))

    o_ref[...] = _gate_rows_to_heads(gt_ref[...], o_c, o_s, o_w)


def _nsa_sample_call(qa, gt, comp, sel_new, win_state, win_new, sel_pool, page_table, T):
    DB, n_pages = page_table.shape
    page = sel_pool.shape[1]
    past = n_pages * page
    wbuf = win_state.shape[1]
    ncp = comp.shape[0] // DB
    w2 = 2 * KV_WIDTH
    assert past % SEL_LEN == 0 and T < CMP_STRIDE and T % 8 == 0 and wbuf == WINDOW and past >= WINDOW
    onehot = (jnp.arange(past)[:, None] // SEL_LEN == jnp.arange(LANES)[None, :]).astype(BF16)
    tok = lambda w: pl.BlockSpec((T, w), lambda b, pt: (b, 0))
    grid_spec = pltpu.PrefetchScalarGridSpec(
        num_scalar_prefetch=1,
        grid=(DB,),
        in_specs=[tok(A_WIDTH), tok(LANES),
                  pl.BlockSpec((ncp, KV_WIDTH), lambda b, pt: (b, 0)),
                  pl.BlockSpec((ncp, KV_WIDTH), lambda b, pt: (b, 1)),
                  tok(w2),
                  pl.BlockSpec((1, wbuf, w2), lambda b, pt: (b, 0, 0)),
                  tok(w2),
                  pl.BlockSpec((past, LANES), lambda b, pt: (0, 0)),
                  pl.BlockSpec(memory_space=pl.ANY)],
        out_specs=[tok(A_WIDTH), pl.BlockSpec((1, wbuf, w2), lambda b, pt: (b, 0, 0))],
        scratch_shapes=[pltpu.VMEM((2, past, w2), F32), pltpu.SemaphoreType.DMA((2,))],
    )
    return pl.pallas_call(
        functools.partial(_nsa_sample_kernel, past=past),
        grid_spec=grid_spec,
        out_shape=[jax.ShapeDtypeStruct((DB * T, A_WIDTH), F32), jax.ShapeDtypeStruct((DB, wbuf, w2), F32)],
        compiler_params=_cparams(("arbitrary",)),
        name="nsa_sample",
    )(page_table, qa, gt, comp, comp, sel_new, win_state, win_new, onehot, sel_pool)


def _nsa_prompt_kernel(q_ref, gt_ref, kc_ref, vc_ref, sel_ref, win_ref, o_ref, *, seq):
    tq = Q_BLOCK
    nh = A_HEADS
    n_sel = seq // SEL_LEN
    s0 = pl.program_id(1) * tq
    qz = _heads_to_rows(q_ref[0])

    kc = kc_ref[0].astype(BF16)
    vc = vc_ref[0].astype(BF16)
    ncp = kc.shape[0]
    s_c = _dot_nt(qz, kc).reshape(nh, tq, ncp)
    tpos = s0 + lax.broadcasted_iota(jnp.int32, (tq, ncp), 0)
    nidx = lax.broadcasted_iota(jnp.int32, (tq, ncp), 1)
    p_c = _softmax_rows(s_c, nidx * CMP_STRIDE + (CMP_LEN - 1) <= tpos)
    o_c = jnp.dot(p_c.reshape(nh * tq, ncp).astype(BF16), vc, preferred_element_type=F32)

    jn = lax.broadcasted_iota(jnp.int32, (n_sel, ncp), 0) * SEL_LEN
    cn = lax.broadcasted_iota(jnp.int32, (n_sel, ncp), 1) * CMP_STRIDE
    ov_t = jnp.where((cn < jn + SEL_LEN) & (cn + CMP_LEN > jn), 1.0, 0.0).astype(BF16)
    jb = lax.broadcasted_iota(jnp.int32, (n_sel, tq), 0)
    tp = s0 + lax.broadcasted_iota(jnp.int32, (n_sel, tq), 1)
    cur = tp // SEL_LEN
    forced = (jb == 0) | (jb == cur) | (jb == cur - 1)
    bias = []
    for g in range(A_KV_HEADS):
        ps = p_c[g * A_GROUP]
        for r in range(1, A_GROUP):
            ps = ps + p_c[g * A_GROUP + r]
        hi = ps.astype(BF16)
        lo = (ps - hi.astype(F32)).astype(BF16)
        score = _dot_nt(ov_t, hi) + _dot_nt(ov_t, lo)
        score = jnp.where(forced, FORCE_SCORE, score)
        score = jnp.where(jb * SEL_LEN <= tp, score, -1.0)
        cnt = jnp.zeros((n_sel, tq), jnp.int32)
        for k in range(n_sel):
            rk = score[k:k + 1, :]
            ahead = (rk > score) | ((rk == score) & (jb > k))
            cnt = cnt + jnp.where(ahead, 1, 0)
        bias_t = jnp.where(cnt < min(SEL_TOP, n_sel), 0.0, _NEG_SEL)
        if n_sel < LANES:
            bias_t = jnp.concatenate([bias_t, jnp.zeros((LANES - n_sel, tq), F32)], axis=0)
        bias.append(bias_t.T.astype(BF16))
    q_aug = jnp.concatenate(
        [qz, jnp.concatenate([bias[hd // A_GROUP] for hd in range(nh)], axis=0)], axis=1)

    kc_n = _SEL_CHUNK

    def chunk(c, carry, causal):
        m, l, acc = carry
        k0 = pl.multiple_of(c * kc_n, kc_n)
        kv = sel_ref[0, pl.ds(k0, kc_n), :]
        kblk = (k0 + lax.broadcasted_iota(jnp.int32, (kc_n, LANES), 0)) // SEL_LEN
        onehot = jnp.where(kblk == lax.broadcasted_iota(jnp.int32, (kc_n, LANES), 1), 1.0, 0.0)
        k_aug = jnp.concatenate([kv[:, :KV_WIDTH], onehot], axis=1).astype(BF16)
        s = _dot_nt(q_aug, k_aug)
        if causal:
            kpos = k0 + lax.broadcasted_iota(jnp.int32, (tq, kc_n), 1)
            qpos = s0 + lax.broadcasted_iota(jnp.int32, (tq, kc_n), 0)
            s = jnp.where((kpos <= qpos)[None], s.reshape(nh, tq, kc_n), _NEG_SEL).reshape(nh * tq, kc_n)
        m_new = jnp.maximum(m, jnp.max(s, axis=-1, keepdims=True))
        a = jnp.exp(m - m_new)
        p = jnp.exp(s - m_new)
        l = a * l + jnp.sum(p, axis=-1, keepdims=True)
        acc = a * acc + jnp.dot(p.astype(BF16), kv[:, KV_WIDTH:].astype(BF16), preferred_element_type=F32)
        return m_new, l, acc

    n_ch = (s0 + tq - 1) // kc_n + 1
    init = (jnp.full((nh * tq, 1), _NEG, F32), jnp.zeros((nh * tq, 1), F32), jnp.zeros((nh * tq, LANES), F32))
    carry = lax.fori_loop(0, n_ch - 1, lambda c, cr: chunk(c, cr, False), init)
    _, l_s, acc_s = chunk(n_ch - 1, carry, True)
    o_s = acc_s / l_s

    wk = WINDOW + tq
    w0 = pl.multiple_of(jnp.clip(s0 - WINDOW, 0, seq - wk), tq)
    kvw = win_ref[0, pl.ds(w0, wk), :]
    s_w = _dot_nt(qz, kvw[:, :KV_WIDTH].astype(BF16)).reshape(nh, tq, wk)
    dist = (s0 + lax.broadcasted_iota(jnp.int32, (tq, wk), 0)) - (w0 + lax.broadcasted_iota(jnp.int32, (tq, wk), 1))
    p_w = _softmax_rows(s_w, (dist >= 0) & (dist < WINDOW))
    o_w = jnp.dot(p_w.reshape(nh * tq, wk).astype(BF16), kvw[:, KV_WIDTH:].astype(BF16),
                  preferred_element_type=F32)

    o_ref[0] = _gate_rows_to_heads(gt_ref[0], o_c, o_s, o_w)


def _nsa_prompt_call(qa, gt, comp, sel_kv, win_kv):
    B, S, _ = qa.shape
    assert S % _SEL_CHUNK == 0 and S >= WINDOW + Q_BLOCK and S // SEL_LEN <= LANES
    ncp = comp.shape[1]
    per_b = lambda r, w: pl.BlockSpec((1, r, w), lambda b, i: (b, 0, 0))
    return pl.pallas_call(
        functools.partial(_nsa_prompt_kernel, seq=S),
        grid=(B, S // Q_BLOCK),
        in_specs=[pl.BlockSpec((1, Q_BLOCK, A_WIDTH), lambda b, i: (b, i, 0)),
                  pl.BlockSpec((1, Q_BLOCK, LANES), lambda b, i: (b, i, 0)),
                  pl.BlockSpec((1, ncp, KV_WIDTH), lambda b, i: (b, 0, 0)),
                  pl.BlockSpec((1, ncp, KV_WIDTH), lambda b, i: (b, 0, 1)),
                  per_b(S, 2 * KV_WIDTH), per_b(S, 2 * KV_WIDTH)],
        out_specs=pl.BlockSpec((1, Q_BLOCK, A_WIDTH), lambda b, i: (b, i, 0)),
        out_shape=jax.ShapeDtypeStruct((B, S, A_WIDTH), F32),
        compiler_params=_cparams(("parallel", "arbitrary")),
        name="nsa_prompt",
    )(qa, gt, comp, comp, sel_kv, win_kv)


def _moe_and_out(h2, logits, x1, mod, moe_w, ln_g, ln_b, bb, tt, alpha, n_experts):
    w_gu, b_gu, w_down, b_down = moe_w
    B, T, d = x1.shape
    gate_w, dest, src, blk_e, n_used = _route(logits, n_experts)
    xb = h2[src].astype(BF16)
    yb = _ffn(xb, blk_e, n_used, w_gu, b_gu, w_down, b_down)
    yg = yb[dest.reshape(B * T, TOP_K).T].reshape(TOP_K, B, T, d)
    return _post2(yg, gate_w.reshape(B, T, TOP_K), x1, mod, ln_g, ln_b, bb, tt, alpha)


def kernel(x_prompt, x_sample, cache_cmp, cache_sel, state_win, state_C, state_n, state_m, page_table,
           c_prompt, c_sample, w_ada, b_ada, w_in, b_in, m_norm_g, cmp_pe_k, cmp_w1_k, cmp_w2_k,
           cmp_pe_v, cmp_w1_v, cmp_w2_v, w_out, ln1_g, ln1_b, w_router, b_router, w_gu, b_gu,
           w_down, b_down, ln2_g, ln2_b):
    B, S, d = x_prompt.shape
    DB, T, _ = x_sample.shape
    depth = w_ada.shape[0]
    n_experts = w_router.shape[-1]
    alpha = (2 * depth) ** 0.25
    past_len = page_table.shape[1] * cache_cmp.shape[2]
    wbuf = state_win.shape[2]
    pos_p = jnp.arange(S)
    pos_s = past_len + jnp.arange(T)
    tt_p = min(S, 256)
    bb_s = min(DB, max(1, 256 // T))
    y_prompt, y_sample = x_prompt, x_sample
    outs = [[] for _ in range(12)]
    for l in range(depth):
        cmp_l = (cmp_pe_k[l], cmp_w1_k[l], cmp_w2_k[l], cmp_pe_v[l], cmp_w1_v[l], cmp_w2_v[l])
        moe_w = (w_gu[l], b_gu[l], w_down[l], b_down[l])
        w_r, b_r = _prep_w_in(w_in[l], b_in[l])
        cmp_w = _compress_weights(cmp_l)
        c_all = jnp.concatenate([c_prompt, c_sample], axis=0)
        n_c = c_all.shape[0]
        c_all = jnp.pad(c_all, ((0, -n_c % 8), (0, 0)))
        mod_all = _ada(c_all, w_ada[l], b_ada[l]).reshape(-1, 6, d)
        mod_p, mod_s = mod_all[:B], mod_all[B:B + DB]

        mq, gt, qa, cmp_kv, sel_kv, win_kv = _pre(y_prompt, mod_p, pos_p, w_r, b_r, 1, tt_p)
        r3 = lambda a: a.reshape(B, S, a.shape[-1])
        kv5 = lambda a, n, t: a.reshape(n, t, 2, A_KV_HEADS, A_DIM)
        zc = jnp.zeros((B, M_HEADS, M_DIM, M_DIM), F32)
        mo, C_p, n_p, m_p = _mlstm_call(r3(mq), r3(gt), zc, zc[..., 0], zc[..., 0, 0], m_norm_g[l],
                                        math.gcd(B, 4))
        comp = _compress_prompt(r3(cmp_kv), cmp_w)
        ma = _nsa_prompt_call(r3(qa), r3(gt), comp, r3(sel_kv), r3(win_kv))
        x1, h2, lg = _post1(mo.reshape(B * S, M_WIDTH), ma.reshape(B * S, A_WIDTH), y_prompt, mod_p,
                            w_out[l], ln1_g[l], ln1_b[l], w_router[l], b_router[l], 1, tt_p, alpha)
        y_prompt = _moe_and_out(h2, lg, x1, mod_p, moe_w, ln2_g[l], ln2_b[l], 1, tt_p, alpha, n_experts)
        for lst, v in zip(outs[:6], (kv5(cmp_kv, B, S), kv5(sel_kv, B, S),
                                     kv5(win_kv, B, S)[:, -min(WINDOW, S):], C_p, n_p, m_p)):
            lst.append(v)

        pre = _pre(y_sample, mod_s, pos_s, w_r, b_r, bb_s, T)
        cmp_kv, sel_kv = kv5(pre[3], DB, T), kv5(pre[4], DB, T)
        mo, C_s, n_s, m_s = _mlstm_call(pre[0].reshape(DB, T, -1), pre[1].reshape(DB, T, -1), state_C[l],
                                        state_n[l], state_m[l], m_norm_g[l], math.gcd(DB, 4))
        n_pool, page = cache_cmp.shape[1:3]
        n_cmp_s = (past_len + T - CMP_LEN) // CMP_STRIDE + 1
        assert n_cmp_s < past_len // CMP_STRIDE
        comp = _compress_sample(cache_cmp[l].reshape(n_pool, page, 2 * KV_WIDTH), page_table, cmp_w, n_cmp_s,
                                math.gcd(DB, 4))
        ma, new_win = _nsa_sample_call(pre[2], pre[1], comp, pre[4], state_win[l].reshape(DB, wbuf, 2 * KV_WIDTH),
                                       pre[5], cache_sel[l].reshape(n_pool, page, 2 * KV_WIDTH), page_table, T)
        new_win = new_win.reshape(state_win.shape[1:])
        x1, h2, lg = _post1(mo.reshape(DB * T, M_WIDTH), ma.reshape(DB * T, A_WIDTH), y_sample, mod_s,
                            w_out[l], ln1_g[l], ln1_b[l], w_router[l], b_router[l], bb_s, T, alpha)
        y_sample = _moe_and_out(h2, lg, x1, mod_s, moe_w, ln2_g[l], ln2_b[l], bb_s, T, alpha, n_experts)
        for lst, v in zip(outs[6:], (cmp_kv, sel_kv, new_win, C_s, n_s, m_s)):
            lst.append(v)
    st = [jnp.stack(v) for v in outs]
    return (y_prompt, y_sample, *st)
```

```python
import functools
import math

import numpy as np
import jax
import jax.numpy as jnp
from jax import lax
from jax.experimental import pallas as pl
from jax.experimental.pallas import tpu as pltpu

F32 = jnp.float32
BF16 = jnp.bfloat16

M_HEADS = 4
M_DIM = 128
M_WIDTH = M_HEADS * M_DIM
M_CHUNK = 64
A_HEADS = 8
A_KV_HEADS = 2
A_GROUP = A_HEADS // A_KV_HEADS
A_DIM = 64
A_WIDTH = A_HEADS * A_DIM
KV_WIDTH = A_KV_HEADS * A_DIM
CMP_LEN = 32
CMP_STRIDE = 16
SEL_LEN = 64
SEL_TOP = 16
WINDOW = 512
Q_BLOCK = 128
FORCE_SCORE = 1.0e4
ROPE_THETA = 500000.0
ROPE_DIM = A_DIM // 4
TOP_K = 4
SWIGLU_LIMIT = 7.0
SWIGLU_ALPHA = 1.702
LN_EPS = 1e-5
LANES = 128
MOE_ROWS = 256
VMEM_LIMIT = 56 * 1024 * 1024

_O_MQ = 0
_O_IF = 4 * M_WIDTH
_O_QA = _O_IF + 2 * M_HEADS
_O_KV = _O_QA + A_WIDTH
_O_GA = _O_KV + 6 * KV_WIDTH
_N_IN = _O_GA + 3 * A_HEADS
_R_QA = 4 * M_WIDTH
_R_KV = _R_QA + A_WIDTH
_R_GT = _R_KV + 6 * KV_WIDTH
_R_END = _R_GT + LANES


def _cparams(sem):
    return pltpu.CompilerParams(dimension_semantics=sem, vmem_limit_bytes=VMEM_LIMIT)


def _ln_core(x):
    mu = jnp.mean(x, axis=-1, keepdims=True)
    xc = x - mu
    var = jnp.mean(xc * xc, axis=-1, keepdims=True)
    return xc * lax.rsqrt(var + LN_EPS)


def _ada_kernel(c_ref, w_ref, b_ref, o_ref):
    o_ref[...] = jnp.dot(c_ref[...].astype(BF16), w_ref[...].astype(BF16),
                         preferred_element_type=F32) + b_ref[...]


def _ada(c, w_ada, b_ada):
    n, d = c.shape
    cols = w_ada.shape[1]
    return pl.pallas_call(
        _ada_kernel,
        grid=(cols // d,),
        in_specs=[pl.BlockSpec((n, d), lambda j: (0, 0)),
                  pl.BlockSpec((d, d), lambda j: (0, j)),
                  pl.BlockSpec((1, d), lambda j: (0, j))],
        out_specs=pl.BlockSpec((n, d), lambda j: (0, j)),
        out_shape=jax.ShapeDtypeStruct((n, cols), F32),
        compiler_params=_cparams(("arbitrary",)),
        name="ada",
    )(c, w_ada, b_ada.reshape(1, cols))


def _rope_apply(v, cos, sa, sb):
    reps = v.shape[1] // LANES
    tile = lambda t: t if reps == 1 else jnp.concatenate([t] * reps, axis=1)
    w = v.shape[1]
    return (v * tile(cos) + pltpu.roll(v, w - ROPE_DIM // 2, 1) * tile(sa)
            + pltpu.roll(v, ROPE_DIM // 2, 1) * tile(sb))


def _pre_kernel(x_ref, mod_ref, cos_ref, sa_ref, sb_ref, w_ref, b_ref,
                mq_ref, gt_ref, qa_ref, cmp_ref, sel_ref, win_ref):
    bb, tt, d = x_ref.shape
    mod = mod_ref[...]
    h = _ln_core(x_ref[...]) * (1.0 + mod[:, 1:2, :]) + mod[:, 0:1, :]
    h = h.reshape(bb * tt, d).astype(BF16)
    z = jnp.dot(h, w_ref[...], preferred_element_type=F32) + b_ref[...]
    cos, sa, sb = cos_ref[...], sa_ref[...], sb_ref[...]
    mq_ref[...] = z[:, :_R_QA]
    gt_ref[...] = z[:, _R_GT:_R_END]
    qa_ref[...] = _rope_apply(z[:, _R_QA:_R_KV], cos, sa, sb)
    for n, ref in enumerate((cmp_ref, sel_ref, win_ref)):
        o = _R_KV + 2 * KV_WIDTH * n
        ref[:, :KV_WIDTH] = _rope_apply(z[:, o:o + KV_WIDTH], cos, sa, sb)
        ref[:, KV_WIDTH:] = z[:, o + KV_WIDTH:o + 2 * KV_WIDTH]


def _rope_tables(pos):
    half = ROPE_DIM // 2
    inv = ROPE_THETA ** (-jnp.arange(0, ROPE_DIM, 2, dtype=F32) / ROPE_DIM)
    ang = pos.astype(F32)[:, None] * inv[None, :]
    cos, sin = jnp.cos(ang), jnp.sin(ang)
    n = pos.shape[0]
    one = jnp.ones((n, A_DIM - ROPE_DIM), F32)
    zero = jnp.zeros((n, A_DIM - ROPE_DIM), F32)
    zh = jnp.zeros((n, half), F32)
    cos_t = jnp.concatenate([cos, cos, one], axis=1)
    sa_t = jnp.concatenate([-sin, zh, zero], axis=1)
    sb_t = jnp.concatenate([zh, sin, zero], axis=1)
    two = lambda t: jnp.concatenate([t, t], axis=1)
    return two(cos_t), two(sa_t), two(sb_t)


def _prep_w_in(w_in, b_in):
    pad = LANES - 2 * M_HEADS - 3 * A_HEADS
    cat = lambda a: jnp.concatenate(
        [a[..., _O_MQ:_O_IF], a[..., _O_QA:_O_GA], a[..., _O_IF:_O_QA], a[..., _O_GA:_N_IN],
         jnp.zeros(a.shape[:-1] + (pad,), a.dtype)], axis=-1)
    return cat(w_in).astype(BF16), cat(b_in[None, :])


def _pre(x, mod, pos, w_r, b_r, bb, tt):
    B, T, d = x.shape
    nt = T // tt
    rows = bb * tt
    cos, sa, sb = _rope_tables(pos)
    if bb > 1:
        cos, sa, sb = (jnp.tile(t, (bb, 1)) for t in (cos, sa, sb))
    n_tok = B * T
    tab = pl.BlockSpec((rows, LANES), lambda i, j: (j, 0))
    row = lambda w: pl.BlockSpec((rows, w), lambda i, j: (i * nt + j, 0))
    widths = (_R_QA, LANES, A_WIDTH, 2 * KV_WIDTH, 2 * KV_WIDTH, 2 * KV_WIDTH)
    return pl.pallas_call(
        _pre_kernel,
        grid=(B // bb, nt),
        in_specs=[pl.BlockSpec((bb, tt, d), lambda i, j: (i, j, 0)),
                  pl.BlockSpec((bb, 6, d), lambda i, j: (i, 0, 0)),
                  tab, tab, tab,
                  pl.BlockSpec((d, _R_END), lambda i, j: (0, 0)),
                  pl.BlockSpec((1, _R_END), lambda i, j: (0, 0))],
        out_specs=[row(w) for w in widths],
        out_shape=[jax.ShapeDtypeStruct((n_tok, w), F32) for w in widths],
        compiler_params=_cparams(("parallel", "arbitrary")),
        name="pre",
    )(x, mod, cos, sa, sb, w_r, b_r)


def _post1_kernel(mo_ref, ao_ref, x_ref, mod_ref, wo_ref, g_ref, b_ref, wr_ref, br_ref,
                  x1_ref, h2_ref, lg_ref, *, alpha):
    bb, tt, d = x_ref.shape
    mod = mod_ref[...]
    mixin = jnp.concatenate([mo_ref[...], ao_ref[...]], axis=1).astype(BF16)
    mix = jnp.dot(mixin, wo_ref[...], preferred_element_type=F32).reshape(bb, tt, d)
    x1 = _ln_core(alpha * x_ref[...] + mod[:, 2:3, :] * mix) * g_ref[...] + b_ref[...]
    h2 = _ln_core(x1) * (1.0 + mod[:, 4:5, :]) + mod[:, 3:4, :]
    x1_ref[...] = x1
    h2f = h2.reshape(bb * tt, d)
    h2_ref[...] = h2f
    lg_ref[...] = jnp.dot(h2f, wr_ref[...], preferred_element_type=F32,
                          precision=lax.Precision.HIGHEST) + br_ref[...]


def _post1(mo, ao, x, mod, w_out, ln_g, ln_b, w_router, b_router, bb, tt, alpha):
    B, T, d = x.shape
    nt = T // tt
    rows = bb * tt
    ne = w_router.shape[1]
    wr = jnp.pad(w_router, ((0, 0), (0, LANES - ne)))
    br = jnp.pad(b_router, (0, LANES - ne), constant_values=-jnp.inf).reshape(1, LANES)
    full = lambda *s: pl.BlockSpec(s, lambda i, j: (0,) * len(s))
    row = lambda w: pl.BlockSpec((rows, w), lambda i, j: (i * nt + j, 0))
    return pl.pallas_call(
        functools.partial(_post1_kernel, alpha=alpha),
        grid=(B // bb, nt),
        in_specs=[row(M_WIDTH), row(A_WIDTH),
                  pl.BlockSpec((bb, tt, d), lambda i, j: (i, j, 0)),
                  pl.BlockSpec((bb, 6, d), lambda i, j: (i, 0, 0)),
                  full(M_WIDTH + A_WIDTH, d), full(1, d), full(1, d), full(d, LANES), full(1, LANES)],
        out_specs=[pl.BlockSpec((bb, tt, d), lambda i, j: (i, j, 0)), row(d), row(LANES)],
        out_shape=[jax.ShapeDtypeStruct((B, T, d), F32),
                   jax.ShapeDtypeStruct((B * T, d), F32),
                   jax.ShapeDtypeStruct((B * T, LANES), F32)],
        compiler_params=_cparams(("parallel", "arbitrary")),
        name="post1",
    )(mo, ao, x, mod, w_out.astype(BF16), ln_g.reshape(1, d), ln_b.reshape(1, d), wr, br)


def _ffn_kernel(be_ref, nu_ref, x_ref, wgu_ref, bgu_ref, wd_ref, bd_ref, y_ref):
    i = pl.program_id(0)
    dff = wd_ref.shape[1]

    @pl.when(i < nu_ref[0])
    def _():
        gu = jnp.dot(x_ref[...].astype(BF16), wgu_ref[0], preferred_element_type=F32) + bgu_ref[0]
        g = jnp.minimum(gu[:, :dff], SWIGLU_LIMIT)
        u = jnp.clip(gu[:, dff:], -SWIGLU_LIMIT, SWIGLU_LIMIT)
        act = (u + 1.0) * (g * jax.nn.sigmoid(SWIGLU_ALPHA * g))
        y_ref[...] = jnp.dot(act.astype(BF16), wd_ref[0], preferred_element_type=F32) + bd_ref[0]

    @pl.when(i >= nu_ref[0])
    def _():
        y_ref[...] = jnp.zeros_like(y_ref)


def _ffn(xb, blk_e, n_used, w_gu, b_gu, w_down, b_down):
    rows, d = xb.shape
    ne, _, f2 = w_gu.shape
    dff = w_down.shape[1]
    nb = rows // MOE_ROWS
    grid_spec = pltpu.PrefetchScalarGridSpec(
        num_scalar_prefetch=2,
        grid=(nb,),
        in_specs=[pl.BlockSpec((MOE_ROWS, d), lambda i, be, nu: (i, 0)),
                  pl.BlockSpec((1, d, f2), lambda i, be, nu: (be[i], 0, 0)),
                  pl.BlockSpec((1, 1, f2), lambda i, be, nu: (be[i], 0, 0)),
                  pl.BlockSpec((1, dff, d), lambda i, be, nu: (be[i], 0, 0)),
                  pl.BlockSpec((1, 1, d), lambda i, be, nu: (be[i], 0, 0))],
        out_specs=pl.BlockSpec((MOE_ROWS, d), lambda i, be, nu: (i, 0)),
    )
    return pl.pallas_call(
        _ffn_kernel,
        grid_spec=grid_spec,
        out_shape=jax.ShapeDtypeStruct((rows, d), F32),
        compiler_params=_cparams(("arbitrary",)),
        name="ffn",
    )(blk_e, n_used, xb, w_gu.astype(BF16), b_gu.reshape(ne, 1, f2),
      w_down.astype(BF16), b_down.reshape(ne, 1, d))


def _post2_kernel(yg_ref, gw_ref, x1_ref, mod_ref, g_ref, b_ref, y_ref, *, alpha):
    bb, tt, d = x1_ref.shape
    gw = gw_ref[...]
    f = yg_ref[0] * gw[:, 0:1]
    for k in range(1, TOP_K):
        f = f + yg_ref[k] * gw[:, k:k + 1]
    y = alpha * x1_ref[...] + mod_ref[...][:, 5:6, :] * f.reshape(bb, tt, d)
    y_ref[...] = _ln_core(y) * g_ref[...] + b_ref[...]


def _post2(yg, gw, row0, x1, mod, ln_g, ln_b, bb, tt, alpha):
    B, T, d = x1.shape
    rows = bb * tt
    nt = T // tt
    blk0 = row0 // rows
    assert row0 % rows == 0
    full = lambda *s: pl.BlockSpec(s, lambda i, j: (0,) * len(s))
    return pl.pallas_call(
        functools.partial(_post2_kernel, alpha=alpha),
        grid=(B // bb, nt),
        in_specs=[pl.BlockSpec((TOP_K, rows, d), lambda i, j: (0, blk0 + i * nt + j, 0)),
                  pl.BlockSpec((rows, TOP_K), lambda i, j: (blk0 + i * nt + j, 0)),
                  pl.BlockSpec((bb, tt, d), lambda i, j: (i, j, 0)),
                  pl.BlockSpec((bb, 6, d), lambda i, j: (i, 0, 0)),
                  full(1, d), full(1, d)],
        out_specs=pl.BlockSpec((bb, tt, d), lambda i, j: (i, j, 0)),
        out_shape=jax.ShapeDtypeStruct((B, T, d), F32),
        compiler_params=_cparams(("parallel", "arbitrary")),
        name="post2",
    )(yg, gw, x1, mod, ln_g.reshape(1, d), ln_b.reshape(1, d))


def _route(logits, n_experts):
    top_val, top_idx = lax.top_k(logits, TOP_K)
    gate_w = jax.nn.softmax(top_val, axis=-1)
    T = logits.shape[0]
    n_slots = T * TOP_K
    expert = top_idx.reshape(-1).astype(jnp.int32)
    order = jnp.argsort(expert)
    sorted_e = expert[order]
    counts = jnp.zeros((n_experts,), jnp.int32).at[expert].add(1)
    padded = (counts + MOE_ROWS - 1) // MOE_ROWS * MOE_ROWS
    pad_end = jnp.cumsum(padded)
    pad_start = pad_end - padded
    grp_start = jnp.cumsum(counts) - counts
    dest_sorted = pad_start[sorted_e] + jnp.arange(n_slots, dtype=jnp.int32) - grp_start[sorted_e]
    dest = jnp.zeros((n_slots,), jnp.int32).at[order].set(dest_sorted)
    n_blocks = -(-n_slots // MOE_ROWS) + n_experts
    src = jnp.zeros((n_blocks * MOE_ROWS,), jnp.int32).at[dest].set(
        jnp.arange(n_slots, dtype=jnp.int32) // TOP_K)
    n_used = (pad_end[-1] // MOE_ROWS).astype(jnp.int32)
    blk = jnp.minimum(jnp.arange(n_blocks, dtype=jnp.int32), n_used - 1) * MOE_ROWS
    blk_e = jnp.minimum(jnp.searchsorted(pad_end, blk, side='right'), n_experts - 1).astype(jnp.int32)
    return gate_w, dest, src, blk_e, n_used.reshape(1)


def _log_sigmoid(x):
    return jnp.minimum(x, 0.0) - jnp.log(1.0 + jnp.exp(-jnp.abs(x)))


def _mlstm_kernel(mq_ref, gt_ref, gtt_ref, c0_ref, n0_ref, m0_ref, g_ref, mo_ref, c_ref, n_ref, m_ref):
    bb, L, _ = mq_ref.shape
    d = M_DIM

    @pl.when(pl.program_id(1) == 0)
    def _():
        c_ref[...] = c0_ref[...]
        n_ref[...] = n0_ref[...]
        m_ref[...] = m0_ref[...]

    tt = lax.broadcasted_iota(jnp.int32, (L, L), 0)
    ss = lax.broadcasted_iota(jnp.int32, (L, L), 1)
    causal = ss <= tt
    for s in range(bb):
        gcol = gt_ref[s]
        grow = gtt_ref[s, 0]
        for h in range(M_HEADS):
            q = mq_ref[s, :, h * d:(h + 1) * d]
            k = mq_ref[s, :, M_WIDTH + h * d:M_WIDTH + (h + 1) * d] * (d ** -0.5)
            v = mq_ref[s, :, 2 * M_WIDTH + h * d:2 * M_WIDTH + (h + 1) * d]
            o_pre = mq_ref[s, :, 3 * M_WIDTH + h * d:3 * M_WIDTH + (h + 1) * d]
            i_c = gcol[:, h:h + 1]
            lf_c = _log_sigmoid(gcol[:, M_HEADS + h:M_HEADS + h + 1])
            i_r = grow[h:h + 1, :]
            lf_r = _log_sigmoid(grow[M_HEADS + h:M_HEADS + h + 1, :])
            b_c = jnp.sum(jnp.where(causal, lf_r, 0.0), axis=1, keepdims=True)
            b_r = jnp.sum(jnp.where(causal, 0.0, lf_c), axis=0, keepdims=True) + lf_r
            c_old = c_ref[s, h]
            n_old = n_ref[s, h:h + 1, :]
            m_old = m_ref[s, :, h:h + 1]
            dmat = jnp.where(causal, b_c - b_r + i_r, _NEG)
            m_t = jnp.maximum(b_c + m_old, jnp.max(dmat, axis=1, keepdims=True))
            inter = jnp.exp(b_c + m_old - m_t)
            qb, kb, vb = q.astype(BF16), k.astype(BF16), v.astype(BF16)
            a = jnp.exp(dmat - m_t) * _dot_nt(qb, kb)
            num = inter * _dot_nt(qb, c_old.astype(BF16)) + jnp.dot(a.astype(BF16), vb, preferred_element_type=F32)
            den = inter * jnp.sum(q * n_old, axis=1, keepdims=True) + jnp.sum(a, axis=1, keepdims=True)
            hv = num / jnp.maximum(jnp.abs(den), jnp.exp(-m_t))
            m_new = m_t[L - 1:L, :]
            b_last = b_c[L - 1:L, :]
            w_src = jnp.exp(b_last - b_c + i_c - m_new)
            w_old = jnp.exp(b_last + m_old - m_new)
            c_ref[s, h] = w_old * c_old + lax.dot_general(
                (w_src * v).astype(BF16), kb, (((0,), (0,)), ((), ())), preferred_element_type=F32)
            n_ref[s, h:h + 1, :] = w_old * n_old + jnp.sum(w_src * k, axis=0, keepdims=True)
            m_ref[s, :, h:h + 1] = m_new
            mo_ref[s, :, h * d:(h + 1) * d] = (_ln_core(hv) * g_ref[:, h * d:(h + 1) * d]) * jax.nn.sigmoid(o_pre)


def _mlstm_call(mq, gt, C0, n0, m0, g, bb):
    B, T, _ = mq.shape
    L = math.gcd(T, M_CHUNK)
    nc = T // L
    gtt = gt[:, :, :2 * M_HEADS].reshape(B, nc, L, 2 * M_HEADS).transpose(0, 1, 3, 2)
    st4 = pl.BlockSpec((bb, M_HEADS, M_DIM, M_DIM), lambda i, c: (i, 0, 0, 0))
    st3 = pl.BlockSpec((bb, M_HEADS, M_DIM), lambda i, c: (i, 0, 0))
    st2 = pl.BlockSpec((bb, 1, M_HEADS), lambda i, c: (i, 0, 0))
    mo, C, n, m = pl.pallas_call(
        _mlstm_kernel,
        grid=(B // bb, nc),
        in_specs=[pl.BlockSpec((bb, L, 4 * M_WIDTH), lambda i, c: (i, c, 0)),
                  pl.BlockSpec((bb, L, LANES), lambda i, c: (i, c, 0)),
                  pl.BlockSpec((bb, 1, 2 * M_HEADS, L), lambda i, c: (i, c, 0, 0)),
                  st4, st3, st2,
                  pl.BlockSpec((1, M_WIDTH), lambda i, c: (0, 0))],
        out_specs=[pl.BlockSpec((bb, L, M_WIDTH), lambda i, c: (i, c, 0)), st4, st3, st2],
        out_shape=[jax.ShapeDtypeStruct((B, T, M_WIDTH), F32),
                   jax.ShapeDtypeStruct(C0.shape, F32), jax.ShapeDtypeStruct(n0.shape, F32),
                   jax.ShapeDtypeStruct((B, 1, M_HEADS), F32)],
        compiler_params=_cparams(("parallel", "arbitrary")),
        name="mlstm",
    )(mq, gt, gtt, C0, n0, m0.reshape(B, 1, M_HEADS), g.reshape(1, M_WIDTH))
    return mo, C, n, m.reshape(B, M_HEADS)


def _gelu_tanh(x):
    return x * (0.5 * (1.0 + jnp.tanh(math.sqrt(2.0 / math.pi) * (x + 0.044715 * (x * x * x)))))


def _compress_body(load, pe_ref, w1_ref, w2_ref, rows_total, nh, n_cmp):
    w = KV_WIDTH
    half = CMP_LEN // 2
    pa = jnp.zeros((rows_total, w), F32)
    pb = jnp.zeros((rows_total, w), F32)
    for j in range(half):
        xj = load(j)
        pa = pa + jnp.dot((xj + pe_ref[j:j + 1, :]).astype(BF16), w1_ref[j], preferred_element_type=F32)
        pb = pb + jnp.dot((xj + pe_ref[half + j:half + j + 1, :]).astype(BF16), w1_ref[half + j],
                          preferred_element_type=F32)
    hid = pa + pltpu.roll(pb, rows_total - 1, 0)
    y = jnp.dot(_gelu_tanh(hid).astype(BF16), w2_ref[...], preferred_element_type=F32)
    rows = lax.broadcasted_iota(jnp.int32, (rows_total, w), 0)
    assert nh & (nh - 1) == 0
    return jnp.where((rows & (nh - 1)) < n_cmp, y, 0.0)


def _compress_kernel(xk_ref, xv_ref, pe_ref, w1_ref, w2_ref, o_ref, *, n_cmp):
    nh = xk_ref.shape[1] // CMP_STRIDE
    for kv, x_ref in enumerate((xk_ref, xv_ref)):
        o_ref[0, :, kv * KV_WIDTH:(kv + 1) * KV_WIDTH] = _compress_body(
            lambda j: x_ref[0, pl.ds(j, nh, stride=CMP_STRIDE), :], pe_ref.at[kv], w1_ref.at[kv], w2_ref.at[kv],
            nh, nh, n_cmp)


def _compress_sample_kernel(pt_ref, pool_ref, pe_ref, w1_ref, w2_ref, o_ref, xbuf, sem, *, seqs, n_pages, n_cmp):
    i = pl.program_id(0)
    slot = i % 2
    page = pool_ref.shape[1]
    nh = n_pages * page // CMP_STRIDE

    def copies(step, sl):
        return [pltpu.make_async_copy(pool_ref.at[pt_ref[step * seqs + s, p], :, pl.ds(kv * KV_WIDTH, KV_WIDTH)],
                                      xbuf.at[sl, kv, pl.ds((s * n_pages + p) * page, page), :], sem.at[sl])
                for s in range(seqs) for p in range(n_pages) for kv in range(2)]

    @pl.when(i == 0)
    def _():
        for cp in copies(0, 0):
            cp.start()

    @pl.when(i + 1 < pl.num_programs(0))
    def _():
        for cp in copies(i + 1, 1 - slot):
            cp.start()

    for cp in copies(i, slot):
        cp.wait()
    for kv in range(2):
        o_ref[:, kv * KV_WIDTH:(kv + 1) * KV_WIDTH] = _compress_body(
            lambda j: xbuf[slot, kv, pl.ds(j, seqs * nh, stride=CMP_STRIDE), :],
            pe_ref.at[kv], w1_ref.at[kv], w2_ref.at[kv], seqs * nh, nh, n_cmp)


def _compress_sample(pool, page_table, cmp_w, n_cmp, seqs):
    n_pool, page, w = pool.shape
    DB, n_pages = page_table.shape
    nh = n_pages * page // CMP_STRIDE
    pe4, w1, w2 = cmp_w
    full = lambda *s: pl.BlockSpec(s, lambda i, pt: (0,) * len(s))
    grid_spec = pltpu.PrefetchScalarGridSpec(
        num_scalar_prefetch=1,
        grid=(DB // seqs,),
        in_specs=[pl.BlockSpec(memory_space=pl.ANY), full(*pe4.shape), full(*w1.shape), full(*w2.shape)],
        out_specs=pl.BlockSpec((seqs * nh, w), lambda i, pt: (i, 0)),
        scratch_shapes=[pltpu.VMEM((2, 2, seqs * n_pages * page, KV_WIDTH), F32), pltpu.SemaphoreType.DMA((2,))],
    )
    return pl.pallas_call(
        functools.partial(_compress_sample_kernel, seqs=seqs, n_pages=n_pages, n_cmp=n_cmp),
        grid_spec=grid_spec,
        out_shape=jax.ShapeDtypeStruct((DB * nh, w), F32),
        compiler_params=_cparams(("arbitrary",)),
        name="compress_sample",
    )(page_table, pool, pe4, w1, w2)


def _block_diag2(a):
    z = jnp.zeros_like(a)
    return jnp.concatenate([jnp.concatenate([a, z], -1), jnp.concatenate([z, a], -1)], -2)


def _compress_weights(cmp_params):
    pe_k, w1_k, w2_k, pe_v, w1_v, w2_v = cmp_params
    pe = jnp.stack([jnp.concatenate([p, p], axis=1) for p in (pe_k, pe_v)])
    w1 = jnp.stack([_block_diag2(w.reshape(CMP_LEN, A_DIM, A_DIM)) for w in (w1_k, w1_v)]).astype(BF16)
    w2 = jnp.stack([_block_diag2(w) for w in (w2_k, w2_v)]).astype(BF16)
    return pe, w1, w2


def _compress_prompt(cmp_rows, cmp_w):
    B, L, w = cmp_rows.shape
    nh = L // CMP_STRIDE
    n_cmp = (L - CMP_LEN) // CMP_STRIDE + 1
    pe4, w1, w2 = cmp_w
    full = lambda *s: pl.BlockSpec(s, lambda b: (0,) * len(s))
    return pl.pallas_call(
        functools.partial(_compress_kernel, n_cmp=n_cmp),
        grid=(B,),
        in_specs=[pl.BlockSpec((1, L, KV_WIDTH), lambda b: (b, 0, 0)),
                  pl.BlockSpec((1, L, KV_WIDTH), lambda b: (b, 0, 1)),
                  full(*pe4.shape), full(*w1.shape), full(*w2.shape)],
        out_specs=pl.BlockSpec((1, nh, w), lambda b: (b, 0, 0)),
        out_shape=jax.ShapeDtypeStruct((B, nh, w), F32),
        compiler_params=_cparams(("parallel",)),
        name="compress_prompt",
    )(cmp_rows, cmp_rows, pe4, w1, w2)


_NEG = -1e30
_NEG_SEL = -1e9
_SEL_CHUNK = 256


def _dot_nt(a, b):
    return lax.dot_general(a, b, (((1,), (1,)), ((), ())), preferred_element_type=F32)


def _softmax_rows(s, valid):
    s = jnp.where(valid[None], s, _NEG)
    m = jnp.max(s, axis=-1, keepdims=True)
    e = jnp.where(valid[None], jnp.exp(s - m), 0.0)
    return e / jnp.maximum(jnp.sum(e, axis=-1, keepdims=True), jnp.finfo(jnp.float32).tiny)


def _heads_to_rows(q):
    tq = q.shape[0]
    lane = lax.broadcasted_iota(jnp.int32, (tq, LANES), 1)
    q = q * (A_DIM ** -0.5)
    rows = []
    for hd in range(A_HEADS):
        g = hd // A_GROUP
        tile = q[:, LANES * (hd // 2):LANES * (hd // 2 + 1)]
        if hd % 2 != g:
            tile = pltpu.roll(tile, A_DIM, 1)
        keep = (lane < A_DIM) if g == 0 else (lane >= A_DIM)
        rows.append(jnp.where(keep, tile, 0.0))
    return jnp.concatenate(rows, axis=0).astype(BF16)


def _gate_rows_to_heads(gt, o_c, o_s, o_w):
    tq = gt.shape[0]
    lane = lax.broadcasted_iota(jnp.int32, (tq, LANES), 1)
    gs = jax.nn.sigmoid(gt)
    tiles = []
    for pair in range(A_HEADS // 2):
        g = (2 * pair) // A_GROUP
        mixed = []
        for hd in (2 * pair, 2 * pair + 1):
            c = 2 * M_HEADS + 3 * hd
            r = slice(hd * tq, (hd + 1) * tq)
            mixed.append(gs[:, c:c + 1] * o_c[r] + gs[:, c + 1:c + 2] * o_s[r] + gs[:, c + 2:c + 3] * o_w[r])
        a, b = mixed
        if g == 0:
            tiles.append(jnp.where(lane < A_DIM, a, pltpu.roll(b, A_DIM, 1)))
        else:
            tiles.append(jnp.where(lane < A_DIM, pltpu.roll(a, A_DIM, 1), b))
    return jnp.concatenate(tiles, axis=1)


def _attend_two(s_a, ok_a, v_a, s_b, ok_b, v_b):
    nh, tq = s_a.shape[:2]
    if ok_a is not None:
        s_a = jnp.where(ok_a[None], s_a, _NEG)
    s_b = jnp.where(ok_b[None], s_b, _NEG)
    m = jnp.maximum(jnp.max(s_a, axis=-1, keepdims=True), jnp.max(s_b, axis=-1, keepdims=True))
    e_a = jnp.exp(s_a - m)
    e_b = jnp.exp(s_b - m)
    l = jnp.sum(e_a, axis=-1, keepdims=True) + jnp.sum(e_b, axis=-1, keepdims=True)
    o = (jnp.dot(e_a.reshape(nh * tq, -1).astype(BF16), v_a, preferred_element_type=F32)
         + jnp.dot(e_b.reshape(nh * tq, -1).astype(BF16), v_b, preferred_element_type=F32))
    return o / l.reshape(nh * tq, 1)


def _nsa_sample_kernel(pt_ref, q_ref, gt_ref, kc_ref, vc_ref, seln_ref, wst_ref, wnew_ref, oh_ref, pool_ref,
                       o_ref, nwin_ref, selbuf, sem, *, past):
    b = pl.program_id(0)
    slot = b % 2
    tq = q_ref.shape[0]
    nh = A_HEADS
    n_pages = pt_ref.shape[1]
    page = pool_ref.shape[1]
    wbuf = wst_ref.shape[1]
    n_sel = -(-(past + tq) // SEL_LEN)

    def copies(seq, sl):
        return [pltpu.make_async_copy(pool_ref.at[pt_ref[seq, p]], selbuf.at[sl, pl.ds(p * page, page), :],
                                      sem.at[sl]) for p in range(n_pages)]

    @pl.when(b == 0)
    def _():
        for cp in copies(0, 0):
            cp.start()

    @pl.when(b + 1 < pl.num_programs(0))
    def _():
        for cp in copies(b + 1, 1 - slot):
            cp.start()

    qz = _heads_to_rows(q_ref[...])
    tpos = past + lax.broadcasted_iota(jnp.int32, (tq, LANES), 0)

    kc = kc_ref[...].astype(BF16)
    vc = vc_ref[...].astype(BF16)
    ncp = kc.shape[0]
    s_c = _dot_nt(qz, kc).reshape(nh, tq, ncp)
    tp_c = past + lax.broadcasted_iota(jnp.int32, (tq, ncp), 0)
    nidx = lax.broadcasted_iota(jnp.int32, (tq, ncp), 1)
    p_c = _softmax_rows(s_c, nidx * CMP_STRIDE + (CMP_LEN - 1) <= tp_c)
    o_c = jnp.dot(p_c.reshape(nh * tq, ncp).astype(BF16), vc, preferred_element_type=F32)

    cn = lax.broadcasted_iota(jnp.int32, (ncp, LANES), 0) * CMP_STRIDE
    jn = lax.broadcasted_iota(jnp.int32, (ncp, LANES), 1) * SEL_LEN
    ov = jnp.where((cn < jn + SEL_LEN) & (cn + CMP_LEN > jn), 1.0, 0.0).astype(BF16)
    jb = lax.broadcasted_iota(jnp.int32, (tq, LANES), 1)
    cur = tpos // SEL_LEN
    forced = (jb == 0) | (jb == cur) | (jb == cur - 1)
    bias = []
    for g in range(A_KV_HEADS):
        ps = p_c[g * A_GROUP]
        for r in range(1, A_GROUP):
            ps = ps + p_c[g * A_GROUP + r]
        hi = ps.astype(BF16)
        lo = (ps - hi.astype(F32)).astype(BF16)
        score = (jnp.dot(hi, ov, preferred_element_type=F32) + jnp.dot(lo, ov, preferred_element_type=F32))
        score = jnp.where(forced, FORCE_SCORE, score)
        score = jnp.where(jb * SEL_LEN <= tpos, score, -1.0)
        cnt = jnp.zeros((tq, LANES), jnp.int32)
        for k in range(n_sel):
            sk = score[:, k:k + 1]
            ahead = (sk > score) | ((sk == score) & (jb > k))
            cnt = cnt + jnp.where(ahead, 1, 0)
        bias.append(jnp.where(cnt < min(SEL_TOP, n_sel), 0.0, _NEG_SEL).astype(BF16))
    q_aug = jnp.concatenate(
        [qz, jnp.concatenate([bias[hd // A_GROUP] for hd in range(nh)], axis=0)], axis=1)

    tw = lax.broadcasted_iota(jnp.int32, (tq, wbuf), 0)
    iw = lax.broadcasted_iota(jnp.int32, (tq, wbuf), 1)
    tn = lax.broadcasted_iota(jnp.int32, (tq, tq), 0)
    un = lax.broadcasted_iota(jnp.int32, (tq, tq), 1)
    wst = wst_ref[0]
    wnew = wnew_ref[...]
    s_wa = _dot_nt(qz, wst[:, :KV_WIDTH].astype(BF16)).reshape(nh, tq, wbuf)
    s_wb = _dot_nt(qz, wnew[:, :KV_WIDTH].astype(BF16)).reshape(nh, tq, tq)
    o_w = _attend_two(s_wa, (wbuf + tw - iw < WINDOW), wst[:, KV_WIDTH:].astype(BF16),
                      s_wb, un <= tn, wnew[:, KV_WIDTH:].astype(BF16))
    nwin_ref[0, :wbuf - tq, :] = wst[tq:, :]
    nwin_ref[0, wbuf - tq:, :] = wnew

    for cp in copies(b, slot):
        cp.wait()
    kv = selbuf[slot]
    seln = seln_ref[...]
    k_aug = jnp.concatenate([kv[:, :KV_WIDTH].astype(BF16), oh_ref[...]], axis=1)
    nblk = (past + lax.broadcasted_iota(jnp.int32, (tq, LANES), 0)) // SEL_LEN
    oh_new = jnp.where(nblk == lax.broadcasted_iota(jnp.int32, (tq, LANES), 1), 1.0, 0.0)
    kn_aug = jnp.concatenate([seln[:, :KV_WIDTH], oh_new], axis=1).astype(BF16)
    s_sa = _dot_nt(q_aug, k_aug).reshape(nh, tq, past)
    s_sb = _dot_nt(q_aug, kn_aug).reshape(nh, tq, tq)
    o_s = _attend_two(s_sa, None, kv[:, KV_WIDTH:].astype(BF16), s_sb, un <= tn, seln[:, KV_WIDTH:].astype(BF16))

    o_ref[...] = _gate_rows_to_heads(gt_ref[...], o_c, o_s, o_w)


def _nsa_sample_call(qa, gt, comp, sel_new, win_state, win_new, sel_pool, page_table, T):
    DB, n_pages = page_table.shape
    page = sel_pool.shape[1]
    past = n_pages * page
    wbuf = win_state.shape[1]
    ncp = comp.shape[0] // DB
    w2 = 2 * KV_WIDTH
    assert past % SEL_LEN == 0 and T < CMP_STRIDE and T % 8 == 0 and wbuf == WINDOW and past >= WINDOW
    onehot = (jnp.arange(past)[:, None] // SEL_LEN == jnp.arange(LANES)[None, :]).astype(BF16)
    tok = lambda w: pl.BlockSpec((T, w), lambda b, pt: (b, 0))
    grid_spec = pltpu.PrefetchScalarGridSpec(
        num_scalar_prefetch=1,
        grid=(DB,),
        in_specs=[tok(A_WIDTH), tok(LANES),
                  pl.BlockSpec((ncp, KV_WIDTH), lambda b, pt: (b, 0)),
                  pl.BlockSpec((ncp, KV_WIDTH), lambda b, pt: (b, 1)),
                  tok(w2),
                  pl.BlockSpec((1, wbuf, w2), lambda b, pt: (b, 0, 0)),
                  tok(w2),
                  pl.BlockSpec((past, LANES), lambda b, pt: (0, 0)),
                  pl.BlockSpec(memory_space=pl.ANY)],
        out_specs=[tok(A_WIDTH), pl.BlockSpec((1, wbuf, w2), lambda b, pt: (b, 0, 0))],
        scratch_shapes=[pltpu.VMEM((2, past, w2), F32), pltpu.SemaphoreType.DMA((2,))],
    )
    return pl.pallas_call(
        functools.partial(_nsa_sample_kernel, past=past),
        grid_spec=grid_spec,
        out_shape=[jax.ShapeDtypeStruct((DB * T, A_WIDTH), F32), jax.ShapeDtypeStruct((DB, wbuf, w2), F32)],
        compiler_params=_cparams(("arbitrary",)),
        name="nsa_sample",
    )(page_table, qa, gt, comp, comp, sel_new, win_state, win_new, onehot, sel_pool)


def _nsa_prompt_kernel(q_ref, gt_ref, kc_ref, vc_ref, sel_ref, win_ref, o_ref, *, seq):
    tq = Q_BLOCK
    nh = A_HEADS
    n_sel = seq // SEL_LEN
    s0 = pl.program_id(1) * tq
    qz = _heads_to_rows(q_ref[0])

    kc = kc_ref[0].astype(BF16)
    vc = vc_ref[0].astype(BF16)
    ncp = kc.shape[0]
    s_c = _dot_nt(qz, kc).reshape(nh, tq, ncp)
    tpos = s0 + lax.broadcasted_iota(jnp.int32, (tq, ncp), 0)
    nidx = lax.broadcasted_iota(jnp.int32, (tq, ncp), 1)
    p_c = _softmax_rows(s_c, nidx * CMP_STRIDE + (CMP_LEN - 1) <= tpos)
    o_c = jnp.dot(p_c.reshape(nh * tq, ncp).astype(BF16), vc, preferred_element_type=F32)

    jn = lax.broadcasted_iota(jnp.int32, (n_sel, ncp), 0) * SEL_LEN
    cn = lax.broadcasted_iota(jnp.int32, (n_sel, ncp), 1) * CMP_STRIDE
    ov_t = jnp.where((cn < jn + SEL_LEN) & (cn + CMP_LEN > jn), 1.0, 0.0).astype(BF16)
    jb = lax.broadcasted_iota(jnp.int32, (n_sel, tq), 0)
    tp = s0 + lax.broadcasted_iota(jnp.int32, (n_sel, tq), 1)
    cur = tp // SEL_LEN
    forced = (jb == 0) | (jb == cur) | (jb == cur - 1)
    bias = []
    for g in range(A_KV_HEADS):
        ps = p_c[g * A_GROUP]
        for r in range(1, A_GROUP):
            ps = ps + p_c[g * A_GROUP + r]
        hi = ps.astype(BF16)
        lo = (ps - hi.astype(F32)).astype(BF16)
        score = _dot_nt(ov_t, hi) + _dot_nt(ov_t, lo)
        score = jnp.where(forced, FORCE_SCORE, score)
        score = jnp.where(jb * SEL_LEN <= tp, score, -1.0)
        cnt = jnp.zeros((n_sel, tq), jnp.int32)
        for k in range(n_sel):
            rk = score[k:k + 1, :]
            ahead = (rk > score) | ((rk == score) & (jb > k))
            cnt = cnt + jnp.where(ahead, 1, 0)
        bias_t = jnp.where(cnt < min(SEL_TOP, n_sel), 0.0, _NEG_SEL)
        if n_sel < LANES:
            bias_t = jnp.concatenate([bias_t, jnp.zeros((LANES - n_sel, tq), F32)], axis=0)
        bias.append(bias_t.T.astype(BF16))
    q_aug = jnp.concatenate(
        [qz, jnp.concatenate([bias[hd // A_GROUP] for hd in range(nh)], axis=0)], axis=1)

    kc_n = _SEL_CHUNK

    def chunk(c, carry, causal):
        m, l, acc = carry
        k0 = pl.multiple_of(c * kc_n, kc_n)
        kv = sel_ref[0, pl.ds(k0, kc_n), :]
        kblk = (k0 + lax.broadcasted_iota(jnp.int32, (kc_n, LANES), 0)) // SEL_LEN
        onehot = jnp.where(kblk == lax.broadcasted_iota(jnp.int32, (kc_n, LANES), 1), 1.0, 0.0)
        k_aug = jnp.concatenate([kv[:, :KV_WIDTH], onehot], axis=1).astype(BF16)
        s = _dot_nt(q_aug, k_aug)
        if causal:
            kpos = k0 + lax.broadcasted_iota(jnp.int32, (tq, kc_n), 1)
            qpos = s0 + lax.broadcasted_iota(jnp.int32, (tq, kc_n), 0)
            s = jnp.where((kpos <= qpos)[None], s.reshape(nh, tq, kc_n), _NEG_SEL).reshape(nh * tq, kc_n)
        m_new = jnp.maximum(m, jnp.max(s, axis=-1, keepdims=True))
        a = jnp.exp(m - m_new)
        p = jnp.exp(s - m_new)
        l = a * l + jnp.sum(p, axis=-1, keepdims=True)
        acc = a * acc + jnp.dot(p.astype(BF16), kv[:, KV_WIDTH:].astype(BF16), preferred_element_type=F32)
        return m_new, l, acc

    n_ch = (s0 + tq - 1) // kc_n + 1
    init = (jnp.full((nh * tq, 1), _NEG, F32), jnp.zeros((nh * tq, 1), F32), jnp.zeros((nh * tq, LANES), F32))
    carry = lax.fori_loop(0, n_ch - 1, lambda c, cr: chunk(c, cr, False), init)
    _, l_s, acc_s = chunk(n_ch - 1, carry, True)
    o_s = acc_s / l_s

    wk = WINDOW + tq
    w0 = pl.multiple_of(jnp.clip(s0 - WINDOW, 0, seq - wk), tq)
    kvw = win_ref[0, pl.ds(w0, wk), :]
    s_w = _dot_nt(qz, kvw[:, :KV_WIDTH].astype(BF16)).reshape(nh, tq, wk)
    dist = (s0 + lax.broadcasted_iota(jnp.int32, (tq, wk), 0)) - (w0 + lax.broadcasted_iota(jnp.int32, (tq, wk), 1))
    p_w = _softmax_rows(s_w, (dist >= 0) & (dist < WINDOW))
    o_w = jnp.dot(p_w.reshape(nh * tq, wk).astype(BF16), kvw[:, KV_WIDTH:].astype(BF16),
                  preferred_element_type=F32)

    o_ref[0] = _gate_rows_to_heads(gt_ref[0], o_c, o_s, o_w)


def _nsa_prompt_call(qa, gt, comp, sel_kv, win_kv):
    B, S, _ = qa.shape
    assert S % _SEL_CHUNK == 0 and S >= WINDOW + Q_BLOCK and S // SEL_LEN <= LANES
    ncp = comp.shape[1]
    per_b = lambda r, w: pl.BlockSpec((1, r, w), lambda b, i: (b, 0, 0))
    return pl.pallas_call(
        functools.partial(_nsa_prompt_kernel, seq=S),
        grid=(B, S // Q_BLOCK),
        in_specs=[pl.BlockSpec((1, Q_BLOCK, A_WIDTH), lambda b, i: (b, i, 0)),
                  pl.BlockSpec((1, Q_BLOCK, LANES), lambda b, i: (b, i, 0)),
                  pl.BlockSpec((1, ncp, KV_WIDTH), lambda b, i: (b, 0, 0)),
                  pl.BlockSpec((1, ncp, KV_WIDTH), lambda b, i: (b, 0, 1)),
                  per_b(S, 2 * KV_WIDTH), per_b(S, 2 * KV_WIDTH)],
        out_specs=pl.BlockSpec((1, Q_BLOCK, A_WIDTH), lambda b, i: (b, i, 0)),
        out_shape=jax.ShapeDtypeStruct((B, S, A_WIDTH), F32),
        compiler_params=_cparams(("parallel", "arbitrary")),
        name="nsa_prompt",
    )(qa, gt, comp, comp, sel_kv, win_kv)


def _moe(h2, logits, moe_w, n_experts):
    w_gu, b_gu, w_down, b_down = moe_w
    n, d = h2.shape
    gate_w, dest, src, blk_e, n_used = _route(logits, n_experts)
    yb = _ffn(h2[src], blk_e, n_used, w_gu, b_gu, w_down, b_down)
    return yb[dest.reshape(n, TOP_K).T], gate_w


def kernel(x_prompt, x_sample, cache_cmp, cache_sel, state_win, state_C, state_n, state_m, page_table,
           c_prompt, c_sample, w_ada, b_ada, w_in, b_in, m_norm_g, cmp_pe_k, cmp_w1_k, cmp_w2_k,
           cmp_pe_v, cmp_w1_v, cmp_w2_v, w_out, ln1_g, ln1_b, w_router, b_router, w_gu, b_gu,
           w_down, b_down, ln2_g, ln2_b):
    B, S, d = x_prompt.shape
    DB, T, _ = x_sample.shape
    depth = w_ada.shape[0]
    n_experts = w_router.shape[-1]
    alpha = (2 * depth) ** 0.25
    n_pool, page = cache_cmp.shape[1:3]
    past_len = page_table.shape[1] * page
    wbuf = state_win.shape[2]
    pos_p = jnp.arange(S)
    pos_s = past_len + jnp.arange(T)
    tt_p = min(S, 256)
    bb_s = min(DB, max(1, 256 // T))
    n_cmp_s = (past_len + T - CMP_LEN) // CMP_STRIDE + 1
    assert n_cmp_s < past_len // CMP_STRIDE
    r3 = lambda a: a.reshape(B, S, a.shape[-1])
    kv5 = lambda a, n, t: a.reshape(n, t, 2, A_KV_HEADS, A_DIM)
    y_prompt, y_sample = x_prompt, x_sample
    outs = [[] for _ in range(12)]
    for l in range(depth):
        cmp_l = (cmp_pe_k[l], cmp_w1_k[l], cmp_w2_k[l], cmp_pe_v[l], cmp_w1_v[l], cmp_w2_v[l])
        moe_w = (w_gu[l], b_gu[l], w_down[l], b_down[l])
        post1_w = (w_out[l], ln1_g[l], ln1_b[l], w_router[l], b_router[l])
        w_r, b_r = _prep_w_in(w_in[l], b_in[l])
        cmp_w = _compress_weights(cmp_l)
        c_all = jnp.concatenate([c_prompt, c_sample], axis=0)
        c_all = jnp.pad(c_all, ((0, -c_all.shape[0] % 8), (0, 0)))
        mod_all = _ada(c_all, w_ada[l], b_ada[l]).reshape(-1, 6, d)
        mod_p, mod_s = mod_all[:B], mod_all[B:B + DB]

        mq, gt, qa, cmp_p, sel_p, win_p = _pre(y_prompt, mod_p, pos_p, w_r, b_r, 1, tt_p)
        zc = jnp.zeros((B, M_HEADS, M_DIM, M_DIM), F32)
        mo, C_p, n_p, m_p = _mlstm_call(r3(mq), r3(gt), zc, zc[..., 0], zc[..., 0, 0], m_norm_g[l],
                                        math.gcd(B, 4))
        comp = _compress_prompt(r3(cmp_p), cmp_w)
        ma = _nsa_prompt_call(r3(qa), r3(gt), comp, r3(sel_p), r3(win_p))
        x1_p, h2_p, lg_p = _post1(mo.reshape(B * S, M_WIDTH), ma.reshape(B * S, A_WIDTH), y_prompt, mod_p,
                                  *post1_w, 1, tt_p, alpha)

        mq, gt, qa, cmp_s, sel_s, win_s = _pre(y_sample, mod_s, pos_s, w_r, b_r, bb_s, T)
        mo, C_s, n_s, m_s = _mlstm_call(mq.reshape(DB, T, -1), gt.reshape(DB, T, -1), state_C[l],
                                        state_n[l], state_m[l], m_norm_g[l], math.gcd(DB, 4))
        comp = _compress_sample(cache_cmp[l].reshape(n_pool, page, 2 * KV_WIDTH), page_table, cmp_w, n_cmp_s,
                                math.gcd(DB, 4))
        ma, new_win = _nsa_sample_call(qa, gt, comp, sel_s, state_win[l].reshape(DB, wbuf, 2 * KV_WIDTH),
                                       win_s, cache_sel[l].reshape(n_pool, page, 2 * KV_WIDTH), page_table, T)
        x1_s, h2_s, lg_s = _post1(mo.reshape(DB * T, M_WIDTH), ma, y_sample, mod_s, *post1_w, bb_s, T, alpha)

        yg, gate_w = _moe(jnp.concatenate([h2_p, h2_s], axis=0), jnp.concatenate([lg_p, lg_s], axis=0),
                          moe_w, n_experts)
        y_prompt = _post2(yg, gate_w, 0, x1_p, mod_p, ln2_g[l], ln2_b[l], 1, tt_p, alpha)
        y_sample = _post2(yg, gate_w, B * S, x1_s, mod_s, ln2_g[l], ln2_b[l], bb_s, T, alpha)

        new = (kv5(cmp_p, B, S), kv5(sel_p, B, S), kv5(win_p, B, S)[:, -min(WINDOW, S):], C_p, n_p, m_p,
               kv5(cmp_s, DB, T), kv5(sel_s, DB, T), new_win.reshape(state_win.shape[1:]), C_s, n_s, m_s)
        for lst, v in zip(outs, new):
            lst.append(v)
    return (y_prompt, y_sample, *[jnp.stack(v) for v in outs])
```

```python
import functools
import math

import numpy as np
import jax
import jax.numpy as jnp
from jax import lax
from jax.experimental import pallas as pl
from jax.experimental.pallas import tpu as pltpu

F32 = jnp.float32
BF16 = jnp.bfloat16

M_HEADS = 4
M_DIM = 128
M_WIDTH = M_HEADS * M_DIM
M_CHUNK = 64
A_HEADS = 8
A_KV_HEADS = 2
A_GROUP = A_HEADS // A_KV_HEADS
A_DIM = 64
A_WIDTH = A_HEADS * A_DIM
KV_WIDTH = A_KV_HEADS * A_DIM
CMP_LEN = 32
CMP_STRIDE = 16
SEL_LEN = 64
SEL_TOP = 16
WINDOW = 512
Q_BLOCK = 128
FORCE_SCORE = 1.0e4
ROPE_THETA = 500000.0
ROPE_DIM = A_DIM // 4
TOP_K = 4
SWIGLU_LIMIT = 7.0
SWIGLU_ALPHA = 1.702
LN_EPS = 1e-5
LANES = 128
MOE_ROWS = 256
VMEM_LIMIT = 56 * 1024 * 1024

_O_MQ = 0
_O_IF = 4 * M_WIDTH
_O_QA = _O_IF + 2 * M_HEADS
_O_KV = _O_QA + A_WIDTH
_O_GA = _O_KV + 6 * KV_WIDTH
_N_IN = _O_GA + 3 * A_HEADS
_R_QA = 4 * M_WIDTH
_R_KV = _R_QA + A_WIDTH
_R_GT = _R_KV + 6 * KV_WIDTH
_R_END = _R_GT + LANES


def _cparams(sem):
    return pltpu.CompilerParams(dimension_semantics=sem, vmem_limit_bytes=VMEM_LIMIT)


def _ln_core(x):
    mu = jnp.mean(x, axis=-1, keepdims=True)
    xc = x - mu
    var = jnp.mean(xc * xc, axis=-1, keepdims=True)
    return xc * lax.rsqrt(var + LN_EPS)


def _ada_kernel(c_ref, w_ref, b_ref, o_ref):
    o_ref[...] = jnp.dot(c_ref[...].astype(BF16), w_ref[...].astype(BF16),
                         preferred_element_type=F32) + b_ref[...]


def _ada(c, w_ada, b_ada):
    n, d = c.shape
    cols = w_ada.shape[1]
    return pl.pallas_call(
        _ada_kernel,
        grid=(cols // d,),
        in_specs=[pl.BlockSpec((n, d), lambda j: (0, 0)),
                  pl.BlockSpec((d, d), lambda j: (0, j)),
                  pl.BlockSpec((1, d), lambda j: (0, j))],
        out_specs=pl.BlockSpec((n, d), lambda j: (0, j)),
        out_shape=jax.ShapeDtypeStruct((n, cols), F32),
        compiler_params=_cparams(("arbitrary",)),
        name="ada",
    )(c, w_ada, b_ada.reshape(1, cols))


def _rope_apply(v, cos, sa, sb):
    reps = v.shape[1] // LANES
    tile = lambda t: t if reps == 1 else jnp.concatenate([t] * reps, axis=1)
    w = v.shape[1]
    return (v * tile(cos) + pltpu.roll(v, w - ROPE_DIM // 2, 1) * tile(sa)
            + pltpu.roll(v, ROPE_DIM // 2, 1) * tile(sb))


def _pre_kernel(x_ref, mod_ref, cos_ref, sa_ref, sb_ref, w_ref, b_ref,
                mq_ref, gt_ref, qa_ref, cmp_ref, sel_ref, win_ref):
    bb, tt, d = x_ref.shape
    mod = mod_ref[...]
    h = _ln_core(x_ref[...]) * (1.0 + mod[:, 1:2, :]) + mod[:, 0:1, :]
    h = h.reshape(bb * tt, d).astype(BF16)
    z = jnp.dot(h, w_ref[...], preferred_element_type=F32) + b_ref[...]
    cos, sa, sb = cos_ref[...], sa_ref[...], sb_ref[...]
    mq_ref[...] = z[:, :_R_QA]
    gt_ref[...] = z[:, _R_GT:_R_END]
    qa_ref[...] = _rope_apply(z[:, _R_QA:_R_KV], cos, sa, sb)
    for n, ref in enumerate((cmp_ref, sel_ref, win_ref)):
        o = _R_KV + 2 * KV_WIDTH * n
        ref[:, :KV_WIDTH] = _rope_apply(z[:, o:o + KV_WIDTH], cos, sa, sb)
        ref[:, KV_WIDTH:] = z[:, o + KV_WIDTH:o + 2 * KV_WIDTH]


def _rope_tables(pos):
    half = ROPE_DIM // 2
    inv = ROPE_THETA ** (-jnp.arange(0, ROPE_DIM, 2, dtype=F32) / ROPE_DIM)
    ang = pos.astype(F32)[:, None] * inv[None, :]
    cos, sin = jnp.cos(ang), jnp.sin(ang)
    n = pos.shape[0]
    one = jnp.ones((n, A_DIM - ROPE_DIM), F32)
    zero = jnp.zeros((n, A_DIM - ROPE_DIM), F32)
    zh = jnp.zeros((n, half), F32)
    cos_t = jnp.concatenate([cos, cos, one], axis=1)
    sa_t = jnp.concatenate([-sin, zh, zero], axis=1)
    sb_t = jnp.concatenate([zh, sin, zero], axis=1)
    two = lambda t: jnp.concatenate([t, t], axis=1)
    return two(cos_t), two(sa_t), two(sb_t)


def _prep_w_in(w_in, b_in):
    pad = LANES - 2 * M_HEADS - 3 * A_HEADS
    cat = lambda a: jnp.concatenate(
        [a[..., _O_MQ:_O_IF], a[..., _O_QA:_O_GA], a[..., _O_IF:_O_QA], a[..., _O_GA:_N_IN],
         jnp.zeros(a.shape[:-1] + (pad,), a.dtype)], axis=-1)
    return cat(w_in).astype(BF16), cat(b_in[None, :])


def _pre(x, mod, pos, w_r, b_r, bb, tt):
    B, T, d = x.shape
    nt = T // tt
    rows = bb * tt
    cos, sa, sb = _rope_tables(pos)
    if bb > 1:
        cos, sa, sb = (jnp.tile(t, (bb, 1)) for t in (cos, sa, sb))
    n_tok = B * T
    tab = pl.BlockSpec((rows, LANES), lambda i, j: (j, 0))
    row = lambda w: pl.BlockSpec((rows, w), lambda i, j: (i * nt + j, 0))
    widths = (_R_QA, LANES, A_WIDTH, 2 * KV_WIDTH, 2 * KV_WIDTH, 2 * KV_WIDTH)
    return pl.pallas_call(
        _pre_kernel,
        grid=(B // bb, nt),
        in_specs=[pl.BlockSpec((bb, tt, d), lambda i, j: (i, j, 0)),
                  pl.BlockSpec((bb, 6, d), lambda i, j: (i, 0, 0)),
                  tab, tab, tab,
                  pl.BlockSpec((d, _R_END), lambda i, j: (0, 0)),
                  pl.BlockSpec((1, _R_END), lambda i, j: (0, 0))],
        out_specs=[row(w) for w in widths],
        out_shape=[jax.ShapeDtypeStruct((n_tok, w), F32) for w in widths],
        compiler_params=_cparams(("parallel", "arbitrary")),
        name="pre",
    )(x, mod, cos, sa, sb, w_r, b_r)


def _post1_kernel(mo_ref, ao_ref, x_ref, mod_ref, wo_ref, g_ref, b_ref, wr_ref, br_ref,
                  x1_ref, h2_ref, lg_ref, *, alpha):
    bb, tt, d = x_ref.shape
    mod = mod_ref[...]
    mixin = jnp.concatenate([mo_ref[...], ao_ref[...]], axis=1).astype(BF16)
    mix = jnp.dot(mixin, wo_ref[...], preferred_element_type=F32).reshape(bb, tt, d)
    x1 = _ln_core(alpha * x_ref[...] + mod[:, 2:3, :] * mix) * g_ref[...] + b_ref[...]
    h2 = _ln_core(x1) * (1.0 + mod[:, 4:5, :]) + mod[:, 3:4, :]
    x1_ref[...] = x1
    h2f = h2.reshape(bb * tt, d)
    h2_ref[...] = h2f
    lg_ref[...] = jnp.dot(h2f, wr_ref[...], preferred_element_type=F32,
                          precision=lax.Precision.HIGHEST) + br_ref[...]


def _post1(mo, ao, x, mod, w_out, ln_g, ln_b, w_router, b_router, bb, tt, alpha):
    B, T, d = x.shape
    nt = T // tt
    rows = bb * tt
    ne = w_router.shape[1]
    wr = jnp.pad(w_router, ((0, 0), (0, LANES - ne)))
    br = jnp.pad(b_router, (0, LANES - ne), constant_values=-jnp.inf).reshape(1, LANES)
    full = lambda *s: pl.BlockSpec(s, lambda i, j: (0,) * len(s))
    row = lambda w: pl.BlockSpec((rows, w), lambda i, j: (i * nt + j, 0))
    return pl.pallas_call(
        functools.partial(_post1_kernel, alpha=alpha),
        grid=(B // bb, nt),
        in_specs=[row(M_WIDTH), row(A_WIDTH),
                  pl.BlockSpec((bb, tt, d), lambda i, j: (i, j, 0)),
                  pl.BlockSpec((bb, 6, d), lambda i, j: (i, 0, 0)),
                  full(M_WIDTH + A_WIDTH, d), full(1, d), full(1, d), full(d, LANES), full(1, LANES)],
        out_specs=[pl.BlockSpec((bb, tt, d), lambda i, j: (i, j, 0)), row(d), row(LANES)],
        out_shape=[jax.ShapeDtypeStruct((B, T, d), F32),
                   jax.ShapeDtypeStruct((B * T, d), F32),
                   jax.ShapeDtypeStruct((B * T, LANES), F32)],
        compiler_params=_cparams(("parallel", "arbitrary")),
        name="post1",
    )(mo, ao, x, mod, w_out.astype(BF16), ln_g.reshape(1, d), ln_b.reshape(1, d), wr, br)


def _ffn_kernel(be_ref, nu_ref, x_ref, wgu_ref, bgu_ref, wd_ref, bd_ref, y_ref):
    i = pl.program_id(0)
    dff = wd_ref.shape[1]

    @pl.when(i < nu_ref[0])
    def _():
        gu = jnp.dot(x_ref[...].astype(BF16), wgu_ref[0], preferred_element_type=F32) + bgu_ref[0]
        g = jnp.minimum(gu[:, :dff], SWIGLU_LIMIT)
        u = jnp.clip(gu[:, dff:], -SWIGLU_LIMIT, SWIGLU_LIMIT)
        act = (u + 1.0) * (g * jax.nn.sigmoid(SWIGLU_ALPHA * g))
        y_ref[...] = jnp.dot(act.astype(BF16), wd_ref[0], preferred_element_type=F32) + bd_ref[0]

    @pl.when(i >= nu_ref[0])
    def _():
        y_ref[...] = jnp.zeros_like(y_ref)


def _ffn(xb, blk_e, n_used, w_gu, b_gu, w_down, b_down):
    rows, d = xb.shape
    ne, _, f2 = w_gu.shape
    dff = w_down.shape[1]
    nb = rows // MOE_ROWS
    grid_spec = pltpu.PrefetchScalarGridSpec(
        num_scalar_prefetch=2,
        grid=(nb,),
        in_specs=[pl.BlockSpec((MOE_ROWS, d), lambda i, be, nu: (i, 0)),
                  pl.BlockSpec((1, d, f2), lambda i, be, nu: (be[i], 0, 0)),
                  pl.BlockSpec((1, 1, f2), lambda i, be, nu: (be[i], 0, 0)),
                  pl.BlockSpec((1, dff, d), lambda i, be, nu: (be[i], 0, 0)),
                  pl.BlockSpec((1, 1, d), lambda i, be, nu: (be[i], 0, 0))],
        out_specs=pl.BlockSpec((MOE_ROWS, d), lambda i, be, nu: (i, 0)),
    )
    return pl.pallas_call(
        _ffn_kernel,
        grid_spec=grid_spec,
        out_shape=jax.ShapeDtypeStruct((rows, d), F32),
        compiler_params=_cparams(("arbitrary",)),
        name="ffn",
    )(blk_e, n_used, xb, w_gu.astype(BF16), b_gu.reshape(ne, 1, f2),
      w_down.astype(BF16), b_down.reshape(ne, 1, d))


def _post2_kernel(yg_ref, gw_ref, x1_ref, mod_ref, g_ref, b_ref, y_ref, *, alpha):
    bb, tt, d = x1_ref.shape
    gw = gw_ref[...]
    f = yg_ref[0] * gw[:, 0:1]
    for k in range(1, TOP_K):
        f = f + yg_ref[k] * gw[:, k:k + 1]
    y = alpha * x1_ref[...] + mod_ref[...][:, 5:6, :] * f.reshape(bb, tt, d)
    y_ref[...] = _ln_core(y) * g_ref[...] + b_ref[...]


def _post2(yg, gw, row0, x1, mod, ln_g, ln_b, bb, tt, alpha):
    B, T, d = x1.shape
    rows = bb * tt
    nt = T // tt
    blk0 = row0 // rows
    assert row0 % rows == 0
    full = lambda *s: pl.BlockSpec(s, lambda i, j: (0,) * len(s))
    return pl.pallas_call(
        functools.partial(_post2_kernel, alpha=alpha),
        grid=(B // bb, nt),
        in_specs=[pl.BlockSpec((TOP_K, rows, d), lambda i, j: (0, blk0 + i * nt + j, 0)),
                  pl.BlockSpec((rows, LANES), lambda i, j: (blk0 + i * nt + j, 0)),
                  pl.BlockSpec((bb, tt, d), lambda i, j: (i, j, 0)),
                  pl.BlockSpec((bb, 6, d), lambda i, j: (i, 0, 0)),
                  full(1, d), full(1, d)],
        out_specs=pl.BlockSpec((bb, tt, d), lambda i, j: (i, j, 0)),
        out_shape=jax.ShapeDtypeStruct((B, T, d), F32),
        compiler_params=_cparams(("parallel", "arbitrary")),
        name="post2",
    )(yg, gw, x1, mod, ln_g.reshape(1, d), ln_b.reshape(1, d))


def _lane_prefix_sum(x):
    lane = lax.broadcasted_iota(jnp.int32, x.shape, 1)
    s = 1
    while s < LANES:
        x = x + jnp.where(lane >= s, pltpu.roll(x, s, 1), 0.0)
        s *= 2
    return x


def _route_kernel(lg_ref, dest_ref, gw_ref, cnt_ref, counts, running):
    phase = pl.program_id(0)
    blk = pl.program_id(1)
    r = lg_ref.shape[0]
    lane = lax.broadcasted_iota(jnp.int32, (r, LANES), 1)

    @pl.when((phase == 0) & (blk == 0))
    def _():
        counts[...] = jnp.zeros_like(counts)

    @pl.when((phase == 1) & (blk == 0))
    def _():
        running[...] = jnp.zeros_like(running)

    vals = lg_ref[...]
    onehots, tops = [], []
    for k in range(TOP_K):
        m = jnp.max(vals, axis=1, keepdims=True)
        idx = jnp.min(jnp.where(vals == m, lane, LANES), axis=1, keepdims=True)
        hit = lane == idx
        onehots.append(hit)
        tops.append(m)
        vals = jnp.where(hit, -jnp.inf, vals)
    ohf = [jnp.where(h, 1.0, 0.0) for h in onehots]
    block_cnt = [jnp.sum(o, axis=0, keepdims=True) for o in ohf]

    @pl.when(phase == 0)
    def _():
        counts[...] += block_cnt[0] + block_cnt[1] + block_cnt[2] + block_cnt[3]

    @pl.when(phase == 1)
    def _():
        cnt = counts[...]
        padded = jnp.ceil(cnt * (1.0 / MOE_ROWS)) * MOE_ROWS
        pad_start = _lane_prefix_sum(padded) - padded
        es = [jnp.exp(t - tops[0]) for t in tops]
        den = es[0] + es[1] + es[2] + es[3]
        ti = lax.broadcasted_iota(jnp.int32, (r, r), 0)
        tj = lax.broadcasted_iota(jnp.int32, (r, r), 1)
        before = jnp.where(tj < ti, 1.0, 0.0).astype(BF16)
        base = pad_start + running[...]
        dest = jnp.zeros((r, LANES), F32)
        gw = jnp.zeros((r, LANES), F32)
        for k in range(TOP_K):
            rank = jnp.dot(before, ohf[k].astype(BF16), preferred_element_type=F32)
            d_k = jnp.sum(ohf[k] * (base + rank), axis=1, keepdims=True)
            dest = jnp.where(lane == k, d_k, dest)
            gw = jnp.where(lane == k, es[k] / den, gw)
            base = base + block_cnt[k]
        running[...] = base - pad_start
        dest_ref[...] = dest.astype(jnp.int32)
        gw_ref[...] = gw
        cnt_ref[...] = jnp.broadcast_to(cnt, cnt_ref.shape)


def _route(logits, n_experts, rows=256):
    n = logits.shape[0]
    assert n % rows == 0
    nb = n // rows
    tok = pl.BlockSpec((rows, LANES), lambda p, i: (i, 0))
    out = pl.BlockSpec((rows, LANES), lambda p, i: (i * p, 0))
    dest, gw, cnt = pl.pallas_call(
        _route_kernel,
        grid=(2, nb),
        in_specs=[tok],
        out_specs=[out, out, pl.BlockSpec((8, LANES), lambda p, i: (0, 0))],
        out_shape=[jax.ShapeDtypeStruct((n, LANES), jnp.int32), jax.ShapeDtypeStruct((n, LANES), F32),
                   jax.ShapeDtypeStruct((8, LANES), F32)],
        scratch_shapes=[pltpu.VMEM((1, LANES), F32), pltpu.VMEM((1, LANES), F32)],
        compiler_params=_cparams(("arbitrary", "arbitrary")),
        name="route",
    )(logits)
    counts = cnt[0, :n_experts].astype(jnp.int32)
    pad_end = jnp.cumsum((counts + MOE_ROWS - 1) // MOE_ROWS * MOE_ROWS)
    n_slots = n * TOP_K
    n_blocks = -(-n_slots // MOE_ROWS) + n_experts
    n_used = pad_end[-1] // MOE_ROWS
    blk = jnp.minimum(jnp.arange(n_blocks, dtype=jnp.int32), n_used - 1) * MOE_ROWS
    blk_e = jnp.minimum(jnp.sum(pad_end[None, :] <= blk[:, None], axis=1), n_experts - 1).astype(jnp.int32)
    dest4 = dest[:, :TOP_K]
    src = jnp.zeros((n_blocks * MOE_ROWS,), jnp.int32).at[dest4.reshape(-1)].set(
        jnp.arange(n_slots, dtype=jnp.int32) // TOP_K)
    return gw, dest4, src, blk_e, n_used.reshape(1).astype(jnp.int32)


def _log_sigmoid(x):
    return jnp.minimum(x, 0.0) - jnp.log(1.0 + jnp.exp(-jnp.abs(x)))


def _mlstm_kernel(mq_ref, gt_ref, gtt_ref, c0_ref, n0_ref, m0_ref, g_ref, mo_ref, c_ref, n_ref, m_ref):
    bb, L, _ = mq_ref.shape
    d = M_DIM

    @pl.when(pl.program_id(1) == 0)
    def _():
        c_ref[...] = c0_ref[...]
        n_ref[...] = n0_ref[...]
        m_ref[...] = m0_ref[...]

    tt = lax.broadcasted_iota(jnp.int32, (L, L), 0)
    ss = lax.broadcasted_iota(jnp.int32, (L, L), 1)
    causal = ss <= tt
    for s in range(bb):
        gcol = gt_ref[s]
        grow = gtt_ref[s, 0]
        for h in range(M_HEADS):
            q = mq_ref[s, :, h * d:(h + 1) * d]
            k = mq_ref[s, :, M_WIDTH + h * d:M_WIDTH + (h + 1) * d] * (d ** -0.5)
            v = mq_ref[s, :, 2 * M_WIDTH + h * d:2 * M_WIDTH + (h + 1) * d]
            o_pre = mq_ref[s, :, 3 * M_WIDTH + h * d:3 * M_WIDTH + (h + 1) * d]
            i_c = gcol[:, h:h + 1]
            lf_c = _log_sigmoid(gcol[:, M_HEADS + h:M_HEADS + h + 1])
            i_r = grow[h:h + 1, :]
            lf_r = _log_sigmoid(grow[M_HEADS + h:M_HEADS + h + 1, :])
            b_c = jnp.sum(jnp.where(causal, lf_r, 0.0), axis=1, keepdims=True)
            b_r = jnp.sum(jnp.where(causal, 0.0, lf_c), axis=0, keepdims=True) + lf_r
            c_old = c_ref[s, h]
            n_old = n_ref[s, h:h + 1, :]
            m_old = m_ref[s, :, h:h + 1]
            dmat = jnp.where(causal, b_c - b_r + i_r, _NEG)
            m_t = jnp.maximum(b_c + m_old, jnp.max(dmat, axis=1, keepdims=True))
            inter = jnp.exp(b_c + m_old - m_t)
            qb, kb, vb = q.astype(BF16), k.astype(BF16), v.astype(BF16)
            a = jnp.exp(dmat - m_t) * _dot_nt(qb, kb)
            num = inter * _dot_nt(qb, c_old.astype(BF16)) + jnp.dot(a.astype(BF16), vb, preferred_element_type=F32)
            den = inter * jnp.sum(q * n_old, axis=1, keepdims=True) + jnp.sum(a, axis=1, keepdims=True)
            hv = num / jnp.maximum(jnp.abs(den), jnp.exp(-m_t))
            m_new = m_t[L - 1:L, :]
            b_last = b_c[L - 1:L, :]
            w_src = jnp.exp(b_last - b_c + i_c - m_new)
            w_old = jnp.exp(b_last + m_old - m_new)
            c_ref[s, h] = w_old * c_old + lax.dot_general(
                (w_src * v).astype(BF16), kb, (((0,), (0,)), ((), ())), preferred_element_type=F32)
            n_ref[s, h:h + 1, :] = w_old * n_old + jnp.sum(w_src * k, axis=0, keepdims=True)
            m_ref[s, :, h:h + 1] = m_new
            mo_ref[s, :, h * d:(h + 1) * d] = (_ln_core(hv) * g_ref[:, h * d:(h + 1) * d]) * jax.nn.sigmoid(o_pre)


def _mlstm_call(mq, gt, C0, n0, m0, g, bb):
    B, T, _ = mq.shape
    L = math.gcd(T, M_CHUNK)
    nc = T // L
    gtt = gt[:, :, :2 * M_HEADS].reshape(B, nc, L, 2 * M_HEADS).transpose(0, 1, 3, 2)
    st4 = pl.BlockSpec((bb, M_HEADS, M_DIM, M_DIM), lambda i, c: (i, 0, 0, 0))
    st3 = pl.BlockSpec((bb, M_HEADS, M_DIM), lambda i, c: (i, 0, 0))
    st2 = pl.BlockSpec((bb, 1, M_HEADS), lambda i, c: (i, 0, 0))
    mo, C, n, m = pl.pallas_call(
        _mlstm_kernel,
        grid=(B // bb, nc),
        in_specs=[pl.BlockSpec((bb, L, 4 * M_WIDTH), lambda i, c: (i, c, 0)),
                  pl.BlockSpec((bb, L, LANES), lambda i, c: (i, c, 0)),
                  pl.BlockSpec((bb, 1, 2 * M_HEADS, L), lambda i, c: (i, c, 0, 0)),
                  st4, st3, st2,
                  pl.BlockSpec((1, M_WIDTH), lambda i, c: (0, 0))],
        out_specs=[pl.BlockSpec((bb, L, M_WIDTH), lambda i, c: (i, c, 0)), st4, st3, st2],
        out_shape=[jax.ShapeDtypeStruct((B, T, M_WIDTH), F32),
                   jax.ShapeDtypeStruct(C0.shape, F32), jax.ShapeDtypeStruct(n0.shape, F32),
                   jax.ShapeDtypeStruct((B, 1, M_HEADS), F32)],
        compiler_params=_cparams(("parallel", "arbitrary")),
        name="mlstm",
    )(mq, gt, gtt, C0, n0, m0.reshape(B, 1, M_HEADS), g.reshape(1, M_WIDTH))
    return mo, C, n, m.reshape(B, M_HEADS)


def _gelu_tanh(x):
    return x * (0.5 * (1.0 + jnp.tanh(math.sqrt(2.0 / math.pi) * (x + 0.044715 * (x * x * x)))))


def _compress_body(load, pe_ref, w1_ref, w2_ref, rows_total, nh, n_cmp):
    w = KV_WIDTH
    half = CMP_LEN // 2
    pa = jnp.zeros((rows_total, w), F32)
    pb = jnp.zeros((rows_total, w), F32)
    for j in range(half):
        xj = load(j)
        pa = pa + jnp.dot((xj + pe_ref[j:j + 1, :]).astype(BF16), w1_ref[j], preferred_element_type=F32)
        pb = pb + jnp.dot((xj + pe_ref[half + j:half + j + 1, :]).astype(BF16), w1_ref[half + j],
                          preferred_element_type=F32)
    hid = pa + pltpu.roll(pb, rows_total - 1, 0)
    y = jnp.dot(_gelu_tanh(hid).astype(BF16), w2_ref[...], preferred_element_type=F32)
    rows = lax.broadcasted_iota(jnp.int32, (rows_total, w), 0)
    assert nh & (nh - 1) == 0
    return jnp.where((rows & (nh - 1)) < n_cmp, y, 0.0)


def _compress_kernel(xk_ref, xv_ref, pe_ref, w1_ref, w2_ref, o_ref, *, n_cmp):
    nh = xk_ref.shape[1] // CMP_STRIDE
    for kv, x_ref in enumerate((xk_ref, xv_ref)):
        o_ref[0, :, kv * KV_WIDTH:(kv + 1) * KV_WIDTH] = _compress_body(
            lambda j: x_ref[0, pl.ds(j, nh, stride=CMP_STRIDE), :], pe_ref.at[kv], w1_ref.at[kv], w2_ref.at[kv],
            nh, nh, n_cmp)


def _compress_sample_kernel(pt_ref, pool_ref, pe_ref, w1_ref, w2_ref, o_ref, xbuf, sem, *, seqs, n_pages, n_cmp):
    i = pl.program_id(0)
    slot = i % 2
    page = pool_ref.shape[1]
    nh = n_pages * page // CMP_STRIDE

    def copies(step, sl):
        return [pltpu.make_async_copy(pool_ref.at[pt_ref[step * seqs + s, p], :, pl.ds(kv * KV_WIDTH, KV_WIDTH)],
                                      xbuf.at[sl, kv, pl.ds((s * n_pages + p) * page, page), :], sem.at[sl])
                for s in range(seqs) for p in range(n_pages) for kv in range(2)]

    @pl.when(i == 0)
    def _():
        for cp in copies(0, 0):
            cp.start()

    @pl.when(i + 1 < pl.num_programs(0))
    def _():
        for cp in copies(i + 1, 1 - slot):
            cp.start()

    for cp in copies(i, slot):
        cp.wait()
    for kv in range(2):
        o_ref[:, kv * KV_WIDTH:(kv + 1) * KV_WIDTH] = _compress_body(
            lambda j: xbuf[slot, kv, pl.ds(j, seqs * nh, stride=CMP_STRIDE), :],
            pe_ref.at[kv], w1_ref.at[kv], w2_ref.at[kv], seqs * nh, nh, n_cmp)


def _compress_sample(pool, page_table, cmp_w, n_cmp, seqs):
    n_pool, page, w = pool.shape
    DB, n_pages = page_table.shape
    nh = n_pages * page // CMP_STRIDE
    pe4, w1, w2 = cmp_w
    full = lambda *s: pl.BlockSpec(s, lambda i, pt: (0,) * len(s))
    grid_spec = pltpu.PrefetchScalarGridSpec(
        num_scalar_prefetch=1,
        grid=(DB // seqs,),
        in_specs=[pl.BlockSpec(memory_space=pl.ANY), full(*pe4.shape), full(*w1.shape), full(*w2.shape)],
        out_specs=pl.BlockSpec((seqs * nh, w), lambda i, pt: (i, 0)),
        scratch_shapes=[pltpu.VMEM((2, 2, seqs * n_pages * page, KV_WIDTH), F32), pltpu.SemaphoreType.DMA((2,))],
    )
    return pl.pallas_call(
        functools.partial(_compress_sample_kernel, seqs=seqs, n_pages=n_pages, n_cmp=n_cmp),
        grid_spec=grid_spec,
        out_shape=jax.ShapeDtypeStruct((DB * nh, w), F32),
        compiler_params=_cparams(("arbitrary",)),
        name="compress_sample",
    )(page_table, pool, pe4, w1, w2)


def _block_diag2(a):
    z = jnp.zeros_like(a)
    return jnp.concatenate([jnp.concatenate([a, z], -1), jnp.concatenate([z, a], -1)], -2)


def _compress_weights(cmp_params):
    pe_k, w1_k, w2_k, pe_v, w1_v, w2_v = cmp_params
    pe = jnp.stack([jnp.concatenate([p, p], axis=1) for p in (pe_k, pe_v)])
    w1 = jnp.stack([_block_diag2(w.reshape(CMP_LEN, A_DIM, A_DIM)) for w in (w1_k, w1_v)]).astype(BF16)
    w2 = jnp.stack([_block_diag2(w) for w in (w2_k, w2_v)]).astype(BF16)
    return pe, w1, w2


def _compress_prompt(cmp_rows, cmp_w):
    B, L, w = cmp_rows.shape
    nh = L // CMP_STRIDE
    n_cmp = (L - CMP_LEN) // CMP_STRIDE + 1
    pe4, w1, w2 = cmp_w
    full = lambda *s: pl.BlockSpec(s, lambda b: (0,) * len(s))
    return pl.pallas_call(
        functools.partial(_compress_kernel, n_cmp=n_cmp),
        grid=(B,),
        in_specs=[pl.BlockSpec((1, L, KV_WIDTH), lambda b: (b, 0, 0)),
                  pl.BlockSpec((1, L, KV_WIDTH), lambda b: (b, 0, 1)),
                  full(*pe4.shape), full(*w1.shape), full(*w2.shape)],
        out_specs=pl.BlockSpec((1, nh, w), lambda b: (b, 0, 0)),
        out_shape=jax.ShapeDtypeStruct((B, nh, w), F32),
        compiler_params=_cparams(("parallel",)),
        name="compress_prompt",
    )(cmp_rows, cmp_rows, pe4, w1, w2)


_NEG = -1e30
_NEG_SEL = -1e9
_SEL_CHUNK = 256


def _dot_nt(a, b):
    return lax.dot_general(a, b, (((1,), (1,)), ((), ())), preferred_element_type=F32)


def _softmax_rows(s, valid):
    s = jnp.where(valid[None], s, _NEG)
    m = jnp.max(s, axis=-1, keepdims=True)
    e = jnp.where(valid[None], jnp.exp(s - m), 0.0)
    return e / jnp.maximum(jnp.sum(e, axis=-1, keepdims=True), jnp.finfo(jnp.float32).tiny)


def _heads_to_rows(q):
    tq = q.shape[0]
    lane = lax.broadcasted_iota(jnp.int32, (tq, LANES), 1)
    q = q * (A_DIM ** -0.5)
    rows = []
    for hd in range(A_HEADS):
        g = hd // A_GROUP
        tile = q[:, LANES * (hd // 2):LANES * (hd // 2 + 1)]
        if hd % 2 != g:
            tile = pltpu.roll(tile, A_DIM, 1)
        keep = (lane < A_DIM) if g == 0 else (lane >= A_DIM)
        rows.append(jnp.where(keep, tile, 0.0))
    return jnp.concatenate(rows, axis=0).astype(BF16)


def _gate_rows_to_heads(gt, o_c, o_s, o_w):
    tq = gt.shape[0]
    lane = lax.broadcasted_iota(jnp.int32, (tq, LANES), 1)
    gs = jax.nn.sigmoid(gt)
    tiles = []
    for pair in range(A_HEADS // 2):
        g = (2 * pair) // A_GROUP
        mixed = []
        for hd in (2 * pair, 2 * pair + 1):
            c = 2 * M_HEADS + 3 * hd
            r = slice(hd * tq, (hd + 1) * tq)
            mixed.append(gs[:, c:c + 1] * o_c[r] + gs[:, c + 1:c + 2] * o_s[r] + gs[:, c + 2:c + 3] * o_w[r])
        a, b = mixed
        if g == 0:
            tiles.append(jnp.where(lane < A_DIM, a, pltpu.roll(b, A_DIM, 1)))
        else:
            tiles.append(jnp.where(lane < A_DIM, pltpu.roll(a, A_DIM, 1), b))
    return jnp.concatenate(tiles, axis=1)


def _attend_two(s_a, ok_a, v_a, s_b, ok_b, v_b, v_a_transposed=False):
    nh, tq = s_a.shape[:2]
    if ok_a is not None:
        s_a = jnp.where(ok_a[None], s_a, _NEG)
    s_b = jnp.where(ok_b[None], s_b, _NEG)
    m = jnp.maximum(jnp.max(s_a, axis=-1, keepdims=True), jnp.max(s_b, axis=-1, keepdims=True))
    e_a = jnp.exp(s_a - m)
    e_b = jnp.exp(s_b - m)
    l = jnp.sum(e_a, axis=-1, keepdims=True) + jnp.sum(e_b, axis=-1, keepdims=True)
    e_a = e_a.reshape(nh * tq, -1).astype(BF16)
    o_a = _dot_nt(e_a, v_a) if v_a_transposed else jnp.dot(e_a, v_a, preferred_element_type=F32)
    o = o_a + jnp.dot(e_b.reshape(nh * tq, -1).astype(BF16), v_b, preferred_element_type=F32)
    return o / l.reshape(nh * tq, 1)


def _nsa_sample_kernel(pt_ref, q_ref, gt_ref, kc_ref, vc_ref, seln_ref, wst_ref, wnew_ref, oh_ref, pool_ref,
                       o_ref, nwin_ref, selbuf, sem, *, past):
    b = pl.program_id(0)
    slot = b % 2
    tq = q_ref.shape[0]
    nh = A_HEADS
    n_pages = pt_ref.shape[1]
    page = pool_ref.shape[2]
    wbuf = wst_ref.shape[1]
    n_sel = -(-(past + tq) // SEL_LEN)

    def copies(seq, sl):
        return [pltpu.make_async_copy(pool_ref.at[pt_ref[seq, p]], selbuf.at[sl, :, pl.ds(p * page, page)],
                                      sem.at[sl]) for p in range(n_pages)]

    @pl.when(b == 0)
    def _():
        for cp in copies(0, 0):
            cp.start()

    @pl.when(b + 1 < pl.num_programs(0))
    def _():
        for cp in copies(b + 1, 1 - slot):
            cp.start()

    qz = _heads_to_rows(q_ref[...])
    tpos = past + lax.broadcasted_iota(jnp.int32, (tq, LANES), 0)

    kc = kc_ref[...].astype(BF16)
    vc = vc_ref[...].astype(BF16)
    ncp = kc.shape[0]
    s_c = _dot_nt(qz, kc).reshape(nh, tq, ncp)
    tp_c = past + lax.broadcasted_iota(jnp.int32, (tq, ncp), 0)
    nidx = lax.broadcasted_iota(jnp.int32, (tq, ncp), 1)
    p_c = _softmax_rows(s_c, nidx * CMP_STRIDE + (CMP_LEN - 1) <= tp_c)
    o_c = jnp.dot(p_c.reshape(nh * tq, ncp).astype(BF16), vc, preferred_element_type=F32)

    cn = lax.broadcasted_iota(jnp.int32, (ncp, LANES), 0) * CMP_STRIDE
    jn = lax.broadcasted_iota(jnp.int32, (ncp, LANES), 1) * SEL_LEN
    ov = jnp.where((cn < jn + SEL_LEN) & (cn + CMP_LEN > jn), 1.0, 0.0).astype(BF16)
    jb = lax.broadcasted_iota(jnp.int32, (tq, LANES), 1)
    cur = tpos // SEL_LEN
    forced = (jb == 0) | (jb == cur) | (jb == cur - 1)
    bias = []
    for g in range(A_KV_HEADS):
        ps = p_c[g * A_GROUP]
        for r in range(1, A_GROUP):
            ps = ps + p_c[g * A_GROUP + r]
        hi = ps.astype(BF16)
        lo = (ps - hi.astype(F32)).astype(BF16)
        score = (jnp.dot(hi, ov, preferred_element_type=F32) + jnp.dot(lo, ov, preferred_element_type=F32))
        score = jnp.where(forced, FORCE_SCORE, score)
        score = jnp.where(jb * SEL_LEN <= tpos, score, -1.0)
        cnt = jnp.zeros((tq, LANES), jnp.int32)
        for k in range(n_sel):
            sk = score[:, k:k + 1]
            ahead = (sk > score) | ((sk == score) & (jb > k))
            cnt = cnt + jnp.where(ahead, 1, 0)
        bias.append(jnp.where(cnt < min(SEL_TOP, n_sel), 0.0, _NEG_SEL).astype(BF16))
    q_aug = jnp.concatenate(
        [qz, jnp.concatenate([bias[hd // A_GROUP] for hd in range(nh)], axis=0)], axis=1)

    tw = lax.broadcasted_iota(jnp.int32, (tq, wbuf), 0)
    iw = lax.broadcasted_iota(jnp.int32, (tq, wbuf), 1)
    tn = lax.broadcasted_iota(jnp.int32, (tq, tq), 0)
    un = lax.broadcasted_iota(jnp.int32, (tq, tq), 1)
    wst = wst_ref[0]
    wnew = wnew_ref[...]
    s_wa = _dot_nt(qz, wst[:, :KV_WIDTH].astype(BF16)).reshape(nh, tq, wbuf)
    s_wb = _dot_nt(qz, wnew[:, :KV_WIDTH].astype(BF16)).reshape(nh, tq, tq)
    o_w = _attend_two(s_wa, (wbuf + tw - iw < WINDOW), wst[:, KV_WIDTH:].astype(BF16),
                      s_wb, un <= tn, wnew[:, KV_WIDTH:].astype(BF16))
    nwin_ref[0, :wbuf - tq, :] = wst[tq:, :]
    nwin_ref[0, wbuf - tq:, :] = wnew

    for cp in copies(b, slot):
        cp.wait()
    seln = seln_ref[...]
    k_aug_t = jnp.concatenate([selbuf[slot, :KV_WIDTH, :].astype(BF16), oh_ref[...]], axis=0)
    nblk = (past + lax.broadcasted_iota(jnp.int32, (tq, LANES), 0)) // SEL_LEN
    oh_new = jnp.where(nblk == lax.broadcasted_iota(jnp.int32, (tq, LANES), 1), 1.0, 0.0)
    kn_aug = jnp.concatenate([seln[:, :KV_WIDTH], oh_new], axis=1).astype(BF16)
    s_sa = jnp.dot(q_aug, k_aug_t, preferred_element_type=F32).reshape(nh, tq, past)
    s_sb = _dot_nt(q_aug, kn_aug).reshape(nh, tq, tq)
    o_s = _attend_two(s_sa, None, selbuf[slot, KV_WIDTH:, :].astype(BF16), s_sb, un <= tn,
                      seln[:, KV_WIDTH:].astype(BF16), v_a_transposed=True)

    o_ref[...] = _gate_rows_to_heads(gt_ref[...], o_c, o_s, o_w)


def _nsa_sample_call(qa, gt, comp, sel_new, win_state, win_new, sel_pool, page_table, T):
    DB, n_pages = page_table.shape
    page = sel_pool.shape[2]
    past = n_pages * page
    wbuf = win_state.shape[1]
    ncp = comp.shape[0] // DB
    w2 = 2 * KV_WIDTH
    assert past % SEL_LEN == 0 and T < CMP_STRIDE and T % 8 == 0 and wbuf == WINDOW and past >= WINDOW
    assert page % LANES == 0
    onehot = (jnp.arange(LANES)[:, None] == jnp.arange(past)[None, :] // SEL_LEN).astype(BF16)
    tok = lambda w: pl.BlockSpec((T, w), lambda b, pt: (b, 0))
    grid_spec = pltpu.PrefetchScalarGridSpec(
        num_scalar_prefetch=1,
        grid=(DB,),
        in_specs=[tok(A_WIDTH), tok(LANES),
                  pl.BlockSpec((ncp, KV_WIDTH), lambda b, pt: (b, 0)),
                  pl.BlockSpec((ncp, KV_WIDTH), lambda b, pt: (b, 1)),
                  tok(w2),
                  pl.BlockSpec((1, wbuf, w2), lambda b, pt: (b, 0, 0)),
                  tok(w2),
                  pl.BlockSpec((LANES, past), lambda b, pt: (0, 0)),
                  pl.BlockSpec(memory_space=pl.ANY)],
        out_specs=[tok(A_WIDTH), pl.BlockSpec((1, wbuf, w2), lambda b, pt: (b, 0, 0))],
        scratch_shapes=[pltpu.VMEM((2, w2, past), F32), pltpu.SemaphoreType.DMA((2,))],
    )
    return pl.pallas_call(
        functools.partial(_nsa_sample_kernel, past=past),
        grid_spec=grid_spec,
        out_shape=[jax.ShapeDtypeStruct((DB * T, A_WIDTH), F32), jax.ShapeDtypeStruct((DB, wbuf, w2), F32)],
        compiler_params=_cparams(("arbitrary",)),
        name="nsa_sample",
    )(page_table, qa, gt, comp, comp, sel_new, win_state, win_new, onehot, sel_pool)


def _nsa_prompt_kernel(q_ref, gt_ref, kc_ref, vc_ref, sel_ref, win_ref, o_ref, *, seq):
    tq = Q_BLOCK
    nh = A_HEADS
    n_sel = seq // SEL_LEN
    s0 = pl.program_id(1) * tq
    qz = _heads_to_rows(q_ref[0])

    kc = kc_ref[0].astype(BF16)
    vc = vc_ref[0].astype(BF16)
    ncp = kc.shape[0]
    s_c = _dot_nt(qz, kc).reshape(nh, tq, ncp)
    tpos = s0 + lax.broadcasted_iota(jnp.int32, (tq, ncp), 0)
    nidx = lax.broadcasted_iota(jnp.int32, (tq, ncp), 1)
    p_c = _softmax_rows(s_c, nidx * CMP_STRIDE + (CMP_LEN - 1) <= tpos)
    o_c = jnp.dot(p_c.reshape(nh * tq, ncp).astype(BF16), vc, preferred_element_type=F32)

    jn = lax.broadcasted_iota(jnp.int32, (n_sel, ncp), 0) * SEL_LEN
    cn = lax.broadcasted_iota(jnp.int32, (n_sel, ncp), 1) * CMP_STRIDE
    ov_t = jnp.where((cn < jn + SEL_LEN) & (cn + CMP_LEN > jn), 1.0, 0.0).astype(BF16)
    jb = lax.broadcasted_iota(jnp.int32, (n_sel, tq), 0)
    tp = s0 + lax.broadcasted_iota(jnp.int32, (n_sel, tq), 1)
    cur = tp // SEL_LEN
    forced = (jb == 0) | (jb == cur) | (jb == cur - 1)
    bias = []
    for g in range(A_KV_HEADS):
        ps = p_c[g * A_GROUP]
        for r in range(1, A_GROUP):
            ps = ps + p_c[g * A_GROUP + r]
        hi = ps.astype(BF16)
        lo = (ps - hi.astype(F32)).astype(BF16)
        score = _dot_nt(ov_t, hi) + _dot_nt(ov_t, lo)
        score = jnp.where(forced, FORCE_SCORE, score)
        score = jnp.where(jb * SEL_LEN <= tp, score, -1.0)
        cnt = jnp.zeros((n_sel, tq), jnp.int32)
        for k in range(n_sel):
            rk = score[k:k + 1, :]
            ahead = (rk > score) | ((rk == score) & (jb > k))
            cnt = cnt + jnp.where(ahead, 1, 0)
        bias_t = jnp.where(cnt < min(SEL_TOP, n_sel), 0.0, _NEG_SEL)
        if n_sel < LANES:
            bias_t = jnp.concatenate([bias_t, jnp.zeros((LANES - n_sel, tq), F32)], axis=0)
        bias.append(bias_t.T.astype(BF16))
    q_aug = jnp.concatenate(
        [qz, jnp.concatenate([bias[hd // A_GROUP] for hd in range(nh)], axis=0)], axis=1)

    kc_n = _SEL_CHUNK

    def chunk(c, carry, causal):
        m, l, acc = carry
        k0 = pl.multiple_of(c * kc_n, kc_n)
        kv = sel_ref[0, pl.ds(k0, kc_n), :]
        kblk = (k0 + lax.broadcasted_iota(jnp.int32, (kc_n, LANES), 0)) // SEL_LEN
        onehot = jnp.where(kblk == lax.broadcasted_iota(jnp.int32, (kc_n, LANES), 1), 1.0, 0.0)
        k_aug = jnp.concatenate([kv[:, :KV_WIDTH], onehot], axis=1).astype(BF16)
        s = _dot_nt(q_aug, k_aug)
        if causal:
            kpos = k0 + lax.broadcasted_iota(jnp.int32, (tq, kc_n), 1)
            qpos = s0 + lax.broadcasted_iota(jnp.int32, (tq, kc_n), 0)
            s = jnp.where((kpos <= qpos)[None], s.reshape(nh, tq, kc_n), _NEG_SEL).reshape(nh * tq, kc_n)
        m_new = jnp.maximum(m, jnp.max(s, axis=-1, keepdims=True))
        a = jnp.exp(m - m_new)
        p = jnp.exp(s - m_new)
        l = a * l + jnp.sum(p, axis=-1, keepdims=True)
        acc = a * acc + jnp.dot(p.astype(BF16), kv[:, KV_WIDTH:].astype(BF16), preferred_element_type=F32)
        return m_new, l, acc

    n_ch = (s0 + tq - 1) // kc_n + 1
    init = (jnp.full((nh * tq, 1), _NEG, F32), jnp.zeros((nh * tq, 1), F32), jnp.zeros((nh * tq, LANES), F32))
    carry = lax.fori_loop(0, n_ch - 1, lambda c, cr: chunk(c, cr, False), init)
    _, l_s, acc_s = chunk(n_ch - 1, carry, True)
    o_s = acc_s / l_s

    wk = WINDOW + tq
    w0 = pl.multiple_of(jnp.clip(s0 - WINDOW, 0, seq - wk), tq)
    kvw = win_ref[0, pl.ds(w0, wk), :]
    s_w = _dot_nt(qz, kvw[:, :KV_WIDTH].astype(BF16)).reshape(nh, tq, wk)
    dist = (s0 + lax.broadcasted_iota(jnp.int32, (tq, wk), 0)) - (w0 + lax.broadcasted_iota(jnp.int32, (tq, wk), 1))
    p_w = _softmax_rows(s_w, (dist >= 0) & (dist < WINDOW))
    o_w = jnp.dot(p_w.reshape(nh * tq, wk).astype(BF16), kvw[:, KV_WIDTH:].astype(BF16),
                  preferred_element_type=F32)

    o_ref[0] = _gate_rows_to_heads(gt_ref[0], o_c, o_s, o_w)


def _nsa_prompt_call(qa, gt, comp, sel_kv, win_kv):
    B, S, _ = qa.shape
    assert S % _SEL_CHUNK == 0 and S >= WINDOW + Q_BLOCK and S // SEL_LEN <= LANES
    ncp = comp.shape[1]
    per_b = lambda r, w: pl.BlockSpec((1, r, w), lambda b, i: (b, 0, 0))
    return pl.pallas_call(
        functools.partial(_nsa_prompt_kernel, seq=S),
        grid=(B, S // Q_BLOCK),
        in_specs=[pl.BlockSpec((1, Q_BLOCK, A_WIDTH), lambda b, i: (b, i, 0)),
                  pl.BlockSpec((1, Q_BLOCK, LANES), lambda b, i: (b, i, 0)),
                  pl.BlockSpec((1, ncp, KV_WIDTH), lambda b, i: (b, 0, 0)),
                  pl.BlockSpec((1, ncp, KV_WIDTH), lambda b, i: (b, 0, 1)),
                  per_b(S, 2 * KV_WIDTH), per_b(S, 2 * KV_WIDTH)],
        out_specs=pl.BlockSpec((1, Q_BLOCK, A_WIDTH), lambda b, i: (b, i, 0)),
        out_shape=jax.ShapeDtypeStruct((B, S, A_WIDTH), F32),
        compiler_params=_cparams(("parallel", "arbitrary")),
        name="nsa_prompt",
    )(qa, gt, comp, comp, sel_kv, win_kv)


def _moe(h2, logits, moe_w, n_experts):
    w_gu, b_gu, w_down, b_down = moe_w
    n, d = h2.shape
    gate_w, dest, src, blk_e, n_used = _route(logits, n_experts)
    yb = _ffn(h2[src], blk_e, n_used, w_gu, b_gu, w_down, b_down)
    return yb[dest.T], gate_w


def kernel(x_prompt, x_sample, cache_cmp, cache_sel, state_win, state_C, state_n, state_m, page_table,
           c_prompt, c_sample, w_ada, b_ada, w_in, b_in, m_norm_g, cmp_pe_k, cmp_w1_k, cmp_w2_k,
           cmp_pe_v, cmp_w1_v, cmp_w2_v, w_out, ln1_g, ln1_b, w_router, b_router, w_gu, b_gu,
           w_down, b_down, ln2_g, ln2_b):
    B, S, d = x_prompt.shape
    DB, T, _ = x_sample.shape
    depth = w_ada.shape[0]
    n_experts = w_router.shape[-1]
    alpha = (2 * depth) ** 0.25
    n_pool, page = cache_cmp.shape[1:3]
    past_len = page_table.shape[1] * page
    wbuf = state_win.shape[2]
    pos_p = jnp.arange(S)
    pos_s = past_len + jnp.arange(T)
    tt_p = min(S, 256)
    bb_s = min(DB, max(1, 256 // T))
    n_cmp_s = (past_len + T - CMP_LEN) // CMP_STRIDE + 1
    assert n_cmp_s < past_len // CMP_STRIDE
    r3 = lambda a: a.reshape(B, S, a.shape[-1])
    kv5 = lambda a, n, t: a.reshape(n, t, 2, A_KV_HEADS, A_DIM)
    y_prompt, y_sample = x_prompt, x_sample
    outs = [[] for _ in range(12)]
    for l in range(depth):
        cmp_l = (cmp_pe_k[l], cmp_w1_k[l], cmp_w2_k[l], cmp_pe_v[l], cmp_w1_v[l], cmp_w2_v[l])
        moe_w = (w_gu[l], b_gu[l], w_down[l], b_down[l])
        post1_w = (w_out[l], ln1_g[l], ln1_b[l], w_router[l], b_router[l])
        w_r, b_r = _prep_w_in(w_in[l], b_in[l])
        cmp_w = _compress_weights(cmp_l)
        c_all = jnp.concatenate([c_prompt, c_sample], axis=0)
        c_all = jnp.pad(c_all, ((0, -c_all.shape[0] % 8), (0, 0)))
        mod_all = _ada(c_all, w_ada[l], b_ada[l]).reshape(-1, 6, d)
        mod_p, mod_s = mod_all[:B], mod_all[B:B + DB]

        mq, gt, qa, cmp_p, sel_p, win_p = _pre(y_prompt, mod_p, pos_p, w_r, b_r, 1, tt_p)
        zc = jnp.zeros((B, M_HEADS, M_DIM, M_DIM), F32)
        mo, C_p, n_p, m_p = _mlstm_call(r3(mq), r3(gt), zc, zc[..., 0], zc[..., 0, 0], m_norm_g[l],
                                        math.gcd(B, 4))
        comp = _compress_prompt(r3(cmp_p), cmp_w)
        ma = _nsa_prompt_call(r3(qa), r3(gt), comp, r3(sel_p), r3(win_p))
        x1_p, h2_p, lg_p = _post1(mo.reshape(B * S, M_WIDTH), ma.reshape(B * S, A_WIDTH), y_prompt, mod_p,
                                  *post1_w, 1, tt_p, alpha)

        mq, gt, qa, cmp_s, sel_s, win_s = _pre(y_sample, mod_s, pos_s, w_r, b_r, bb_s, T)
        mo, C_s, n_s, m_s = _mlstm_call(mq.reshape(DB, T, -1), gt.reshape(DB, T, -1), state_C[l],
                                        state_n[l], state_m[l], m_norm_g[l], math.gcd(DB, 4))
        comp = _compress_sample(cache_cmp[l].reshape(n_pool, page, 2 * KV_WIDTH), page_table, cmp_w, n_cmp_s,
                                math.gcd(DB, 4))
        sel_pool_t = cache_sel[l].transpose(0, 2, 3, 4, 1).reshape(n_pool, 2 * KV_WIDTH, page)
        ma, new_win = _nsa_sample_call(qa, gt, comp, sel_s, state_win[l].reshape(DB, wbuf, 2 * KV_WIDTH),
                                       win_s, sel_pool_t, page_table, T)
        x1_s, h2_s, lg_s = _post1(mo.reshape(DB * T, M_WIDTH), ma, y_sample, mod_s, *post1_w, bb_s, T, alpha)

        yg, gate_w = _moe(jnp.concatenate([h2_p, h2_s], axis=0), jnp.concatenate([lg_p, lg_s], axis=0),
                          moe_w, n_experts)
        y_prompt = _post2(yg, gate_w, 0, x1_p, mod_p, ln2_g[l], ln2_b[l], 1, tt_p, alpha)
        y_sample = _post2(yg, gate_w, B * S, x1_s, mod_s, ln2_g[l], ln2_b[l], bb_s, T, alpha)

        new = (kv5(cmp_p, B, S), kv5(sel_p, B, S), kv5(win_p, B, S)[:, -min(WINDOW, S):], C_p, n_p, m_p,
               kv5(cmp_s, DB, T), kv5(sel_s, DB, T), new_win.reshape(state_win.shape[1:]), C_s, n_s, m_s)
        for lst, v in zip(outs, new):
            lst.append(v)
    return (y_prompt, y_sample, *[jnp.stack(v) for v in outs])
```

```python
import functools
import math

import numpy as np
import jax
import jax.numpy as jnp
from jax import lax
from jax.experimental import pallas as pl
from jax.experimental.pallas import tpu as pltpu

F32 = jnp.float32
BF16 = jnp.bfloat16

M_HEADS = 4
M_DIM = 128
M_WIDTH = M_HEADS * M_DIM
M_CHUNK = 64
A_HEADS = 8
A_KV_HEADS = 2
A_GROUP = A_HEADS // A_KV_HEADS
A_DIM = 64
A_WIDTH = A_HEADS * A_DIM
KV_WIDTH = A_KV_HEADS * A_DIM
CMP_LEN = 32
CMP_STRIDE = 16
SEL_LEN = 64
SEL_TOP = 16
WINDOW = 512
Q_BLOCK = 128
FORCE_SCORE = 1.0e4
ROPE_THETA = 500000.0
ROPE_DIM = A_DIM // 4
TOP_K = 4
SWIGLU_LIMIT = 7.0
SWIGLU_ALPHA = 1.702
LN_EPS = 1e-5
LANES = 128
MOE_ROWS = 256
VMEM_LIMIT = 56 * 1024 * 1024

_O_MQ = 0
_O_IF = 4 * M_WIDTH
_O_QA = _O_IF + 2 * M_HEADS
_O_KV = _O_QA + A_WIDTH
_O_GA = _O_KV + 6 * KV_WIDTH
_N_IN = _O_GA + 3 * A_HEADS
_R_QA = 4 * M_WIDTH
_R_KV = _R_QA + A_WIDTH
_R_GT = _R_KV + 6 * KV_WIDTH
_R_END = _R_GT + LANES


def _cparams(sem):
    return pltpu.CompilerParams(dimension_semantics=sem, vmem_limit_bytes=VMEM_LIMIT)


def _ln_core(x):
    mu = jnp.mean(x, axis=-1, keepdims=True)
    xc = x - mu
    var = jnp.mean(xc * xc, axis=-1, keepdims=True)
    return xc * lax.rsqrt(var + LN_EPS)


def _ada_kernel(c_ref, w_ref, b_ref, o_ref):
    o_ref[...] = jnp.dot(c_ref[...].astype(BF16), w_ref[...].astype(BF16),
                         preferred_element_type=F32) + b_ref[...]


def _ada(c, w_ada, b_ada):
    n, d = c.shape
    cols = w_ada.shape[1]
    return pl.pallas_call(
        _ada_kernel,
        grid=(cols // d,),
        in_specs=[pl.BlockSpec((n, d), lambda j: (0, 0)),
                  pl.BlockSpec((d, d), lambda j: (0, j)),
                  pl.BlockSpec((1, d), lambda j: (0, j))],
        out_specs=pl.BlockSpec((n, d), lambda j: (0, j)),
        out_shape=jax.ShapeDtypeStruct((n, cols), F32),
        compiler_params=_cparams(("arbitrary",)),
        name="ada",
    )(c, w_ada, b_ada.reshape(1, cols))


def _rope_apply(v, cos, sa, sb):
    reps = v.shape[1] // LANES
    tile = lambda t: t if reps == 1 else jnp.concatenate([t] * reps, axis=1)
    w = v.shape[1]
    return (v * tile(cos) + pltpu.roll(v, w - ROPE_DIM // 2, 1) * tile(sa)
            + pltpu.roll(v, ROPE_DIM // 2, 1) * tile(sb))


def _pre_kernel(x_ref, mod_ref, cos_ref, sa_ref, sb_ref, w_ref, b_ref,
                mq_ref, gt_ref, qa_ref, cmp_ref, sel_ref, win_ref, *t_refs):
    bb, tt, d = x_ref.shape
    mod = mod_ref[...]
    h = _ln_core(x_ref[...]) * (1.0 + mod[:, 1:2, :]) + mod[:, 0:1, :]
    h = h.reshape(bb * tt, d).astype(BF16)
    z = jnp.dot(h, w_ref[...], preferred_element_type=F32) + b_ref[...]
    cos, sa, sb = cos_ref[...], sa_ref[...], sb_ref[...]
    mq_ref[...] = z[:, :_R_QA]
    gt_ref[...] = z[:, _R_GT:_R_END]
    qa_ref[...] = _rope_apply(z[:, _R_QA:_R_KV], cos, sa, sb)
    for n, ref in enumerate((cmp_ref, sel_ref, win_ref)):
        o = _R_KV + 2 * KV_WIDTH * n
        kv = jnp.concatenate([_rope_apply(z[:, o:o + KV_WIDTH], cos, sa, sb),
                              z[:, o + KV_WIDTH:o + 2 * KV_WIDTH]], axis=1)
        ref[...] = kv
        if t_refs:
            t_refs[n][0] = kv.T


def _rope_tables(pos):
    half = ROPE_DIM // 2
    inv = ROPE_THETA ** (-jnp.arange(0, ROPE_DIM, 2, dtype=F32) / ROPE_DIM)
    ang = pos.astype(F32)[:, None] * inv[None, :]
    cos, sin = jnp.cos(ang), jnp.sin(ang)
    n = pos.shape[0]
    one = jnp.ones((n, A_DIM - ROPE_DIM), F32)
    zero = jnp.zeros((n, A_DIM - ROPE_DIM), F32)
    zh = jnp.zeros((n, half), F32)
    cos_t = jnp.concatenate([cos, cos, one], axis=1)
    sa_t = jnp.concatenate([-sin, zh, zero], axis=1)
    sb_t = jnp.concatenate([zh, sin, zero], axis=1)
    two = lambda t: jnp.concatenate([t, t], axis=1)
    return two(cos_t), two(sa_t), two(sb_t)


def _prep_w_in(w_in, b_in):
    pad = LANES - 2 * M_HEADS - 3 * A_HEADS
    cat = lambda a: jnp.concatenate(
        [a[..., _O_MQ:_O_IF], a[..., _O_QA:_O_GA], a[..., _O_IF:_O_QA], a[..., _O_GA:_N_IN],
         jnp.zeros(a.shape[:-1] + (pad,), a.dtype)], axis=-1)
    return cat(w_in).astype(BF16), cat(b_in[None, :])


def _pre(x, mod, pos, w_r, b_r, bb, tt, feature_major=False):
    B, T, d = x.shape
    nt = T // tt
    rows = bb * tt
    cos, sa, sb = _rope_tables(pos)
    if bb > 1:
        cos, sa, sb = (jnp.tile(t, (bb, 1)) for t in (cos, sa, sb))
    n_tok = B * T
    tab = pl.BlockSpec((rows, LANES), lambda i, j: (j, 0))
    row = lambda w: pl.BlockSpec((rows, w), lambda i, j: (i * nt + j, 0))
    widths = (_R_QA, LANES, A_WIDTH, 2 * KV_WIDTH, 2 * KV_WIDTH, 2 * KV_WIDTH)
    out_specs = [row(w) for w in widths]
    out_shape = [jax.ShapeDtypeStruct((n_tok, w), F32) for w in widths]
    if feature_major:
        assert bb == 1 and tt % LANES == 0
        out_specs += [pl.BlockSpec((1, 2 * KV_WIDTH, tt), lambda i, j: (i, 0, j))] * 3
        out_shape += [jax.ShapeDtypeStruct((B, 2 * KV_WIDTH, T), F32)] * 3
    return pl.pallas_call(
        _pre_kernel,
        grid=(B // bb, nt),
        in_specs=[pl.BlockSpec((bb, tt, d), lambda i, j: (i, j, 0)),
                  pl.BlockSpec((bb, 6, d), lambda i, j: (i, 0, 0)),
                  tab, tab, tab,
                  pl.BlockSpec((d, _R_END), lambda i, j: (0, 0)),
                  pl.BlockSpec((1, _R_END), lambda i, j: (0, 0))],
        out_specs=out_specs,
        out_shape=out_shape,
        compiler_params=_cparams(("parallel", "arbitrary")),
        name="pre",
    )(x, mod, cos, sa, sb, w_r, b_r)


def _post1_kernel(mo_ref, ao_ref, x_ref, mod_ref, wo_ref, g_ref, b_ref, wr_ref, br_ref,
                  x1_ref, h2_ref, lg_ref, *, alpha):
    bb, tt, d = x_ref.shape
    mod = mod_ref[...]
    mixin = jnp.concatenate([mo_ref[...], ao_ref[...]], axis=1).astype(BF16)
    mix = jnp.dot(mixin, wo_ref[...], preferred_element_type=F32).reshape(bb, tt, d)
    x1 = _ln_core(alpha * x_ref[...] + mod[:, 2:3, :] * mix) * g_ref[...] + b_ref[...]
    h2 = _ln_core(x1) * (1.0 + mod[:, 4:5, :]) + mod[:, 3:4, :]
    x1_ref[...] = x1
    h2f = h2.reshape(bb * tt, d)
    h2_ref[...] = h2f
    lg_ref[...] = jnp.dot(h2f, wr_ref[...], preferred_element_type=F32,
                          precision=lax.Precision.HIGHEST) + br_ref[...]


def _post1(mo, ao, x, mod, w_out, ln_g, ln_b, w_router, b_router, bb, tt, alpha):
    B, T, d = x.shape
    nt = T // tt
    rows = bb * tt
    ne = w_router.shape[1]
    wr = jnp.pad(w_router, ((0, 0), (0, LANES - ne)))
    br = jnp.pad(b_router, (0, LANES - ne), constant_values=-jnp.inf).reshape(1, LANES)
    full = lambda *s: pl.BlockSpec(s, lambda i, j: (0,) * len(s))
    row = lambda w: pl.BlockSpec((rows, w), lambda i, j: (i * nt + j, 0))
    return pl.pallas_call(
        functools.partial(_post1_kernel, alpha=alpha),
        grid=(B // bb, nt),
        in_specs=[row(M_WIDTH), row(A_WIDTH),
                  pl.BlockSpec((bb, tt, d), lambda i, j: (i, j, 0)),
                  pl.BlockSpec((bb, 6, d), lambda i, j: (i, 0, 0)),
                  full(M_WIDTH + A_WIDTH, d), full(1, d), full(1, d), full(d, LANES), full(1, LANES)],
        out_specs=[pl.BlockSpec((bb, tt, d), lambda i, j: (i, j, 0)), row(d), row(LANES)],
        out_shape=[jax.ShapeDtypeStruct((B, T, d), F32),
                   jax.ShapeDtypeStruct((B * T, d), F32),
                   jax.ShapeDtypeStruct((B * T, LANES), F32)],
        compiler_params=_cparams(("parallel", "arbitrary")),
        name="post1",
    )(mo, ao, x, mod, w_out.astype(BF16), ln_g.reshape(1, d), ln_b.reshape(1, d), wr, br)


def _ffn_kernel(be_ref, nu_ref, x_ref, wgu_ref, bgu_ref, wd_ref, bd_ref, y_ref, wgu_bf, wd_bf):
    i = pl.program_id(0)
    dff = wd_ref.shape[1]

    @pl.when((i == 0) | (be_ref[i] != be_ref[jnp.maximum(i - 1, 0)]))
    def _():
        wgu_bf[...] = wgu_ref[0].astype(BF16)
        wd_bf[...] = wd_ref[0].astype(BF16)

    @pl.when(i < nu_ref[0])
    def _():
        gu = jnp.dot(x_ref[...].astype(BF16), wgu_bf[...], preferred_element_type=F32) + bgu_ref[0]
        g = jnp.minimum(gu[:, :dff], SWIGLU_LIMIT)
        u = jnp.clip(gu[:, dff:], -SWIGLU_LIMIT, SWIGLU_LIMIT)
        act = (u + 1.0) * (g * jax.nn.sigmoid(SWIGLU_ALPHA * g))
        y_ref[...] = jnp.dot(act.astype(BF16), wd_bf[...], preferred_element_type=F32) + bd_ref[0]

    @pl.when(i >= nu_ref[0])
    def _():
        y_ref[...] = jnp.zeros_like(y_ref)


def _ffn(xb, blk_e, n_used, w_gu, b_gu, w_down, b_down):
    rows, d = xb.shape
    ne, _, f2 = w_gu.shape
    dff = w_down.shape[1]
    nb = rows // MOE_ROWS
    grid_spec = pltpu.PrefetchScalarGridSpec(
        num_scalar_prefetch=2,
        grid=(nb,),
        in_specs=[pl.BlockSpec((MOE_ROWS, d), lambda i, be, nu: (i, 0)),
                  pl.BlockSpec((1, d, f2), lambda i, be, nu: (be[i], 0, 0)),
                  pl.BlockSpec((1, 1, f2), lambda i, be, nu: (be[i], 0, 0)),
                  pl.BlockSpec((1, dff, d), lambda i, be, nu: (be[i], 0, 0)),
                  pl.BlockSpec((1, 1, d), lambda i, be, nu: (be[i], 0, 0))],
        out_specs=pl.BlockSpec((MOE_ROWS, d), lambda i, be, nu: (i, 0)),
        scratch_shapes=[pltpu.VMEM((d, f2), BF16), pltpu.VMEM((dff, d), BF16)],
    )
    return pl.pallas_call(
        _ffn_kernel,
        grid_spec=grid_spec,
        out_shape=jax.ShapeDtypeStruct((rows, d), F32),
        compiler_params=_cparams(("arbitrary",)),
        name="ffn",
    )(blk_e, n_used, xb, w_gu, b_gu.reshape(ne, 1, f2), w_down, b_down.reshape(ne, 1, d))


def _post2_kernel(yg_ref, gw_ref, x1_ref, mod_ref, g_ref, b_ref, y_ref, *, alpha):
    bb, tt, d = x1_ref.shape
    gw = gw_ref[...]
    f = yg_ref[0] * gw[:, 0:1]
    for k in range(1, TOP_K):
        f = f + yg_ref[k] * gw[:, k:k + 1]
    y = alpha * x1_ref[...] + mod_ref[...][:, 5:6, :] * f.reshape(bb, tt, d)
    y_ref[...] = _ln_core(y) * g_ref[...] + b_ref[...]


def _post2(yg, gw, row0, x1, mod, ln_g, ln_b, bb, tt, alpha):
    B, T, d = x1.shape
    rows = bb * tt
    nt = T // tt
    blk0 = row0 // rows
    assert row0 % rows == 0
    full = lambda *s: pl.BlockSpec(s, lambda i, j: (0,) * len(s))
    return pl.pallas_call(
        functools.partial(_post2_kernel, alpha=alpha),
        grid=(B // bb, nt),
        in_specs=[pl.BlockSpec((TOP_K, rows, d), lambda i, j: (0, blk0 + i * nt + j, 0)),
                  pl.BlockSpec((rows, LANES), lambda i, j: (blk0 + i * nt + j, 0)),
                  pl.BlockSpec((bb, tt, d), lambda i, j: (i, j, 0)),
                  pl.BlockSpec((bb, 6, d), lambda i, j: (i, 0, 0)),
                  full(1, d), full(1, d)],
        out_specs=pl.BlockSpec((bb, tt, d), lambda i, j: (i, j, 0)),
        out_shape=jax.ShapeDtypeStruct((B, T, d), F32),
        compiler_params=_cparams(("parallel", "arbitrary")),
        name="post2",
    )(yg, gw, x1, mod, ln_g.reshape(1, d), ln_b.reshape(1, d))


def _lane_prefix_sum(x):
    lane = lax.broadcasted_iota(jnp.int32, x.shape, 1)
    s = 1
    while s < LANES:
        x = x + jnp.where(lane >= s, pltpu.roll(x, s, 1), 0.0)
        s *= 2
    return x


def _route_kernel(lg_ref, dest_ref, gw_ref, cnt_ref, counts, running):
    phase = pl.program_id(0)
    blk = pl.program_id(1)
    r = lg_ref.shape[0]
    lane = lax.broadcasted_iota(jnp.int32, (r, LANES), 1)

    @pl.when((phase == 0) & (blk == 0))
    def _():
        counts[...] = jnp.zeros_like(counts)

    @pl.when((phase == 1) & (blk == 0))
    def _():
        running[...] = jnp.zeros_like(running)

    vals = lg_ref[...]
    onehots, tops = [], []
    for k in range(TOP_K):
        m = jnp.max(vals, axis=1, keepdims=True)
        idx = jnp.min(jnp.where(vals == m, lane, LANES), axis=1, keepdims=True)
        hit = lane == idx
        onehots.append(hit)
        tops.append(m)
        vals = jnp.where(hit, -jnp.inf, vals)
    ohf = [jnp.where(h, 1.0, 0.0) for h in onehots]
    block_cnt = [jnp.sum(o, axis=0, keepdims=True) for o in ohf]

    @pl.when(phase == 0)
    def _():
        counts[...] += block_cnt[0] + block_cnt[1] + block_cnt[2] + block_cnt[3]

    @pl.when(phase == 1)
    def _():
        cnt = counts[...]
        padded = jnp.ceil(cnt * (1.0 / MOE_ROWS)) * MOE_ROWS
        pad_start = _lane_prefix_sum(padded) - padded
        es = [jnp.exp(t - tops[0]) for t in tops]
        den = es[0] + es[1] + es[2] + es[3]
        ti = lax.broadcasted_iota(jnp.int32, (r, r), 0)
        tj = lax.broadcasted_iota(jnp.int32, (r, r), 1)
        before = jnp.where(tj < ti, 1.0, 0.0).astype(BF16)
        base = pad_start + running[...]
        dest = jnp.zeros((r, LANES), F32)
        gw = jnp.zeros((r, LANES), F32)
        for k in range(TOP_K):
            rank = jnp.dot(before, ohf[k].astype(BF16), preferred_element_type=F32)
            d_k = jnp.sum(ohf[k] * (base + rank), axis=1, keepdims=True)
            dest = jnp.where(lane == k, d_k, dest)
            gw = jnp.where(lane == k, es[k] / den, gw)
            base = base + block_cnt[k]
        running[...] = base - pad_start
        dest_ref[...] = dest.astype(jnp.int32)
        gw_ref[...] = gw
        cnt_ref[...] = jnp.broadcast_to(cnt, cnt_ref.shape)


def _route(logits, n_experts, rows=256):
    n = logits.shape[0]
    assert n % rows == 0
    nb = n // rows
    tok = pl.BlockSpec((rows, LANES), lambda p, i: (i, 0))
    out = pl.BlockSpec((rows, LANES), lambda p, i: (i * p, 0))
    dest, gw, cnt = pl.pallas_call(
        _route_kernel,
        grid=(2, nb),
        in_specs=[tok],
        out_specs=[out, out, pl.BlockSpec((8, LANES), lambda p, i: (0, 0))],
        out_shape=[jax.ShapeDtypeStruct((n, LANES), jnp.int32), jax.ShapeDtypeStruct((n, LANES), F32),
                   jax.ShapeDtypeStruct((8, LANES), F32)],
        scratch_shapes=[pltpu.VMEM((1, LANES), F32), pltpu.VMEM((1, LANES), F32)],
        compiler_params=_cparams(("arbitrary", "arbitrary")),
        name="route",
    )(logits)
    counts = cnt[0, :n_experts].astype(jnp.int32)
    pad_end = jnp.cumsum((counts + MOE_ROWS - 1) // MOE_ROWS * MOE_ROWS)
    n_slots = n * TOP_K
    n_blocks = -(-n_slots // MOE_ROWS) + n_experts
    n_used = pad_end[-1] // MOE_ROWS
    blk = jnp.minimum(jnp.arange(n_blocks, dtype=jnp.int32), n_used - 1) * MOE_ROWS
    blk_e = jnp.minimum(jnp.sum(pad_end[None, :] <= blk[:, None], axis=1), n_experts - 1).astype(jnp.int32)
    dest4 = dest[:, :TOP_K]
    src = jnp.zeros((n_blocks * MOE_ROWS,), jnp.int32).at[dest4.reshape(-1)].set(
        jnp.arange(n_slots, dtype=jnp.int32) // TOP_K)
    return gw, dest4, src, blk_e, n_used.reshape(1).astype(jnp.int32)


def _log_sigmoid(x):
    return jnp.minimum(x, 0.0) - jnp.log(1.0 + jnp.exp(-jnp.abs(x)))


def _mlstm_kernel(mq_ref, gt_ref, gtt_ref, c0_ref, n0_ref, m0_ref, g_ref, mo_ref, c_ref, n_ref, m_ref):
    bb, L, _ = mq_ref.shape
    d = M_DIM

    @pl.when(pl.program_id(1) == 0)
    def _():
        c_ref[...] = c0_ref[...]
        n_ref[...] = n0_ref[...]
        m_ref[...] = m0_ref[...]

    tt = lax.broadcasted_iota(jnp.int32, (L, L), 0)
    ss = lax.broadcasted_iota(jnp.int32, (L, L), 1)
    causal = ss <= tt
    for s in range(bb):
        gcol = gt_ref[s]
        grow = gtt_ref[s, 0]
        for h in range(M_HEADS):
            q = mq_ref[s, :, h * d:(h + 1) * d]
            k = mq_ref[s, :, M_WIDTH + h * d:M_WIDTH + (h + 1) * d] * (d ** -0.5)
            v = mq_ref[s, :, 2 * M_WIDTH + h * d:2 * M_WIDTH + (h + 1) * d]
            o_pre = mq_ref[s, :, 3 * M_WIDTH + h * d:3 * M_WIDTH + (h + 1) * d]
            i_c = gcol[:, h:h + 1]
            lf_c = _log_sigmoid(gcol[:, M_HEADS + h:M_HEADS + h + 1])
            i_r = grow[h:h + 1, :]
            lf_r = _log_sigmoid(grow[M_HEADS + h:M_HEADS + h + 1, :])
            b_c = jnp.sum(jnp.where(causal, lf_r, 0.0), axis=1, keepdims=True)
            b_r = jnp.sum(jnp.where(causal, 0.0, lf_c), axis=0, keepdims=True) + lf_r
            c_old = c_ref[s, h]
            n_old = n_ref[s, h:h + 1, :]
            m_old = m_ref[s, :, h:h + 1]
            dmat = jnp.where(causal, b_c - b_r + i_r, _NEG)
            m_t = jnp.maximum(b_c + m_old, jnp.max(dmat, axis=1, keepdims=True))
            inter = jnp.exp(b_c + m_old - m_t)
            qb, kb, vb = q.astype(BF16), k.astype(BF16), v.astype(BF16)
            a = jnp.exp(dmat - m_t) * _dot_nt(qb, kb)
            num = inter * _dot_nt(qb, c_old.astype(BF16)) + jnp.dot(a.astype(BF16), vb, preferred_element_type=F32)
            den = inter * jnp.sum(q * n_old, axis=1, keepdims=True) + jnp.sum(a, axis=1, keepdims=True)
            hv = num / jnp.maximum(jnp.abs(den), jnp.exp(-m_t))
            m_new = m_t[L - 1:L, :]
            b_last = b_c[L - 1:L, :]
            w_src = jnp.exp(b_last - b_c + i_c - m_new)
            w_old = jnp.exp(b_last + m_old - m_new)
            c_ref[s, h] = w_old * c_old + lax.dot_general(
                (w_src * v).astype(BF16), kb, (((0,), (0,)), ((), ())), preferred_element_type=F32)
            n_ref[s, h:h + 1, :] = w_old * n_old + jnp.sum(w_src * k, axis=0, keepdims=True)
            m_ref[s, :, h:h + 1] = m_new
            mo_ref[s, :, h * d:(h + 1) * d] = (_ln_core(hv) * g_ref[:, h * d:(h + 1) * d]) * jax.nn.sigmoid(o_pre)


def _mlstm_call(mq, gt, C0, n0, m0, g, bb):
    B, T, _ = mq.shape
    L = math.gcd(T, M_CHUNK)
    nc = T // L
    gtt = gt[:, :, :2 * M_HEADS].reshape(B, nc, L, 2 * M_HEADS).transpose(0, 1, 3, 2)
    st4 = pl.BlockSpec((bb, M_HEADS, M_DIM, M_DIM), lambda i, c: (i, 0, 0, 0))
    st3 = pl.BlockSpec((bb, M_HEADS, M_DIM), lambda i, c: (i, 0, 0))
    st2 = pl.BlockSpec((bb, 1, M_HEADS), lambda i, c: (i, 0, 0))
    mo, C, n, m = pl.pallas_call(
        _mlstm_kernel,
        grid=(B // bb, nc),
        in_specs=[pl.BlockSpec((bb, L, 4 * M_WIDTH), lambda i, c: (i, c, 0)),
                  pl.BlockSpec((bb, L, LANES), lambda i, c: (i, c, 0)),
                  pl.BlockSpec((bb, 1, 2 * M_HEADS, L), lambda i, c: (i, c, 0, 0)),
                  st4, st3, st2,
                  pl.BlockSpec((1, M_WIDTH), lambda i, c: (0, 0))],
        out_specs=[pl.BlockSpec((bb, L, M_WIDTH), lambda i, c: (i, c, 0)), st4, st3, st2],
        out_shape=[jax.ShapeDtypeStruct((B, T, M_WIDTH), F32),
                   jax.ShapeDtypeStruct(C0.shape, F32), jax.ShapeDtypeStruct(n0.shape, F32),
                   jax.ShapeDtypeStruct((B, 1, M_HEADS), F32)],
        compiler_params=_cparams(("parallel", "arbitrary")),
        name="mlstm",
    )(mq, gt, gtt, C0, n0, m0.reshape(B, 1, M_HEADS), g.reshape(1, M_WIDTH))
    return mo, C, n, m.reshape(B, M_HEADS)


def _gelu_tanh(x):
    return x * (0.5 * (1.0 + jnp.tanh(math.sqrt(2.0 / math.pi) * (x + 0.044715 * (x * x * x)))))


def _compress_body(load, pe_ref, w1_ref, w2_ref, rows_total, nh, n_cmp):
    w = KV_WIDTH
    half = CMP_LEN // 2
    pa = jnp.zeros((rows_total, w), F32)
    pb = jnp.zeros((rows_total, w), F32)
    for j in range(half):
        xj = load(j)
        pa = pa + jnp.dot((xj + pe_ref[j:j + 1, :]).astype(BF16), w1_ref[j], preferred_element_type=F32)
        pb = pb + jnp.dot((xj + pe_ref[half + j:half + j + 1, :]).astype(BF16), w1_ref[half + j],
                          preferred_element_type=F32)
    hid = pa + pltpu.roll(pb, rows_total - 1, 0)
    y = jnp.dot(_gelu_tanh(hid).astype(BF16), w2_ref[...], preferred_element_type=F32)
    rows = lax.broadcasted_iota(jnp.int32, (rows_total, w), 0)
    assert nh & (nh - 1) == 0
    return jnp.where((rows & (nh - 1)) < n_cmp, y, 0.0)


def _compress_kernel(xk_ref, xv_ref, pe_ref, w1_ref, w2_ref, o_ref, *, n_cmp):
    nh = xk_ref.shape[1] // CMP_STRIDE
    for kv, x_ref in enumerate((xk_ref, xv_ref)):
        o_ref[0, :, kv * KV_WIDTH:(kv + 1) * KV_WIDTH] = _compress_body(
            lambda j: x_ref[0, pl.ds(j, nh, stride=CMP_STRIDE), :], pe_ref.at[kv], w1_ref.at[kv], w2_ref.at[kv],
            nh, nh, n_cmp)


def _compress_sample_kernel(pt_ref, pool_ref, pe_ref, w1_ref, w2_ref, o_ref, xbuf, sem, *, seqs, n_pages, n_cmp):
    i = pl.program_id(0)
    slot = i % 2
    page = pool_ref.shape[1]
    nh = n_pages * page // CMP_STRIDE

    def copies(step, sl):
        return [pltpu.make_async_copy(pool_ref.at[pt_ref[step * seqs + s, p], :, pl.ds(kv * KV_WIDTH, KV_WIDTH)],
                                      xbuf.at[sl, kv, pl.ds((s * n_pages + p) * page, page), :], sem.at[sl])
                for s in range(seqs) for p in range(n_pages) for kv in range(2)]

    @pl.when(i == 0)
    def _():
        for cp in copies(0, 0):
            cp.start()

    @pl.when(i + 1 < pl.num_programs(0))
    def _():
        for cp in copies(i + 1, 1 - slot):
            cp.start()

    for cp in copies(i, slot):
        cp.wait()
    for kv in range(2):
        o_ref[:, kv * KV_WIDTH:(kv + 1) * KV_WIDTH] = _compress_body(
            lambda j: xbuf[slot, kv, pl.ds(j, seqs * nh, stride=CMP_STRIDE), :],
            pe_ref.at[kv], w1_ref.at[kv], w2_ref.at[kv], seqs * nh, nh, n_cmp)


def _compress_sample(pool, page_table, cmp_w, n_cmp, seqs):
    n_pool, page, w = pool.shape
    DB, n_pages = page_table.shape
    nh = n_pages * page // CMP_STRIDE
    pe4, w1, w2 = cmp_w
    full = lambda *s: pl.BlockSpec(s, lambda i, pt: (0,) * len(s))
    grid_spec = pltpu.PrefetchScalarGridSpec(
        num_scalar_prefetch=1,
        grid=(DB // seqs,),
        in_specs=[pl.BlockSpec(memory_space=pl.ANY), full(*pe4.shape), full(*w1.shape), full(*w2.shape)],
        out_specs=pl.BlockSpec((seqs * nh, w), lambda i, pt: (i, 0)),
        scratch_shapes=[pltpu.VMEM((2, 2, seqs * n_pages * page, KV_WIDTH), F32), pltpu.SemaphoreType.DMA((2,))],
    )
    return pl.pallas_call(
        functools.partial(_compress_sample_kernel, seqs=seqs, n_pages=n_pages, n_cmp=n_cmp),
        grid_spec=grid_spec,
        out_shape=jax.ShapeDtypeStruct((DB * nh, w), F32),
        compiler_params=_cparams(("arbitrary",)),
        name="compress_sample",
    )(page_table, pool, pe4, w1, w2)


def _block_diag2(a):
    z = jnp.zeros_like(a)
    return jnp.concatenate([jnp.concatenate([a, z], -1), jnp.concatenate([z, a], -1)], -2)


def _compress_weights(cmp_params):
    pe_k, w1_k, w2_k, pe_v, w1_v, w2_v = cmp_params
    pe = jnp.stack([jnp.concatenate([p, p], axis=1) for p in (pe_k, pe_v)])
    w1 = jnp.stack([_block_diag2(w.reshape(CMP_LEN, A_DIM, A_DIM)) for w in (w1_k, w1_v)]).astype(BF16)
    w2 = jnp.stack([_block_diag2(w) for w in (w2_k, w2_v)]).astype(BF16)
    return pe, w1, w2


def _compress_prompt(cmp_rows, cmp_w):
    B, L, w = cmp_rows.shape
    nh = L // CMP_STRIDE
    n_cmp = (L - CMP_LEN) // CMP_STRIDE + 1
    pe4, w1, w2 = cmp_w
    full = lambda *s: pl.BlockSpec(s, lambda b: (0,) * len(s))
    return pl.pallas_call(
        functools.partial(_compress_kernel, n_cmp=n_cmp),
        grid=(B,),
        in_specs=[pl.BlockSpec((1, L, KV_WIDTH), lambda b: (b, 0, 0)),
                  pl.BlockSpec((1, L, KV_WIDTH), lambda b: (b, 0, 1)),
                  full(*pe4.shape), full(*w1.shape), full(*w2.shape)],
        out_specs=pl.BlockSpec((1, nh, w), lambda b: (b, 0, 0)),
        out_shape=jax.ShapeDtypeStruct((B, nh, w), F32),
        compiler_params=_cparams(("parallel",)),
        name="compress_prompt",
    )(cmp_rows, cmp_rows, pe4, w1, w2)


_NEG = -1e30
_NEG_SEL = -1e9
_SEL_CHUNK = 256


def _dot_nt(a, b):
    return lax.dot_general(a, b, (((1,), (1,)), ((), ())), preferred_element_type=F32)


def _softmax_rows(s, valid):
    s = jnp.where(valid[None], s, _NEG)
    m = jnp.max(s, axis=-1, keepdims=True)
    e = jnp.where(valid[None], jnp.exp(s - m), 0.0)
    return e / jnp.maximum(jnp.sum(e, axis=-1, keepdims=True), jnp.finfo(jnp.float32).tiny)


def _heads_to_rows(q):
    tq = q.shape[0]
    lane = lax.broadcasted_iota(jnp.int32, (tq, LANES), 1)
    q = q * (A_DIM ** -0.5)
    rows = []
    for hd in range(A_HEADS):
        g = hd // A_GROUP
        tile = q[:, LANES * (hd // 2):LANES * (hd // 2 + 1)]
        if hd % 2 != g:
            tile = pltpu.roll(tile, A_DIM, 1)
        keep = (lane < A_DIM) if g == 0 else (lane >= A_DIM)
        rows.append(jnp.where(keep, tile, 0.0))
    return jnp.concatenate(rows, axis=0).astype(BF16)


def _gate_rows_to_heads(gt, o_c, o_s, o_w):
    tq = gt.shape[0]
    lane = lax.broadcasted_iota(jnp.int32, (tq, LANES), 1)
    gs = jax.nn.sigmoid(gt)
    tiles = []
    for pair in range(A_HEADS // 2):
        g = (2 * pair) // A_GROUP
        mixed = []
        for hd in (2 * pair, 2 * pair + 1):
            c = 2 * M_HEADS + 3 * hd
            r = slice(hd * tq, (hd + 1) * tq)
            mixed.append(gs[:, c:c + 1] * o_c[r] + gs[:, c + 1:c + 2] * o_s[r] + gs[:, c + 2:c + 3] * o_w[r])
        a, b = mixed
        if g == 0:
            tiles.append(jnp.where(lane < A_DIM, a, pltpu.roll(b, A_DIM, 1)))
        else:
            tiles.append(jnp.where(lane < A_DIM, pltpu.roll(a, A_DIM, 1), b))
    return jnp.concatenate(tiles, axis=1)


def _attend_two(s_a, ok_a, v_a, s_b, ok_b, v_b, v_a_transposed=False):
    nh, tq = s_a.shape[:2]
    if ok_a is not None:
        s_a = jnp.where(ok_a[None], s_a, _NEG)
    s_b = jnp.where(ok_b[None], s_b, _NEG)
    m = jnp.maximum(jnp.max(s_a, axis=-1, keepdims=True), jnp.max(s_b, axis=-1, keepdims=True))
    e_a = jnp.exp(s_a - m)
    e_b = jnp.exp(s_b - m)
    l = jnp.sum(e_a, axis=-1, keepdims=True) + jnp.sum(e_b, axis=-1, keepdims=True)
    e_a = e_a.reshape(nh * tq, -1).astype(BF16)
    o_a = _dot_nt(e_a, v_a) if v_a_transposed else jnp.dot(e_a, v_a, preferred_element_type=F32)
    o = o_a + jnp.dot(e_b.reshape(nh * tq, -1).astype(BF16), v_b, preferred_element_type=F32)
    return o / l.reshape(nh * tq, 1)


def _nsa_sample_kernel(pt_ref, q_ref, gt_ref, kc_ref, vc_ref, seln_ref, wst_ref, wnew_ref, oh_ref, pool_ref,
                       o_ref, nwin_ref, selbuf, sem, *, past):
    b = pl.program_id(0)
    slot = b % 2
    tq = q_ref.shape[0]
    nh = A_HEADS
    n_pages = pt_ref.shape[1]
    page = pool_ref.shape[2]
    wbuf = wst_ref.shape[1]
    n_sel = -(-(past + tq) // SEL_LEN)

    def copies(seq, sl):
        return [pltpu.make_async_copy(pool_ref.at[pt_ref[seq, p]], selbuf.at[sl, :, pl.ds(p * page, page)],
                                      sem.at[sl]) for p in range(n_pages)]

    @pl.when(b == 0)
    def _():
        for cp in copies(0, 0):
            cp.start()

    @pl.when(b + 1 < pl.num_programs(0))
    def _():
        for cp in copies(b + 1, 1 - slot):
            cp.start()

    qz = _heads_to_rows(q_ref[...])
    tpos = past + lax.broadcasted_iota(jnp.int32, (tq, LANES), 0)

    kc = kc_ref[...].astype(BF16)
    vc = vc_ref[...].astype(BF16)
    ncp = kc.shape[0]
    s_c = _dot_nt(qz, kc).reshape(nh, tq, ncp)
    tp_c = past + lax.broadcasted_iota(jnp.int32, (tq, ncp), 0)
    nidx = lax.broadcasted_iota(jnp.int32, (tq, ncp), 1)
    p_c = _softmax_rows(s_c, nidx * CMP_STRIDE + (CMP_LEN - 1) <= tp_c)
    o_c = jnp.dot(p_c.reshape(nh * tq, ncp).astype(BF16), vc, preferred_element_type=F32)

    cn = lax.broadcasted_iota(jnp.int32, (ncp, LANES), 0) * CMP_STRIDE
    jn = lax.broadcasted_iota(jnp.int32, (ncp, LANES), 1) * SEL_LEN
    ov = jnp.where((cn < jn + SEL_LEN) & (cn + CMP_LEN > jn), 1.0, 0.0).astype(BF16)
    jb = lax.broadcasted_iota(jnp.int32, (tq, LANES), 1)
    cur = tpos // SEL_LEN
    forced = (jb == 0) | (jb == cur) | (jb == cur - 1)
    bias = []
    for g in range(A_KV_HEADS):
        ps = p_c[g * A_GROUP]
        for r in range(1, A_GROUP):
            ps = ps + p_c[g * A_GROUP + r]
        hi = ps.astype(BF16)
        lo = (ps - hi.astype(F32)).astype(BF16)
        score = (jnp.dot(hi, ov, preferred_element_type=F32) + jnp.dot(lo, ov, preferred_element_type=F32))
        score = jnp.where(forced, FORCE_SCORE, score)
        score = jnp.where(jb * SEL_LEN <= tpos, score, -1.0)
        cnt = jnp.zeros((tq, LANES), jnp.int32)
        for k in range(n_sel):
            sk = score[:, k:k + 1]
            ahead = (sk > score) | ((sk == score) & (jb > k))
            cnt = cnt + jnp.where(ahead, 1, 0)
        bias.append(jnp.where(cnt < min(SEL_TOP, n_sel), 0.0, _NEG_SEL).astype(BF16))
    q_aug = jnp.concatenate(
        [qz, jnp.concatenate([bias[hd // A_GROUP] for hd in range(nh)], axis=0)], axis=1)

    tw = lax.broadcasted_iota(jnp.int32, (tq, wbuf), 0)
    iw = lax.broadcasted_iota(jnp.int32, (tq, wbuf), 1)
    tn = lax.broadcasted_iota(jnp.int32, (tq, tq), 0)
    un = lax.broadcasted_iota(jnp.int32, (tq, tq), 1)
    wst = wst_ref[0]
    wnew = wnew_ref[...]
    s_wa = _dot_nt(qz, wst[:, :KV_WIDTH].astype(BF16)).reshape(nh, tq, wbuf)
    s_wb = _dot_nt(qz, wnew[:, :KV_WIDTH].astype(BF16)).reshape(nh, tq, tq)
    o_w = _attend_two(s_wa, (wbuf + tw - iw < WINDOW), wst[:, KV_WIDTH:].astype(BF16),
                      s_wb, un <= tn, wnew[:, KV_WIDTH:].astype(BF16))
    nwin_ref[0, :wbuf - tq, :] = wst[tq:, :]
    nwin_ref[0, wbuf - tq:, :] = wnew

    for cp in copies(b, slot):
        cp.wait()
    seln = seln_ref[...]
    k_aug_t = jnp.concatenate([selbuf[slot, :KV_WIDTH, :].astype(BF16), oh_ref[...]], axis=0)
    nblk = (past + lax.broadcasted_iota(jnp.int32, (tq, LANES), 0)) // SEL_LEN
    oh_new = jnp.where(nblk == lax.broadcasted_iota(jnp.int32, (tq, LANES), 1), 1.0, 0.0)
    kn_aug = jnp.concatenate([seln[:, :KV_WIDTH], oh_new], axis=1).astype(BF16)
    s_sa = jnp.dot(q_aug, k_aug_t, preferred_element_type=F32).reshape(nh, tq, past)
    s_sb = _dot_nt(q_aug, kn_aug).reshape(nh, tq, tq)
    o_s = _attend_two(s_sa, None, selbuf[slot, KV_WIDTH:, :].astype(BF16), s_sb, un <= tn,
                      seln[:, KV_WIDTH:].astype(BF16), v_a_transposed=True)

    o_ref[...] = _gate_rows_to_heads(gt_ref[...], o_c, o_s, o_w)


def _nsa_sample_call(qa, gt, comp, sel_new, win_state, win_new, sel_pool, page_table, T):
    DB, n_pages = page_table.shape
    page = sel_pool.shape[2]
    past = n_pages * page
    wbuf = win_state.shape[1]
    ncp = comp.shape[0] // DB
    w2 = 2 * KV_WIDTH
    assert past % SEL_LEN == 0 and T < CMP_STRIDE and T % 8 == 0 and wbuf == WINDOW and past >= WINDOW
    assert page % LANES == 0
    onehot = (jnp.arange(LANES)[:, None] == jnp.arange(past)[None, :] // SEL_LEN).astype(BF16)
    tok = lambda w: pl.BlockSpec((T, w), lambda b, pt: (b, 0))
    grid_spec = pltpu.PrefetchScalarGridSpec(
        num_scalar_prefetch=1,
        grid=(DB,),
        in_specs=[tok(A_WIDTH), tok(LANES),
                  pl.BlockSpec((ncp, KV_WIDTH), lambda b, pt: (b, 0)),
                  pl.BlockSpec((ncp, KV_WIDTH), lambda b, pt: (b, 1)),
                  tok(w2),
                  pl.BlockSpec((1, wbuf, w2), lambda b, pt: (b, 0, 0)),
                  tok(w2),
                  pl.BlockSpec((LANES, past), lambda b, pt: (0, 0)),
                  pl.BlockSpec(memory_space=pl.ANY)],
        out_specs=[tok(A_WIDTH), pl.BlockSpec((1, wbuf, w2), lambda b, pt: (b, 0, 0))],
        scratch_shapes=[pltpu.VMEM((2, w2, past), F32), pltpu.SemaphoreType.DMA((2,))],
    )
    return pl.pallas_call(
        functools.partial(_nsa_sample_kernel, past=past),
        grid_spec=grid_spec,
        out_shape=[jax.ShapeDtypeStruct((DB * T, A_WIDTH), F32), jax.ShapeDtypeStruct((DB, wbuf, w2), F32)],
        compiler_params=_cparams(("arbitrary",)),
        name="nsa_sample",
    )(page_table, qa, gt, comp, comp, sel_new, win_state, win_new, onehot, sel_pool)


def _value_with_ones(v, g):
    lane = lax.broadcasted_iota(jnp.int32, v.shape, 1)
    keep = (lane < A_DIM) if g == 0 else (lane >= A_DIM)
    return jnp.where(keep, v, 1.0).astype(BF16)


def _pv_with_sums(p, v):
    half = p.shape[0] // A_KV_HEADS
    return jnp.concatenate(
        [jnp.dot(p[g * half:(g + 1) * half], _value_with_ones(v, g), preferred_element_type=F32)
         for g in range(A_KV_HEADS)], axis=0)


def _normalise_rows(acc):
    half = acc.shape[0] // A_KV_HEADS
    return jnp.concatenate(
        [acc[:half] * (1.0 / acc[:half, A_DIM:A_DIM + 1]), acc[half:] * (1.0 / acc[half:, 0:1])], axis=0)


def _nsa_prompt_kernel(q_ref, gt_ref, kc_ref, vc_ref, sel_ref, win_ref, o_ref, *, seq):
    tq = Q_BLOCK
    nh = A_HEADS
    n_sel = seq // SEL_LEN
    s0 = pl.program_id(1) * tq
    qz = _heads_to_rows(q_ref[0])

    kc = kc_ref[0].astype(BF16)
    vc = vc_ref[0].astype(BF16)
    ncp = kc.shape[0]
    s_c = _dot_nt(qz, kc).reshape(nh, tq, ncp)
    tpos = s0 + lax.broadcasted_iota(jnp.int32, (tq, ncp), 0)
    nidx = lax.broadcasted_iota(jnp.int32, (tq, ncp), 1)
    p_c = _softmax_rows(s_c, nidx * CMP_STRIDE + (CMP_LEN - 1) <= tpos)
    o_c = jnp.dot(p_c.reshape(nh * tq, ncp).astype(BF16), vc, preferred_element_type=F32)
    mass = [sum(p_c[g * A_GROUP + r] for r in range(1, A_GROUP)) + p_c[g * A_GROUP] for g in range(A_KV_HEADS)]

    jn = lax.broadcasted_iota(jnp.int32, (n_sel, ncp), 0) * SEL_LEN
    cn = lax.broadcasted_iota(jnp.int32, (n_sel, ncp), 1) * CMP_STRIDE
    ov_t = jnp.where((cn < jn + SEL_LEN) & (cn + CMP_LEN > jn), 1.0, 0.0).astype(BF16)
    jb = lax.broadcasted_iota(jnp.int32, (n_sel, tq), 0)
    tp = s0 + lax.broadcasted_iota(jnp.int32, (n_sel, tq), 1)
    cur = tp // SEL_LEN
    forced = (jb == 0) | (jb == cur) | (jb == cur - 1)
    bias = []
    for g in range(A_KV_HEADS):
        hi = mass[g].astype(BF16)
        lo = (mass[g] - hi.astype(F32)).astype(BF16)
        score = _dot_nt(ov_t, hi) + _dot_nt(ov_t, lo)
        score = jnp.where(forced, FORCE_SCORE, score)
        score = jnp.where(jb * SEL_LEN <= tp, score, -1.0)
        cnt = jnp.zeros((n_sel, tq), jnp.int32)
        for k in range(n_sel):
            rk = score[k:k + 1, :]
            ahead = (rk > score) | ((rk == score) & (jb > k))
            cnt = cnt + jnp.where(ahead, 1, 0)
        bias_t = jnp.where(cnt < min(SEL_TOP, n_sel), 0.0, _NEG_SEL)
        if n_sel < LANES:
            bias_t = jnp.concatenate([bias_t, jnp.zeros((LANES - n_sel, tq), F32)], axis=0)
        bias.append(bias_t.T.astype(BF16))
    q_aug = jnp.concatenate(
        [qz, jnp.concatenate([bias[hd // A_GROUP] for hd in range(nh)], axis=0)], axis=1)

    wk = WINDOW + tq
    w0 = pl.multiple_of(jnp.clip(s0 - WINDOW, 0, seq - wk), tq)
    kvw = win_ref[0, pl.ds(w0, wk), :]
    s_w = _dot_nt(qz, kvw[:, :KV_WIDTH].astype(BF16)).reshape(nh, tq, wk)
    dist = (s0 + lax.broadcasted_iota(jnp.int32, (tq, wk), 0)) - (w0 + lax.broadcasted_iota(jnp.int32, (tq, wk), 1))
    s_w = jnp.where(((dist >= 0) & (dist < WINDOW))[None], s_w, _NEG)
    e_w = jnp.exp(s_w - jnp.max(s_w, axis=-1, keepdims=True)).reshape(nh * tq, wk).astype(BF16)
    o_w = _normalise_rows(_pv_with_sums(e_w, kvw[:, KV_WIDTH:]))

    kc_n = _SEL_CHUNK

    def chunk(c, carry, causal):
        m, acc = carry
        k0 = pl.multiple_of(c * kc_n, kc_n)
        kv = sel_ref[0, pl.ds(k0, kc_n), :]
        kblk = (k0 + lax.broadcasted_iota(jnp.int32, (kc_n, LANES), 0)) // SEL_LEN
        onehot = jnp.where(kblk == lax.broadcasted_iota(jnp.int32, (kc_n, LANES), 1), 1.0, 0.0)
        k_aug = jnp.concatenate([kv[:, :KV_WIDTH], onehot], axis=1).astype(BF16)
        s = _dot_nt(q_aug, k_aug)
        if causal:
            kpos = k0 + lax.broadcasted_iota(jnp.int32, (tq, kc_n), 1)
            qpos = s0 + lax.broadcasted_iota(jnp.int32, (tq, kc_n), 0)
            s = jnp.where((kpos <= qpos)[None], s.reshape(nh, tq, kc_n), _NEG_SEL).reshape(nh * tq, kc_n)
        m_new = jnp.maximum(m, jnp.max(s, axis=-1, keepdims=True))
        p = jnp.exp(s - m_new).astype(BF16)
        acc = jnp.exp(m - m_new) * acc + _pv_with_sums(p, kv[:, KV_WIDTH:])
        return m_new, acc

    n_ch = (s0 + tq - 1) // kc_n + 1
    init = (jnp.full((nh * tq, 1), _NEG, F32), jnp.zeros((nh * tq, LANES), F32))
    carry = lax.fori_loop(0, n_ch - 1, lambda c, cr: chunk(c, cr, False), init)
    o_s = _normalise_rows(chunk(n_ch - 1, carry, True)[1])

    o_ref[0] = _gate_rows_to_heads(gt_ref[0], o_c, o_s, o_w)


def _nsa_prompt_call(qa, gt, comp, sel_kv, win_kv):
    B, S, _ = qa.shape
    assert S % _SEL_CHUNK == 0 and S >= WINDOW + Q_BLOCK and S // SEL_LEN <= LANES
    ncp = comp.shape[1]
    per_b = lambda r, w: pl.BlockSpec((1, r, w), lambda b, i: (b, 0, 0))
    return pl.pallas_call(
        functools.partial(_nsa_prompt_kernel, seq=S),
        grid=(B, S // Q_BLOCK),
        in_specs=[pl.BlockSpec((1, Q_BLOCK, A_WIDTH), lambda b, i: (b, i, 0)),
                  pl.BlockSpec((1, Q_BLOCK, LANES), lambda b, i: (b, i, 0)),
                  pl.BlockSpec((1, ncp, KV_WIDTH), lambda b, i: (b, 0, 0)),
                  pl.BlockSpec((1, ncp, KV_WIDTH), lambda b, i: (b, 0, 1)),
                  per_b(S, 2 * KV_WIDTH), per_b(S, 2 * KV_WIDTH)],
        out_specs=pl.BlockSpec((1, Q_BLOCK, A_WIDTH), lambda b, i: (b, i, 0)),
        out_shape=jax.ShapeDtypeStruct((B, S, A_WIDTH), F32),
        compiler_params=_cparams(("parallel", "arbitrary")),
        name="nsa_prompt",
    )(qa, gt, comp, comp, sel_kv, win_kv)


def _moe(h2, logits, moe_w, n_experts):
    w_gu, b_gu, w_down, b_down = moe_w
    n, d = h2.shape
    gate_w, dest, src, blk_e, n_used = _route(logits, n_experts)
    yb = _ffn(h2[src], blk_e, n_used, w_gu, b_gu, w_down, b_down)
    return yb[dest.T], gate_w


def kernel(x_prompt, x_sample, cache_cmp, cache_sel, state_win, state_C, state_n, state_m, page_table,
           c_prompt, c_sample, w_ada, b_ada, w_in, b_in, m_norm_g, cmp_pe_k, cmp_w1_k, cmp_w2_k,
           cmp_pe_v, cmp_w1_v, cmp_w2_v, w_out, ln1_g, ln1_b, w_router, b_router, w_gu, b_gu,
           w_down, b_down, ln2_g, ln2_b):
    B, S, d = x_prompt.shape
    DB, T, _ = x_sample.shape
    depth = w_ada.shape[0]
    n_experts = w_router.shape[-1]
    alpha = (2 * depth) ** 0.25
    n_pool, page = cache_cmp.shape[1:3]
    past_len = page_table.shape[1] * page
    wbuf = state_win.shape[2]
    pos_p = jnp.arange(S)
    pos_s = past_len + jnp.arange(T)
    tt_p = min(S, 256)
    bb_s = min(DB, max(1, 256 // T))
    n_cmp_s = (past_len + T - CMP_LEN) // CMP_STRIDE + 1
    assert n_cmp_s < past_len // CMP_STRIDE
    r3 = lambda a: a.reshape(B, S, a.shape[-1])
    kv5 = lambda a, n, t: a.reshape(n, t, 2, A_KV_HEADS, A_DIM)
    y_prompt, y_sample = x_prompt, x_sample
    outs = [[] for _ in range(12)]
    for l in range(depth):
        cmp_l = (cmp_pe_k[l], cmp_w1_k[l], cmp_w2_k[l], cmp_pe_v[l], cmp_w1_v[l], cmp_w2_v[l])
        moe_w = (w_gu[l], b_gu[l], w_down[l], b_down[l])
        post1_w = (w_out[l], ln1_g[l], ln1_b[l], w_router[l], b_router[l])
        w_r, b_r = _prep_w_in(w_in[l], b_in[l])
        cmp_w = _compress_weights(cmp_l)
        c_all = jnp.concatenate([c_prompt, c_sample], axis=0)
        c_all = jnp.pad(c_all, ((0, -c_all.shape[0] % 8), (0, 0)))
        mod_all = _ada(c_all, w_ada[l], b_ada[l]).reshape(-1, 6, d)
        mod_p, mod_s = mod_all[:B], mod_all[B:B + DB]

        mq, gt, qa, cmp_p, sel_p, win_p, cmp_t, sel_t, win_t = _pre(y_prompt, mod_p, pos_p, w_r, b_r, 1, tt_p,
                                                                    feature_major=True)
        zc = jnp.zeros((B, M_HEADS, M_DIM, M_DIM), F32)
        mo, C_p, n_p, m_p = _mlstm_call(r3(mq), r3(gt), zc, zc[..., 0], zc[..., 0, 0], m_norm_g[l],
                                        math.gcd(B, 4))
        comp = _compress_prompt(r3(cmp_p), cmp_w)
        ma = _nsa_prompt_call(r3(qa), r3(gt), comp, r3(sel_p), r3(win_p))
        x1_p, h2_p, lg_p = _post1(mo.reshape(B * S, M_WIDTH), ma.reshape(B * S, A_WIDTH), y_prompt, mod_p,
                                  *post1_w, 1, tt_p, alpha)

        mq, gt, qa, cmp_s, sel_s, win_s = _pre(y_sample, mod_s, pos_s, w_r, b_r, bb_s, T)
        mo, C_s, n_s, m_s = _mlstm_call(mq.reshape(DB, T, -1), gt.reshape(DB, T, -1), state_C[l],
                                        state_n[l], state_m[l], m_norm_g[l], math.gcd(DB, 4))
        comp = _compress_sample(cache_cmp[l].reshape(n_pool, page, 2 * KV_WIDTH), page_table, cmp_w, n_cmp_s,
                                math.gcd(DB, 4))
        sel_pool_t = cache_sel[l].transpose(0, 2, 3, 4, 1).reshape(n_pool, 2 * KV_WIDTH, page)
        ma, new_win = _nsa_sample_call(qa, gt, comp, sel_s, state_win[l].reshape(DB, wbuf, 2 * KV_WIDTH),
                                       win_s, sel_pool_t, page_table, T)
        x1_s, h2_s, lg_s = _post1(mo.reshape(DB * T, M_WIDTH), ma, y_sample, mod_s, *post1_w, bb_s, T, alpha)

        yg, gate_w = _moe(jnp.concatenate([h2_p, h2_s], axis=0), jnp.concatenate([lg_p, lg_s], axis=0),
                          moe_w, n_experts)
        y_prompt = _post2(yg, gate_w, 0, x1_p, mod_p, ln2_g[l], ln2_b[l], 1, tt_p, alpha)
        y_sample = _post2(yg, gate_w, B * S, x1_s, mod_s, ln2_g[l], ln2_b[l], bb_s, T, alpha)

        kv5_t = lambda a: a.reshape(B, 2, A_KV_HEADS, A_DIM, -1).transpose(0, 4, 1, 2, 3)
        new = (kv5_t(cmp_t), kv5_t(sel_t), kv5_t(win_t[:, :, -min(WINDOW, S):]), C_p, n_p, m_p,
               kv5(cmp_s, DB, T), kv5(sel_s, DB, T), new_win.reshape(state_win.shape[1:]), C_s, n_s, m_s)
        for lst, v in zip(outs, new):
            lst.append(v)
    return (y_prompt, y_sample, *[jnp.stack(v) for v in outs])
```

```python
import functools
import math

import numpy as np
import jax
import jax.numpy as jnp
from jax import lax
from jax.experimental import pallas as pl
from jax.experimental.pallas import tpu as pltpu

F32 = jnp.float32
BF16 = jnp.bfloat16

M_HEADS = 4
M_DIM = 128
M_WIDTH = M_HEADS * M_DIM
M_CHUNK = 64
A_HEADS = 8
A_KV_HEADS = 2
A_GROUP = A_HEADS // A_KV_HEADS
A_DIM = 64
A_WIDTH = A_HEADS * A_DIM
KV_WIDTH = A_KV_HEADS * A_DIM
CMP_LEN = 32
CMP_STRIDE = 16
SEL_LEN = 64
SEL_TOP = 16
WINDOW = 512
Q_BLOCK = 128
FORCE_SCORE = 1.0e4
ROPE_THETA = 500000.0
ROPE_DIM = A_DIM // 4
TOP_K = 4
SWIGLU_LIMIT = 7.0
SWIGLU_ALPHA = 1.702
LN_EPS = 1e-5
LANES = 128
MOE_ROWS = 256
VMEM_LIMIT = 56 * 1024 * 1024

_O_MQ = 0
_O_IF = 4 * M_WIDTH
_O_QA = _O_IF + 2 * M_HEADS
_O_KV = _O_QA + A_WIDTH
_O_GA = _O_KV + 6 * KV_WIDTH
_N_IN = _O_GA + 3 * A_HEADS
_R_QA = 4 * M_WIDTH
_R_KV = _R_QA + A_WIDTH
_R_GT = _R_KV + 6 * KV_WIDTH
_R_END = _R_GT + LANES


def _cparams(sem):
    return pltpu.CompilerParams(dimension_semantics=sem, vmem_limit_bytes=VMEM_LIMIT)


def _ln_core(x):
    mu = jnp.mean(x, axis=-1, keepdims=True)
    xc = x - mu
    var = jnp.mean(xc * xc, axis=-1, keepdims=True)
    return xc * lax.rsqrt(var + LN_EPS)


def _ada_kernel(c_ref, w_ref, b_ref, o_ref):
    o_ref[...] = jnp.dot(c_ref[...].astype(BF16), w_ref[...].astype(BF16),
                         preferred_element_type=F32) + b_ref[...]


def _ada(c, w_ada, b_ada):
    n, d = c.shape
    cols = w_ada.shape[1]
    return pl.pallas_call(
        _ada_kernel,
        grid=(cols // d,),
        in_specs=[pl.BlockSpec((n, d), lambda j: (0, 0)),
                  pl.BlockSpec((d, d), lambda j: (0, j)),
                  pl.BlockSpec((1, d), lambda j: (0, j))],
        out_specs=pl.BlockSpec((n, d), lambda j: (0, j)),
        out_shape=jax.ShapeDtypeStruct((n, cols), F32),
        compiler_params=_cparams(("arbitrary",)),
        name="ada",
    )(c, w_ada, b_ada.reshape(1, cols))


def _rope_apply(v, cos, sa, sb):
    reps = v.shape[1] // LANES
    tile = lambda t: t if reps == 1 else jnp.concatenate([t] * reps, axis=1)
    w = v.shape[1]
    return (v * tile(cos) + pltpu.roll(v, w - ROPE_DIM // 2, 1) * tile(sa)
            + pltpu.roll(v, ROPE_DIM // 2, 1) * tile(sb))


def _pre_kernel(x_ref, mod_ref, cos_ref, sa_ref, sb_ref, w_ref, b_ref,
                mq_ref, gt_ref, qa_ref, cmp_ref, sel_ref, win_ref, *t_refs):
    bb, tt, d = x_ref.shape
    mod = mod_ref[...]
    h = _ln_core(x_ref[...]) * (1.0 + mod[:, 1:2, :]) + mod[:, 0:1, :]
    h = h.reshape(bb * tt, d).astype(BF16)
    z = jnp.dot(h, w_ref[...], preferred_element_type=F32) + b_ref[...]
    cos, sa, sb = cos_ref[...], sa_ref[...], sb_ref[...]
    mq_ref[...] = z[:, :_R_QA]
    gt_ref[...] = z[:, _R_GT:_R_END]
    qa_ref[...] = _rope_apply(z[:, _R_QA:_R_KV], cos, sa, sb)
    for n, ref in enumerate((cmp_ref, sel_ref, win_ref)):
        o = _R_KV + 2 * KV_WIDTH * n
        kv = jnp.concatenate([_rope_apply(z[:, o:o + KV_WIDTH], cos, sa, sb),
                              z[:, o + KV_WIDTH:o + 2 * KV_WIDTH]], axis=1)
        ref[...] = kv
        if t_refs:
            t_refs[n][0] = kv.T


def _rope_tables(pos):
    half = ROPE_DIM // 2
    inv = ROPE_THETA ** (-jnp.arange(0, ROPE_DIM, 2, dtype=F32) / ROPE_DIM)
    ang = pos.astype(F32)[:, None] * inv[None, :]
    cos, sin = jnp.cos(ang), jnp.sin(ang)
    n = pos.shape[0]
    one = jnp.ones((n, A_DIM - ROPE_DIM), F32)
    zero = jnp.zeros((n, A_DIM - ROPE_DIM), F32)
    zh = jnp.zeros((n, half), F32)
    cos_t = jnp.concatenate([cos, cos, one], axis=1)
    sa_t = jnp.concatenate([-sin, zh, zero], axis=1)
    sb_t = jnp.concatenate([zh, sin, zero], axis=1)
    two = lambda t: jnp.concatenate([t, t], axis=1)
    return two(cos_t), two(sa_t), two(sb_t)


def _prep_w_in(w_in, b_in):
    pad = LANES - 2 * M_HEADS - 3 * A_HEADS
    cat = lambda a: jnp.concatenate(
        [a[..., _O_MQ:_O_IF], a[..., _O_QA:_O_GA], a[..., _O_IF:_O_QA], a[..., _O_GA:_N_IN],
         jnp.zeros(a.shape[:-1] + (pad,), a.dtype)], axis=-1)
    return cat(w_in).astype(BF16), cat(b_in[None, :])


def _pre(x, mod, pos, w_r, b_r, bb, tt, feature_major=False):
    B, T, d = x.shape
    nt = T // tt
    rows = bb * tt
    cos, sa, sb = _rope_tables(pos)
    if bb > 1:
        cos, sa, sb = (jnp.tile(t, (bb, 1)) for t in (cos, sa, sb))
    n_tok = B * T
    tab = pl.BlockSpec((rows, LANES), lambda i, j: (j, 0))
    row = lambda w: pl.BlockSpec((rows, w), lambda i, j: (i * nt + j, 0))
    widths = (_R_QA, LANES, A_WIDTH, 2 * KV_WIDTH, 2 * KV_WIDTH, 2 * KV_WIDTH)
    out_specs = [row(w) for w in widths]
    out_shape = [jax.ShapeDtypeStruct((n_tok, w), F32) for w in widths]
    if feature_major:
        assert bb == 1 and tt % LANES == 0
        out_specs += [pl.BlockSpec((1, 2 * KV_WIDTH, tt), lambda i, j: (i, 0, j))] * 3
        out_shape += [jax.ShapeDtypeStruct((B, 2 * KV_WIDTH, T), F32)] * 3
    return pl.pallas_call(
        _pre_kernel,
        grid=(B // bb, nt),
        in_specs=[pl.BlockSpec((bb, tt, d), lambda i, j: (i, j, 0)),
                  pl.BlockSpec((bb, 6, d), lambda i, j: (i, 0, 0)),
                  tab, tab, tab,
                  pl.BlockSpec((d, _R_END), lambda i, j: (0, 0)),
                  pl.BlockSpec((1, _R_END), lambda i, j: (0, 0))],
        out_specs=out_specs,
        out_shape=out_shape,
        compiler_params=_cparams(("parallel", "arbitrary")),
        name="pre",
    )(x, mod, cos, sa, sb, w_r, b_r)


def _post1_kernel(mo_ref, ao_ref, x_ref, mod_ref, wo_ref, g_ref, b_ref, wr_ref, br_ref,
                  x1_ref, h2_ref, lg_ref, *, alpha):
    bb, tt, d = x_ref.shape
    mod = mod_ref[...]
    mixin = jnp.concatenate([mo_ref[...], ao_ref[...]], axis=1).astype(BF16)
    mix = jnp.dot(mixin, wo_ref[...], preferred_element_type=F32).reshape(bb, tt, d)
    x1 = _ln_core(alpha * x_ref[...] + mod[:, 2:3, :] * mix) * g_ref[...] + b_ref[...]
    h2 = _ln_core(x1) * (1.0 + mod[:, 4:5, :]) + mod[:, 3:4, :]
    x1_ref[...] = x1
    h2f = h2.reshape(bb * tt, d)
    h2_ref[...] = h2f
    lg_ref[...] = jnp.dot(h2f, wr_ref[...], preferred_element_type=F32,
                          precision=lax.Precision.HIGHEST) + br_ref[...]


def _post1(mo, ao, x, mod, w_out, ln_g, ln_b, w_router, b_router, bb, tt, alpha):
    B, T, d = x.shape
    nt = T // tt
    rows = bb * tt
    ne = w_router.shape[1]
    wr = jnp.pad(w_router, ((0, 0), (0, LANES - ne)))
    br = jnp.pad(b_router, (0, LANES - ne), constant_values=-jnp.inf).reshape(1, LANES)
    full = lambda *s: pl.BlockSpec(s, lambda i, j: (0,) * len(s))
    row = lambda w: pl.BlockSpec((rows, w), lambda i, j: (i * nt + j, 0))
    return pl.pallas_call(
        functools.partial(_post1_kernel, alpha=alpha),
        grid=(B // bb, nt),
        in_specs=[row(M_WIDTH), row(A_WIDTH),
                  pl.BlockSpec((bb, tt, d), lambda i, j: (i, j, 0)),
                  pl.BlockSpec((bb, 6, d), lambda i, j: (i, 0, 0)),
                  full(M_WIDTH + A_WIDTH, d), full(1, d), full(1, d), full(d, LANES), full(1, LANES)],
        out_specs=[pl.BlockSpec((bb, tt, d), lambda i, j: (i, j, 0)), row(d), row(LANES)],
        out_shape=[jax.ShapeDtypeStruct((B, T, d), F32),
                   jax.ShapeDtypeStruct((B * T, d), F32),
                   jax.ShapeDtypeStruct((B * T, LANES), F32)],
        compiler_params=_cparams(("parallel", "arbitrary")),
        name="post1",
    )(mo, ao, x, mod, w_out.astype(BF16), ln_g.reshape(1, d), ln_b.reshape(1, d), wr, br)


def _ffn_kernel(be_ref, nu_ref, x_ref, wgu_ref, bgu_ref, wd_ref, bd_ref, y_ref, wgu_bf, wd_bf):
    i = pl.program_id(0)
    dff = wd_ref.shape[1]

    @pl.when((i == 0) | (be_ref[i] != be_ref[jnp.maximum(i - 1, 0)]))
    def _():
        wgu_bf[...] = wgu_ref[0].astype(BF16)
        wd_bf[...] = wd_ref[0].astype(BF16)

    @pl.when(i < nu_ref[0])
    def _():
        gu = jnp.dot(x_ref[...].astype(BF16), wgu_bf[...], preferred_element_type=F32) + bgu_ref[0]
        g = jnp.minimum(gu[:, :dff], SWIGLU_LIMIT)
        u = jnp.clip(gu[:, dff:], -SWIGLU_LIMIT, SWIGLU_LIMIT)
        act = (u + 1.0) * (g * jax.nn.sigmoid(SWIGLU_ALPHA * g))
        y_ref[...] = jnp.dot(act.astype(BF16), wd_bf[...], preferred_element_type=F32) + bd_ref[0]

    @pl.when(i >= nu_ref[0])
    def _():
        y_ref[...] = jnp.zeros_like(y_ref)


def _ffn(xb, blk_e, n_used, w_gu, b_gu, w_down, b_down):
    rows, d = xb.shape
    ne, _, f2 = w_gu.shape
    dff = w_down.shape[1]
    nb = rows // MOE_ROWS
    grid_spec = pltpu.PrefetchScalarGridSpec(
        num_scalar_prefetch=2,
        grid=(nb,),
        in_specs=[pl.BlockSpec((MOE_ROWS, d), lambda i, be, nu: (i, 0)),
                  pl.BlockSpec((1, d, f2), lambda i, be, nu: (be[i], 0, 0)),
                  pl.BlockSpec((1, 1, f2), lambda i, be, nu: (be[i], 0, 0)),
                  pl.BlockSpec((1, dff, d), lambda i, be, nu: (be[i], 0, 0)),
                  pl.BlockSpec((1, 1, d), lambda i, be, nu: (be[i], 0, 0))],
        out_specs=pl.BlockSpec((MOE_ROWS, d), lambda i, be, nu: (i, 0)),
        scratch_shapes=[pltpu.VMEM((d, f2), BF16), pltpu.VMEM((dff, d), BF16)],
    )
    return pl.pallas_call(
        _ffn_kernel,
        grid_spec=grid_spec,
        out_shape=jax.ShapeDtypeStruct((rows, d), F32),
        compiler_params=_cparams(("arbitrary",)),
        name="ffn",
    )(blk_e, n_used, xb, w_gu, b_gu.reshape(ne, 1, f2), w_down, b_down.reshape(ne, 1, d))


def _post2_kernel(yg_ref, gw_ref, x1_ref, mod_ref, g_ref, b_ref, y_ref, *, alpha):
    bb, tt, d = x1_ref.shape
    gw = gw_ref[...]
    f = yg_ref[0] * gw[:, 0:1]
    for k in range(1, TOP_K):
        f = f + yg_ref[k] * gw[:, k:k + 1]
    y = alpha * x1_ref[...] + mod_ref[...][:, 5:6, :] * f.reshape(bb, tt, d)
    y_ref[...] = _ln_core(y) * g_ref[...] + b_ref[...]


def _post2(yg, gw, row0, x1, mod, ln_g, ln_b, bb, tt, alpha):
    B, T, d = x1.shape
    rows = bb * tt
    nt = T // tt
    blk0 = row0 // rows
    assert row0 % rows == 0
    full = lambda *s: pl.BlockSpec(s, lambda i, j: (0,) * len(s))
    return pl.pallas_call(
        functools.partial(_post2_kernel, alpha=alpha),
        grid=(B // bb, nt),
        in_specs=[pl.BlockSpec((TOP_K, rows, d), lambda i, j: (0, blk0 + i * nt + j, 0)),
                  pl.BlockSpec((rows, LANES), lambda i, j: (blk0 + i * nt + j, 0)),
                  pl.BlockSpec((bb, tt, d), lambda i, j: (i, j, 0)),
                  pl.BlockSpec((bb, 6, d), lambda i, j: (i, 0, 0)),
                  full(1, d), full(1, d)],
        out_specs=pl.BlockSpec((bb, tt, d), lambda i, j: (i, j, 0)),
        out_shape=jax.ShapeDtypeStruct((B, T, d), F32),
        compiler_params=_cparams(("parallel", "arbitrary")),
        name="post2",
    )(yg, gw, x1, mod, ln_g.reshape(1, d), ln_b.reshape(1, d))


def _lane_prefix_sum(x):
    lane = lax.broadcasted_iota(jnp.int32, x.shape, 1)
    s = 1
    while s < LANES:
        x = x + jnp.where(lane >= s, pltpu.roll(x, s, 1), 0.0)
        s *= 2
    return x


def _route_kernel(lg_ref, dest_ref, gw_ref, cnt_ref, counts, running):
    phase = pl.program_id(0)
    blk = pl.program_id(1)
    r = lg_ref.shape[0]
    lane = lax.broadcasted_iota(jnp.int32, (r, LANES), 1)

    @pl.when((phase == 0) & (blk == 0))
    def _():
        counts[...] = jnp.zeros_like(counts)

    @pl.when((phase == 1) & (blk == 0))
    def _():
        running[...] = jnp.zeros_like(running)

    vals = lg_ref[...]
    onehots, tops = [], []
    for k in range(TOP_K):
        m = jnp.max(vals, axis=1, keepdims=True)
        idx = jnp.min(jnp.where(vals == m, lane, LANES), axis=1, keepdims=True)
        hit = lane == idx
        onehots.append(hit)
        tops.append(m)
        vals = jnp.where(hit, -jnp.inf, vals)
    ohf = [jnp.where(h, 1.0, 0.0) for h in onehots]
    block_cnt = [jnp.sum(o, axis=0, keepdims=True) for o in ohf]

    @pl.when(phase == 0)
    def _():
        counts[...] += block_cnt[0] + block_cnt[1] + block_cnt[2] + block_cnt[3]

    @pl.when(phase == 1)
    def _():
        cnt = counts[...]
        padded = jnp.ceil(cnt * (1.0 / MOE_ROWS)) * MOE_ROWS
        pad_start = _lane_prefix_sum(padded) - padded
        es = [jnp.exp(t - tops[0]) for t in tops]
        den = es[0] + es[1] + es[2] + es[3]
        ti = lax.broadcasted_iota(jnp.int32, (r, r), 0)
        tj = lax.broadcasted_iota(jnp.int32, (r, r), 1)
        before = jnp.where(tj < ti, 1.0, 0.0).astype(BF16)
        base = pad_start + running[...]
        dest = jnp.zeros((r, LANES), F32)
        gw = jnp.zeros((r, LANES), F32)
        for k in range(TOP_K):
            rank = jnp.dot(before, ohf[k].astype(BF16), preferred_element_type=F32)
            d_k = jnp.sum(ohf[k] * (base + rank), axis=1, keepdims=True)
            dest = jnp.where(lane == k, d_k, dest)
            gw = jnp.where(lane == k, es[k] / den, gw)
            base = base + block_cnt[k]
        running[...] = base - pad_start
        dest_ref[...] = dest.astype(jnp.int32)
        gw_ref[...] = gw
        cnt_ref[...] = jnp.broadcast_to(cnt, cnt_ref.shape)


def _route(logits, n_experts, rows=256):
    n = logits.shape[0]
    assert n % rows == 0
    nb = n // rows
    tok = pl.BlockSpec((rows, LANES), lambda p, i: (i, 0))
    out = pl.BlockSpec((rows, LANES), lambda p, i: (i * p, 0))
    dest, gw, cnt = pl.pallas_call(
        _route_kernel,
        grid=(2, nb),
        in_specs=[tok],
        out_specs=[out, out, pl.BlockSpec((8, LANES), lambda p, i: (0, 0))],
        out_shape=[jax.ShapeDtypeStruct((n, LANES), jnp.int32), jax.ShapeDtypeStruct((n, LANES), F32),
                   jax.ShapeDtypeStruct((8, LANES), F32)],
        scratch_shapes=[pltpu.VMEM((1, LANES), F32), pltpu.VMEM((1, LANES), F32)],
        compiler_params=_cparams(("arbitrary", "arbitrary")),
        name="route",
    )(logits)
    counts = cnt[0, :n_experts].astype(jnp.int32)
    pad_end = jnp.cumsum((counts + MOE_ROWS - 1) // MOE_ROWS * MOE_ROWS)
    n_slots = n * TOP_K
    n_blocks = -(-n_slots // MOE_ROWS) + n_experts
    n_used = pad_end[-1] // MOE_ROWS
    blk = jnp.minimum(jnp.arange(n_blocks, dtype=jnp.int32), n_used - 1) * MOE_ROWS
    blk_e = jnp.minimum(jnp.sum(pad_end[None, :] <= blk[:, None], axis=1), n_experts - 1).astype(jnp.int32)
    dest4 = dest[:, :TOP_K]
    src = jnp.zeros((n_blocks * MOE_ROWS,), jnp.int32).at[dest4.reshape(-1)].set(
        jnp.arange(n_slots, dtype=jnp.int32) // TOP_K)
    return gw, dest4, src, blk_e, n_used.reshape(1).astype(jnp.int32)


def _log_sigmoid(x):
    return jnp.minimum(x, 0.0) - jnp.log(1.0 + jnp.exp(-jnp.abs(x)))


def _mlstm_kernel(mq_ref, gt_ref, gtt_ref, c0_ref, n0_ref, m0_ref, g_ref, mo_ref, c_ref, n_ref, m_ref):
    bb, L, _ = mq_ref.shape
    d = M_DIM

    @pl.when(pl.program_id(1) == 0)
    def _():
        c_ref[...] = c0_ref[...]
        n_ref[...] = n0_ref[...]
        m_ref[...] = m0_ref[...]

    tt = lax.broadcasted_iota(jnp.int32, (L, L), 0)
    ss = lax.broadcasted_iota(jnp.int32, (L, L), 1)
    causal = ss <= tt
    pairs = [(s, h) for s in range(bb) for h in range(M_HEADS)]
    col = lambda s, h, off: mq_ref[s, :, off + h * d:off + (h + 1) * d]
    old = {p: (c_ref[p[0], p[1]], n_ref[p[0], p[1]:p[1] + 1, :], m_ref[p[0], :, p[1]:p[1] + 1]) for p in pairs}

    qb = {p: col(*p, 0).astype(BF16) for p in pairs}
    kb = {p: (col(*p, M_WIDTH) * (d ** -0.5)).astype(BF16) for p in pairs}
    qk = {p: _dot_nt(qb[p], kb[p]) for p in pairs}
    qc = {p: _dot_nt(qb[p], old[p][0].astype(BF16)) for p in pairs}

    i_c = {(s, h): gt_ref[s][:, h:h + 1] for s, h in pairs}
    i_r = {(s, h): gtt_ref[s, 0][h:h + 1, :] for s, h in pairs}
    lf_c = {(s, h): _log_sigmoid(gt_ref[s][:, M_HEADS + h:M_HEADS + h + 1]) for s, h in pairs}
    lf_r = {(s, h): _log_sigmoid(gtt_ref[s, 0][M_HEADS + h:M_HEADS + h + 1, :]) for s, h in pairs}
    b_c = {p: jnp.sum(jnp.where(causal, lf_r[p], 0.0), axis=1, keepdims=True) for p in pairs}
    b_r = {p: jnp.sum(jnp.where(causal, 0.0, lf_c[p]), axis=0, keepdims=True) + lf_r[p] for p in pairs}

    dmat = {p: jnp.where(causal, b_c[p] - b_r[p] + i_r[p], _NEG) for p in pairs}
    dmax = {p: jnp.max(dmat[p], axis=1, keepdims=True) for p in pairs}
    m_t = {p: jnp.maximum(b_c[p] + old[p][2], dmax[p]) for p in pairs}
    inter = {p: jnp.exp(b_c[p] + old[p][2] - m_t[p]) for p in pairs}
    a = {p: jnp.exp(dmat[p] - m_t[p]) * qk[p] for p in pairs}
    av = {p: jnp.dot(a[p].astype(BF16), col(*p, 2 * M_WIDTH).astype(BF16), preferred_element_type=F32)
          for p in pairs}

    qn = {p: jnp.sum(col(*p, 0) * old[p][1], axis=1, keepdims=True) for p in pairs}
    asum = {p: jnp.sum(a[p], axis=1, keepdims=True) for p in pairs}
    hv = {p: (inter[p] * qc[p] + av[p]) / jnp.maximum(jnp.abs(inter[p] * qn[p] + asum[p]), jnp.exp(-m_t[p]))
          for p in pairs}
    mu = {p: jnp.mean(hv[p], axis=1, keepdims=True) for p in pairs}
    hc = {p: hv[p] - mu[p] for p in pairs}
    var = {p: jnp.mean(hc[p] * hc[p], axis=1, keepdims=True) for p in pairs}
    out = {(s, h): (hc[s, h] * lax.rsqrt(var[s, h] + LN_EPS) * g_ref[:, h * d:(h + 1) * d])
           * jax.nn.sigmoid(col(s, h, 3 * M_WIDTH)) for s, h in pairs}

    new = {}
    for p in pairs:
        c_old, n_old, m_old = old[p]
        k = col(*p, M_WIDTH) * (d ** -0.5)
        m_new = m_t[p][L - 1:L, :]
        b_last = b_c[p][L - 1:L, :]
        w_src = jnp.exp(b_last - b_c[p] + i_c[p] - m_new)
        w_old = jnp.exp(b_last + m_old - m_new)
        c_new = w_old * c_old + lax.dot_general(
            (w_src * col(*p, 2 * M_WIDTH)).astype(BF16), kb[p], (((0,), (0,)), ((), ())),
            preferred_element_type=F32)
        n_new = w_old * n_old + jnp.sum(w_src * k, axis=0, keepdims=True)
        new[p] = (c_new, n_new, m_new, out[p])
    for (s, h), (c_new, n_new, m_new, out) in new.items():
        c_ref[s, h] = c_new
        n_ref[s, h:h + 1, :] = n_new
        m_ref[s, :, h:h + 1] = m_new
        mo_ref[s, :, h * d:(h + 1) * d] = out


def _mlstm_call(mq, gt, C0, n0, m0, g, bb):
    B, T, _ = mq.shape
    L = math.gcd(T, M_CHUNK)
    nc = T // L
    gtt = gt[:, :, :2 * M_HEADS].reshape(B, nc, L, 2 * M_HEADS).transpose(0, 1, 3, 2)
    st4 = pl.BlockSpec((bb, M_HEADS, M_DIM, M_DIM), lambda i, c: (i, 0, 0, 0))
    st3 = pl.BlockSpec((bb, M_HEADS, M_DIM), lambda i, c: (i, 0, 0))
    st2 = pl.BlockSpec((bb, 1, M_HEADS), lambda i, c: (i, 0, 0))
    mo, C, n, m = pl.pallas_call(
        _mlstm_kernel,
        grid=(B // bb, nc),
        in_specs=[pl.BlockSpec((bb, L, 4 * M_WIDTH), lambda i, c: (i, c, 0)),
                  pl.BlockSpec((bb, L, LANES), lambda i, c: (i, c, 0)),
                  pl.BlockSpec((bb, 1, 2 * M_HEADS, L), lambda i, c: (i, c, 0, 0)),
                  st4, st3, st2,
                  pl.BlockSpec((1, M_WIDTH), lambda i, c: (0, 0))],
        out_specs=[pl.BlockSpec((bb, L, M_WIDTH), lambda i, c: (i, c, 0)), st4, st3, st2],
        out_shape=[jax.ShapeDtypeStruct((B, T, M_WIDTH), F32),
                   jax.ShapeDtypeStruct(C0.shape, F32), jax.ShapeDtypeStruct(n0.shape, F32),
                   jax.ShapeDtypeStruct((B, 1, M_HEADS), F32)],
        compiler_params=_cparams(("parallel", "arbitrary")),
        name="mlstm",
    )(mq, gt, gtt, C0, n0, m0.reshape(B, 1, M_HEADS), g.reshape(1, M_WIDTH))
    return mo, C, n, m.reshape(B, M_HEADS)


def _gelu_tanh(x):
    return x * (0.5 * (1.0 + jnp.tanh(math.sqrt(2.0 / math.pi) * (x + 0.044715 * (x * x * x)))))


def _compress_body(load, pe_ref, w1_ref, w2_ref, rows_total, nh, n_cmp):
    w = KV_WIDTH
    half = CMP_LEN // 2
    pa = jnp.zeros((rows_total, w), F32)
    pb = jnp.zeros((rows_total, w), F32)
    for j in range(half):
        xj = load(j)
        pa = pa + jnp.dot((xj + pe_ref[j:j + 1, :]).astype(BF16), w1_ref[j], preferred_element_type=F32)
        pb = pb + jnp.dot((xj + pe_ref[half + j:half + j + 1, :]).astype(BF16), w1_ref[half + j],
                          preferred_element_type=F32)
    hid = pa + pltpu.roll(pb, rows_total - 1, 0)
    y = jnp.dot(_gelu_tanh(hid).astype(BF16), w2_ref[...], preferred_element_type=F32)
    rows = lax.broadcasted_iota(jnp.int32, (rows_total, w), 0)
    assert nh & (nh - 1) == 0
    return jnp.where((rows & (nh - 1)) < n_cmp, y, 0.0)


def _compress_kernel(xk_ref, xv_ref, pe_ref, w1_ref, w2_ref, o_ref, *, n_cmp):
    nh = xk_ref.shape[1] // CMP_STRIDE
    for kv, x_ref in enumerate((xk_ref, xv_ref)):
        o_ref[0, :, kv * KV_WIDTH:(kv + 1) * KV_WIDTH] = _compress_body(
            lambda j: x_ref[0, pl.ds(j, nh, stride=CMP_STRIDE), :], pe_ref.at[kv], w1_ref.at[kv], w2_ref.at[kv],
            nh, nh, n_cmp)


def _compress_sample_kernel(pt_ref, pool_ref, pe_ref, w1_ref, w2_ref, o_ref, stage, xrow, sem, *,
                            seqs, n_pages, n_cmp):
    i = pl.program_id(0)
    slot = i % 2
    page = pool_ref.shape[2]
    nh = n_pages * page // CMP_STRIDE

    def copies(step, sl):
        return [pltpu.make_async_copy(pool_ref.at[pt_ref[step * seqs + s, p], pl.ds(kv * KV_WIDTH, KV_WIDTH), :],
                                      stage.at[sl, kv, s * n_pages + p], sem.at[sl])
                for s in range(seqs) for p in range(n_pages) for kv in range(2)]

    @pl.when(i == 0)
    def _():
        for cp in copies(0, 0):
            cp.start()

    @pl.when(i + 1 < pl.num_programs(0))
    def _():
        for cp in copies(i + 1, 1 - slot):
            cp.start()

    for cp in copies(i, slot):
        cp.wait()
    for kv in range(2):
        def to_rows(pi, carry):
            xrow[kv, pl.ds(pl.multiple_of(pi * page, page), page), :] = stage[slot, kv, pi].T
            return carry

        lax.fori_loop(0, seqs * n_pages, to_rows, 0)
        o_ref[:, kv * KV_WIDTH:(kv + 1) * KV_WIDTH] = _compress_body(
            lambda j: xrow[kv, pl.ds(j, seqs * nh, stride=CMP_STRIDE), :],
            pe_ref.at[kv], w1_ref.at[kv], w2_ref.at[kv], seqs * nh, nh, n_cmp)


def _compress_sample(pool, page_table, cmp_w, n_cmp, seqs):
    n_pool, w, page = pool.shape
    DB, n_pages = page_table.shape
    nh = n_pages * page // CMP_STRIDE
    pe4, w1, w2 = cmp_w
    full = lambda *s: pl.BlockSpec(s, lambda i, pt: (0,) * len(s))
    grid_spec = pltpu.PrefetchScalarGridSpec(
        num_scalar_prefetch=1,
        grid=(DB // seqs,),
        in_specs=[pl.BlockSpec(memory_space=pl.ANY), full(*pe4.shape), full(*w1.shape), full(*w2.shape)],
        out_specs=pl.BlockSpec((seqs * nh, w), lambda i, pt: (i, 0)),
        scratch_shapes=[pltpu.VMEM((2, 2, seqs * n_pages, KV_WIDTH, page), F32),
                        pltpu.VMEM((2, seqs * n_pages * page, KV_WIDTH), F32),
                        pltpu.SemaphoreType.DMA((2,))],
    )
    return pl.pallas_call(
        functools.partial(_compress_sample_kernel, seqs=seqs, n_pages=n_pages, n_cmp=n_cmp),
        grid_spec=grid_spec,
        out_shape=jax.ShapeDtypeStruct((DB * nh, w), F32),
        compiler_params=_cparams(("arbitrary",)),
        name="compress_sample",
    )(page_table, pool, pe4, w1, w2)


def _block_diag2(a):
    z = jnp.zeros_like(a)
    return jnp.concatenate([jnp.concatenate([a, z], -1), jnp.concatenate([z, a], -1)], -2)


def _compress_weights(cmp_params):
    pe_k, w1_k, w2_k, pe_v, w1_v, w2_v = cmp_params
    pe = jnp.stack([jnp.concatenate([p, p], axis=1) for p in (pe_k, pe_v)])
    w1 = jnp.stack([_block_diag2(w.reshape(CMP_LEN, A_DIM, A_DIM)) for w in (w1_k, w1_v)]).astype(BF16)
    w2 = jnp.stack([_block_diag2(w) for w in (w2_k, w2_v)]).astype(BF16)
    return pe, w1, w2


def _compress_prompt(cmp_rows, cmp_w):
    B, L, w = cmp_rows.shape
    nh = L // CMP_STRIDE
    n_cmp = (L - CMP_LEN) // CMP_STRIDE + 1
    pe4, w1, w2 = cmp_w
    full = lambda *s: pl.BlockSpec(s, lambda b: (0,) * len(s))
    return pl.pallas_call(
        functools.partial(_compress_kernel, n_cmp=n_cmp),
        grid=(B,),
        in_specs=[pl.BlockSpec((1, L, KV_WIDTH), lambda b: (b, 0, 0)),
                  pl.BlockSpec((1, L, KV_WIDTH), lambda b: (b, 0, 1)),
                  full(*pe4.shape), full(*w1.shape), full(*w2.shape)],
        out_specs=pl.BlockSpec((1, nh, w), lambda b: (b, 0, 0)),
        out_shape=jax.ShapeDtypeStruct((B, nh, w), F32),
        compiler_params=_cparams(("parallel",)),
        name="compress_prompt",
    )(cmp_rows, cmp_rows, pe4, w1, w2)


_NEG = -1e30
_NEG_SEL = -1e9
_SEL_CHUNK = 256


def _dot_nt(a, b):
    return lax.dot_general(a, b, (((1,), (1,)), ((), ())), preferred_element_type=F32)


def _softmax_rows(s, valid):
    s = jnp.where(valid[None], s, _NEG)
    m = jnp.max(s, axis=-1, keepdims=True)
    e = jnp.where(valid[None], jnp.exp(s - m), 0.0)
    return e / jnp.maximum(jnp.sum(e, axis=-1, keepdims=True), jnp.finfo(jnp.float32).tiny)


def _heads_to_rows(q):
    tq = q.shape[0]
    lane = lax.broadcasted_iota(jnp.int32, (tq, LANES), 1)
    q = q * (A_DIM ** -0.5)
    rows = []
    for hd in range(A_HEADS):
        g = hd // A_GROUP
        tile = q[:, LANES * (hd // 2):LANES * (hd // 2 + 1)]
        if hd % 2 != g:
            tile = pltpu.roll(tile, A_DIM, 1)
        keep = (lane < A_DIM) if g == 0 else (lane >= A_DIM)
        rows.append(jnp.where(keep, tile, 0.0))
    return jnp.concatenate(rows, axis=0).astype(BF16)


def _gate_rows_to_heads(gt, o_c, o_s, o_w):
    tq = gt.shape[0]
    lane = lax.broadcasted_iota(jnp.int32, (tq, LANES), 1)
    gs = jax.nn.sigmoid(gt)
    tiles = []
    for pair in range(A_HEADS // 2):
        g = (2 * pair) // A_GROUP
        mixed = []
        for hd in (2 * pair, 2 * pair + 1):
            c = 2 * M_HEADS + 3 * hd
            r = slice(hd * tq, (hd + 1) * tq)
            mixed.append(gs[:, c:c + 1] * o_c[r] + gs[:, c + 1:c + 2] * o_s[r] + gs[:, c + 2:c + 3] * o_w[r])
        a, b = mixed
        if g == 0:
            tiles.append(jnp.where(lane < A_DIM, a, pltpu.roll(b, A_DIM, 1)))
        else:
            tiles.append(jnp.where(lane < A_DIM, pltpu.roll(a, A_DIM, 1), b))
    return jnp.concatenate(tiles, axis=1)


def _attend_two(s_a, ok_a, v_a, s_b, ok_b, v_b, v_a_transposed=False):
    nh, tq = s_a.shape[:2]
    if ok_a is not None:
        s_a = jnp.where(ok_a[None], s_a, _NEG)
    s_b = jnp.where(ok_b[None], s_b, _NEG)
    m = jnp.maximum(jnp.max(s_a, axis=-1, keepdims=True), jnp.max(s_b, axis=-1, keepdims=True))
    e_a = jnp.exp(s_a - m)
    e_b = jnp.exp(s_b - m)
    l = jnp.sum(e_a, axis=-1, keepdims=True) + jnp.sum(e_b, axis=-1, keepdims=True)
    e_a = e_a.reshape(nh * tq, -1).astype(BF16)
    o_a = _dot_nt(e_a, v_a) if v_a_transposed else jnp.dot(e_a, v_a, preferred_element_type=F32)
    o = o_a + jnp.dot(e_b.reshape(nh * tq, -1).astype(BF16), v_b, preferred_element_type=F32)
    return o / l.reshape(nh * tq, 1)


def _nsa_sample_kernel(pt_ref, q_ref, gt_ref, kc_ref, vc_ref, seln_ref, wst_ref, wnew_ref, oh_ref, pool_ref,
                       o_ref, nwin_ref, selbuf, sem, *, past):
    b = pl.program_id(0)
    slot = b % 2
    tq = q_ref.shape[0]
    nh = A_HEADS
    n_pages = pt_ref.shape[1]
    page = pool_ref.shape[2]
    wbuf = wst_ref.shape[1]
    n_sel = -(-(past + tq) // SEL_LEN)

    def copies(seq, sl):
        return [pltpu.make_async_copy(pool_ref.at[pt_ref[seq, p]], selbuf.at[sl, :, pl.ds(p * page, page)],
                                      sem.at[sl]) for p in range(n_pages)]

    @pl.when(b == 0)
    def _():
        for cp in copies(0, 0):
            cp.start()

    @pl.when(b + 1 < pl.num_programs(0))
    def _():
        for cp in copies(b + 1, 1 - slot):
            cp.start()

    qz = _heads_to_rows(q_ref[...])
    tpos = past + lax.broadcasted_iota(jnp.int32, (tq, LANES), 0)

    kc = kc_ref[...].astype(BF16)
    vc = vc_ref[...].astype(BF16)
    ncp = kc.shape[0]
    s_c = _dot_nt(qz, kc).reshape(nh, tq, ncp)
    tp_c = past + lax.broadcasted_iota(jnp.int32, (tq, ncp), 0)
    nidx = lax.broadcasted_iota(jnp.int32, (tq, ncp), 1)
    p_c = _softmax_rows(s_c, nidx * CMP_STRIDE + (CMP_LEN - 1) <= tp_c)
    o_c = jnp.dot(p_c.reshape(nh * tq, ncp).astype(BF16), vc, preferred_element_type=F32)

    cn = lax.broadcasted_iota(jnp.int32, (ncp, LANES), 0) * CMP_STRIDE
    jn = lax.broadcasted_iota(jnp.int32, (ncp, LANES), 1) * SEL_LEN
    ov = jnp.where((cn < jn + SEL_LEN) & (cn + CMP_LEN > jn), 1.0, 0.0).astype(BF16)
    jb = lax.broadcasted_iota(jnp.int32, (tq, LANES), 1)
    cur = tpos // SEL_LEN
    forced = (jb == 0) | (jb == cur) | (jb == cur - 1)
    bias = []
    for g in range(A_KV_HEADS):
        ps = p_c[g * A_GROUP]
        for r in range(1, A_GROUP):
            ps = ps + p_c[g * A_GROUP + r]
        hi = ps.astype(BF16)
        lo = (ps - hi.astype(F32)).astype(BF16)
        score = (jnp.dot(hi, ov, preferred_element_type=F32) + jnp.dot(lo, ov, preferred_element_type=F32))
        score = jnp.where(forced, FORCE_SCORE, score)
        score = jnp.where(jb * SEL_LEN <= tpos, score, -1.0)
        cnt = jnp.zeros((tq, LANES), jnp.int32)
        for k in range(n_sel):
            sk = score[:, k:k + 1]
            ahead = (sk > score) | ((sk == score) & (jb > k))
            cnt = cnt + jnp.where(ahead, 1, 0)
        bias.append(jnp.where(cnt < min(SEL_TOP, n_sel), 0.0, _NEG_SEL).astype(BF16))
    q_aug = jnp.concatenate(
        [qz, jnp.concatenate([bias[hd // A_GROUP] for hd in range(nh)], axis=0)], axis=1)

    tw = lax.broadcasted_iota(jnp.int32, (tq, wbuf), 0)
    iw = lax.broadcasted_iota(jnp.int32, (tq, wbuf), 1)
    tn = lax.broadcasted_iota(jnp.int32, (tq, tq), 0)
    un = lax.broadcasted_iota(jnp.int32, (tq, tq), 1)
    wst = wst_ref[0]
    wnew = wnew_ref[...]
    s_wa = _dot_nt(qz, wst[:, :KV_WIDTH].astype(BF16)).reshape(nh, tq, wbuf)
    s_wb = _dot_nt(qz, wnew[:, :KV_WIDTH].astype(BF16)).reshape(nh, tq, tq)
    o_w = _attend_two(s_wa, (wbuf + tw - iw < WINDOW), wst[:, KV_WIDTH:].astype(BF16),
                      s_wb, un <= tn, wnew[:, KV_WIDTH:].astype(BF16))
    nwin_ref[0, :wbuf - tq, :] = wst[tq:, :]
    nwin_ref[0, wbuf - tq:, :] = wnew

    for cp in copies(b, slot):
        cp.wait()
    seln = seln_ref[...]
    k_aug_t = jnp.concatenate([selbuf[slot, :KV_WIDTH, :].astype(BF16), oh_ref[...]], axis=0)
    nblk = (past + lax.broadcasted_iota(jnp.int32, (tq, LANES), 0)) // SEL_LEN
    oh_new = jnp.where(nblk == lax.broadcasted_iota(jnp.int32, (tq, LANES), 1), 1.0, 0.0)
    kn_aug = jnp.concatenate([seln[:, :KV_WIDTH], oh_new], axis=1).astype(BF16)
    s_sa = jnp.dot(q_aug, k_aug_t, preferred_element_type=F32).reshape(nh, tq, past)
    s_sb = _dot_nt(q_aug, kn_aug).reshape(nh, tq, tq)
    o_s = _attend_two(s_sa, None, selbuf[slot, KV_WIDTH:, :].astype(BF16), s_sb, un <= tn,
                      seln[:, KV_WIDTH:].astype(BF16), v_a_transposed=True)

    o_ref[...] = _gate_rows_to_heads(gt_ref[...], o_c, o_s, o_w)


def _nsa_sample_call(qa, gt, comp, sel_new, win_state, win_new, sel_pool, page_table, T):
    DB, n_pages = page_table.shape
    page = sel_pool.shape[2]
    past = n_pages * page
    wbuf = win_state.shape[1]
    ncp = comp.shape[0] // DB
    w2 = 2 * KV_WIDTH
    assert past % SEL_LEN == 0 and T < CMP_STRIDE and T % 8 == 0 and wbuf == WINDOW and past >= WINDOW
    assert page % LANES == 0
    onehot = (jnp.arange(LANES)[:, None] == jnp.arange(past)[None, :] // SEL_LEN).astype(BF16)
    tok = lambda w: pl.BlockSpec((T, w), lambda b, pt: (b, 0))
    grid_spec = pltpu.PrefetchScalarGridSpec(
        num_scalar_prefetch=1,
        grid=(DB,),
        in_specs=[tok(A_WIDTH), tok(LANES),
                  pl.BlockSpec((ncp, KV_WIDTH), lambda b, pt: (b, 0)),
                  pl.BlockSpec((ncp, KV_WIDTH), lambda b, pt: (b, 1)),
                  tok(w2),
                  pl.BlockSpec((1, wbuf, w2), lambda b, pt: (b, 0, 0)),
                  tok(w2),
                  pl.BlockSpec((LANES, past), lambda b, pt: (0, 0)),
                  pl.BlockSpec(memory_space=pl.ANY)],
        out_specs=[tok(A_WIDTH), pl.BlockSpec((1, wbuf, w2), lambda b, pt: (b, 0, 0))],
        scratch_shapes=[pltpu.VMEM((2, w2, past), F32), pltpu.SemaphoreType.DMA((2,))],
    )
    return pl.pallas_call(
        functools.partial(_nsa_sample_kernel, past=past),
        grid_spec=grid_spec,
        out_shape=[jax.ShapeDtypeStruct((DB * T, A_WIDTH), F32), jax.ShapeDtypeStruct((DB, wbuf, w2), F32)],
        compiler_params=_cparams(("arbitrary",)),
        name="nsa_sample",
    )(page_table, qa, gt, comp, comp, sel_new, win_state, win_new, onehot, sel_pool)


def _value_with_ones(v, g):
    lane = lax.broadcasted_iota(jnp.int32, v.shape, 1)
    keep = (lane < A_DIM) if g == 0 else (lane >= A_DIM)
    return jnp.where(keep, v, 1.0).astype(BF16)


def _pv_with_sums(p, v):
    half = p.shape[0] // A_KV_HEADS
    return jnp.concatenate(
        [jnp.dot(p[g * half:(g + 1) * half], _value_with_ones(v, g), preferred_element_type=F32)
         for g in range(A_KV_HEADS)], axis=0)


def _normalise_rows(acc):
    half = acc.shape[0] // A_KV_HEADS
    return jnp.concatenate(
        [acc[:half] * (1.0 / acc[:half, A_DIM:A_DIM + 1]), acc[half:] * (1.0 / acc[half:, 0:1])], axis=0)


def _nsa_prompt_kernel(q_ref, gt_ref, kc_ref, vc_ref, sel_ref, win_ref, o_ref, *, seq):
    tq = Q_BLOCK
    nh = A_HEADS
    n_sel = seq // SEL_LEN
    s0 = pl.program_id(1) * tq
    qz = _heads_to_rows(q_ref[0])

    kc = kc_ref[0].astype(BF16)
    vc = vc_ref[0].astype(BF16)
    ncp = kc.shape[0]
    s_c = _dot_nt(qz, kc).reshape(nh, tq, ncp)
    tpos = s0 + lax.broadcasted_iota(jnp.int32, (tq, ncp), 0)
    nidx = lax.broadcasted_iota(jnp.int32, (tq, ncp), 1)
    p_c = _softmax_rows(s_c, nidx * CMP_STRIDE + (CMP_LEN - 1) <= tpos)
    o_c = jnp.dot(p_c.reshape(nh * tq, ncp).astype(BF16), vc, preferred_element_type=F32)
    mass = [sum(p_c[g * A_GROUP + r] for r in range(1, A_GROUP)) + p_c[g * A_GROUP] for g in range(A_KV_HEADS)]

    jn = lax.broadcasted_iota(jnp.int32, (n_sel, ncp), 0) * SEL_LEN
    cn = lax.broadcasted_iota(jnp.int32, (n_sel, ncp), 1) * CMP_STRIDE
    ov_t = jnp.where((cn < jn + SEL_LEN) & (cn + CMP_LEN > jn), 1.0, 0.0).astype(BF16)
    jb = lax.broadcasted_iota(jnp.int32, (n_sel, tq), 0)
    tp = s0 + lax.broadcasted_iota(jnp.int32, (n_sel, tq), 1)
    cur = tp // SEL_LEN
    forced = (jb == 0) | (jb == cur) | (jb == cur - 1)
    bias = []
    for g in range(A_KV_HEADS):
        hi = mass[g].astype(BF16)
        lo = (mass[g] - hi.astype(F32)).astype(BF16)
        score = _dot_nt(ov_t, hi) + _dot_nt(ov_t, lo)
        score = jnp.where(forced, FORCE_SCORE, score)
        score = jnp.where(jb * SEL_LEN <= tp, score, -1.0)
        cnt = jnp.zeros((n_sel, tq), jnp.int32)
        for k in range(n_sel):
            rk = score[k:k + 1, :]
            ahead = (rk > score) | ((rk == score) & (jb > k))
            cnt = cnt + jnp.where(ahead, 1, 0)
        bias_t = jnp.where(cnt < min(SEL_TOP, n_sel), 0.0, _NEG_SEL)
        if n_sel < LANES:
            bias_t = jnp.concatenate([bias_t, jnp.zeros((LANES - n_sel, tq), F32)], axis=0)
        bias.append(bias_t.T.astype(BF16))
    q_aug = jnp.concatenate(
        [qz, jnp.concatenate([bias[hd // A_GROUP] for hd in range(nh)], axis=0)], axis=1)

    wk = WINDOW + tq
    w0 = pl.multiple_of(jnp.clip(s0 - WINDOW, 0, seq - wk), tq)
    kvw = win_ref[0, pl.ds(w0, wk), :]
    s_w = _dot_nt(qz, kvw[:, :KV_WIDTH].astype(BF16)).reshape(nh, tq, wk)
    dist = (s0 + lax.broadcasted_iota(jnp.int32, (tq, wk), 0)) - (w0 + lax.broadcasted_iota(jnp.int32, (tq, wk), 1))
    s_w = jnp.where(((dist >= 0) & (dist < WINDOW))[None], s_w, _NEG)
    e_w = jnp.exp(s_w - jnp.max(s_w, axis=-1, keepdims=True)).reshape(nh * tq, wk).astype(BF16)
    o_w = _normalise_rows(_pv_with_sums(e_w, kvw[:, KV_WIDTH:]))

    kc_n = _SEL_CHUNK

    def chunk(c, carry, causal):
        m, acc = carry
        k0 = pl.multiple_of(c * kc_n, kc_n)
        kv = sel_ref[0, pl.ds(k0, kc_n), :]
        kblk = (k0 + lax.broadcasted_iota(jnp.int32, (kc_n, LANES), 0)) // SEL_LEN
        onehot = jnp.where(kblk == lax.broadcasted_iota(jnp.int32, (kc_n, LANES), 1), 1.0, 0.0)
        k_aug = jnp.concatenate([kv[:, :KV_WIDTH], onehot], axis=1).astype(BF16)
        s = _dot_nt(q_aug, k_aug)
        if causal:
            kpos = k0 + lax.broadcasted_iota(jnp.int32, (tq, kc_n), 1)
            qpos = s0 + lax.broadcasted_iota(jnp.int32, (tq, kc_n), 0)
            s = jnp.where((kpos <= qpos)[None], s.reshape(nh, tq, kc_n), _NEG_SEL).reshape(nh * tq, kc_n)
        m_new = jnp.maximum(m, jnp.max(s, axis=-1, keepdims=True))
        p = jnp.exp(s - m_new).astype(BF16)
        acc = jnp.exp(m - m_new) * acc + _pv_with_sums(p, kv[:, KV_WIDTH:])
        return m_new, acc

    n_ch = (s0 + tq - 1) // kc_n + 1
    init = (jnp.full((nh * tq, 1), _NEG, F32), jnp.zeros((nh * tq, LANES), F32))
    carry = lax.fori_loop(0, n_ch - 1, lambda c, cr: chunk(c, cr, False), init)
    o_s = _normalise_rows(chunk(n_ch - 1, carry, True)[1])

    o_ref[0] = _gate_rows_to_heads(gt_ref[0], o_c, o_s, o_w)


def _nsa_prompt_call(qa, gt, comp, sel_kv, win_kv):
    B, S, _ = qa.shape
    assert S % _SEL_CHUNK == 0 and S >= WINDOW + Q_BLOCK and S // SEL_LEN <= LANES
    ncp = comp.shape[1]
    per_b = lambda r, w: pl.BlockSpec((1, r, w), lambda b, i: (b, 0, 0))
    return pl.pallas_call(
        functools.partial(_nsa_prompt_kernel, seq=S),
        grid=(B, S // Q_BLOCK),
        in_specs=[pl.BlockSpec((1, Q_BLOCK, A_WIDTH), lambda b, i: (b, i, 0)),
                  pl.BlockSpec((1, Q_BLOCK, LANES), lambda b, i: (b, i, 0)),
                  pl.BlockSpec((1, ncp, KV_WIDTH), lambda b, i: (b, 0, 0)),
                  pl.BlockSpec((1, ncp, KV_WIDTH), lambda b, i: (b, 0, 1)),
                  per_b(S, 2 * KV_WIDTH), per_b(S, 2 * KV_WIDTH)],
        out_specs=pl.BlockSpec((1, Q_BLOCK, A_WIDTH), lambda b, i: (b, i, 0)),
        out_shape=jax.ShapeDtypeStruct((B, S, A_WIDTH), F32),
        compiler_params=_cparams(("parallel", "arbitrary")),
        name="nsa_prompt",
    )(qa, gt, comp, comp, sel_kv, win_kv)


def _moe(h2, logits, moe_w, n_experts):
    w_gu, b_gu, w_down, b_down = moe_w
    n, d = h2.shape
    gate_w, dest, src, blk_e, n_used = _route(logits, n_experts)
    yb = _ffn(h2[src], blk_e, n_used, w_gu, b_gu, w_down, b_down)
    return yb[dest.T], gate_w


def kernel(x_prompt, x_sample, cache_cmp, cache_sel, state_win, state_C, state_n, state_m, page_table,
           c_prompt, c_sample, w_ada, b_ada, w_in, b_in, m_norm_g, cmp_pe_k, cmp_w1_k, cmp_w2_k,
           cmp_pe_v, cmp_w1_v, cmp_w2_v, w_out, ln1_g, ln1_b, w_router, b_router, w_gu, b_gu,
           w_down, b_down, ln2_g, ln2_b):
    B, S, d = x_prompt.shape
    DB, T, _ = x_sample.shape
    depth = w_ada.shape[0]
    n_experts = w_router.shape[-1]
    alpha = (2 * depth) ** 0.25
    n_pool, page = cache_cmp.shape[1:3]
    past_len = page_table.shape[1] * page
    wbuf = state_win.shape[2]
    pos_p = jnp.arange(S)
    pos_s = past_len + jnp.arange(T)
    tt_p = min(S, 256)
    bb_s = min(DB, max(1, 256 // T))
    n_cmp_s = (past_len + T - CMP_LEN) // CMP_STRIDE + 1
    assert n_cmp_s < past_len // CMP_STRIDE
    r3 = lambda a: a.reshape(B, S, a.shape[-1])
    kv5 = lambda a, n, t: a.reshape(n, t, 2, A_KV_HEADS, A_DIM)
    y_prompt, y_sample = x_prompt, x_sample
    outs = [[] for _ in range(12)]
    for l in range(depth):
        cmp_l = (cmp_pe_k[l], cmp_w1_k[l], cmp_w2_k[l], cmp_pe_v[l], cmp_w1_v[l], cmp_w2_v[l])
        moe_w = (w_gu[l], b_gu[l], w_down[l], b_down[l])
        post1_w = (w_out[l], ln1_g[l], ln1_b[l], w_router[l], b_router[l])
        w_r, b_r = _prep_w_in(w_in[l], b_in[l])
        cmp_w = _compress_weights(cmp_l)
        c_all = jnp.concatenate([c_prompt, c_sample], axis=0)
        c_all = jnp.pad(c_all, ((0, -c_all.shape[0] % 8), (0, 0)))
        mod_all = _ada(c_all, w_ada[l], b_ada[l]).reshape(-1, 6, d)
        mod_p, mod_s = mod_all[:B], mod_all[B:B + DB]

        mq, gt, qa, cmp_p, sel_p, win_p, cmp_t, sel_t, win_t = _pre(y_prompt, mod_p, pos_p, w_r, b_r, 1, tt_p,
                                                                    feature_major=True)
        zc = jnp.zeros((B, M_HEADS, M_DIM, M_DIM), F32)
        mo, C_p, n_p, m_p = _mlstm_call(r3(mq), r3(gt), zc, zc[..., 0], zc[..., 0, 0], m_norm_g[l],
                                        math.gcd(B, 4))
        comp = _compress_prompt(r3(cmp_p), cmp_w)
        ma = _nsa_prompt_call(r3(qa), r3(gt), comp, r3(sel_p), r3(win_p))
        x1_p, h2_p, lg_p = _post1(mo.reshape(B * S, M_WIDTH), ma.reshape(B * S, A_WIDTH), y_prompt, mod_p,
                                  *post1_w, 1, tt_p, alpha)

        mq, gt, qa, cmp_s, sel_s, win_s = _pre(y_sample, mod_s, pos_s, w_r, b_r, bb_s, T)
        mo, C_s, n_s, m_s = _mlstm_call(mq.reshape(DB, T, -1), gt.reshape(DB, T, -1), state_C[l],
                                        state_n[l], state_m[l], m_norm_g[l], math.gcd(DB, 4))
        feature_major = lambda pool: pool.transpose(0, 2, 3, 4, 1).reshape(n_pool, 2 * KV_WIDTH, page)
        comp = _compress_sample(feature_major(cache_cmp[l]), page_table, cmp_w, n_cmp_s, math.gcd(DB, 4))
        ma, new_win = _nsa_sample_call(qa, gt, comp, sel_s, state_win[l].reshape(DB, wbuf, 2 * KV_WIDTH),
                                       win_s, feature_major(cache_sel[l]), page_table, T)
        x1_s, h2_s, lg_s = _post1(mo.reshape(DB * T, M_WIDTH), ma, y_sample, mod_s, *post1_w, bb_s, T, alpha)

        yg, gate_w = _moe(jnp.concatenate([h2_p, h2_s], axis=0), jnp.concatenate([lg_p, lg_s], axis=0),
                          moe_w, n_experts)
        y_prompt = _post2(yg, gate_w, 0, x1_p, mod_p, ln2_g[l], ln2_b[l], 1, tt_p, alpha)
        y_sample = _post2(yg, gate_w, B * S, x1_s, mod_s, ln2_g[l], ln2_b[l], bb_s, T, alpha)

        kv5_t = lambda a: a.reshape(B, 2, A_KV_HEADS, A_DIM, -1).transpose(0, 4, 1, 2, 3)
        new = (kv5_t(cmp_t), kv5_t(sel_t), kv5_t(win_t[:, :, -min(WINDOW, S):]), C_p, n_p, m_p,
               kv5(cmp_s, DB, T), kv5(sel_s, DB, T), new_win.reshape(state_win.shape[1:]), C_s, n_s, m_s)
        for lst, v in zip(outs, new):
            lst.append(v)
    return (y_prompt, y_sample, *[jnp.stack(v) for v in outs])
```

```python
import functools
import math

import numpy as np
import jax
import jax.numpy as jnp
from jax import lax
from jax.experimental import pallas as pl
from jax.experimental.pallas import tpu as pltpu

F32 = jnp.float32
BF16 = jnp.bfloat16

M_HEADS = 4
M_DIM = 128
M_WIDTH = M_HEADS * M_DIM
M_CHUNK = 64
A_HEADS = 8
A_KV_HEADS = 2
A_GROUP = A_HEADS // A_KV_HEADS
A_DIM = 64
A_WIDTH = A_HEADS * A_DIM
KV_WIDTH = A_KV_HEADS * A_DIM
CMP_LEN = 32
CMP_STRIDE = 16
SEL_LEN = 64
SEL_TOP = 16
WINDOW = 512
Q_BLOCK = 128
FORCE_SCORE = 1.0e4
ROPE_THETA = 500000.0
ROPE_DIM = A_DIM // 4
TOP_K = 4
SWIGLU_LIMIT = 7.0
SWIGLU_ALPHA = 1.702
LN_EPS = 1e-5
LANES = 128
MOE_ROWS = 256
VMEM_LIMIT = 56 * 1024 * 1024

_O_MQ = 0
_O_IF = 4 * M_WIDTH
_O_QA = _O_IF + 2 * M_HEADS
_O_KV = _O_QA + A_WIDTH
_O_GA = _O_KV + 6 * KV_WIDTH
_N_IN = _O_GA + 3 * A_HEADS
_R_QA = 4 * M_WIDTH
_R_KV = _R_QA + A_WIDTH
_R_GT = _R_KV + 6 * KV_WIDTH
_R_END = _R_GT + LANES


def _cparams(sem):
    return pltpu.CompilerParams(dimension_semantics=sem, vmem_limit_bytes=VMEM_LIMIT)


def _ln_core(x):
    mu = jnp.mean(x, axis=-1, keepdims=True)
    xc = x - mu
    var = jnp.mean(xc * xc, axis=-1, keepdims=True)
    return xc * lax.rsqrt(var + LN_EPS)


def _ada_kernel(c_ref, w_ref, b_ref, o_ref):
    o_ref[...] = jnp.dot(c_ref[...].astype(BF16), w_ref[...].astype(BF16),
                         preferred_element_type=F32) + b_ref[...]


def _ada(c, w_ada, b_ada):
    n, d = c.shape
    cols = w_ada.shape[1]
    return pl.pallas_call(
        _ada_kernel,
        grid=(cols // d,),
        in_specs=[pl.BlockSpec((n, d), lambda j: (0, 0)),
                  pl.BlockSpec((d, d), lambda j: (0, j)),
                  pl.BlockSpec((1, d), lambda j: (0, j))],
        out_specs=pl.BlockSpec((n, d), lambda j: (0, j)),
        out_shape=jax.ShapeDtypeStruct((n, cols), F32),
        compiler_params=_cparams(("arbitrary",)),
        name="ada",
    )(c, w_ada, b_ada.reshape(1, cols))


def _rope_apply(v, cos, sa, sb):
    reps = v.shape[1] // LANES
    tile = lambda t: t if reps == 1 else jnp.concatenate([t] * reps, axis=1)
    w = v.shape[1]
    return (v * tile(cos) + pltpu.roll(v, w - ROPE_DIM // 2, 1) * tile(sa)
            + pltpu.roll(v, ROPE_DIM // 2, 1) * tile(sb))


def _pre_kernel(x_ref, mod_ref, cos_ref, sa_ref, sb_ref, w_ref, b_ref,
                mq_ref, gt_ref, qa_ref, cmp_ref, sel_ref, win_ref, *t_refs):
    bb, tt, d = x_ref.shape
    mod = mod_ref[...]
    h = _ln_core(x_ref[...]) * (1.0 + mod[:, 1:2, :]) + mod[:, 0:1, :]
    h = h.reshape(bb * tt, d).astype(BF16)
    z = jnp.dot(h, w_ref[...], preferred_element_type=F32) + b_ref[...]
    cos, sa, sb = cos_ref[...], sa_ref[...], sb_ref[...]
    mq_ref[...] = z[:, :_R_QA]
    gt_ref[...] = z[:, _R_GT:_R_END]
    qa_ref[...] = _rope_apply(z[:, _R_QA:_R_KV], cos, sa, sb)
    for n, ref in enumerate((cmp_ref, sel_ref, win_ref)):
        o = _R_KV + 2 * KV_WIDTH * n
        kv = jnp.concatenate([_rope_apply(z[:, o:o + KV_WIDTH], cos, sa, sb),
                              z[:, o + KV_WIDTH:o + 2 * KV_WIDTH]], axis=1)
        ref[...] = kv
        if t_refs:
            kv_t = kv.T
            t_refs[n][0] = kv_t
            if n > 0:
                for c in range(kv_t.shape[1] // LANES):
                    t_refs[2 + n][0, c] = kv_t[KV_WIDTH:, c * LANES:(c + 1) * LANES]


def _rope_tables(pos):
    half = ROPE_DIM // 2
    inv = ROPE_THETA ** (-jnp.arange(0, ROPE_DIM, 2, dtype=F32) / ROPE_DIM)
    ang = pos.astype(F32)[:, None] * inv[None, :]
    cos, sin = jnp.cos(ang), jnp.sin(ang)
    n = pos.shape[0]
    one = jnp.ones((n, A_DIM - ROPE_DIM), F32)
    zero = jnp.zeros((n, A_DIM - ROPE_DIM), F32)
    zh = jnp.zeros((n, half), F32)
    cos_t = jnp.concatenate([cos, cos, one], axis=1)
    sa_t = jnp.concatenate([-sin, zh, zero], axis=1)
    sb_t = jnp.concatenate([zh, sin, zero], axis=1)
    two = lambda t: jnp.concatenate([t, t], axis=1)
    return two(cos_t), two(sa_t), two(sb_t)


def _prep_w_in(w_in, b_in):
    pad = LANES - 2 * M_HEADS - 3 * A_HEADS
    cat = lambda a: jnp.concatenate(
        [a[..., _O_MQ:_O_IF], a[..., _O_QA:_O_GA], a[..., _O_IF:_O_QA], a[..., _O_GA:_N_IN],
         jnp.zeros(a.shape[:-1] + (pad,), a.dtype)], axis=-1)
    return cat(w_in).astype(BF16), cat(b_in[None, :])


def _pre(x, mod, pos, w_r, b_r, bb, tt, feature_major=False):
    B, T, d = x.shape
    nt = T // tt
    rows = bb * tt
    cos, sa, sb = _rope_tables(pos)
    if bb > 1:
        cos, sa, sb = (jnp.tile(t, (bb, 1)) for t in (cos, sa, sb))
    n_tok = B * T
    tab = pl.BlockSpec((rows, LANES), lambda i, j: (j, 0))
    row = lambda w: pl.BlockSpec((rows, w), lambda i, j: (i * nt + j, 0))
    widths = (_R_QA, LANES, A_WIDTH, 2 * KV_WIDTH, 2 * KV_WIDTH, 2 * KV_WIDTH)
    out_specs = [row(w) for w in widths]
    out_shape = [jax.ShapeDtypeStruct((n_tok, w), F32) for w in widths]
    if feature_major:
        assert bb == 1 and tt % LANES == 0
        out_specs += [pl.BlockSpec((1, 2 * KV_WIDTH, tt), lambda i, j: (i, 0, j))] * 3
        out_shape += [jax.ShapeDtypeStruct((B, 2 * KV_WIDTH, T), F32)] * 3
        out_specs += [pl.BlockSpec((1, tt // LANES, KV_WIDTH, LANES), lambda i, j: (i, j, 0, 0))] * 2
        out_shape += [jax.ShapeDtypeStruct((B, T // LANES, KV_WIDTH, LANES), F32)] * 2
    return pl.pallas_call(
        _pre_kernel,
        grid=(B // bb, nt),
        in_specs=[pl.BlockSpec((bb, tt, d), lambda i, j: (i, j, 0)),
                  pl.BlockSpec((bb, 6, d), lambda i, j: (i, 0, 0)),
                  tab, tab, tab,
                  pl.BlockSpec((d, _R_END), lambda i, j: (0, 0)),
                  pl.BlockSpec((1, _R_END), lambda i, j: (0, 0))],
        out_specs=out_specs,
        out_shape=out_shape,
        compiler_params=_cparams(("parallel", "arbitrary")),
        name="pre",
    )(x, mod, cos, sa, sb, w_r, b_r)


def _post1_kernel(mo_ref, ao_ref, x_ref, mod_ref, wo_ref, g_ref, b_ref, wr_ref, br_ref,
                  x1_ref, h2_ref, lg_ref, *, alpha):
    bb, tt, d = x_ref.shape
    mod = mod_ref[...]
    mixin = jnp.concatenate([mo_ref[...], ao_ref[...]], axis=1).astype(BF16)
    mix = jnp.dot(mixin, wo_ref[...], preferred_element_type=F32).reshape(bb, tt, d)
    x1 = _ln_core(alpha * x_ref[...] + mod[:, 2:3, :] * mix) * g_ref[...] + b_ref[...]
    h2 = _ln_core(x1) * (1.0 + mod[:, 4:5, :]) + mod[:, 3:4, :]
    x1_ref[...] = x1
    h2f = h2.reshape(bb * tt, d)
    h2_ref[...] = h2f
    lg_ref[...] = jnp.dot(h2f, wr_ref[...], preferred_element_type=F32,
                          precision=lax.Precision.HIGHEST) + br_ref[...]


def _post1(mo, ao, x, mod, w_out, ln_g, ln_b, w_router, b_router, bb, tt, alpha):
    B, T, d = x.shape
    nt = T // tt
    rows = bb * tt
    ne = w_router.shape[1]
    wr = jnp.pad(w_router, ((0, 0), (0, LANES - ne)))
    br = jnp.pad(b_router, (0, LANES - ne), constant_values=-jnp.inf).reshape(1, LANES)
    full = lambda *s: pl.BlockSpec(s, lambda i, j: (0,) * len(s))
    row = lambda w: pl.BlockSpec((rows, w), lambda i, j: (i * nt + j, 0))
    return pl.pallas_call(
        functools.partial(_post1_kernel, alpha=alpha),
        grid=(B // bb, nt),
        in_specs=[row(M_WIDTH), row(A_WIDTH),
                  pl.BlockSpec((bb, tt, d), lambda i, j: (i, j, 0)),
                  pl.BlockSpec((bb, 6, d), lambda i, j: (i, 0, 0)),
                  full(M_WIDTH + A_WIDTH, d), full(1, d), full(1, d), full(d, LANES), full(1, LANES)],
        out_specs=[pl.BlockSpec((bb, tt, d), lambda i, j: (i, j, 0)), row(d), row(LANES)],
        out_shape=[jax.ShapeDtypeStruct((B, T, d), F32),
                   jax.ShapeDtypeStruct((B * T, d), F32),
                   jax.ShapeDtypeStruct((B * T, LANES), F32)],
        compiler_params=_cparams(("parallel", "arbitrary")),
        name="post1",
    )(mo, ao, x, mod, w_out.astype(BF16), ln_g.reshape(1, d), ln_b.reshape(1, d), wr, br)


def _ffn_kernel(be_ref, nu_ref, x_ref, wgu_ref, bgu_ref, wd_ref, bd_ref, y_ref, wgu_bf, wd_bf):
    i = pl.program_id(0)
    dff = wd_ref.shape[1]

    @pl.when((i == 0) | (be_ref[i] != be_ref[jnp.maximum(i - 1, 0)]))
    def _():
        wgu_bf[...] = wgu_ref[0].astype(BF16)
        wd_bf[...] = wd_ref[0].astype(BF16)

    @pl.when(i < nu_ref[0])
    def _():
        gu = jnp.dot(x_ref[...].astype(BF16), wgu_bf[...], preferred_element_type=F32) + bgu_ref[0]
        g = jnp.minimum(gu[:, :dff], SWIGLU_LIMIT)
        u = jnp.clip(gu[:, dff:], -SWIGLU_LIMIT, SWIGLU_LIMIT)
        act = (u + 1.0) * (g * jax.nn.sigmoid(SWIGLU_ALPHA * g))
        y_ref[...] = jnp.dot(act.astype(BF16), wd_bf[...], preferred_element_type=F32) + bd_ref[0]

    @pl.when(i >= nu_ref[0])
    def _():
        y_ref[...] = jnp.zeros_like(y_ref)


def _ffn(xb, blk_e, n_used, w_gu, b_gu, w_down, b_down):
    rows, d = xb.shape
    ne, _, f2 = w_gu.shape
    dff = w_down.shape[1]
    nb = rows // MOE_ROWS
    grid_spec = pltpu.PrefetchScalarGridSpec(
        num_scalar_prefetch=2,
        grid=(nb,),
        in_specs=[pl.BlockSpec((MOE_ROWS, d), lambda i, be, nu: (i, 0)),
                  pl.BlockSpec((1, d, f2), lambda i, be, nu: (be[i], 0, 0)),
                  pl.BlockSpec((1, 1, f2), lambda i, be, nu: (be[i], 0, 0)),
                  pl.BlockSpec((1, dff, d), lambda i, be, nu: (be[i], 0, 0)),
                  pl.BlockSpec((1, 1, d), lambda i, be, nu: (be[i], 0, 0))],
        out_specs=pl.BlockSpec((MOE_ROWS, d), lambda i, be, nu: (i, 0)),
        scratch_shapes=[pltpu.VMEM((d, f2), BF16), pltpu.VMEM((dff, d), BF16)],
    )
    return pl.pallas_call(
        _ffn_kernel,
        grid_spec=grid_spec,
        out_shape=jax.ShapeDtypeStruct((rows, d), F32),
        compiler_params=_cparams(("arbitrary",)),
        name="ffn",
    )(blk_e, n_used, xb, w_gu, b_gu.reshape(ne, 1, f2), w_down, b_down.reshape(ne, 1, d))


def _post2_kernel(yg_ref, gw_ref, x1_ref, mod_ref, g_ref, b_ref, y_ref, *, alpha):
    bb, tt, d = x1_ref.shape
    gw = gw_ref[...]
    f = yg_ref[0] * gw[:, 0:1]
    for k in range(1, TOP_K):
        f = f + yg_ref[k] * gw[:, k:k + 1]
    y = alpha * x1_ref[...] + mod_ref[...][:, 5:6, :] * f.reshape(bb, tt, d)
    y_ref[...] = _ln_core(y) * g_ref[...] + b_ref[...]


def _post2(yg, gw, row0, x1, mod, ln_g, ln_b, bb, tt, alpha):
    B, T, d = x1.shape
    rows = bb * tt
    nt = T // tt
    blk0 = row0 // rows
    assert row0 % rows == 0
    full = lambda *s: pl.BlockSpec(s, lambda i, j: (0,) * len(s))
    return pl.pallas_call(
        functools.partial(_post2_kernel, alpha=alpha),
        grid=(B // bb, nt),
        in_specs=[pl.BlockSpec((TOP_K, rows, d), lambda i, j: (0, blk0 + i * nt + j, 0)),
                  pl.BlockSpec((rows, LANES), lambda i, j: (blk0 + i * nt + j, 0)),
                  pl.BlockSpec((bb, tt, d), lambda i, j: (i, j, 0)),
                  pl.BlockSpec((bb, 6, d), lambda i, j: (i, 0, 0)),
                  full(1, d), full(1, d)],
        out_specs=pl.BlockSpec((bb, tt, d), lambda i, j: (i, j, 0)),
        out_shape=jax.ShapeDtypeStruct((B, T, d), F32),
        compiler_params=_cparams(("parallel", "arbitrary")),
        name="post2",
    )(yg, gw, x1, mod, ln_g.reshape(1, d), ln_b.reshape(1, d))


def _lane_prefix_sum(x):
    lane = lax.broadcasted_iota(jnp.int32, x.shape, 1)
    s = 1
    while s < LANES:
        x = x + jnp.where(lane >= s, pltpu.roll(x, s, 1), 0.0)
        s *= 2
    return x


def _route_kernel(lg_ref, dest_ref, gw_ref, cnt_ref, counts, running):
    phase = pl.program_id(0)
    blk = pl.program_id(1)
    r = lg_ref.shape[0]
    lane = lax.broadcasted_iota(jnp.int32, (r, LANES), 1)

    @pl.when((phase == 0) & (blk == 0))
    def _():
        counts[...] = jnp.zeros_like(counts)

    @pl.when((phase == 1) & (blk == 0))
    def _():
        running[...] = jnp.zeros_like(running)

    vals = lg_ref[...]
    onehots, tops = [], []
    for k in range(TOP_K):
        m = jnp.max(vals, axis=1, keepdims=True)
        idx = jnp.min(jnp.where(vals == m, lane, LANES), axis=1, keepdims=True)
        hit = lane == idx
        onehots.append(hit)
        tops.append(m)
        vals = jnp.where(hit, -jnp.inf, vals)
    ohf = [jnp.where(h, 1.0, 0.0) for h in onehots]
    block_cnt = [jnp.sum(o, axis=0, keepdims=True) for o in ohf]

    @pl.when(phase == 0)
    def _():
        counts[...] += block_cnt[0] + block_cnt[1] + block_cnt[2] + block_cnt[3]

    @pl.when(phase == 1)
    def _():
        cnt = counts[...]
        padded = jnp.ceil(cnt * (1.0 / MOE_ROWS)) * MOE_ROWS
        pad_start = _lane_prefix_sum(padded) - padded
        es = [jnp.exp(t - tops[0]) for t in tops]
        den = es[0] + es[1] + es[2] + es[3]
        ti = lax.broadcasted_iota(jnp.int32, (r, r), 0)
        tj = lax.broadcasted_iota(jnp.int32, (r, r), 1)
        before = jnp.where(tj < ti, 1.0, 0.0).astype(BF16)
        base = pad_start + running[...]
        dest = jnp.zeros((r, LANES), F32)
        gw = jnp.zeros((r, LANES), F32)
        for k in range(TOP_K):
            rank = jnp.dot(before, ohf[k].astype(BF16), preferred_element_type=F32)
            d_k = jnp.sum(ohf[k] * (base + rank), axis=1, keepdims=True)
            dest = jnp.where(lane == k, d_k, dest)
            gw = jnp.where(lane == k, es[k] / den, gw)
            base = base + block_cnt[k]
        running[...] = base - pad_start
        dest_ref[...] = dest.astype(jnp.int32)
        gw_ref[...] = gw
        cnt_ref[...] = jnp.broadcast_to(cnt, cnt_ref.shape)


def _route(logits, n_experts, rows=256):
    n = logits.shape[0]
    assert n % rows == 0
    nb = n // rows
    tok = pl.BlockSpec((rows, LANES), lambda p, i: (i, 0))
    out = pl.BlockSpec((rows, LANES), lambda p, i: (i * p, 0))
    dest, gw, cnt = pl.pallas_call(
        _route_kernel,
        grid=(2, nb),
        in_specs=[tok],
        out_specs=[out, out, pl.BlockSpec((8, LANES), lambda p, i: (0, 0))],
        out_shape=[jax.ShapeDtypeStruct((n, LANES), jnp.int32), jax.ShapeDtypeStruct((n, LANES), F32),
                   jax.ShapeDtypeStruct((8, LANES), F32)],
        scratch_shapes=[pltpu.VMEM((1, LANES), F32), pltpu.VMEM((1, LANES), F32)],
        compiler_params=_cparams(("arbitrary", "arbitrary")),
        name="route",
    )(logits)
    counts = cnt[0, :n_experts].astype(jnp.int32)
    pad_end = jnp.cumsum((counts + MOE_ROWS - 1) // MOE_ROWS * MOE_ROWS)
    n_slots = n * TOP_K
    n_blocks = -(-n_slots // MOE_ROWS) + n_experts
    n_used = pad_end[-1] // MOE_ROWS
    blk = jnp.minimum(jnp.arange(n_blocks, dtype=jnp.int32), n_used - 1) * MOE_ROWS
    blk_e = jnp.minimum(jnp.sum(pad_end[None, :] <= blk[:, None], axis=1), n_experts - 1).astype(jnp.int32)
    dest4 = dest[:, :TOP_K]
    src = jnp.zeros((n_blocks * MOE_ROWS,), jnp.int32).at[dest4.reshape(-1)].set(
        jnp.arange(n_slots, dtype=jnp.int32) // TOP_K)
    return gw, dest4, src, blk_e, n_used.reshape(1).astype(jnp.int32)


def _log_sigmoid(x):
    return jnp.minimum(x, 0.0) - jnp.log(1.0 + jnp.exp(-jnp.abs(x)))


def _mlstm_kernel(mq_ref, gt_ref, gtt_ref, c0_ref, n0_ref, m0_ref, g_ref, mo_ref, c_ref, n_ref, m_ref):
    bb, L, _ = mq_ref.shape
    d = M_DIM

    @pl.when(pl.program_id(1) == 0)
    def _():
        c_ref[...] = c0_ref[...]
        n_ref[...] = n0_ref[...]
        m_ref[...] = m0_ref[...]

    tt = lax.broadcasted_iota(jnp.int32, (L, L), 0)
    ss = lax.broadcasted_iota(jnp.int32, (L, L), 1)
    causal = ss <= tt
    pairs = [(s, h) for s in range(bb) for h in range(M_HEADS)]
    col = lambda s, h, off: mq_ref[s, :, off + h * d:off + (h + 1) * d]
    old = {p: (c_ref[p[0], p[1]], n_ref[p[0], p[1]:p[1] + 1, :], m_ref[p[0], :, p[1]:p[1] + 1]) for p in pairs}

    qb = {p: col(*p, 0).astype(BF16) for p in pairs}
    kb = {p: (col(*p, M_WIDTH) * (d ** -0.5)).astype(BF16) for p in pairs}
    qk = {p: _dot_nt(qb[p], kb[p]) for p in pairs}
    qc = {p: _dot_nt(qb[p], old[p][0].astype(BF16)) for p in pairs}

    i_c = {(s, h): gt_ref[s][:, h:h + 1] for s, h in pairs}
    i_r = {(s, h): gtt_ref[s, 0][h:h + 1, :] for s, h in pairs}
    lf_c = {(s, h): _log_sigmoid(gt_ref[s][:, M_HEADS + h:M_HEADS + h + 1]) for s, h in pairs}
    lf_r = {(s, h): _log_sigmoid(gtt_ref[s, 0][M_HEADS + h:M_HEADS + h + 1, :]) for s, h in pairs}
    b_c = {p: jnp.sum(jnp.where(causal, lf_r[p], 0.0), axis=1, keepdims=True) for p in pairs}
    b_r = {p: jnp.sum(jnp.where(causal, 0.0, lf_c[p]), axis=0, keepdims=True) + lf_r[p] for p in pairs}

    dmat = {p: jnp.where(causal, b_c[p] - b_r[p] + i_r[p], _NEG) for p in pairs}
    dmax = {p: jnp.max(dmat[p], axis=1, keepdims=True) for p in pairs}
    m_t = {p: jnp.maximum(b_c[p] + old[p][2], dmax[p]) for p in pairs}
    inter = {p: jnp.exp(b_c[p] + old[p][2] - m_t[p]) for p in pairs}
    a = {p: jnp.exp(dmat[p] - m_t[p]) * qk[p] for p in pairs}
    av = {p: jnp.dot(a[p].astype(BF16), col(*p, 2 * M_WIDTH).astype(BF16), preferred_element_type=F32)
          for p in pairs}

    qn = {p: jnp.sum(col(*p, 0) * old[p][1], axis=1, keepdims=True) for p in pairs}
    asum = {p: jnp.sum(a[p], axis=1, keepdims=True) for p in pairs}
    hv = {p: (inter[p] * qc[p] + av[p]) / jnp.maximum(jnp.abs(inter[p] * qn[p] + asum[p]), jnp.exp(-m_t[p]))
          for p in pairs}
    mu = {p: jnp.mean(hv[p], axis=1, keepdims=True) for p in pairs}
    hc = {p: hv[p] - mu[p] for p in pairs}
    var = {p: jnp.mean(hc[p] * hc[p], axis=1, keepdims=True) for p in pairs}
    out = {(s, h): (hc[s, h] * lax.rsqrt(var[s, h] + LN_EPS) * g_ref[:, h * d:(h + 1) * d])
           * jax.nn.sigmoid(col(s, h, 3 * M_WIDTH)) for s, h in pairs}

    new = {}
    for p in pairs:
        c_old, n_old, m_old = old[p]
        k = col(*p, M_WIDTH) * (d ** -0.5)
        m_new = m_t[p][L - 1:L, :]
        b_last = b_c[p][L - 1:L, :]
        w_src = jnp.exp(b_last - b_c[p] + i_c[p] - m_new)
        w_old = jnp.exp(b_last + m_old - m_new)
        c_new = w_old * c_old + lax.dot_general(
            (w_src * col(*p, 2 * M_WIDTH)).astype(BF16), kb[p], (((0,), (0,)), ((), ())),
            preferred_element_type=F32)
        n_new = w_old * n_old + jnp.sum(w_src * k, axis=0, keepdims=True)
        new[p] = (c_new, n_new, m_new, out[p])
    for (s, h), (c_new, n_new, m_new, out) in new.items():
        c_ref[s, h] = c_new
        n_ref[s, h:h + 1, :] = n_new
        m_ref[s, :, h:h + 1] = m_new
        mo_ref[s, :, h * d:(h + 1) * d] = out


def _mlstm_call(mq, gt, C0, n0, m0, g, bb):
    B, T, _ = mq.shape
    L = math.gcd(T, M_CHUNK)
    nc = T // L
    gtt = gt[:, :, :2 * M_HEADS].reshape(B, nc, L, 2 * M_HEADS).transpose(0, 1, 3, 2)
    st4 = pl.BlockSpec((bb, M_HEADS, M_DIM, M_DIM), lambda i, c: (i, 0, 0, 0))
    st3 = pl.BlockSpec((bb, M_HEADS, M_DIM), lambda i, c: (i, 0, 0))
    st2 = pl.BlockSpec((bb, 1, M_HEADS), lambda i, c: (i, 0, 0))
    mo, C, n, m = pl.pallas_call(
        _mlstm_kernel,
        grid=(B // bb, nc),
        in_specs=[pl.BlockSpec((bb, L, 4 * M_WIDTH), lambda i, c: (i, c, 0)),
                  pl.BlockSpec((bb, L, LANES), lambda i, c: (i, c, 0)),
                  pl.BlockSpec((bb, 1, 2 * M_HEADS, L), lambda i, c: (i, c, 0, 0)),
                  st4, st3, st2,
                  pl.BlockSpec((1, M_WIDTH), lambda i, c: (0, 0))],
        out_specs=[pl.BlockSpec((bb, L, M_WIDTH), lambda i, c: (i, c, 0)), st4, st3, st2],
        out_shape=[jax.ShapeDtypeStruct((B, T, M_WIDTH), F32),
                   jax.ShapeDtypeStruct(C0.shape, F32), jax.ShapeDtypeStruct(n0.shape, F32),
                   jax.ShapeDtypeStruct((B, 1, M_HEADS), F32)],
        compiler_params=_cparams(("parallel", "arbitrary")),
        name="mlstm",
    )(mq, gt, gtt, C0, n0, m0.reshape(B, 1, M_HEADS), g.reshape(1, M_WIDTH))
    return mo, C, n, m.reshape(B, M_HEADS)


def _gelu_tanh(x):
    return x * (0.5 * (1.0 + jnp.tanh(math.sqrt(2.0 / math.pi) * (x + 0.044715 * (x * x * x)))))


def _compress_body(load, pe_ref, w1_ref, w2_ref, rows_total, nh, n_cmp):
    w = KV_WIDTH
    half = CMP_LEN // 2
    pa = jnp.zeros((rows_total, w), F32)
    pb = jnp.zeros((rows_total, w), F32)
    for j in range(half):
        xj = load(j)
        pa = pa + jnp.dot((xj + pe_ref[j:j + 1, :]).astype(BF16), w1_ref[j], preferred_element_type=F32)
        pb = pb + jnp.dot((xj + pe_ref[half + j:half + j + 1, :]).astype(BF16), w1_ref[half + j],
                          preferred_element_type=F32)
    hid = pa + pltpu.roll(pb, rows_total - 1, 0)
    y = jnp.dot(_gelu_tanh(hid).astype(BF16), w2_ref[...], preferred_element_type=F32)
    rows = lax.broadcasted_iota(jnp.int32, (rows_total, w), 0)
    assert nh & (nh - 1) == 0
    return jnp.where((rows & (nh - 1)) < n_cmp, y, 0.0)


def _compress_kernel(xk_ref, xv_ref, pe_ref, w1_ref, w2_ref, o_ref, *, n_cmp):
    nh = xk_ref.shape[1] // CMP_STRIDE
    for kv, x_ref in enumerate((xk_ref, xv_ref)):
        o_ref[0, :, kv * KV_WIDTH:(kv + 1) * KV_WIDTH] = _compress_body(
            lambda j: x_ref[0, pl.ds(j, nh, stride=CMP_STRIDE), :], pe_ref.at[kv], w1_ref.at[kv], w2_ref.at[kv],
            nh, nh, n_cmp)


def _compress_sample_kernel(pt_ref, pool_ref, pe_ref, w1_ref, w2_ref, o_ref, stage, xrow_k, xrow_v, sem, *,
                            seqs, n_pages, n_cmp):
    i = pl.program_id(0)
    slot = i % 2
    page = pool_ref.shape[2]
    nh = n_pages * page // CMP_STRIDE

    def copies(step, sl):
        return [pltpu.make_async_copy(pool_ref.at[pt_ref[step * seqs + s, p], pl.ds(kv * KV_WIDTH, KV_WIDTH), :],
                                      stage.at[sl, kv, s * n_pages + p], sem.at[sl])
                for s in range(seqs) for p in range(n_pages) for kv in range(2)]

    @pl.when(i == 0)
    def _():
        for cp in copies(0, 0):
            cp.start()

    @pl.when(i + 1 < pl.num_programs(0))
    def _():
        for cp in copies(i + 1, 1 - slot):
            cp.start()

    for cp in copies(i, slot):
        cp.wait()
    xrows = (xrow_k, xrow_v)
    for kv in range(2):
        for pi in range(seqs * n_pages):
            xrows[kv][pi * page:(pi + 1) * page, :] = stage[slot, kv, pi].T
    for kv in range(2):
        o_ref[:, kv * KV_WIDTH:(kv + 1) * KV_WIDTH] = _compress_body(
            lambda j: xrows[kv][pl.ds(j, seqs * nh, stride=CMP_STRIDE), :],
            pe_ref.at[kv], w1_ref.at[kv], w2_ref.at[kv], seqs * nh, nh, n_cmp)


def _compress_sample(pool, page_table, cmp_w, n_cmp, seqs):
    n_pool, w, page = pool.shape
    DB, n_pages = page_table.shape
    nh = n_pages * page // CMP_STRIDE
    pe4, w1, w2 = cmp_w
    full = lambda *s: pl.BlockSpec(s, lambda i, pt: (0,) * len(s))
    grid_spec = pltpu.PrefetchScalarGridSpec(
        num_scalar_prefetch=1,
        grid=(DB // seqs,),
        in_specs=[pl.BlockSpec(memory_space=pl.ANY), full(*pe4.shape), full(*w1.shape), full(*w2.shape)],
        out_specs=pl.BlockSpec((seqs * nh, w), lambda i, pt: (i, 0)),
        scratch_shapes=[pltpu.VMEM((2, 2, seqs * n_pages, KV_WIDTH, page), F32),
                        pltpu.VMEM((seqs * n_pages * page, KV_WIDTH), F32),
                        pltpu.VMEM((seqs * n_pages * page, KV_WIDTH), F32),
                        pltpu.SemaphoreType.DMA((2,))],
    )
    return pl.pallas_call(
        functools.partial(_compress_sample_kernel, seqs=seqs, n_pages=n_pages, n_cmp=n_cmp),
        grid_spec=grid_spec,
        out_shape=jax.ShapeDtypeStruct((DB * nh, w), F32),
        compiler_params=_cparams(("arbitrary",)),
        name="compress_sample",
    )(page_table, pool, pe4, w1, w2)


def _block_diag2(a):
    z = jnp.zeros_like(a)
    return jnp.concatenate([jnp.concatenate([a, z], -1), jnp.concatenate([z, a], -1)], -2)


def _compress_weights(cmp_params):
    pe_k, w1_k, w2_k, pe_v, w1_v, w2_v = cmp_params
    pe = jnp.stack([jnp.concatenate([p, p], axis=1) for p in (pe_k, pe_v)])
    w1 = jnp.stack([_block_diag2(w.reshape(CMP_LEN, A_DIM, A_DIM)) for w in (w1_k, w1_v)]).astype(BF16)
    w2 = jnp.stack([_block_diag2(w) for w in (w2_k, w2_v)]).astype(BF16)
    return pe, w1, w2


def _compress_prompt(cmp_rows, cmp_w):
    B, L, w = cmp_rows.shape
    nh = L // CMP_STRIDE
    n_cmp = (L - CMP_LEN) // CMP_STRIDE + 1
    pe4, w1, w2 = cmp_w
    full = lambda *s: pl.BlockSpec(s, lambda b: (0,) * len(s))
    return pl.pallas_call(
        functools.partial(_compress_kernel, n_cmp=n_cmp),
        grid=(B,),
        in_specs=[pl.BlockSpec((1, L, KV_WIDTH), lambda b: (b, 0, 0)),
                  pl.BlockSpec((1, L, KV_WIDTH), lambda b: (b, 0, 1)),
                  full(*pe4.shape), full(*w1.shape), full(*w2.shape)],
        out_specs=pl.BlockSpec((1, nh, w), lambda b: (b, 0, 0)),
        out_shape=jax.ShapeDtypeStruct((B, nh, w), F32),
        compiler_params=_cparams(("parallel",)),
        name="compress_prompt",
    )(cmp_rows, cmp_rows, pe4, w1, w2)


_NEG = -1e30
_NEG_SEL = -1e9
_SEL_CHUNK = 512


def _dot_nt(a, b):
    return lax.dot_general(a, b, (((1,), (1,)), ((), ())), preferred_element_type=F32)


def _softmax_rows(s, valid):
    s = jnp.where(valid[None], s, _NEG)
    m = jnp.max(s, axis=-1, keepdims=True)
    e = jnp.where(valid[None], jnp.exp(s - m), 0.0)
    return e / jnp.maximum(jnp.sum(e, axis=-1, keepdims=True), jnp.finfo(jnp.float32).tiny)


def _heads_to_rows(q):
    tq = q.shape[0]
    lane = lax.broadcasted_iota(jnp.int32, (tq, LANES), 1)
    q = q * (A_DIM ** -0.5)
    rows = []
    for hd in range(A_HEADS):
        g = hd // A_GROUP
        tile = q[:, LANES * (hd // 2):LANES * (hd // 2 + 1)]
        if hd % 2 != g:
            tile = pltpu.roll(tile, A_DIM, 1)
        keep = (lane < A_DIM) if g == 0 else (lane >= A_DIM)
        rows.append(jnp.where(keep, tile, 0.0))
    return jnp.concatenate(rows, axis=0).astype(BF16)


def _gate_rows_to_heads(gt, o_c, o_s, o_w):
    tq = gt.shape[0]
    lane = lax.broadcasted_iota(jnp.int32, (tq, LANES), 1)
    gs = jax.nn.sigmoid(gt)
    tiles = []
    for pair in range(A_HEADS // 2):
        g = (2 * pair) // A_GROUP
        mixed = []
        for hd in (2 * pair, 2 * pair + 1):
            c = 2 * M_HEADS + 3 * hd
            r = slice(hd * tq, (hd + 1) * tq)
            mixed.append(gs[:, c:c + 1] * o_c[r] + gs[:, c + 1:c + 2] * o_s[r] + gs[:, c + 2:c + 3] * o_w[r])
        a, b = mixed
        if g == 0:
            tiles.append(jnp.where(lane < A_DIM, a, pltpu.roll(b, A_DIM, 1)))
        else:
            tiles.append(jnp.where(lane < A_DIM, pltpu.roll(a, A_DIM, 1), b))
    return jnp.concatenate(tiles, axis=1)


def _attend_two(s_a, ok_a, v_a, s_b, ok_b, v_b, v_a_transposed=False):
    nh, tq = s_a.shape[:2]
    if ok_a is not None:
        s_a = jnp.where(ok_a[None], s_a, _NEG)
    s_b = jnp.where(ok_b[None], s_b, _NEG)
    m = jnp.maximum(jnp.max(s_a, axis=-1, keepdims=True), jnp.max(s_b, axis=-1, keepdims=True))
    e_a = jnp.exp(s_a - m)
    e_b = jnp.exp(s_b - m)
    l = jnp.sum(e_a, axis=-1, keepdims=True) + jnp.sum(e_b, axis=-1, keepdims=True)
    e_a = e_a.reshape(nh * tq, -1).astype(BF16)
    o_a = _dot_nt(e_a, v_a) if v_a_transposed else jnp.dot(e_a, v_a, preferred_element_type=F32)
    o = o_a + jnp.dot(e_b.reshape(nh * tq, -1).astype(BF16), v_b, preferred_element_type=F32)
    return o / l.reshape(nh * tq, 1)


def _nsa_sample_kernel(pt_ref, q_ref, gt_ref, kc_ref, vc_ref, seln_ref, wst_ref, wnew_ref, oh_ref, pool_ref,
                       o_ref, nwin_ref, selbuf, sem, *, past):
    b = pl.program_id(0)
    slot = b % 2
    tq = q_ref.shape[0]
    nh = A_HEADS
    n_pages = pt_ref.shape[1]
    page = pool_ref.shape[2]
    wbuf = wst_ref.shape[1]
    n_sel = -(-(past + tq) // SEL_LEN)

    def copies(seq, sl):
        return [pltpu.make_async_copy(pool_ref.at[pt_ref[seq, p]], selbuf.at[sl, :, pl.ds(p * page, page)],
                                      sem.at[sl]) for p in range(n_pages)]

    @pl.when(b == 0)
    def _():
        for cp in copies(0, 0):
            cp.start()

    @pl.when(b + 1 < pl.num_programs(0))
    def _():
        for cp in copies(b + 1, 1 - slot):
            cp.start()

    qz = _heads_to_rows(q_ref[...])
    tpos = past + lax.broadcasted_iota(jnp.int32, (tq, LANES), 0)

    kc = kc_ref[...].astype(BF16)
    vc = vc_ref[...].astype(BF16)
    ncp = kc.shape[0]
    s_c = _dot_nt(qz, kc).reshape(nh, tq, ncp)
    tp_c = past + lax.broadcasted_iota(jnp.int32, (tq, ncp), 0)
    nidx = lax.broadcasted_iota(jnp.int32, (tq, ncp), 1)
    p_c = _softmax_rows(s_c, nidx * CMP_STRIDE + (CMP_LEN - 1) <= tp_c)
    o_c = jnp.dot(p_c.reshape(nh * tq, ncp).astype(BF16), vc, preferred_element_type=F32)

    cn = lax.broadcasted_iota(jnp.int32, (ncp, LANES), 0) * CMP_STRIDE
    jn = lax.broadcasted_iota(jnp.int32, (ncp, LANES), 1) * SEL_LEN
    ov = jnp.where((cn < jn + SEL_LEN) & (cn + CMP_LEN > jn), 1.0, 0.0).astype(BF16)
    jb = lax.broadcasted_iota(jnp.int32, (tq, LANES), 1)
    cur = tpos // SEL_LEN
    forced = (jb == 0) | (jb == cur) | (jb == cur - 1)
    bias = []
    for g in range(A_KV_HEADS):
        ps = p_c[g * A_GROUP]
        for r in range(1, A_GROUP):
            ps = ps + p_c[g * A_GROUP + r]
        hi = ps.astype(BF16)
        lo = (ps - hi.astype(F32)).astype(BF16)
        score = (jnp.dot(hi, ov, preferred_element_type=F32) + jnp.dot(lo, ov, preferred_element_type=F32))
        score = jnp.where(forced, FORCE_SCORE, score)
        score = jnp.where(jb * SEL_LEN <= tpos, score, -1.0)
        cnt = jnp.zeros((tq, LANES), jnp.int32)
        for k in range(n_sel):
            sk = score[:, k:k + 1]
            ahead = (sk > score) | ((sk == score) & (jb > k))
            cnt = cnt + jnp.where(ahead, 1, 0)
        bias.append(jnp.where(cnt < min(SEL_TOP, n_sel), 0.0, _NEG_SEL).astype(BF16))
    q_aug = jnp.concatenate(
        [qz, jnp.concatenate([bias[hd // A_GROUP] for hd in range(nh)], axis=0)], axis=1)

    tw = lax.broadcasted_iota(jnp.int32, (tq, wbuf), 0)
    iw = lax.broadcasted_iota(jnp.int32, (tq, wbuf), 1)
    tn = lax.broadcasted_iota(jnp.int32, (tq, tq), 0)
    un = lax.broadcasted_iota(jnp.int32, (tq, tq), 1)
    wst = wst_ref[0]
    wnew = wnew_ref[...]
    s_wa = _dot_nt(qz, wst[:, :KV_WIDTH].astype(BF16)).reshape(nh, tq, wbuf)
    s_wb = _dot_nt(qz, wnew[:, :KV_WIDTH].astype(BF16)).reshape(nh, tq, tq)
    o_w = _attend_two(s_wa, (wbuf + tw - iw < WINDOW), wst[:, KV_WIDTH:].astype(BF16),
                      s_wb, un <= tn, wnew[:, KV_WIDTH:].astype(BF16))
    nwin_ref[0, :wbuf - tq, :] = wst[tq:, :]
    nwin_ref[0, wbuf - tq:, :] = wnew

    for cp in copies(b, slot):
        cp.wait()
    seln = seln_ref[...]
    k_aug_t = jnp.concatenate([selbuf[slot, :KV_WIDTH, :].astype(BF16), oh_ref[...]], axis=0)
    nblk = (past + lax.broadcasted_iota(jnp.int32, (tq, LANES), 0)) // SEL_LEN
    oh_new = jnp.where(nblk == lax.broadcasted_iota(jnp.int32, (tq, LANES), 1), 1.0, 0.0)
    kn_aug = jnp.concatenate([seln[:, :KV_WIDTH], oh_new], axis=1).astype(BF16)
    s_sa = jnp.dot(q_aug, k_aug_t, preferred_element_type=F32).reshape(nh, tq, past)
    s_sb = _dot_nt(q_aug, kn_aug).reshape(nh, tq, tq)
    o_s = _attend_two(s_sa, None, selbuf[slot, KV_WIDTH:, :].astype(BF16), s_sb, un <= tn,
                      seln[:, KV_WIDTH:].astype(BF16), v_a_transposed=True)

    o_ref[...] = _gate_rows_to_heads(gt_ref[...], o_c, o_s, o_w)


def _nsa_sample_call(qa, gt, comp, sel_new, win_state, win_new, sel_pool, page_table, T):
    DB, n_pages = page_table.shape
    page = sel_pool.shape[2]
    past = n_pages * page
    wbuf = win_state.shape[1]
    ncp = comp.shape[0] // DB
    w2 = 2 * KV_WIDTH
    assert past % SEL_LEN == 0 and T < CMP_STRIDE and T % 8 == 0 and wbuf == WINDOW and past >= WINDOW
    assert page % LANES == 0
    onehot = (jnp.arange(LANES)[:, None] == jnp.arange(past)[None, :] // SEL_LEN).astype(BF16)
    tok = lambda w: pl.BlockSpec((T, w), lambda b, pt: (b, 0))
    grid_spec = pltpu.PrefetchScalarGridSpec(
        num_scalar_prefetch=1,
        grid=(DB,),
        in_specs=[tok(A_WIDTH), tok(LANES),
                  pl.BlockSpec((ncp, KV_WIDTH), lambda b, pt: (b, 0)),
                  pl.BlockSpec((ncp, KV_WIDTH), lambda b, pt: (b, 1)),
                  tok(w2),
                  pl.BlockSpec((1, wbuf, w2), lambda b, pt: (b, 0, 0)),
                  tok(w2),
                  pl.BlockSpec((LANES, past), lambda b, pt: (0, 0)),
                  pl.BlockSpec(memory_space=pl.ANY)],
        out_specs=[tok(A_WIDTH), pl.BlockSpec((1, wbuf, w2), lambda b, pt: (b, 0, 0))],
        scratch_shapes=[pltpu.VMEM((2, w2, past), F32), pltpu.SemaphoreType.DMA((2,))],
    )
    return pl.pallas_call(
        functools.partial(_nsa_sample_kernel, past=past),
        grid_spec=grid_spec,
        out_shape=[jax.ShapeDtypeStruct((DB * T, A_WIDTH), F32), jax.ShapeDtypeStruct((DB, wbuf, w2), F32)],
        compiler_params=_cparams(("arbitrary",)),
        name="nsa_sample",
    )(page_table, qa, gt, comp, comp, sel_new, win_state, win_new, onehot, sel_pool)


def _value_with_ones(v, g):
    lane = lax.broadcasted_iota(jnp.int32, v.shape, 1)
    keep = (lane < A_DIM) if g == 0 else (lane >= A_DIM)
    return jnp.where(keep, v, 1.0).astype(BF16)


def _pv_with_sums(p, v):
    half = p.shape[0] // A_KV_HEADS
    return jnp.concatenate(
        [jnp.dot(p[g * half:(g + 1) * half], _value_with_ones(v, g), preferred_element_type=F32)
         for g in range(A_KV_HEADS)], axis=0)


def _normalise_rows(acc):
    half = acc.shape[0] // A_KV_HEADS
    return jnp.concatenate(
        [acc[:half] * (1.0 / acc[:half, A_DIM:A_DIM + 1]), acc[half:] * (1.0 / acc[half:, 0:1])], axis=0)


def _nsa_prompt_kernel(q_ref, gt_ref, kc_ref, vc_ref, sel_ref, win_ref, o_ref, *, seq):
    tq = Q_BLOCK
    nh = A_HEADS
    n_sel = seq // SEL_LEN
    s0 = pl.program_id(1) * tq
    qz = _heads_to_rows(q_ref[0])

    kc = kc_ref[0].astype(BF16)
    vc = vc_ref[0].astype(BF16)
    ncp = kc.shape[0]
    s_c = _dot_nt(qz, kc).reshape(nh, tq, ncp)
    tpos = s0 + lax.broadcasted_iota(jnp.int32, (tq, ncp), 0)
    nidx = lax.broadcasted_iota(jnp.int32, (tq, ncp), 1)
    p_c = _softmax_rows(s_c, nidx * CMP_STRIDE + (CMP_LEN - 1) <= tpos)
    o_c = jnp.dot(p_c.reshape(nh * tq, ncp).astype(BF16), vc, preferred_element_type=F32)
    mass = [sum(p_c[g * A_GROUP + r] for r in range(1, A_GROUP)) + p_c[g * A_GROUP] for g in range(A_KV_HEADS)]

    jn = lax.broadcasted_iota(jnp.int32, (n_sel, ncp), 0) * SEL_LEN
    cn = lax.broadcasted_iota(jnp.int32, (n_sel, ncp), 1) * CMP_STRIDE
    ov_t = jnp.where((cn < jn + SEL_LEN) & (cn + CMP_LEN > jn), 1.0, 0.0).astype(BF16)
    jb = lax.broadcasted_iota(jnp.int32, (n_sel, tq), 0)
    tp = s0 + lax.broadcasted_iota(jnp.int32, (n_sel, tq), 1)
    cur = tp // SEL_LEN
    forced = (jb == 0) | (jb == cur) | (jb == cur - 1)
    bias = []
    for g in range(A_KV_HEADS):
        hi = mass[g].astype(BF16)
        lo = (mass[g] - hi.astype(F32)).astype(BF16)
        score = _dot_nt(ov_t, hi) + _dot_nt(ov_t, lo)
        score = jnp.where(forced, FORCE_SCORE, score)
        score = jnp.where(jb * SEL_LEN <= tp, score, -1.0)
        cnt = jnp.zeros((n_sel, tq), jnp.int32)
        for k in range(n_sel):
            rk = score[k:k + 1, :]
            ahead = (rk > score) | ((rk == score) & (jb > k))
            cnt = cnt + jnp.where(ahead, 1, 0)
        bias_t = jnp.where(cnt < min(SEL_TOP, n_sel), 0.0, _NEG_SEL)
        if n_sel < LANES:
            bias_t = jnp.concatenate([bias_t, jnp.zeros((LANES - n_sel, tq), F32)], axis=0)
        bias.append(bias_t.T.astype(BF16))
    q_aug = jnp.concatenate(
        [qz, jnp.concatenate([bias[hd // A_GROUP] for hd in range(nh)], axis=0)], axis=1)

    wk = WINDOW + tq
    w0 = pl.multiple_of(jnp.clip(s0 - WINDOW, 0, seq - wk), tq)
    kvw = win_ref[0, pl.ds(w0, wk), :]
    s_w = _dot_nt(qz, kvw[:, :KV_WIDTH].astype(BF16)).reshape(nh, tq, wk)
    dist = (s0 + lax.broadcasted_iota(jnp.int32, (tq, wk), 0)) - (w0 + lax.broadcasted_iota(jnp.int32, (tq, wk), 1))
    s_w = jnp.where(((dist >= 0) & (dist < WINDOW))[None], s_w, _NEG)
    e_w = jnp.exp(s_w - jnp.max(s_w, axis=-1, keepdims=True)).reshape(nh * tq, wk).astype(BF16)
    o_w = _normalise_rows(_pv_with_sums(e_w, kvw[:, KV_WIDTH:]))

    kc_n = _SEL_CHUNK

    def chunk(c, carry, causal):
        m, acc = carry
        k0 = pl.multiple_of(c * kc_n, kc_n)
        kv = sel_ref[0, pl.ds(k0, kc_n), :]
        kblk = (k0 + lax.broadcasted_iota(jnp.int32, (kc_n, LANES), 0)) // SEL_LEN
        onehot = jnp.where(kblk == lax.broadcasted_iota(jnp.int32, (kc_n, LANES), 1), 1.0, 0.0)
        k_aug = jnp.concatenate([kv[:, :KV_WIDTH], onehot], axis=1).astype(BF16)
        s = _dot_nt(q_aug, k_aug)
        if causal:
            kpos = k0 + lax.broadcasted_iota(jnp.int32, (tq, kc_n), 1)
            qpos = s0 + lax.broadcasted_iota(jnp.int32, (tq, kc_n), 0)
            s = jnp.where((kpos <= qpos)[None], s.reshape(nh, tq, kc_n), _NEG_SEL).reshape(nh * tq, kc_n)
        m_new = jnp.maximum(m, jnp.max(s, axis=-1, keepdims=True))
        p = jnp.exp(s - m_new).astype(BF16)
        acc = jnp.exp(m - m_new) * acc + _pv_with_sums(p, kv[:, KV_WIDTH:])
        return m_new, acc

    n_ch = (s0 + tq - 1) // kc_n + 1
    init = (jnp.full((nh * tq, 1), _NEG, F32), jnp.zeros((nh * tq, LANES), F32))
    carry = lax.fori_loop(0, n_ch - 1, lambda c, cr: chunk(c, cr, False), init)
    o_s = _normalise_rows(chunk(n_ch - 1, carry, True)[1])

    o_ref[0] = _gate_rows_to_heads(gt_ref[0], o_c, o_s, o_w)


def _nsa_prompt_call(qa, gt, comp, sel_kv, win_kv):
    B, S, _ = qa.shape
    assert S % _SEL_CHUNK == 0 and S >= WINDOW + Q_BLOCK and S // SEL_LEN <= LANES
    ncp = comp.shape[1]
    per_b = lambda r, w: pl.BlockSpec((1, r, w), lambda b, i: (b, 0, 0))
    return pl.pallas_call(
        functools.partial(_nsa_prompt_kernel, seq=S),
        grid=(B, S // Q_BLOCK),
        in_specs=[pl.BlockSpec((1, Q_BLOCK, A_WIDTH), lambda b, i: (b, i, 0)),
                  pl.BlockSpec((1, Q_BLOCK, LANES), lambda b, i: (b, i, 0)),
                  pl.BlockSpec((1, ncp, KV_WIDTH), lambda b, i: (b, 0, 0)),
                  pl.BlockSpec((1, ncp, KV_WIDTH), lambda b, i: (b, 0, 1)),
                  per_b(S, 2 * KV_WIDTH), per_b(S, 2 * KV_WIDTH)],
        out_specs=pl.BlockSpec((1, Q_BLOCK, A_WIDTH), lambda b, i: (b, i, 0)),
        out_shape=jax.ShapeDtypeStruct((B, S, A_WIDTH), F32),
        compiler_params=_cparams(("parallel", "arbitrary")),
        name="nsa_prompt",
    )(qa, gt, comp, comp, sel_kv, win_kv)


def _nsa_prompt_t_kernel(q_ref, gt_ref, kc_ref, vc_ref, selk_ref, selvt_ref, wink_ref, winvt_ref, o_ref, *, seq):
    tq = Q_BLOCK
    nh = A_HEADS
    rows = nh * tq
    n_sel = seq // SEL_LEN
    kt = LANES
    s0 = pl.program_id(1) * tq
    qz = _heads_to_rows(q_ref[0])
    htile = lambda x, hd: x[:, hd * tq:(hd + 1) * tq]

    def qpos(n):
        return s0 + (lax.broadcasted_iota(jnp.int32, (n, rows), 1) & (tq - 1))

    def with_ones(vt):
        return jnp.concatenate([vt, jnp.ones((8, vt.shape[1]), F32)], axis=0).astype(BF16)

    def normalise(acc):
        return acc[:LANES] * (1.0 / acc[LANES:LANES + 1])

    kc = kc_ref[0].astype(BF16)
    ncp = kc.shape[0]
    s_c = _dot_nt(kc, qz)
    ok_c = lax.broadcasted_iota(jnp.int32, (ncp, rows), 0) * CMP_STRIDE + (CMP_LEN - 1) <= qpos(ncp)
    s_c = jnp.where(ok_c, s_c, _NEG)
    e_c = jnp.where(ok_c, jnp.exp(s_c - jnp.max(s_c, axis=0, keepdims=True)), 0.0)
    p_c = e_c * (1.0 / jnp.maximum(jnp.sum(e_c, axis=0, keepdims=True), jnp.finfo(jnp.float32).tiny))
    o_c = jnp.dot(vc_ref[0].T.astype(BF16), p_c.astype(BF16), preferred_element_type=F32)

    jn = lax.broadcasted_iota(jnp.int32, (n_sel, ncp), 0) * SEL_LEN
    cn = lax.broadcasted_iota(jnp.int32, (n_sel, ncp), 1) * CMP_STRIDE
    ov_t = jnp.where((cn < jn + SEL_LEN) & (cn + CMP_LEN > jn), 1.0, 0.0).astype(BF16)
    jb = lax.broadcasted_iota(jnp.int32, (n_sel, tq), 0)
    tp = s0 + lax.broadcasted_iota(jnp.int32, (n_sel, tq), 1)
    cur = tp // SEL_LEN
    forced = (jb == 0) | (jb == cur) | (jb == cur - 1)
    bias = []
    for g in range(A_KV_HEADS):
        mass = htile(p_c, g * A_GROUP)
        for r in range(1, A_GROUP):
            mass = mass + htile(p_c, g * A_GROUP + r)
        hi = mass.astype(BF16)
        lo = (mass - hi.astype(F32)).astype(BF16)
        score = (jnp.dot(ov_t, hi, preferred_element_type=F32) + jnp.dot(ov_t, lo, preferred_element_type=F32))
        score = jnp.where(forced, FORCE_SCORE, score)
        score = jnp.where(jb * SEL_LEN <= tp, score, -1.0)
        cnt = jnp.zeros((n_sel, tq), jnp.int32)
        for k in range(n_sel):
            rk = score[k:k + 1, :]
            ahead = (rk > score) | ((rk == score) & (jb > k))
            cnt = cnt + jnp.where(ahead, 1, 0)
        bias_t = jnp.where(cnt < min(SEL_TOP, n_sel), 0.0, _NEG_SEL)
        if n_sel < LANES:
            bias_t = jnp.concatenate([bias_t, jnp.zeros((LANES - n_sel, tq), F32)], axis=0)
        bias.append(bias_t.T.astype(BF16))
    q_aug = jnp.concatenate(
        [qz, jnp.concatenate([bias[hd // A_GROUP] for hd in range(nh)], axis=0)], axis=1)

    wk = WINDOW + tq
    w0 = pl.multiple_of(jnp.clip(s0 - WINDOW, 0, seq - wk), tq)
    s_w = _dot_nt(wink_ref[0, pl.ds(w0, wk), :].astype(BF16), qz)
    dist = qpos(wk) - (w0 + lax.broadcasted_iota(jnp.int32, (wk, rows), 0))
    s_w = jnp.where((dist >= 0) & (dist < WINDOW), s_w, _NEG)
    e_w = jnp.exp(s_w - jnp.max(s_w, axis=0, keepdims=True)).astype(BF16)
    vw_t = jnp.concatenate([winvt_ref[0, w0 // kt + i] for i in range(wk // kt)], axis=1)
    o_w = normalise(jnp.dot(with_ones(vw_t), e_w, preferred_element_type=F32))

    kc_n = _SEL_CHUNK

    def chunk(c, carry, causal):
        m, acc = carry
        k0 = pl.multiple_of(c * kc_n, kc_n)
        kblk = (k0 + lax.broadcasted_iota(jnp.int32, (kc_n, LANES), 0)) // SEL_LEN
        onehot = jnp.where(kblk == lax.broadcasted_iota(jnp.int32, (kc_n, LANES), 1), 1.0, 0.0)
        k_aug = jnp.concatenate([selk_ref[0, pl.ds(k0, kc_n), :], onehot], axis=1).astype(BF16)
        s = _dot_nt(k_aug, q_aug)
        if causal:
            s = jnp.where(k0 + lax.broadcasted_iota(jnp.int32, (kc_n, rows), 0) <= qpos(kc_n), s, _NEG_SEL)
        m_new = jnp.maximum(m, jnp.max(s, axis=0, keepdims=True))
        p = jnp.exp(s - m_new).astype(BF16)
        v_t = jnp.concatenate([selvt_ref[0, c * (kc_n // kt) + i] for i in range(kc_n // kt)], axis=1)
        acc = jnp.exp(m - m_new) * acc + jnp.dot(with_ones(v_t), p, preferred_element_type=F32)
        return m_new, acc

    n_ch = (s0 + tq - 1) // kc_n + 1
    init = (jnp.full((1, rows), _NEG, F32), jnp.zeros((LANES + 8, rows), F32))
    carry = lax.fori_loop(0, n_ch - 1, lambda c, cr: chunk(c, cr, False), init)
    o_s = normalise(chunk(n_ch - 1, carry, True)[1])

    gs_t = jax.nn.sigmoid(gt_ref[0]).T
    mixed = []
    for hd in range(nh):
        c = 2 * M_HEADS + 3 * hd
        g = hd // A_GROUP
        mix = (gs_t[c:c + 1, :] * htile(o_c, hd) + gs_t[c + 1:c + 2, :] * htile(o_s, hd)
               + gs_t[c + 2:c + 3, :] * htile(o_w, hd))
        mixed.append(mix[g * A_DIM:(g + 1) * A_DIM, :])
    o_ref[0] = jnp.concatenate(
        [jnp.concatenate([mixed[2 * pr], mixed[2 * pr + 1]], axis=0).T for pr in range(nh // 2)], axis=1)


def _nsa_prompt_t_call(qa, gt, comp, sel_k, sel_vt, win_k, win_vt):
    B, S, _ = qa.shape
    assert S % _SEL_CHUNK == 0 and S >= WINDOW + Q_BLOCK and S // SEL_LEN <= LANES
    ncp = comp.shape[1]
    keys = pl.BlockSpec((1, S, KV_WIDTH), lambda b, i: (b, 0, 0))
    vals = pl.BlockSpec((1, S // LANES, KV_WIDTH, LANES), lambda b, i: (b, 0, 0, 0))
    return pl.pallas_call(
        functools.partial(_nsa_prompt_t_kernel, seq=S),
        grid=(B, S // Q_BLOCK),
        in_specs=[pl.BlockSpec((1, Q_BLOCK, A_WIDTH), lambda b, i: (b, i, 0)),
                  pl.BlockSpec((1, Q_BLOCK, LANES), lambda b, i: (b, i, 0)),
                  pl.BlockSpec((1, ncp, KV_WIDTH), lambda b, i: (b, 0, 0)),
                  pl.BlockSpec((1, ncp, KV_WIDTH), lambda b, i: (b, 0, 1)),
                  keys, vals, keys, vals],
        out_specs=pl.BlockSpec((1, Q_BLOCK, A_WIDTH), lambda b, i: (b, i, 0)),
        out_shape=jax.ShapeDtypeStruct((B, S, A_WIDTH), F32),
        compiler_params=_cparams(("parallel", "arbitrary")),
        name="nsa_prompt",
    )(qa, gt, comp, comp, sel_k, sel_vt, win_k, win_vt)


def _moe(h2, logits, moe_w, n_experts):
    w_gu, b_gu, w_down, b_down = moe_w
    n, d = h2.shape
    gate_w, dest, src, blk_e, n_used = _route(logits, n_experts)
    yb = _ffn(h2[src], blk_e, n_used, w_gu, b_gu, w_down, b_down)
    return yb[dest.T], gate_w


def kernel(x_prompt, x_sample, cache_cmp, cache_sel, state_win, state_C, state_n, state_m, page_table,
           c_prompt, c_sample, w_ada, b_ada, w_in, b_in, m_norm_g, cmp_pe_k, cmp_w1_k, cmp_w2_k,
           cmp_pe_v, cmp_w1_v, cmp_w2_v, w_out, ln1_g, ln1_b, w_router, b_router, w_gu, b_gu,
           w_down, b_down, ln2_g, ln2_b):
    B, S, d = x_prompt.shape
    DB, T, _ = x_sample.shape
    depth = w_ada.shape[0]
    n_experts = w_router.shape[-1]
    alpha = (2 * depth) ** 0.25
    n_pool, page = cache_cmp.shape[1:3]
    past_len = page_table.shape[1] * page
    wbuf = state_win.shape[2]
    pos_p = jnp.arange(S)
    pos_s = past_len + jnp.arange(T)
    tt_p = min(S, 256)
    bb_s = min(DB, max(1, 256 // T))
    n_cmp_s = (past_len + T - CMP_LEN) // CMP_STRIDE + 1
    assert n_cmp_s < past_len // CMP_STRIDE
    r3 = lambda a: a.reshape(B, S, a.shape[-1])
    kv5 = lambda a, n, t: a.reshape(n, t, 2, A_KV_HEADS, A_DIM)
    y_prompt, y_sample = x_prompt, x_sample
    outs = [[] for _ in range(12)]
    for l in range(depth):
        cmp_l = (cmp_pe_k[l], cmp_w1_k[l], cmp_w2_k[l], cmp_pe_v[l], cmp_w1_v[l], cmp_w2_v[l])
        moe_w = (w_gu[l], b_gu[l], w_down[l], b_down[l])
        post1_w = (w_out[l], ln1_g[l], ln1_b[l], w_router[l], b_router[l])
        w_r, b_r = _prep_w_in(w_in[l], b_in[l])
        cmp_w = _compress_weights(cmp_l)
        c_all = jnp.concatenate([c_prompt, c_sample], axis=0)
        c_all = jnp.pad(c_all, ((0, -c_all.shape[0] % 8), (0, 0)))
        mod_all = _ada(c_all, w_ada[l], b_ada[l]).reshape(-1, 6, d)
        mod_p, mod_s = mod_all[:B], mod_all[B:B + DB]

        (mq, gt, qa, cmp_p, sel_p, win_p, cmp_t, sel_t, win_t, sel_vt, win_vt) = _pre(
            y_prompt, mod_p, pos_p, w_r, b_r, 1, tt_p, feature_major=True)
        zc = jnp.zeros((B, M_HEADS, M_DIM, M_DIM), F32)
        mo, C_p, n_p, m_p = _mlstm_call(r3(mq), r3(gt), zc, zc[..., 0], zc[..., 0, 0], m_norm_g[l],
                                        math.gcd(B, 4))
        comp = _compress_prompt(r3(cmp_p), cmp_w)
        ma = _nsa_prompt_t_call(r3(qa), r3(gt), comp, r3(sel_p), sel_vt, r3(win_p), win_vt)
        x1_p, h2_p, lg_p = _post1(mo.reshape(B * S, M_WIDTH), ma.reshape(B * S, A_WIDTH), y_prompt, mod_p,
                                  *post1_w, 1, tt_p, alpha)

        mq, gt, qa, cmp_s, sel_s, win_s = _pre(y_sample, mod_s, pos_s, w_r, b_r, bb_s, T)
        mo, C_s, n_s, m_s = _mlstm_call(mq.reshape(DB, T, -1), gt.reshape(DB, T, -1), state_C[l],
                                        state_n[l], state_m[l], m_norm_g[l], math.gcd(DB, 4))
        feature_major = lambda pool: pool.transpose(0, 2, 3, 4, 1).reshape(n_pool, 2 * KV_WIDTH, page)
        comp = _compress_sample(feature_major(cache_cmp[l]), page_table, cmp_w, n_cmp_s, math.gcd(DB, 4))
        ma, new_win = _nsa_sample_call(qa, gt, comp, sel_s, state_win[l].reshape(DB, wbuf, 2 * KV_WIDTH),
                                       win_s, feature_major(cache_sel[l]), page_table, T)
        x1_s, h2_s, lg_s = _post1(mo.reshape(DB * T, M_WIDTH), ma, y_sample, mod_s, *post1_w, bb_s, T, alpha)

        yg, gate_w = _moe(jnp.concatenate([h2_p, h2_s], axis=0), jnp.concatenate([lg_p, lg_s], axis=0),
                          moe_w, n_experts)
        y_prompt = _post2(yg, gate_w, 0, x1_p, mod_p, ln2_g[l], ln2_b[l], 1, tt_p, alpha)
        y_sample = _post2(yg, gate_w, B * S, x1_s, mod_s, ln2_g[l], ln2_b[l], bb_s, T, alpha)

        kv5_t = lambda a: a.reshape(B, 2, A_KV_HEADS, A_DIM, -1).transpose(0, 4, 1, 2, 3)
        new = (kv5_t(cmp_t), kv5_t(sel_t), kv5_t(win_t[:, :, -min(WINDOW, S):]), C_p, n_p, m_p,
               kv5(cmp_s, DB, T), kv5(sel_s, DB, T), new_win.reshape(state_win.shape[1:]), C_s, n_s, m_s)
        for lst, v in zip(outs, new):
            lst.append(v)
    return (y_prompt, y_sample, *[jnp.stack(v) for v in outs])
```

```python
import functools
import math

import numpy as np
import jax
import jax.numpy as jnp
from jax import lax
from jax.experimental import pallas as pl
from jax.experimental.pallas import tpu as pltpu

F32 = jnp.float32
BF16 = jnp.bfloat16

M_HEADS = 4
M_DIM = 128
M_WIDTH = M_HEADS * M_DIM
M_CHUNK = 64
A_HEADS = 8
A_KV_HEADS = 2
A_GROUP = A_HEADS // A_KV_HEADS
A_DIM = 64
A_WIDTH = A_HEADS * A_DIM
KV_WIDTH = A_KV_HEADS * A_DIM
CMP_LEN = 32
CMP_STRIDE = 16
SEL_LEN = 64
SEL_TOP = 16
WINDOW = 512
Q_BLOCK = 128
FORCE_SCORE = 1.0e4
ROPE_THETA = 500000.0
ROPE_DIM = A_DIM // 4
TOP_K = 4
SWIGLU_LIMIT = 7.0
SWIGLU_ALPHA = 1.702
LN_EPS = 1e-5
LANES = 128
MOE_ROWS = 512
VMEM_LIMIT = 56 * 1024 * 1024

_O_MQ = 0
_O_IF = 4 * M_WIDTH
_O_QA = _O_IF + 2 * M_HEADS
_O_KV = _O_QA + A_WIDTH
_O_GA = _O_KV + 6 * KV_WIDTH
_N_IN = _O_GA + 3 * A_HEADS
_R_QA = 4 * M_WIDTH
_R_KV = _R_QA + A_WIDTH
_R_GT = _R_KV + 6 * KV_WIDTH
_R_END = _R_GT + LANES


def _cparams(sem):
    return pltpu.CompilerParams(dimension_semantics=sem, vmem_limit_bytes=VMEM_LIMIT)


def _ln_core(x):
    mu = jnp.mean(x, axis=-1, keepdims=True)
    xc = x - mu
    var = jnp.mean(xc * xc, axis=-1, keepdims=True)
    return xc * lax.rsqrt(var + LN_EPS)


def _ada_kernel(c_ref, w_ref, b_ref, o_ref):
    o_ref[...] = jnp.dot(c_ref[...].astype(BF16), w_ref[...].astype(BF16),
                         preferred_element_type=F32) + b_ref[...]


def _ada(c, w_ada, b_ada):
    n, d = c.shape
    cols = w_ada.shape[1]
    return pl.pallas_call(
        _ada_kernel,
        grid=(cols // d,),
        in_specs=[pl.BlockSpec((n, d), lambda j: (0, 0)),
                  pl.BlockSpec((d, d), lambda j: (0, j)),
                  pl.BlockSpec((1, d), lambda j: (0, j))],
        out_specs=pl.BlockSpec((n, d), lambda j: (0, j)),
        out_shape=jax.ShapeDtypeStruct((n, cols), F32),
        compiler_params=_cparams(("arbitrary",)),
        name="ada",
    )(c, w_ada, b_ada.reshape(1, cols))


def _rope_apply(v, cos, sa, sb):
    reps = v.shape[1] // LANES
    tile = lambda t: t if reps == 1 else jnp.concatenate([t] * reps, axis=1)
    w = v.shape[1]
    return (v * tile(cos) + pltpu.roll(v, w - ROPE_DIM // 2, 1) * tile(sa)
            + pltpu.roll(v, ROPE_DIM // 2, 1) * tile(sb))


def _pre_kernel(x_ref, mod_ref, cos_ref, sa_ref, sb_ref, w_ref, b_ref,
                mq_ref, gt_ref, qa_ref, cmp_ref, sel_ref, win_ref, *t_refs):
    bb, tt, d = x_ref.shape
    mod = mod_ref[...]
    h = _ln_core(x_ref[...]) * (1.0 + mod[:, 1:2, :]) + mod[:, 0:1, :]
    h = h.reshape(bb * tt, d).astype(BF16)
    z = jnp.dot(h, w_ref[...], preferred_element_type=F32) + b_ref[...]
    cos, sa, sb = cos_ref[...], sa_ref[...], sb_ref[...]
    mq_ref[...] = z[:, :_R_QA]
    gt_ref[...] = z[:, _R_GT:_R_END]
    qa_ref[...] = _rope_apply(z[:, _R_QA:_R_KV], cos, sa, sb)
    for n, ref in enumerate((cmp_ref, sel_ref, win_ref)):
        o = _R_KV + 2 * KV_WIDTH * n
        kv = jnp.concatenate([_rope_apply(z[:, o:o + KV_WIDTH], cos, sa, sb),
                              z[:, o + KV_WIDTH:o + 2 * KV_WIDTH]], axis=1)
        ref[...] = kv
        if t_refs:
            kv_t = kv.T
            t_refs[n][0] = kv_t
            if n > 0:
                for c in range(kv_t.shape[1] // LANES):
                    t_refs[2 + n][0, c] = kv_t[KV_WIDTH:, c * LANES:(c + 1) * LANES]


def _rope_tables(pos):
    half = ROPE_DIM // 2
    inv = ROPE_THETA ** (-jnp.arange(0, ROPE_DIM, 2, dtype=F32) / ROPE_DIM)
    ang = pos.astype(F32)[:, None] * inv[None, :]
    cos, sin = jnp.cos(ang), jnp.sin(ang)
    n = pos.shape[0]
    one = jnp.ones((n, A_DIM - ROPE_DIM), F32)
    zero = jnp.zeros((n, A_DIM - ROPE_DIM), F32)
    zh = jnp.zeros((n, half), F32)
    cos_t = jnp.concatenate([cos, cos, one], axis=1)
    sa_t = jnp.concatenate([-sin, zh, zero], axis=1)
    sb_t = jnp.concatenate([zh, sin, zero], axis=1)
    two = lambda t: jnp.concatenate([t, t], axis=1)
    return two(cos_t), two(sa_t), two(sb_t)


def _prep_w_in(w_in, b_in):
    pad = LANES - 2 * M_HEADS - 3 * A_HEADS
    cat = lambda a: jnp.concatenate(
        [a[..., _O_MQ:_O_IF], a[..., _O_QA:_O_GA], a[..., _O_IF:_O_QA], a[..., _O_GA:_N_IN],
         jnp.zeros(a.shape[:-1] + (pad,), a.dtype)], axis=-1)
    return cat(w_in).astype(BF16), cat(b_in[None, :])


def _pre(x, mod, pos, w_r, b_r, bb, tt, feature_major=False):
    B, T, d = x.shape
    nt = T // tt
    rows = bb * tt
    cos, sa, sb = _rope_tables(pos)
    if bb > 1:
        cos, sa, sb = (jnp.tile(t, (bb, 1)) for t in (cos, sa, sb))
    n_tok = B * T
    tab = pl.BlockSpec((rows, LANES), lambda i, j: (j, 0))
    row = lambda w: pl.BlockSpec((rows, w), lambda i, j: (i * nt + j, 0))
    widths = (_R_QA, LANES, A_WIDTH, 2 * KV_WIDTH, 2 * KV_WIDTH, 2 * KV_WIDTH)
    out_specs = [row(w) for w in widths]
    out_shape = [jax.ShapeDtypeStruct((n_tok, w), F32) for w in widths]
    if feature_major:
        assert bb == 1 and tt % LANES == 0
        out_specs += [pl.BlockSpec((1, 2 * KV_WIDTH, tt), lambda i, j: (i, 0, j))] * 3
        out_shape += [jax.ShapeDtypeStruct((B, 2 * KV_WIDTH, T), F32)] * 3
        out_specs += [pl.BlockSpec((1, tt // LANES, KV_WIDTH, LANES), lambda i, j: (i, j, 0, 0))] * 2
        out_shape += [jax.ShapeDtypeStruct((B, T // LANES, KV_WIDTH, LANES), F32)] * 2
    return pl.pallas_call(
        _pre_kernel,
        grid=(B // bb, nt),
        in_specs=[pl.BlockSpec((bb, tt, d), lambda i, j: (i, j, 0)),
                  pl.BlockSpec((bb, 6, d), lambda i, j: (i, 0, 0)),
                  tab, tab, tab,
                  pl.BlockSpec((d, _R_END), lambda i, j: (0, 0)),
                  pl.BlockSpec((1, _R_END), lambda i, j: (0, 0))],
        out_specs=out_specs,
        out_shape=out_shape,
        compiler_params=_cparams(("parallel", "arbitrary")),
        name="pre",
    )(x, mod, cos, sa, sb, w_r, b_r)


def _post1_kernel(mo_ref, ao_ref, x_ref, mod_ref, wo_ref, g_ref, b_ref, wr_ref, br_ref,
                  x1_ref, h2_ref, lg_ref, *, alpha):
    bb, tt, d = x_ref.shape
    mod = mod_ref[...]
    mixin = jnp.concatenate([mo_ref[...], ao_ref[...]], axis=1).astype(BF16)
    mix = jnp.dot(mixin, wo_ref[...], preferred_element_type=F32).reshape(bb, tt, d)
    x1 = _ln_core(alpha * x_ref[...] + mod[:, 2:3, :] * mix) * g_ref[...] + b_ref[...]
    h2 = _ln_core(x1) * (1.0 + mod[:, 4:5, :]) + mod[:, 3:4, :]
    x1_ref[...] = x1
    h2f = h2.reshape(bb * tt, d)
    h2_ref[...] = h2f
    lg_ref[...] = jnp.dot(h2f, wr_ref[...], preferred_element_type=F32,
                          precision=lax.Precision.HIGHEST) + br_ref[...]


def _post1(mo, ao, x, mod, w_out, ln_g, ln_b, w_router, b_router, bb, tt, alpha):
    B, T, d = x.shape
    nt = T // tt
    rows = bb * tt
    ne = w_router.shape[1]
    wr = jnp.pad(w_router, ((0, 0), (0, LANES - ne)))
    br = jnp.pad(b_router, (0, LANES - ne), constant_values=-jnp.inf).reshape(1, LANES)
    full = lambda *s: pl.BlockSpec(s, lambda i, j: (0,) * len(s))
    row = lambda w: pl.BlockSpec((rows, w), lambda i, j: (i * nt + j, 0))
    return pl.pallas_call(
        functools.partial(_post1_kernel, alpha=alpha),
        grid=(B // bb, nt),
        in_specs=[row(M_WIDTH), row(A_WIDTH),
                  pl.BlockSpec((bb, tt, d), lambda i, j: (i, j, 0)),
                  pl.BlockSpec((bb, 6, d), lambda i, j: (i, 0, 0)),
                  full(M_WIDTH + A_WIDTH, d), full(1, d), full(1, d), full(d, LANES), full(1, LANES)],
        out_specs=[pl.BlockSpec((bb, tt, d), lambda i, j: (i, j, 0)), row(d), row(LANES)],
        out_shape=[jax.ShapeDtypeStruct((B, T, d), F32),
                   jax.ShapeDtypeStruct((B * T, d), F32),
                   jax.ShapeDtypeStruct((B * T, LANES), F32)],
        compiler_params=_cparams(("parallel", "arbitrary")),
        name="post1",
    )(mo, ao, x, mod, w_out.astype(BF16), ln_g.reshape(1, d), ln_b.reshape(1, d), wr, br)


def _ffn_kernel(be_ref, nu_ref, x_ref, wgu_ref, bgu_ref, wd_ref, bd_ref, y_ref, wgu_bf, wd_bf):
    i = pl.program_id(0)
    dff = wd_ref.shape[1]

    @pl.when((i == 0) | (be_ref[i] != be_ref[jnp.maximum(i - 1, 0)]))
    def _():
        wgu_bf[...] = wgu_ref[0].astype(BF16)
        wd_bf[...] = wd_ref[0].astype(BF16)

    @pl.when(i < nu_ref[0])
    def _():
        half = x_ref.shape[0] // 2
        halves = [slice(h * half, (h + 1) * half) for h in range(2)]
        gus = [jnp.dot(x_ref[r, :].astype(BF16), wgu_bf[...], preferred_element_type=F32) + bgu_ref[0]
               for r in halves]
        acts = []
        for gu in gus:
            g = jnp.minimum(gu[:, :dff], SWIGLU_LIMIT)
            u = jnp.clip(gu[:, dff:], -SWIGLU_LIMIT, SWIGLU_LIMIT)
            acts.append(((u + 1.0) * (g * jax.nn.sigmoid(SWIGLU_ALPHA * g))).astype(BF16))
        for r, act in zip(halves, acts):
            y_ref[r, :] = jnp.dot(act, wd_bf[...], preferred_element_type=F32) + bd_ref[0]

    @pl.when(i >= nu_ref[0])
    def _():
        y_ref[...] = jnp.zeros_like(y_ref)


def _ffn(xb, blk_e, n_used, w_gu, b_gu, w_down, b_down):
    rows, d = xb.shape
    ne, _, f2 = w_gu.shape
    dff = w_down.shape[1]
    nb = rows // MOE_ROWS
    grid_spec = pltpu.PrefetchScalarGridSpec(
        num_scalar_prefetch=2,
        grid=(nb,),
        in_specs=[pl.BlockSpec((MOE_ROWS, d), lambda i, be, nu: (i, 0)),
                  pl.BlockSpec((1, d, f2), lambda i, be, nu: (be[i], 0, 0)),
                  pl.BlockSpec((1, 1, f2), lambda i, be, nu: (be[i], 0, 0)),
                  pl.BlockSpec((1, dff, d), lambda i, be, nu: (be[i], 0, 0)),
                  pl.BlockSpec((1, 1, d), lambda i, be, nu: (be[i], 0, 0))],
        out_specs=pl.BlockSpec((MOE_ROWS, d), lambda i, be, nu: (i, 0)),
        scratch_shapes=[pltpu.VMEM((d, f2), BF16), pltpu.VMEM((dff, d), BF16)],
    )
    return pl.pallas_call(
        _ffn_kernel,
        grid_spec=grid_spec,
        out_shape=jax.ShapeDtypeStruct((rows, d), F32),
        compiler_params=_cparams(("arbitrary",)),
        name="ffn",
    )(blk_e, n_used, xb, w_gu, b_gu.reshape(ne, 1, f2), w_down, b_down.reshape(ne, 1, d))


def _post2_kernel(yg_ref, gw_ref, x1_ref, mod_ref, g_ref, b_ref, y_ref, *, alpha):
    bb, tt, d = x1_ref.shape
    gw = gw_ref[...]
    f = yg_ref[0] * gw[:, 0:1]
    for k in range(1, TOP_K):
        f = f + yg_ref[k] * gw[:, k:k + 1]
    y = alpha * x1_ref[...] + mod_ref[...][:, 5:6, :] * f.reshape(bb, tt, d)
    y_ref[...] = _ln_core(y) * g_ref[...] + b_ref[...]


def _post2(yg, gw, row0, x1, mod, ln_g, ln_b, bb, tt, alpha):
    B, T, d = x1.shape
    rows = bb * tt
    nt = T // tt
    blk0 = row0 // rows
    assert row0 % rows == 0
    full = lambda *s: pl.BlockSpec(s, lambda i, j: (0,) * len(s))
    return pl.pallas_call(
        functools.partial(_post2_kernel, alpha=alpha),
        grid=(B // bb, nt),
        in_specs=[pl.BlockSpec((TOP_K, rows, d), lambda i, j: (0, blk0 + i * nt + j, 0)),
                  pl.BlockSpec((rows, LANES), lambda i, j: (blk0 + i * nt + j, 0)),
                  pl.BlockSpec((bb, tt, d), lambda i, j: (i, j, 0)),
                  pl.BlockSpec((bb, 6, d), lambda i, j: (i, 0, 0)),
                  full(1, d), full(1, d)],
        out_specs=pl.BlockSpec((bb, tt, d), lambda i, j: (i, j, 0)),
        out_shape=jax.ShapeDtypeStruct((B, T, d), F32),
        compiler_params=_cparams(("parallel", "arbitrary")),
        name="post2",
    )(yg, gw, x1, mod, ln_g.reshape(1, d), ln_b.reshape(1, d))


def _lane_prefix_sum(x):
    lane = lax.broadcasted_iota(jnp.int32, x.shape, 1)
    s = 1
    while s < LANES:
        x = x + jnp.where(lane >= s, pltpu.roll(x, s, 1), 0.0)
        s *= 2
    return x


def _route_kernel(lg_ref, dest_ref, gw_ref, cnt_ref, counts, running):
    phase = pl.program_id(0)
    blk = pl.program_id(1)
    r = lg_ref.shape[0]
    lane = lax.broadcasted_iota(jnp.int32, (r, LANES), 1)

    @pl.when((phase == 0) & (blk == 0))
    def _():
        counts[...] = jnp.zeros_like(counts)

    @pl.when((phase == 1) & (blk == 0))
    def _():
        running[...] = jnp.zeros_like(running)

    vals = lg_ref[...]
    onehots, tops = [], []
    for k in range(TOP_K):
        m = jnp.max(vals, axis=1, keepdims=True)
        idx = jnp.min(jnp.where(vals == m, lane, LANES), axis=1, keepdims=True)
        hit = lane == idx
        onehots.append(hit)
        tops.append(m)
        vals = jnp.where(hit, -jnp.inf, vals)
    ohf = [jnp.where(h, 1.0, 0.0) for h in onehots]
    block_cnt = [jnp.sum(o, axis=0, keepdims=True) for o in ohf]

    @pl.when(phase == 0)
    def _():
        counts[...] += block_cnt[0] + block_cnt[1] + block_cnt[2] + block_cnt[3]

    @pl.when(phase == 1)
    def _():
        cnt = counts[...]
        padded = jnp.ceil(cnt * (1.0 / MOE_ROWS)) * MOE_ROWS
        pad_start = _lane_prefix_sum(padded) - padded
        es = [jnp.exp(t - tops[0]) for t in tops]
        den = es[0] + es[1] + es[2] + es[3]
        ti = lax.broadcasted_iota(jnp.int32, (r, r), 0)
        tj = lax.broadcasted_iota(jnp.int32, (r, r), 1)
        before = jnp.where(tj < ti, 1.0, 0.0).astype(BF16)
        base = pad_start + running[...]
        dest = jnp.zeros((r, LANES), F32)
        gw = jnp.zeros((r, LANES), F32)
        for k in range(TOP_K):
            rank = jnp.dot(before, ohf[k].astype(BF16), preferred_element_type=F32)
            d_k = jnp.sum(ohf[k] * (base + rank), axis=1, keepdims=True)
            dest = jnp.where(lane == k, d_k, dest)
            gw = jnp.where(lane == k, es[k] / den, gw)
            base = base + block_cnt[k]
        running[...] = base - pad_start
        dest_ref[...] = dest.astype(jnp.int32)
        gw_ref[...] = gw
        cnt_ref[...] = jnp.broadcast_to(cnt, cnt_ref.shape)


def _route(logits, n_experts, rows=256):
    n = logits.shape[0]
    assert n % rows == 0
    nb = n // rows
    tok = pl.BlockSpec((rows, LANES), lambda p, i: (i, 0))
    out = pl.BlockSpec((rows, LANES), lambda p, i: (i * p, 0))
    dest, gw, cnt = pl.pallas_call(
        _route_kernel,
        grid=(2, nb),
        in_specs=[tok],
        out_specs=[out, out, pl.BlockSpec((8, LANES), lambda p, i: (0, 0))],
        out_shape=[jax.ShapeDtypeStruct((n, LANES), jnp.int32), jax.ShapeDtypeStruct((n, LANES), F32),
                   jax.ShapeDtypeStruct((8, LANES), F32)],
        scratch_shapes=[pltpu.VMEM((1, LANES), F32), pltpu.VMEM((1, LANES), F32)],
        compiler_params=_cparams(("arbitrary", "arbitrary")),
        name="route",
    )(logits)
    counts = cnt[0, :n_experts].astype(jnp.int32)
    pad_end = jnp.cumsum((counts + MOE_ROWS - 1) // MOE_ROWS * MOE_ROWS)
    n_slots = n * TOP_K
    n_blocks = -(-n_slots // MOE_ROWS) + n_experts
    n_used = pad_end[-1] // MOE_ROWS
    blk = jnp.minimum(jnp.arange(n_blocks, dtype=jnp.int32), n_used - 1) * MOE_ROWS
    blk_e = jnp.minimum(jnp.sum(pad_end[None, :] <= blk[:, None], axis=1), n_experts - 1).astype(jnp.int32)
    dest4 = dest[:, :TOP_K]
    src = jnp.zeros((n_blocks * MOE_ROWS,), jnp.int32).at[dest4.reshape(-1)].set(
        jnp.arange(n_slots, dtype=jnp.int32) // TOP_K, unique_indices=True)
    return gw, dest4, src, blk_e, n_used.reshape(1).astype(jnp.int32)


def _log_sigmoid(x):
    return jnp.minimum(x, 0.0) - jnp.log(1.0 + jnp.exp(-jnp.abs(x)))


def _mlstm_kernel(mq_ref, gt_ref, gtt_ref, c0_ref, n0_ref, m0_ref, g_ref, mo_ref, c_ref, n_ref, m_ref):
    bb, L, _ = mq_ref.shape
    d = M_DIM

    @pl.when(pl.program_id(1) == 0)
    def _():
        c_ref[...] = c0_ref[...]
        n_ref[...] = n0_ref[...]
        m_ref[...] = m0_ref[...]

    tt = lax.broadcasted_iota(jnp.int32, (L, L), 0)
    ss = lax.broadcasted_iota(jnp.int32, (L, L), 1)
    causal = ss <= tt
    pairs = [(s, h) for s in range(bb) for h in range(M_HEADS)]
    col = lambda s, h, off: mq_ref[s, :, off + h * d:off + (h + 1) * d]
    old = {p: (c_ref[p[0], p[1]], n_ref[p[0], p[1]:p[1] + 1, :], m_ref[p[0], :, p[1]:p[1] + 1]) for p in pairs}

    qb = {p: col(*p, 0).astype(BF16) for p in pairs}
    kb = {p: (col(*p, M_WIDTH) * (d ** -0.5)).astype(BF16) for p in pairs}
    qk = {p: _dot_nt(qb[p], kb[p]) for p in pairs}
    qc = {p: _dot_nt(qb[p], old[p][0].astype(BF16)) for p in pairs}

    i_c = {(s, h): gt_ref[s][:, h:h + 1] for s, h in pairs}
    i_r = {(s, h): gtt_ref[s, 0][h:h + 1, :] for s, h in pairs}
    lf_c = {(s, h): _log_sigmoid(gt_ref[s][:, M_HEADS + h:M_HEADS + h + 1]) for s, h in pairs}
    lf_r = {(s, h): _log_sigmoid(gtt_ref[s, 0][M_HEADS + h:M_HEADS + h + 1, :]) for s, h in pairs}
    b_c = {p: jnp.sum(jnp.where(causal, lf_r[p], 0.0), axis=1, keepdims=True) for p in pairs}
    b_r = {p: jnp.sum(jnp.where(causal, 0.0, lf_c[p]), axis=0, keepdims=True) + lf_r[p] for p in pairs}

    dmat = {p: jnp.where(causal, b_c[p] - b_r[p] + i_r[p], _NEG) for p in pairs}
    dmax = {p: jnp.max(dmat[p], axis=1, keepdims=True) for p in pairs}
    m_t = {p: jnp.maximum(b_c[p] + old[p][2], dmax[p]) for p in pairs}
    inter = {p: jnp.exp(b_c[p] + old[p][2] - m_t[p]) for p in pairs}
    a = {p: jnp.exp(dmat[p] - m_t[p]) * qk[p] for p in pairs}
    av = {p: jnp.dot(a[p].astype(BF16), col(*p, 2 * M_WIDTH).astype(BF16), preferred_element_type=F32)
          for p in pairs}

    qn = {p: jnp.sum(col(*p, 0) * old[p][1], axis=1, keepdims=True) for p in pairs}
    asum = {p: jnp.sum(a[p], axis=1, keepdims=True) for p in pairs}
    hv = {p: (inter[p] * qc[p] + av[p]) / jnp.maximum(jnp.abs(inter[p] * qn[p] + asum[p]), jnp.exp(-m_t[p]))
          for p in pairs}
    mu = {p: jnp.mean(hv[p], axis=1, keepdims=True) for p in pairs}
    hc = {p: hv[p] - mu[p] for p in pairs}
    var = {p: jnp.mean(hc[p] * hc[p], axis=1, keepdims=True) for p in pairs}
    out = {(s, h): (hc[s, h] * lax.rsqrt(var[s, h] + LN_EPS) * g_ref[:, h * d:(h + 1) * d])
           * jax.nn.sigmoid(col(s, h, 3 * M_WIDTH)) for s, h in pairs}

    new = {}
    for p in pairs:
        c_old, n_old, m_old = old[p]
        k = col(*p, M_WIDTH) * (d ** -0.5)
        m_new = m_t[p][L - 1:L, :]
        b_last = b_c[p][L - 1:L, :]
        w_src = jnp.exp(b_last - b_c[p] + i_c[p] - m_new)
        w_old = jnp.exp(b_last + m_old - m_new)
        c_new = w_old * c_old + lax.dot_general(
            (w_src * col(*p, 2 * M_WIDTH)).astype(BF16), kb[p], (((0,), (0,)), ((), ())),
            preferred_element_type=F32)
        n_new = w_old * n_old + jnp.sum(w_src * k, axis=0, keepdims=True)
        new[p] = (c_new, n_new, m_new, out[p])
    for (s, h), (c_new, n_new, m_new, out) in new.items():
        c_ref[s, h] = c_new
        n_ref[s, h:h + 1, :] = n_new
        m_ref[s, :, h:h + 1] = m_new
        mo_ref[s, :, h * d:(h + 1) * d] = out


def _mlstm_call(mq, gt, C0, n0, m0, g, bb):
    B, T, _ = mq.shape
    L = math.gcd(T, M_CHUNK)
    nc = T // L
    gtt = gt[:, :, :2 * M_HEADS].reshape(B, nc, L, 2 * M_HEADS).transpose(0, 1, 3, 2)
    st4 = pl.BlockSpec((bb, M_HEADS, M_DIM, M_DIM), lambda i, c: (i, 0, 0, 0))
    st3 = pl.BlockSpec((bb, M_HEADS, M_DIM), lambda i, c: (i, 0, 0))
    st2 = pl.BlockSpec((bb, 1, M_HEADS), lambda i, c: (i, 0, 0))
    mo, C, n, m = pl.pallas_call(
        _mlstm_kernel,
        grid=(B // bb, nc),
        in_specs=[pl.BlockSpec((bb, L, 4 * M_WIDTH), lambda i, c: (i, c, 0)),
                  pl.BlockSpec((bb, L, LANES), lambda i, c: (i, c, 0)),
                  pl.BlockSpec((bb, 1, 2 * M_HEADS, L), lambda i, c: (i, c, 0, 0)),
                  st4, st3, st2,
                  pl.BlockSpec((1, M_WIDTH), lambda i, c: (0, 0))],
        out_specs=[pl.BlockSpec((bb, L, M_WIDTH), lambda i, c: (i, c, 0)), st4, st3, st2],
        out_shape=[jax.ShapeDtypeStruct((B, T, M_WIDTH), F32),
                   jax.ShapeDtypeStruct(C0.shape, F32), jax.ShapeDtypeStruct(n0.shape, F32),
                   jax.ShapeDtypeStruct((B, 1, M_HEADS), F32)],
        compiler_params=_cparams(("parallel", "arbitrary")),
        name="mlstm",
    )(mq, gt, gtt, C0, n0, m0.reshape(B, 1, M_HEADS), g.reshape(1, M_WIDTH))
    return mo, C, n, m.reshape(B, M_HEADS)


def _gelu_tanh(x):
    return x * (0.5 * (1.0 + jnp.tanh(math.sqrt(2.0 / math.pi) * (x + 0.044715 * (x * x * x)))))


def _compress_body(load, pe_ref, w1_ref, w2_ref, rows_total, nh, n_cmp):
    w = KV_WIDTH
    half = CMP_LEN // 2
    pa = jnp.zeros((rows_total, w), F32)
    pb = jnp.zeros((rows_total, w), F32)
    for j in range(half):
        xj = load(j)
        pa = pa + jnp.dot((xj + pe_ref[j:j + 1, :]).astype(BF16), w1_ref[j], preferred_element_type=F32)
        pb = pb + jnp.dot((xj + pe_ref[half + j:half + j + 1, :]).astype(BF16), w1_ref[half + j],
                          preferred_element_type=F32)
    hid = pa + pltpu.roll(pb, rows_total - 1, 0)
    y = jnp.dot(_gelu_tanh(hid).astype(BF16), w2_ref[...], preferred_element_type=F32)
    rows = lax.broadcasted_iota(jnp.int32, (rows_total, w), 0)
    assert nh & (nh - 1) == 0
    return jnp.where((rows & (nh - 1)) < n_cmp, y, 0.0)


def _compress_kernel(xk_ref, xv_ref, pe_ref, w1_ref, w2_ref, o_ref, *, n_cmp):
    nh = xk_ref.shape[1] // CMP_STRIDE
    for kv, x_ref in enumerate((xk_ref, xv_ref)):
        o_ref[0, :, kv * KV_WIDTH:(kv + 1) * KV_WIDTH] = _compress_body(
            lambda j: x_ref[0, pl.ds(j, nh, stride=CMP_STRIDE), :], pe_ref.at[kv], w1_ref.at[kv], w2_ref.at[kv],
            nh, nh, n_cmp)


def _compress_sample_kernel(pt_ref, pool_ref, pe_ref, w1_ref, w2_ref, o_ref, stage, xrow_k, xrow_v, sem, *,
                            seqs, n_pages, n_cmp):
    i = pl.program_id(0)
    slot = i % 2
    page = pool_ref.shape[2]
    nh = n_pages * page // CMP_STRIDE

    def copies(step, sl):
        return [pltpu.make_async_copy(pool_ref.at[pt_ref[step * seqs + s, p], pl.ds(kv * KV_WIDTH, KV_WIDTH), :],
                                      stage.at[sl, kv, s * n_pages + p], sem.at[sl])
                for s in range(seqs) for p in range(n_pages) for kv in range(2)]

    @pl.when(i == 0)
    def _():
        for cp in copies(0, 0):
            cp.start()

    @pl.when(i + 1 < pl.num_programs(0))
    def _():
        for cp in copies(i + 1, 1 - slot):
            cp.start()

    for cp in copies(i, slot):
        cp.wait()
    xrows = (xrow_k, xrow_v)
    for kv in range(2):
        for pi in range(seqs * n_pages):
            xrows[kv][pi * page:(pi + 1) * page, :] = stage[slot, kv, pi].T
    for kv in range(2):
        o_ref[:, kv * KV_WIDTH:(kv + 1) * KV_WIDTH] = _compress_body(
            lambda j: xrows[kv][pl.ds(j, seqs * nh, stride=CMP_STRIDE), :],
            pe_ref.at[kv], w1_ref.at[kv], w2_ref.at[kv], seqs * nh, nh, n_cmp)


def _compress_sample(pool, page_table, cmp_w, n_cmp, seqs):
    n_pool, w, page = pool.shape
    DB, n_pages = page_table.shape
    nh = n_pages * page // CMP_STRIDE
    pe4, w1, w2 = cmp_w
    full = lambda *s: pl.BlockSpec(s, lambda i, pt: (0,) * len(s))
    grid_spec = pltpu.PrefetchScalarGridSpec(
        num_scalar_prefetch=1,
        grid=(DB // seqs,),
        in_specs=[pl.BlockSpec(memory_space=pl.ANY), full(*pe4.shape), full(*w1.shape), full(*w2.shape)],
        out_specs=pl.BlockSpec((seqs * nh, w), lambda i, pt: (i, 0)),
        scratch_shapes=[pltpu.VMEM((2, 2, seqs * n_pages, KV_WIDTH, page), F32),
                        pltpu.VMEM((seqs * n_pages * page, KV_WIDTH), F32),
                        pltpu.VMEM((seqs * n_pages * page, KV_WIDTH), F32),
                        pltpu.SemaphoreType.DMA((2,))],
    )
    return pl.pallas_call(
        functools.partial(_compress_sample_kernel, seqs=seqs, n_pages=n_pages, n_cmp=n_cmp),
        grid_spec=grid_spec,
        out_shape=jax.ShapeDtypeStruct((DB * nh, w), F32),
        compiler_params=_cparams(("arbitrary",)),
        name="compress_sample",
    )(page_table, pool, pe4, w1, w2)


def _block_diag2(a):
    z = jnp.zeros_like(a)
    return jnp.concatenate([jnp.concatenate([a, z], -1), jnp.concatenate([z, a], -1)], -2)


def _compress_weights(cmp_params):
    pe_k, w1_k, w2_k, pe_v, w1_v, w2_v = cmp_params
    pe = jnp.stack([jnp.concatenate([p, p], axis=1) for p in (pe_k, pe_v)])
    w1 = jnp.stack([_block_diag2(w.reshape(CMP_LEN, A_DIM, A_DIM)) for w in (w1_k, w1_v)]).astype(BF16)
    w2 = jnp.stack([_block_diag2(w) for w in (w2_k, w2_v)]).astype(BF16)
    return pe, w1, w2


def _compress_prompt(cmp_rows, cmp_w):
    B, L, w = cmp_rows.shape
    nh = L // CMP_STRIDE
    n_cmp = (L - CMP_LEN) // CMP_STRIDE + 1
    pe4, w1, w2 = cmp_w
    full = lambda *s: pl.BlockSpec(s, lambda b: (0,) * len(s))
    return pl.pallas_call(
        functools.partial(_compress_kernel, n_cmp=n_cmp),
        grid=(B,),
        in_specs=[pl.BlockSpec((1, L, KV_WIDTH), lambda b: (b, 0, 0)),
                  pl.BlockSpec((1, L, KV_WIDTH), lambda b: (b, 0, 1)),
                  full(*pe4.shape), full(*w1.shape), full(*w2.shape)],
        out_specs=pl.BlockSpec((1, nh, w), lambda b: (b, 0, 0)),
        out_shape=jax.ShapeDtypeStruct((B, nh, w), F32),
        compiler_params=_cparams(("parallel",)),
        name="compress_prompt",
    )(cmp_rows, cmp_rows, pe4, w1, w2)


_NEG = -1e30
_NEG_SEL = -1e9
_SEL_CHUNK = 512


def _dot_nt(a, b):
    return lax.dot_general(a, b, (((1,), (1,)), ((), ())), preferred_element_type=F32)


def _softmax_rows(s, valid):
    s = jnp.where(valid[None], s, _NEG)
    m = jnp.max(s, axis=-1, keepdims=True)
    e = jnp.where(valid[None], jnp.exp(s - m), 0.0)
    return e / jnp.maximum(jnp.sum(e, axis=-1, keepdims=True), jnp.finfo(jnp.float32).tiny)


def _heads_to_rows(q):
    tq = q.shape[0]
    lane = lax.broadcasted_iota(jnp.int32, (tq, LANES), 1)
    q = q * (A_DIM ** -0.5)
    rows = []
    for hd in range(A_HEADS):
        g = hd // A_GROUP
        tile = q[:, LANES * (hd // 2):LANES * (hd // 2 + 1)]
        if hd % 2 != g:
            tile = pltpu.roll(tile, A_DIM, 1)
        keep = (lane < A_DIM) if g == 0 else (lane >= A_DIM)
        rows.append(jnp.where(keep, tile, 0.0))
    return jnp.concatenate(rows, axis=0).astype(BF16)


def _gate_rows_to_heads(gt, o_c, o_s, o_w):
    tq = gt.shape[0]
    lane = lax.broadcasted_iota(jnp.int32, (tq, LANES), 1)
    gs = jax.nn.sigmoid(gt)
    tiles = []
    for pair in range(A_HEADS // 2):
        g = (2 * pair) // A_GROUP
        mixed = []
        for hd in (2 * pair, 2 * pair + 1):
            c = 2 * M_HEADS + 3 * hd
            r = slice(hd * tq, (hd + 1) * tq)
            mixed.append(gs[:, c:c + 1] * o_c[r] + gs[:, c + 1:c + 2] * o_s[r] + gs[:, c + 2:c + 3] * o_w[r])
        a, b = mixed
        if g == 0:
            tiles.append(jnp.where(lane < A_DIM, a, pltpu.roll(b, A_DIM, 1)))
        else:
            tiles.append(jnp.where(lane < A_DIM, pltpu.roll(a, A_DIM, 1), b))
    return jnp.concatenate(tiles, axis=1)


def _attend_two(s_a, ok_a, v_a, s_b, ok_b, v_b, v_a_transposed=False):
    nh, tq = s_a.shape[:2]
    if ok_a is not None:
        s_a = jnp.where(ok_a[None], s_a, _NEG)
    s_b = jnp.where(ok_b[None], s_b, _NEG)
    m = jnp.maximum(jnp.max(s_a, axis=-1, keepdims=True), jnp.max(s_b, axis=-1, keepdims=True))
    e_a = jnp.exp(s_a - m)
    e_b = jnp.exp(s_b - m)
    l = jnp.sum(e_a, axis=-1, keepdims=True) + jnp.sum(e_b, axis=-1, keepdims=True)
    e_a = e_a.reshape(nh * tq, -1).astype(BF16)
    o_a = _dot_nt(e_a, v_a) if v_a_transposed else jnp.dot(e_a, v_a, preferred_element_type=F32)
    o = o_a + jnp.dot(e_b.reshape(nh * tq, -1).astype(BF16), v_b, preferred_element_type=F32)
    return o / l.reshape(nh * tq, 1)


def _nsa_sample_kernel(pt_ref, q_ref, gt_ref, kc_ref, vc_ref, seln_ref, wst_ref, wnew_ref, oh_ref, pool_ref,
                       o_ref, nwin_ref, selbuf, sem, *, past):
    b = pl.program_id(0)
    slot = b % 2
    tq = q_ref.shape[0]
    nh = A_HEADS
    n_pages = pt_ref.shape[1]
    page = pool_ref.shape[2]
    wbuf = wst_ref.shape[1]
    n_sel = -(-(past + tq) // SEL_LEN)

    def copies(seq, sl):
        return [pltpu.make_async_copy(pool_ref.at[pt_ref[seq, p]], selbuf.at[sl, :, pl.ds(p * page, page)],
                                      sem.at[sl]) for p in range(n_pages)]

    @pl.when(b == 0)
    def _():
        for cp in copies(0, 0):
            cp.start()

    @pl.when(b + 1 < pl.num_programs(0))
    def _():
        for cp in copies(b + 1, 1 - slot):
            cp.start()

    qz = _heads_to_rows(q_ref[...])
    tpos = past + lax.broadcasted_iota(jnp.int32, (tq, LANES), 0)

    kc = kc_ref[...].astype(BF16)
    vc = vc_ref[...].astype(BF16)
    ncp = kc.shape[0]
    s_c = _dot_nt(qz, kc).reshape(nh, tq, ncp)
    tp_c = past + lax.broadcasted_iota(jnp.int32, (tq, ncp), 0)
    nidx = lax.broadcasted_iota(jnp.int32, (tq, ncp), 1)
    p_c = _softmax_rows(s_c, nidx * CMP_STRIDE + (CMP_LEN - 1) <= tp_c)
    o_c = jnp.dot(p_c.reshape(nh * tq, ncp).astype(BF16), vc, preferred_element_type=F32)

    cn = lax.broadcasted_iota(jnp.int32, (ncp, LANES), 0) * CMP_STRIDE
    jn = lax.broadcasted_iota(jnp.int32, (ncp, LANES), 1) * SEL_LEN
    ov = jnp.where((cn < jn + SEL_LEN) & (cn + CMP_LEN > jn), 1.0, 0.0).astype(BF16)
    jb = lax.broadcasted_iota(jnp.int32, (tq, LANES), 1)
    cur = tpos // SEL_LEN
    forced = (jb == 0) | (jb == cur) | (jb == cur - 1)
    bias = []
    for g in range(A_KV_HEADS):
        ps = p_c[g * A_GROUP]
        for r in range(1, A_GROUP):
            ps = ps + p_c[g * A_GROUP + r]
        hi = ps.astype(BF16)
        lo = (ps - hi.astype(F32)).astype(BF16)
        score = (jnp.dot(hi, ov, preferred_element_type=F32) + jnp.dot(lo, ov, preferred_element_type=F32))
        score = jnp.where(forced, FORCE_SCORE, score)
        score = jnp.where(jb * SEL_LEN <= tpos, score, -1.0)
        cnt = jnp.zeros((tq, LANES), jnp.int32)
        for k in range(n_sel):
            sk = score[:, k:k + 1]
            ahead = (sk > score) | ((sk == score) & (jb > k))
            cnt = cnt + jnp.where(ahead, 1, 0)
        bias.append(jnp.where(cnt < min(SEL_TOP, n_sel), 0.0, _NEG_SEL).astype(BF16))
    q_aug = jnp.concatenate(
        [qz, jnp.concatenate([bias[hd // A_GROUP] for hd in range(nh)], axis=0)], axis=1)

    tw = lax.broadcasted_iota(jnp.int32, (tq, wbuf), 0)
    iw = lax.broadcasted_iota(jnp.int32, (tq, wbuf), 1)
    tn = lax.broadcasted_iota(jnp.int32, (tq, tq), 0)
    un = lax.broadcasted_iota(jnp.int32, (tq, tq), 1)
    wst = wst_ref[0]
    wnew = wnew_ref[...]
    s_wa = _dot_nt(qz, wst[:, :KV_WIDTH].astype(BF16)).reshape(nh, tq, wbuf)
    s_wb = _dot_nt(qz, wnew[:, :KV_WIDTH].astype(BF16)).reshape(nh, tq, tq)
    o_w = _attend_two(s_wa, (wbuf + tw - iw < WINDOW), wst[:, KV_WIDTH:].astype(BF16),
                      s_wb, un <= tn, wnew[:, KV_WIDTH:].astype(BF16))
    nwin_ref[0, :wbuf - tq, :] = wst[tq:, :]
    nwin_ref[0, wbuf - tq:, :] = wnew

    for cp in copies(b, slot):
        cp.wait()
    seln = seln_ref[...]
    k_aug_t = jnp.concatenate([selbuf[slot, :KV_WIDTH, :].astype(BF16), oh_ref[...]], axis=0)
    nblk = (past + lax.broadcasted_iota(jnp.int32, (tq, LANES), 0)) // SEL_LEN
    oh_new = jnp.where(nblk == lax.broadcasted_iota(jnp.int32, (tq, LANES), 1), 1.0, 0.0)
    kn_aug = jnp.concatenate([seln[:, :KV_WIDTH], oh_new], axis=1).astype(BF16)
    s_sa = jnp.dot(q_aug, k_aug_t, preferred_element_type=F32).reshape(nh, tq, past)
    s_sb = _dot_nt(q_aug, kn_aug).reshape(nh, tq, tq)
    o_s = _attend_two(s_sa, None, selbuf[slot, KV_WIDTH:, :].astype(BF16), s_sb, un <= tn,
                      seln[:, KV_WIDTH:].astype(BF16), v_a_transposed=True)

    o_ref[...] = _gate_rows_to_heads(gt_ref[...], o_c, o_s, o_w)


def _nsa_sample_call(qa, gt, comp, sel_new, win_state, win_new, sel_pool, page_table, T):
    DB, n_pages = page_table.shape
    page = sel_pool.shape[2]
    past = n_pages * page
    wbuf = win_state.shape[1]
    ncp = comp.shape[0] // DB
    w2 = 2 * KV_WIDTH
    assert past % SEL_LEN == 0 and T < CMP_STRIDE and T % 8 == 0 and wbuf == WINDOW and past >= WINDOW
    assert page % LANES == 0
    onehot = (jnp.arange(LANES)[:, None] == jnp.arange(past)[None, :] // SEL_LEN).astype(BF16)
    tok = lambda w: pl.BlockSpec((T, w), lambda b, pt: (b, 0))
    grid_spec = pltpu.PrefetchScalarGridSpec(
        num_scalar_prefetch=1,
        grid=(DB,),
        in_specs=[tok(A_WIDTH), tok(LANES),
                  pl.BlockSpec((ncp, KV_WIDTH), lambda b, pt: (b, 0)),
                  pl.BlockSpec((ncp, KV_WIDTH), lambda b, pt: (b, 1)),
                  tok(w2),
                  pl.BlockSpec((1, wbuf, w2), lambda b, pt: (b, 0, 0)),
                  tok(w2),
                  pl.BlockSpec((LANES, past), lambda b, pt: (0, 0)),
                  pl.BlockSpec(memory_space=pl.ANY)],
        out_specs=[tok(A_WIDTH), pl.BlockSpec((1, wbuf, w2), lambda b, pt: (b, 0, 0))],
        scratch_shapes=[pltpu.VMEM((2, w2, past), F32), pltpu.SemaphoreType.DMA((2,))],
    )
    return pl.pallas_call(
        functools.partial(_nsa_sample_kernel, past=past),
        grid_spec=grid_spec,
        out_shape=[jax.ShapeDtypeStruct((DB * T, A_WIDTH), F32), jax.ShapeDtypeStruct((DB, wbuf, w2), F32)],
        compiler_params=_cparams(("arbitrary",)),
        name="nsa_sample",
    )(page_table, qa, gt, comp, comp, sel_new, win_state, win_new, onehot, sel_pool)


def _value_with_ones(v, g):
    lane = lax.broadcasted_iota(jnp.int32, v.shape, 1)
    keep = (lane < A_DIM) if g == 0 else (lane >= A_DIM)
    return jnp.where(keep, v, 1.0).astype(BF16)


def _pv_with_sums(p, v):
    half = p.shape[0] // A_KV_HEADS
    return jnp.concatenate(
        [jnp.dot(p[g * half:(g + 1) * half], _value_with_ones(v, g), preferred_element_type=F32)
         for g in range(A_KV_HEADS)], axis=0)


def _normalise_rows(acc):
    half = acc.shape[0] // A_KV_HEADS
    return jnp.concatenate(
        [acc[:half] * (1.0 / acc[:half, A_DIM:A_DIM + 1]), acc[half:] * (1.0 / acc[half:, 0:1])], axis=0)


def _nsa_prompt_kernel(q_ref, gt_ref, kc_ref, vc_ref, sel_ref, win_ref, o_ref, *, seq):
    tq = Q_BLOCK
    nh = A_HEADS
    n_sel = seq // SEL_LEN
    s0 = pl.program_id(1) * tq
    qz = _heads_to_rows(q_ref[0])

    kc = kc_ref[0].astype(BF16)
    vc = vc_ref[0].astype(BF16)
    ncp = kc.shape[0]
    s_c = _dot_nt(qz, kc).reshape(nh, tq, ncp)
    tpos = s0 + lax.broadcasted_iota(jnp.int32, (tq, ncp), 0)
    nidx = lax.broadcasted_iota(jnp.int32, (tq, ncp), 1)
    p_c = _softmax_rows(s_c, nidx * CMP_STRIDE + (CMP_LEN - 1) <= tpos)
    o_c = jnp.dot(p_c.reshape(nh * tq, ncp).astype(BF16), vc, preferred_element_type=F32)
    mass = [sum(p_c[g * A_GROUP + r] for r in range(1, A_GROUP)) + p_c[g * A_GROUP] for g in range(A_KV_HEADS)]

    jn = lax.broadcasted_iota(jnp.int32, (n_sel, ncp), 0) * SEL_LEN
    cn = lax.broadcasted_iota(jnp.int32, (n_sel, ncp), 1) * CMP_STRIDE
    ov_t = jnp.where((cn < jn + SEL_LEN) & (cn + CMP_LEN > jn), 1.0, 0.0).astype(BF16)
    jb = lax.broadcasted_iota(jnp.int32, (n_sel, tq), 0)
    tp = s0 + lax.broadcasted_iota(jnp.int32, (n_sel, tq), 1)
    cur = tp // SEL_LEN
    forced = (jb == 0) | (jb == cur) | (jb == cur - 1)
    bias = []
    for g in range(A_KV_HEADS):
        hi = mass[g].astype(BF16)
        lo = (mass[g] - hi.astype(F32)).astype(BF16)
        score = _dot_nt(ov_t, hi) + _dot_nt(ov_t, lo)
        score = jnp.where(forced, FORCE_SCORE, score)
        score = jnp.where(jb * SEL_LEN <= tp, score, -1.0)
        cnt = jnp.zeros((n_sel, tq), jnp.int32)
        for k in range(n_sel):
            rk = score[k:k + 1, :]
            ahead = (rk > score) | ((rk == score) & (jb > k))
            cnt = cnt + jnp.where(ahead, 1, 0)
        bias_t = jnp.where(cnt < min(SEL_TOP, n_sel), 0.0, _NEG_SEL)
        if n_sel < LANES:
            bias_t = jnp.concatenate([bias_t, jnp.zeros((LANES - n_sel, tq), F32)], axis=0)
        bias.append(bias_t.T.astype(BF16))
    q_aug = jnp.concatenate(
        [qz, jnp.concatenate([bias[hd // A_GROUP] for hd in range(nh)], axis=0)], axis=1)

    wk = WINDOW + tq
    w0 = pl.multiple_of(jnp.clip(s0 - WINDOW, 0, seq - wk), tq)
    kvw = win_ref[0, pl.ds(w0, wk), :]
    s_w = _dot_nt(qz, kvw[:, :KV_WIDTH].astype(BF16)).reshape(nh, tq, wk)
    dist = (s0 + lax.broadcasted_iota(jnp.int32, (tq, wk), 0)) - (w0 + lax.broadcasted_iota(jnp.int32, (tq, wk), 1))
    s_w = jnp.where(((dist >= 0) & (dist < WINDOW))[None], s_w, _NEG)
    e_w = jnp.exp(s_w - jnp.max(s_w, axis=-1, keepdims=True)).reshape(nh * tq, wk).astype(BF16)
    o_w = _normalise_rows(_pv_with_sums(e_w, kvw[:, KV_WIDTH:]))

    kc_n = _SEL_CHUNK

    def chunk(c, carry, causal):
        m, acc = carry
        k0 = pl.multiple_of(c * kc_n, kc_n)
        kv = sel_ref[0, pl.ds(k0, kc_n), :]
        kblk = (k0 + lax.broadcasted_iota(jnp.int32, (kc_n, LANES), 0)) // SEL_LEN
        onehot = jnp.where(kblk == lax.broadcasted_iota(jnp.int32, (kc_n, LANES), 1), 1.0, 0.0)
        k_aug = jnp.concatenate([kv[:, :KV_WIDTH], onehot], axis=1).astype(BF16)
        s = _dot_nt(q_aug, k_aug)
        if causal:
            kpos = k0 + lax.broadcasted_iota(jnp.int32, (tq, kc_n), 1)
            qpos = s0 + lax.broadcasted_iota(jnp.int32, (tq, kc_n), 0)
            s = jnp.where((kpos <= qpos)[None], s.reshape(nh, tq, kc_n), _NEG_SEL).reshape(nh * tq, kc_n)
        m_new = jnp.maximum(m, jnp.max(s, axis=-1, keepdims=True))
        p = jnp.exp(s - m_new).astype(BF16)
        acc = jnp.exp(m - m_new) * acc + _pv_with_sums(p, kv[:, KV_WIDTH:])
        return m_new, acc

    n_ch = (s0 + tq - 1) // kc_n + 1
    init = (jnp.full((nh * tq, 1), _NEG, F32), jnp.zeros((nh * tq, LANES), F32))
    carry = lax.fori_loop(0, n_ch - 1, lambda c, cr: chunk(c, cr, False), init)
    o_s = _normalise_rows(chunk(n_ch - 1, carry, True)[1])

    o_ref[0] = _gate_rows_to_heads(gt_ref[0], o_c, o_s, o_w)


def _nsa_prompt_call(qa, gt, comp, sel_kv, win_kv):
    B, S, _ = qa.shape
    assert S % _SEL_CHUNK == 0 and S >= WINDOW + Q_BLOCK and S // SEL_LEN <= LANES
    ncp = comp.shape[1]
    per_b = lambda r, w: pl.BlockSpec((1, r, w), lambda b, i: (b, 0, 0))
    return pl.pallas_call(
        functools.partial(_nsa_prompt_kernel, seq=S),
        grid=(B, S // Q_BLOCK),
        in_specs=[pl.BlockSpec((1, Q_BLOCK, A_WIDTH), lambda b, i: (b, i, 0)),
                  pl.BlockSpec((1, Q_BLOCK, LANES), lambda b, i: (b, i, 0)),
                  pl.BlockSpec((1, ncp, KV_WIDTH), lambda b, i: (b, 0, 0)),
                  pl.BlockSpec((1, ncp, KV_WIDTH), lambda b, i: (b, 0, 1)),
                  per_b(S, 2 * KV_WIDTH), per_b(S, 2 * KV_WIDTH)],
        out_specs=pl.BlockSpec((1, Q_BLOCK, A_WIDTH), lambda b, i: (b, i, 0)),
        out_shape=jax.ShapeDtypeStruct((B, S, A_WIDTH), F32),
        compiler_params=_cparams(("parallel", "arbitrary")),
        name="nsa_prompt",
    )(qa, gt, comp, comp, sel_kv, win_kv)


def _nsa_prompt_t_kernel(q_ref, gt_ref, kc_ref, vc_ref, selk_ref, selvt_ref, wink_ref, winvt_ref, o_ref, *, seq):
    tq = Q_BLOCK
    nh = A_HEADS
    rows = nh * tq
    n_sel = seq // SEL_LEN
    kt = LANES
    s0 = pl.program_id(1) * tq
    qz = _heads_to_rows(q_ref[0])
    htile = lambda x, hd: x[:, hd * tq:(hd + 1) * tq]

    def qpos(n):
        return s0 + (lax.broadcasted_iota(jnp.int32, (n, rows), 1) & (tq - 1))

    def with_ones(vt):
        return jnp.concatenate([vt, jnp.ones((8, vt.shape[1]), F32)], axis=0).astype(BF16)

    def normalise(acc):
        return acc[:LANES] * (1.0 / acc[LANES:LANES + 1])

    kc = kc_ref[0].astype(BF16)
    ncp = kc.shape[0]
    s_c = _dot_nt(kc, qz)
    ok_c = lax.broadcasted_iota(jnp.int32, (ncp, rows), 0) * CMP_STRIDE + (CMP_LEN - 1) <= qpos(ncp)
    s_c = jnp.where(ok_c, s_c, _NEG)
    e_c = jnp.where(ok_c, jnp.exp(s_c - jnp.max(s_c, axis=0, keepdims=True)), 0.0)
    p_c = e_c * (1.0 / jnp.maximum(jnp.sum(e_c, axis=0, keepdims=True), jnp.finfo(jnp.float32).tiny))
    o_c = jnp.dot(vc_ref[0].T.astype(BF16), p_c.astype(BF16), preferred_element_type=F32)

    jn = lax.broadcasted_iota(jnp.int32, (n_sel, ncp), 0) * SEL_LEN
    cn = lax.broadcasted_iota(jnp.int32, (n_sel, ncp), 1) * CMP_STRIDE
    ov_t = jnp.where((cn < jn + SEL_LEN) & (cn + CMP_LEN > jn), 1.0, 0.0).astype(BF16)
    jb = lax.broadcasted_iota(jnp.int32, (n_sel, tq), 0)
    tp = s0 + lax.broadcasted_iota(jnp.int32, (n_sel, tq), 1)
    cur = tp // SEL_LEN
    forced = (jb == 0) | (jb == cur) | (jb == cur - 1)
    bias = []
    for g in range(A_KV_HEADS):
        mass = htile(p_c, g * A_GROUP)
        for r in range(1, A_GROUP):
            mass = mass + htile(p_c, g * A_GROUP + r)
        hi = mass.astype(BF16)
        lo = (mass - hi.astype(F32)).astype(BF16)
        score = (jnp.dot(ov_t, hi, preferred_element_type=F32) + jnp.dot(ov_t, lo, preferred_element_type=F32))
        score = jnp.where(forced, FORCE_SCORE, score)
        score = jnp.where(jb * SEL_LEN <= tp, score, -1.0)
        cnt = jnp.zeros((n_sel, tq), jnp.int32)
        for k in range(n_sel):
            rk = score[k:k + 1, :]
            ahead = (rk > score) | ((rk == score) & (jb > k))
            cnt = cnt + jnp.where(ahead, 1, 0)
        bias_t = jnp.where(cnt < min(SEL_TOP, n_sel), 0.0, _NEG_SEL)
        if n_sel < LANES:
            bias_t = jnp.concatenate([bias_t, jnp.zeros((LANES - n_sel, tq), F32)], axis=0)
        bias.append(bias_t.T.astype(BF16))
    q_aug = jnp.concatenate(
        [qz, jnp.concatenate([bias[hd // A_GROUP] for hd in range(nh)], axis=0)], axis=1)

    wk = WINDOW + tq
    w0 = pl.multiple_of(jnp.clip(s0 - WINDOW, 0, seq - wk), tq)
    s_w = _dot_nt(wink_ref[0, pl.ds(w0, wk), :].astype(BF16), qz)
    dist = qpos(wk) - (w0 + lax.broadcasted_iota(jnp.int32, (wk, rows), 0))
    s_w = jnp.where((dist >= 0) & (dist < WINDOW), s_w, _NEG)
    e_w = jnp.exp(s_w - jnp.max(s_w, axis=0, keepdims=True)).astype(BF16)
    vw_t = jnp.concatenate([winvt_ref[0, w0 // kt + i] for i in range(wk // kt)], axis=1)
    o_w = normalise(jnp.dot(with_ones(vw_t), e_w, preferred_element_type=F32))

    kc_n = _SEL_CHUNK

    def chunk(c, carry, causal):
        m, acc = carry
        k0 = pl.multiple_of(c * kc_n, kc_n)
        kblk = (k0 + lax.broadcasted_iota(jnp.int32, (kc_n, LANES), 0)) // SEL_LEN
        onehot = jnp.where(kblk == lax.broadcasted_iota(jnp.int32, (kc_n, LANES), 1), 1.0, 0.0)
        k_aug = jnp.concatenate([selk_ref[0, pl.ds(k0, kc_n), :], onehot], axis=1).astype(BF16)
        s = _dot_nt(k_aug, q_aug)
        if causal:
            s = jnp.where(k0 + lax.broadcasted_iota(jnp.int32, (kc_n, rows), 0) <= qpos(kc_n), s, _NEG_SEL)
        m_new = jnp.maximum(m, jnp.max(s, axis=0, keepdims=True))
        p = jnp.exp(s - m_new).astype(BF16)
        v_t = jnp.concatenate([selvt_ref[0, c * (kc_n // kt) + i] for i in range(kc_n // kt)], axis=1)
        acc = jnp.exp(m - m_new) * acc + jnp.dot(with_ones(v_t), p, preferred_element_type=F32)
        return m_new, acc

    n_ch = (s0 + tq - 1) // kc_n + 1
    init = (jnp.full((1, rows), _NEG, F32), jnp.zeros((LANES + 8, rows), F32))
    carry = lax.fori_loop(0, n_ch - 1, lambda c, cr: chunk(c, cr, False), init)
    o_s = normalise(chunk(n_ch - 1, carry, True)[1])

    gs_t = jax.nn.sigmoid(gt_ref[0]).T
    mixed = []
    for hd in range(nh):
        c = 2 * M_HEADS + 3 * hd
        g = hd // A_GROUP
        mix = (gs_t[c:c + 1, :] * htile(o_c, hd) + gs_t[c + 1:c + 2, :] * htile(o_s, hd)
               + gs_t[c + 2:c + 3, :] * htile(o_w, hd))
        mixed.append(mix[g * A_DIM:(g + 1) * A_DIM, :])
    o_ref[0] = jnp.concatenate(
        [jnp.concatenate([mixed[2 * pr], mixed[2 * pr + 1]], axis=0).T for pr in range(nh // 2)], axis=1)


def _nsa_prompt_t_call(qa, gt, comp, sel_k, sel_vt, win_k, win_vt):
    B, S, _ = qa.shape
    assert S % _SEL_CHUNK == 0 and S >= WINDOW + Q_BLOCK and S // SEL_LEN <= LANES
    ncp = comp.shape[1]
    keys = pl.BlockSpec((1, S, KV_WIDTH), lambda b, i: (b, 0, 0))
    vals = pl.BlockSpec((1, S // LANES, KV_WIDTH, LANES), lambda b, i: (b, 0, 0, 0))
    return pl.pallas_call(
        functools.partial(_nsa_prompt_t_kernel, seq=S),
        grid=(B, S // Q_BLOCK),
        in_specs=[pl.BlockSpec((1, Q_BLOCK, A_WIDTH), lambda b, i: (b, i, 0)),
                  pl.BlockSpec((1, Q_BLOCK, LANES), lambda b, i: (b, i, 0)),
                  pl.BlockSpec((1, ncp, KV_WIDTH), lambda b, i: (b, 0, 0)),
                  pl.BlockSpec((1, ncp, KV_WIDTH), lambda b, i: (b, 0, 1)),
                  keys, vals, keys, vals],
        out_specs=pl.BlockSpec((1, Q_BLOCK, A_WIDTH), lambda b, i: (b, i, 0)),
        out_shape=jax.ShapeDtypeStruct((B, S, A_WIDTH), F32),
        compiler_params=_cparams(("parallel", "arbitrary")),
        name="nsa_prompt",
    )(qa, gt, comp, comp, sel_k, sel_vt, win_k, win_vt)


def _moe(h2, logits, moe_w, n_experts):
    w_gu, b_gu, w_down, b_down = moe_w
    n, d = h2.shape
    gate_w, dest, src, blk_e, n_used = _route(logits, n_experts)
    yb = _ffn(h2[src], blk_e, n_used, w_gu, b_gu, w_down, b_down)
    return yb[dest.T], gate_w


def kernel(x_prompt, x_sample, cache_cmp, cache_sel, state_win, state_C, state_n, state_m, page_table,
           c_prompt, c_sample, w_ada, b_ada, w_in, b_in, m_norm_g, cmp_pe_k, cmp_w1_k, cmp_w2_k,
           cmp_pe_v, cmp_w1_v, cmp_w2_v, w_out, ln1_g, ln1_b, w_router, b_router, w_gu, b_gu,
           w_down, b_down, ln2_g, ln2_b):
    B, S, d = x_prompt.shape
    DB, T, _ = x_sample.shape
    depth = w_ada.shape[0]
    n_experts = w_router.shape[-1]
    alpha = (2 * depth) ** 0.25
    n_pool, page = cache_cmp.shape[1:3]
    past_len = page_table.shape[1] * page
    wbuf = state_win.shape[2]
    pos_p = jnp.arange(S)
    pos_s = past_len + jnp.arange(T)
    tt_p = min(S, 256)
    bb_s = min(DB, max(1, 256 // T))
    n_cmp_s = (past_len + T - CMP_LEN) // CMP_STRIDE + 1
    assert n_cmp_s < past_len // CMP_STRIDE
    r3 = lambda a: a.reshape(B, S, a.shape[-1])
    kv5 = lambda a, n, t: a.reshape(n, t, 2, A_KV_HEADS, A_DIM)
    y_prompt, y_sample = x_prompt, x_sample
    outs = [[] for _ in range(12)]
    for l in range(depth):
        cmp_l = (cmp_pe_k[l], cmp_w1_k[l], cmp_w2_k[l], cmp_pe_v[l], cmp_w1_v[l], cmp_w2_v[l])
        moe_w = (w_gu[l], b_gu[l], w_down[l], b_down[l])
        post1_w = (w_out[l], ln1_g[l], ln1_b[l], w_router[l], b_router[l])
        w_r, b_r = _prep_w_in(w_in[l], b_in[l])
        cmp_w = _compress_weights(cmp_l)
        c_all = jnp.concatenate([c_prompt, c_sample], axis=0)
        c_all = jnp.pad(c_all, ((0, -c_all.shape[0] % 8), (0, 0)))
        mod_all = _ada(c_all, w_ada[l], b_ada[l]).reshape(-1, 6, d)
        mod_p, mod_s = mod_all[:B], mod_all[B:B + DB]

        (mq, gt, qa, cmp_p, sel_p, win_p, cmp_t, sel_t, win_t, sel_vt, win_vt) = _pre(
            y_prompt, mod_p, pos_p, w_r, b_r, 1, tt_p, feature_major=True)
        zc = jnp.zeros((B, M_HEADS, M_DIM, M_DIM), F32)
        mo, C_p, n_p, m_p = _mlstm_call(r3(mq), r3(gt), zc, zc[..., 0], zc[..., 0, 0], m_norm_g[l],
                                        math.gcd(B, 4))
        comp = _compress_prompt(r3(cmp_p), cmp_w)
        ma = _nsa_prompt_t_call(r3(qa), r3(gt), comp, r3(sel_p), sel_vt, r3(win_p), win_vt)
        x1_p, h2_p, lg_p = _post1(mo.reshape(B * S, M_WIDTH), ma.reshape(B * S, A_WIDTH), y_prompt, mod_p,
                                  *post1_w, 1, tt_p, alpha)

        mq, gt, qa, cmp_s, sel_s, win_s = _pre(y_sample, mod_s, pos_s, w_r, b_r, bb_s, T)
        mo, C_s, n_s, m_s = _mlstm_call(mq.reshape(DB, T, -1), gt.reshape(DB, T, -1), state_C[l],
                                        state_n[l], state_m[l], m_norm_g[l], math.gcd(DB, 4))
        feature_major = lambda pool: pool.transpose(0, 2, 3, 4, 1).reshape(n_pool, 2 * KV_WIDTH, page)
        comp = _compress_sample(feature_major(cache_cmp[l]), page_table, cmp_w, n_cmp_s, math.gcd(DB, 4))
        ma, new_win = _nsa_sample_call(qa, gt, comp, sel_s, state_win[l].reshape(DB, wbuf, 2 * KV_WIDTH),
                                       win_s, feature_major(cache_sel[l]), page_table, T)
        x1_s, h2_s, lg_s = _post1(mo.reshape(DB * T, M_WIDTH), ma, y_sample, mod_s, *post1_w, bb_s, T, alpha)

        yg, gate_w = _moe(jnp.concatenate([h2_p, h2_s], axis=0), jnp.concatenate([lg_p, lg_s], axis=0),
                          moe_w, n_experts)
        y_prompt = _post2(yg, gate_w, 0, x1_p, mod_p, ln2_g[l], ln2_b[l], 1, tt_p, alpha)
        y_sample = _post2(yg, gate_w, B * S, x1_s, mod_s, ln2_g[l], ln2_b[l], bb_s, T, alpha)

        kv5_t = lambda a: a.reshape(B, 2, A_KV_HEADS, A_DIM, -1).transpose(0, 4, 1, 2, 3)
        new = (kv5_t(cmp_t), kv5_t(sel_t), kv5_t(win_t[:, :, -min(WINDOW, S):]), C_p, n_p, m_p,
               kv5(cmp_s, DB, T), kv5(sel_s, DB, T), new_win.reshape(state_win.shape[1:]), C_s, n_s, m_s)
        for lst, v in zip(outs, new):
            lst.append(v)
    return (y_prompt, y_sample, *[jnp.stack(v) for v in outs])
```

```python
import functools
import math

import numpy as np
import jax
import jax.numpy as jnp
from jax import lax
from jax.experimental import pallas as pl
from jax.experimental.pallas import tpu as pltpu

F32 = jnp.float32
BF16 = jnp.bfloat16

M_HEADS = 4
M_DIM = 128
M_WIDTH = M_HEADS * M_DIM
M_CHUNK = 64
A_HEADS = 8
A_KV_HEADS = 2
A_GROUP = A_HEADS // A_KV_HEADS
A_DIM = 64
A_WIDTH = A_HEADS * A_DIM
KV_WIDTH = A_KV_HEADS * A_DIM
CMP_LEN = 32
CMP_STRIDE = 16
SEL_LEN = 64
SEL_TOP = 16
WINDOW = 512
Q_BLOCK = 128
FORCE_SCORE = 1.0e4
ROPE_THETA = 500000.0
ROPE_DIM = A_DIM // 4
TOP_K = 4
SWIGLU_LIMIT = 7.0
SWIGLU_ALPHA = 1.702
LN_EPS = 1e-5
LANES = 128
MOE_ROWS = 512
VMEM_LIMIT = 56 * 1024 * 1024

_O_MQ = 0
_O_IF = 4 * M_WIDTH
_O_QA = _O_IF + 2 * M_HEADS
_O_KV = _O_QA + A_WIDTH
_O_GA = _O_KV + 6 * KV_WIDTH
_N_IN = _O_GA + 3 * A_HEADS
_R_QA = 4 * M_WIDTH
_R_KV = _R_QA + A_WIDTH
_R_GT = _R_KV + 6 * KV_WIDTH
_R_END = _R_GT + LANES


def _cparams(sem):
    return pltpu.CompilerParams(dimension_semantics=sem, vmem_limit_bytes=VMEM_LIMIT)


def _ln_core(x):
    mu = jnp.mean(x, axis=-1, keepdims=True)
    xc = x - mu
    var = jnp.mean(xc * xc, axis=-1, keepdims=True)
    return xc * lax.rsqrt(var + LN_EPS)


def _ada_kernel(c_ref, w_ref, b_ref, o_ref):
    o_ref[...] = jnp.dot(c_ref[...].astype(BF16), w_ref[...].astype(BF16),
                         preferred_element_type=F32) + b_ref[...]


def _ada(c, w_ada, b_ada):
    n, d = c.shape
    cols = w_ada.shape[1]
    return pl.pallas_call(
        _ada_kernel,
        grid=(cols // d,),
        in_specs=[pl.BlockSpec((n, d), lambda j: (0, 0)),
                  pl.BlockSpec((d, d), lambda j: (0, j)),
                  pl.BlockSpec((1, d), lambda j: (0, j))],
        out_specs=pl.BlockSpec((n, d), lambda j: (0, j)),
        out_shape=jax.ShapeDtypeStruct((n, cols), F32),
        compiler_params=_cparams(("arbitrary",)),
        name="ada",
    )(c, w_ada, b_ada.reshape(1, cols))


def _rope_apply(v, cos, sa, sb):
    reps = v.shape[1] // LANES
    tile = lambda t: t if reps == 1 else jnp.concatenate([t] * reps, axis=1)
    w = v.shape[1]
    return (v * tile(cos) + pltpu.roll(v, w - ROPE_DIM // 2, 1) * tile(sa)
            + pltpu.roll(v, ROPE_DIM // 2, 1) * tile(sb))


def _pre_kernel(x_ref, mod_ref, cos_ref, sa_ref, sb_ref, w_ref, b_ref,
                mq_ref, gt_ref, qa_ref, cmp_ref, sel_ref, win_ref, *t_refs):
    bb, tt, d = x_ref.shape
    mod = mod_ref[...]
    h = _ln_core(x_ref[...]) * (1.0 + mod[:, 1:2, :]) + mod[:, 0:1, :]
    h = h.reshape(bb * tt, d).astype(BF16)
    z = jnp.dot(h, w_ref[...], preferred_element_type=F32) + b_ref[...]
    cos, sa, sb = cos_ref[...], sa_ref[...], sb_ref[...]
    mq_ref[...] = z[:, :_R_QA]
    gt_ref[...] = z[:, _R_GT:_R_END]
    qa_ref[...] = _rope_apply(z[:, _R_QA:_R_KV], cos, sa, sb)
    for n, ref in enumerate((cmp_ref, sel_ref, win_ref)):
        o = _R_KV + 2 * KV_WIDTH * n
        kv = jnp.concatenate([_rope_apply(z[:, o:o + KV_WIDTH], cos, sa, sb),
                              z[:, o + KV_WIDTH:o + 2 * KV_WIDTH]], axis=1)
        ref[...] = kv
        if t_refs:
            kv_t = kv.T
            t_refs[n][0] = kv_t
            if n > 0:
                for c in range(kv_t.shape[1] // LANES):
                    t_refs[2 + n][0, c] = kv_t[KV_WIDTH:, c * LANES:(c + 1) * LANES]


def _rope_tables(pos):
    half = ROPE_DIM // 2
    inv = ROPE_THETA ** (-jnp.arange(0, ROPE_DIM, 2, dtype=F32) / ROPE_DIM)
    ang = pos.astype(F32)[:, None] * inv[None, :]
    cos, sin = jnp.cos(ang), jnp.sin(ang)
    n = pos.shape[0]
    one = jnp.ones((n, A_DIM - ROPE_DIM), F32)
    zero = jnp.zeros((n, A_DIM - ROPE_DIM), F32)
    zh = jnp.zeros((n, half), F32)
    cos_t = jnp.concatenate([cos, cos, one], axis=1)
    sa_t = jnp.concatenate([-sin, zh, zero], axis=1)
    sb_t = jnp.concatenate([zh, sin, zero], axis=1)
    two = lambda t: jnp.concatenate([t, t], axis=1)
    return two(cos_t), two(sa_t), two(sb_t)


def _prep_w_in(w_in, b_in):
    pad = LANES - 2 * M_HEADS - 3 * A_HEADS
    cat = lambda a: jnp.concatenate(
        [a[..., _O_MQ:_O_IF], a[..., _O_QA:_O_GA], a[..., _O_IF:_O_QA], a[..., _O_GA:_N_IN],
         jnp.zeros(a.shape[:-1] + (pad,), a.dtype)], axis=-1)
    return cat(w_in).astype(BF16), cat(b_in[None, :])


def _pre(x, mod, pos, w_r, b_r, bb, tt, feature_major=False):
    B, T, d = x.shape
    nt = T // tt
    rows = bb * tt
    cos, sa, sb = _rope_tables(pos)
    if bb > 1:
        cos, sa, sb = (jnp.tile(t, (bb, 1)) for t in (cos, sa, sb))
    n_tok = B * T
    tab = pl.BlockSpec((rows, LANES), lambda i, j: (j, 0))
    row = lambda w: pl.BlockSpec((rows, w), lambda i, j: (i * nt + j, 0))
    widths = (_R_QA, LANES, A_WIDTH, 2 * KV_WIDTH, 2 * KV_WIDTH, 2 * KV_WIDTH)
    out_specs = [row(w) for w in widths]
    out_shape = [jax.ShapeDtypeStruct((n_tok, w), F32) for w in widths]
    if feature_major:
        assert bb == 1 and tt % LANES == 0
        out_specs += [pl.BlockSpec((1, 2 * KV_WIDTH, tt), lambda i, j: (i, 0, j))] * 3
        out_shape += [jax.ShapeDtypeStruct((B, 2 * KV_WIDTH, T), F32)] * 3
        out_specs += [pl.BlockSpec((1, tt // LANES, KV_WIDTH, LANES), lambda i, j: (i, j, 0, 0))] * 2
        out_shape += [jax.ShapeDtypeStruct((B, T // LANES, KV_WIDTH, LANES), F32)] * 2
    return pl.pallas_call(
        _pre_kernel,
        grid=(B // bb, nt),
        in_specs=[pl.BlockSpec((bb, tt, d), lambda i, j: (i, j, 0)),
                  pl.BlockSpec((bb, 6, d), lambda i, j: (i, 0, 0)),
                  tab, tab, tab,
                  pl.BlockSpec((d, _R_END), lambda i, j: (0, 0)),
                  pl.BlockSpec((1, _R_END), lambda i, j: (0, 0))],
        out_specs=out_specs,
        out_shape=out_shape,
        compiler_params=_cparams(("parallel", "arbitrary")),
        name="pre",
    )(x, mod, cos, sa, sb, w_r, b_r)


def _post1_kernel(mo_ref, ao_ref, x_ref, mod_ref, wo_ref, g_ref, b_ref, wr_ref, br_ref,
                  x1_ref, h2_ref, lg_ref, *, alpha):
    bb, tt, d = x_ref.shape
    mod = mod_ref[...]
    mixin = jnp.concatenate([mo_ref[...], ao_ref[...]], axis=1).astype(BF16)
    mix = jnp.dot(mixin, wo_ref[...], preferred_element_type=F32).reshape(bb, tt, d)
    x1 = _ln_core(alpha * x_ref[...] + mod[:, 2:3, :] * mix) * g_ref[...] + b_ref[...]
    h2 = _ln_core(x1) * (1.0 + mod[:, 4:5, :]) + mod[:, 3:4, :]
    x1_ref[...] = x1
    h2f = h2.reshape(bb * tt, d)
    h2_ref[...] = h2f
    hi = h2f.astype(BF16)
    lo = (h2f - hi.astype(F32)).astype(BF16)
    lg_ref[...] = (jnp.dot(hi, wr_ref[0], preferred_element_type=F32)
                   + jnp.dot(hi, wr_ref[1], preferred_element_type=F32)
                   + jnp.dot(lo, wr_ref[0], preferred_element_type=F32)) + br_ref[...]


def _post1(mo, ao, x, mod, w_out, ln_g, ln_b, w_router, b_router, bb, tt, alpha):
    B, T, d = x.shape
    nt = T // tt
    rows = bb * tt
    ne = w_router.shape[1]
    wr = jnp.pad(w_router, ((0, 0), (0, LANES - ne)))
    wr_hi = wr.astype(BF16)
    wr = jnp.stack([wr_hi, (wr - wr_hi.astype(F32)).astype(BF16)])
    br = jnp.pad(b_router, (0, LANES - ne), constant_values=-jnp.inf).reshape(1, LANES)
    full = lambda *s: pl.BlockSpec(s, lambda i, j: (0,) * len(s))
    row = lambda w: pl.BlockSpec((rows, w), lambda i, j: (i * nt + j, 0))
    return pl.pallas_call(
        functools.partial(_post1_kernel, alpha=alpha),
        grid=(B // bb, nt),
        in_specs=[row(M_WIDTH), row(A_WIDTH),
                  pl.BlockSpec((bb, tt, d), lambda i, j: (i, j, 0)),
                  pl.BlockSpec((bb, 6, d), lambda i, j: (i, 0, 0)),
                  full(M_WIDTH + A_WIDTH, d), full(1, d), full(1, d), full(2, d, LANES), full(1, LANES)],
        out_specs=[pl.BlockSpec((bb, tt, d), lambda i, j: (i, j, 0)), row(d), row(LANES)],
        out_shape=[jax.ShapeDtypeStruct((B, T, d), F32),
                   jax.ShapeDtypeStruct((B * T, d), F32),
                   jax.ShapeDtypeStruct((B * T, LANES), F32)],
        compiler_params=_cparams(("parallel", "arbitrary")),
        name="post1",
    )(mo, ao, x, mod, w_out.astype(BF16), ln_g.reshape(1, d), ln_b.reshape(1, d), wr, br)


def _ffn_kernel(be_ref, nu_ref, x_ref, wgu_ref, bgu_ref, wd_ref, bd_ref, y_ref, wgu_bf, wd_bf):
    i = pl.program_id(0)
    dff = wd_ref.shape[1]

    @pl.when((i == 0) | (be_ref[i] != be_ref[jnp.maximum(i - 1, 0)]))
    def _():
        wgu_bf[...] = wgu_ref[0].astype(BF16)
        wd_bf[...] = wd_ref[0].astype(BF16)

    @pl.when(i < nu_ref[0])
    def _():
        half = x_ref.shape[0] // 2
        halves = [slice(h * half, (h + 1) * half) for h in range(2)]
        gus = [jnp.dot(x_ref[r, :].astype(BF16), wgu_bf[...], preferred_element_type=F32) + bgu_ref[0]
               for r in halves]
        acts = []
        for gu in gus:
            g = jnp.minimum(gu[:, :dff], SWIGLU_LIMIT)
            u = jnp.clip(gu[:, dff:], -SWIGLU_LIMIT, SWIGLU_LIMIT)
            acts.append(((u + 1.0) * (g * jax.nn.sigmoid(SWIGLU_ALPHA * g))).astype(BF16))
        for r, act in zip(halves, acts):
            y_ref[r, :] = jnp.dot(act, wd_bf[...], preferred_element_type=F32) + bd_ref[0]

    @pl.when(i >= nu_ref[0])
    def _():
        y_ref[...] = jnp.zeros_like(y_ref)


def _ffn(xb, blk_e, n_used, w_gu, b_gu, w_down, b_down):
    rows, d = xb.shape
    ne, _, f2 = w_gu.shape
    dff = w_down.shape[1]
    nb = rows // MOE_ROWS
    grid_spec = pltpu.PrefetchScalarGridSpec(
        num_scalar_prefetch=2,
        grid=(nb,),
        in_specs=[pl.BlockSpec((MOE_ROWS, d), lambda i, be, nu: (i, 0)),
                  pl.BlockSpec((1, d, f2), lambda i, be, nu: (be[i], 0, 0)),
                  pl.BlockSpec((1, 1, f2), lambda i, be, nu: (be[i], 0, 0)),
                  pl.BlockSpec((1, dff, d), lambda i, be, nu: (be[i], 0, 0)),
                  pl.BlockSpec((1, 1, d), lambda i, be, nu: (be[i], 0, 0))],
        out_specs=pl.BlockSpec((MOE_ROWS, d), lambda i, be, nu: (i, 0)),
        scratch_shapes=[pltpu.VMEM((d, f2), BF16), pltpu.VMEM((dff, d), BF16)],
    )
    return pl.pallas_call(
        _ffn_kernel,
        grid_spec=grid_spec,
        out_shape=jax.ShapeDtypeStruct((rows, d), F32),
        compiler_params=_cparams(("arbitrary",)),
        name="ffn",
    )(blk_e, n_used, xb, w_gu, b_gu.reshape(ne, 1, f2), w_down, b_down.reshape(ne, 1, d))


def _post2_kernel(yg_ref, gw_ref, x1_ref, mod_ref, g_ref, b_ref, y_ref, *, alpha):
    bb, tt, d = x1_ref.shape
    gw = gw_ref[...]
    f = yg_ref[0] * gw[:, 0:1]
    for k in range(1, TOP_K):
        f = f + yg_ref[k] * gw[:, k:k + 1]
    y = alpha * x1_ref[...] + mod_ref[...][:, 5:6, :] * f.reshape(bb, tt, d)
    y_ref[...] = _ln_core(y) * g_ref[...] + b_ref[...]


def _post2(yg, gw, row0, x1, mod, ln_g, ln_b, bb, tt, alpha):
    B, T, d = x1.shape
    rows = bb * tt
    nt = T // tt
    blk0 = row0 // rows
    assert row0 % rows == 0
    full = lambda *s: pl.BlockSpec(s, lambda i, j: (0,) * len(s))
    return pl.pallas_call(
        functools.partial(_post2_kernel, alpha=alpha),
        grid=(B // bb, nt),
        in_specs=[pl.BlockSpec((TOP_K, rows, d), lambda i, j: (0, blk0 + i * nt + j, 0)),
                  pl.BlockSpec((rows, LANES), lambda i, j: (blk0 + i * nt + j, 0)),
                  pl.BlockSpec((bb, tt, d), lambda i, j: (i, j, 0)),
                  pl.BlockSpec((bb, 6, d), lambda i, j: (i, 0, 0)),
                  full(1, d), full(1, d)],
        out_specs=pl.BlockSpec((bb, tt, d), lambda i, j: (i, j, 0)),
        out_shape=jax.ShapeDtypeStruct((B, T, d), F32),
        compiler_params=_cparams(("parallel", "arbitrary")),
        name="post2",
    )(yg, gw, x1, mod, ln_g.reshape(1, d), ln_b.reshape(1, d))


def _lane_prefix_sum(x):
    lane = lax.broadcasted_iota(jnp.int32, x.shape, 1)
    s = 1
    while s < LANES:
        x = x + jnp.where(lane >= s, pltpu.roll(x, s, 1), 0.0)
        s *= 2
    return x


def _route_kernel(lg_ref, dest_ref, gw_ref, cnt_ref, counts, running):
    phase = pl.program_id(0)
    blk = pl.program_id(1)
    r = lg_ref.shape[0]
    lane = lax.broadcasted_iota(jnp.int32, (r, LANES), 1)

    @pl.when((phase == 0) & (blk == 0))
    def _():
        counts[...] = jnp.zeros_like(counts)

    @pl.when((phase == 1) & (blk == 0))
    def _():
        running[...] = jnp.zeros_like(running)

    vals = lg_ref[...]
    onehots, tops = [], []
    for k in range(TOP_K):
        m = jnp.max(vals, axis=1, keepdims=True)
        idx = jnp.min(jnp.where(vals == m, lane, LANES), axis=1, keepdims=True)
        hit = lane == idx
        onehots.append(hit)
        tops.append(m)
        vals = jnp.where(hit, -jnp.inf, vals)
    ohf = [jnp.where(h, 1.0, 0.0) for h in onehots]
    block_cnt = [jnp.sum(o, axis=0, keepdims=True) for o in ohf]

    @pl.when(phase == 0)
    def _():
        counts[...] += block_cnt[0] + block_cnt[1] + block_cnt[2] + block_cnt[3]

    @pl.when(phase == 1)
    def _():
        cnt = counts[...]
        padded = jnp.ceil(cnt * (1.0 / MOE_ROWS)) * MOE_ROWS
        pad_start = _lane_prefix_sum(padded) - padded
        es = [jnp.exp(t - tops[0]) for t in tops]
        den = es[0] + es[1] + es[2] + es[3]
        ti = lax.broadcasted_iota(jnp.int32, (r, r), 0)
        tj = lax.broadcasted_iota(jnp.int32, (r, r), 1)
        before = jnp.where(tj < ti, 1.0, 0.0).astype(BF16)
        base = pad_start + running[...]
        dest = jnp.zeros((r, LANES), F32)
        gw = jnp.zeros((r, LANES), F32)
        for k in range(TOP_K):
            rank = jnp.dot(before, ohf[k].astype(BF16), preferred_element_type=F32)
            d_k = jnp.sum(ohf[k] * (base + rank), axis=1, keepdims=True)
            dest = jnp.where(lane == k, d_k, dest)
            gw = jnp.where(lane == k, es[k] / den, gw)
            base = base + block_cnt[k]
        running[...] = base - pad_start
        dest_ref[...] = dest.astype(jnp.int32)
        gw_ref[...] = gw
        cnt_ref[...] = jnp.broadcast_to(cnt, cnt_ref.shape)


def _route(logits, n_experts):
    n = logits.shape[0]
    rows = 512 if n % 512 == 0 else 256
    assert n % rows == 0
    nb = n // rows
    tok = pl.BlockSpec((rows, LANES), lambda p, i: (i, 0))
    out = pl.BlockSpec((rows, LANES), lambda p, i: (i * p, 0))
    dest, gw, cnt = pl.pallas_call(
        _route_kernel,
        grid=(2, nb),
        in_specs=[tok],
        out_specs=[out, out, pl.BlockSpec((8, LANES), lambda p, i: (0, 0))],
        out_shape=[jax.ShapeDtypeStruct((n, LANES), jnp.int32), jax.ShapeDtypeStruct((n, LANES), F32),
                   jax.ShapeDtypeStruct((8, LANES), F32)],
        scratch_shapes=[pltpu.VMEM((1, LANES), F32), pltpu.VMEM((1, LANES), F32)],
        compiler_params=_cparams(("arbitrary", "arbitrary")),
        name="route",
    )(logits)
    counts = cnt[0, :n_experts].astype(jnp.int32)
    pad_end = jnp.cumsum((counts + MOE_ROWS - 1) // MOE_ROWS * MOE_ROWS)
    n_slots = n * TOP_K
    n_blocks = -(-n_slots // MOE_ROWS) + n_experts
    n_used = pad_end[-1] // MOE_ROWS
    blk = jnp.minimum(jnp.arange(n_blocks, dtype=jnp.int32), n_used - 1) * MOE_ROWS
    blk_e = jnp.minimum(jnp.sum(pad_end[None, :] <= blk[:, None], axis=1), n_experts - 1).astype(jnp.int32)
    dest4 = dest[:, :TOP_K]
    src = jnp.zeros((n_blocks * MOE_ROWS,), jnp.int32).at[dest4.reshape(-1)].set(
        jnp.arange(n_slots, dtype=jnp.int32) // TOP_K, unique_indices=True)
    return gw, dest4, src, blk_e, n_used.reshape(1).astype(jnp.int32)


def _log_sigmoid(x):
    return jnp.minimum(x, 0.0) - jnp.log(1.0 + jnp.exp(-jnp.abs(x)))


def _mlstm_kernel(mq_ref, gt_ref, gtt_ref, c0_ref, n0_ref, m0_ref, g_ref, mo_ref, c_ref, n_ref, m_ref):
    bb, L, _ = mq_ref.shape
    d = M_DIM

    @pl.when(pl.program_id(1) == 0)
    def _():
        c_ref[...] = c0_ref[...]
        n_ref[...] = n0_ref[...]
        m_ref[...] = m0_ref[...]

    tt = lax.broadcasted_iota(jnp.int32, (L, L), 0)
    ss = lax.broadcasted_iota(jnp.int32, (L, L), 1)
    causal = ss <= tt
    pairs = [(s, h) for s in range(bb) for h in range(M_HEADS)]
    col = lambda s, h, off: mq_ref[s, :, off + h * d:off + (h + 1) * d]
    old = {p: (c_ref[p[0], p[1]], n_ref[p[0], p[1]:p[1] + 1, :], m_ref[p[0], :, p[1]:p[1] + 1]) for p in pairs}

    qb = {p: col(*p, 0).astype(BF16) for p in pairs}
    kb = {p: (col(*p, M_WIDTH) * (d ** -0.5)).astype(BF16) for p in pairs}
    qk = {p: _dot_nt(qb[p], kb[p]) for p in pairs}
    qc = {p: _dot_nt(qb[p], old[p][0].astype(BF16)) for p in pairs}

    i_c = {(s, h): gt_ref[s][:, h:h + 1] for s, h in pairs}
    i_r = {(s, h): gtt_ref[s, 0][h:h + 1, :] for s, h in pairs}
    lf_c = {(s, h): _log_sigmoid(gt_ref[s][:, M_HEADS + h:M_HEADS + h + 1]) for s, h in pairs}
    lf_r = {(s, h): _log_sigmoid(gtt_ref[s, 0][M_HEADS + h:M_HEADS + h + 1, :]) for s, h in pairs}
    b_c = {p: jnp.sum(jnp.where(causal, lf_r[p], 0.0), axis=1, keepdims=True) for p in pairs}
    b_r = {p: jnp.sum(jnp.where(causal, 0.0, lf_c[p]), axis=0, keepdims=True) + lf_r[p] for p in pairs}

    dmat = {p: jnp.where(causal, b_c[p] - b_r[p] + i_r[p], _NEG) for p in pairs}
    dmax = {p: jnp.max(dmat[p], axis=1, keepdims=True) for p in pairs}
    m_t = {p: jnp.maximum(b_c[p] + old[p][2], dmax[p]) for p in pairs}
    inter = {p: jnp.exp(b_c[p] + old[p][2] - m_t[p]) for p in pairs}
    a = {p: jnp.exp(dmat[p] - m_t[p]) * qk[p] for p in pairs}
    av = {p: jnp.dot(a[p].astype(BF16), col(*p, 2 * M_WIDTH).astype(BF16), preferred_element_type=F32)
          for p in pairs}

    qn = {p: jnp.sum(col(*p, 0) * old[p][1], axis=1, keepdims=True) for p in pairs}
    asum = {p: jnp.sum(a[p], axis=1, keepdims=True) for p in pairs}
    hv = {p: (inter[p] * qc[p] + av[p]) / jnp.maximum(jnp.abs(inter[p] * qn[p] + asum[p]), jnp.exp(-m_t[p]))
          for p in pairs}
    mu = {p: jnp.mean(hv[p], axis=1, keepdims=True) for p in pairs}
    hc = {p: hv[p] - mu[p] for p in pairs}
    var = {p: jnp.mean(hc[p] * hc[p], axis=1, keepdims=True) for p in pairs}
    out = {(s, h): (hc[s, h] * lax.rsqrt(var[s, h] + LN_EPS) * g_ref[:, h * d:(h + 1) * d])
           * jax.nn.sigmoid(col(s, h, 3 * M_WIDTH)) for s, h in pairs}

    new = {}
    for p in pairs:
        c_old, n_old, m_old = old[p]
        k = col(*p, M_WIDTH) * (d ** -0.5)
        m_new = m_t[p][L - 1:L, :]
        b_last = b_c[p][L - 1:L, :]
        w_src = jnp.exp(b_last - b_c[p] + i_c[p] - m_new)
        w_old = jnp.exp(b_last + m_old - m_new)
        c_new = w_old * c_old + lax.dot_general(
            (w_src * col(*p, 2 * M_WIDTH)).astype(BF16), kb[p], (((0,), (0,)), ((), ())),
            preferred_element_type=F32)
        n_new = w_old * n_old + jnp.sum(w_src * k, axis=0, keepdims=True)
        new[p] = (c_new, n_new, m_new, out[p])
    for (s, h), (c_new, n_new, m_new, out) in new.items():
        c_ref[s, h] = c_new
        n_ref[s, h:h + 1, :] = n_new
        m_ref[s, :, h:h + 1] = m_new
        mo_ref[s, :, h * d:(h + 1) * d] = out


def _mlstm_call(mq, gt, C0, n0, m0, g, bb):
    B, T, _ = mq.shape
    L = math.gcd(T, M_CHUNK)
    nc = T // L
    gtt = gt[:, :, :2 * M_HEADS].reshape(B, nc, L, 2 * M_HEADS).transpose(0, 1, 3, 2)
    st4 = pl.BlockSpec((bb, M_HEADS, M_DIM, M_DIM), lambda i, c: (i, 0, 0, 0))
    st3 = pl.BlockSpec((bb, M_HEADS, M_DIM), lambda i, c: (i, 0, 0))
    st2 = pl.BlockSpec((bb, 1, M_HEADS), lambda i, c: (i, 0, 0))
    mo, C, n, m = pl.pallas_call(
        _mlstm_kernel,
        grid=(B // bb, nc),
        in_specs=[pl.BlockSpec((bb, L, 4 * M_WIDTH), lambda i, c: (i, c, 0)),
                  pl.BlockSpec((bb, L, LANES), lambda i, c: (i, c, 0)),
                  pl.BlockSpec((bb, 1, 2 * M_HEADS, L), lambda i, c: (i, c, 0, 0)),
                  st4, st3, st2,
                  pl.BlockSpec((1, M_WIDTH), lambda i, c: (0, 0))],
        out_specs=[pl.BlockSpec((bb, L, M_WIDTH), lambda i, c: (i, c, 0)), st4, st3, st2],
        out_shape=[jax.ShapeDtypeStruct((B, T, M_WIDTH), F32),
                   jax.ShapeDtypeStruct(C0.shape, F32), jax.ShapeDtypeStruct(n0.shape, F32),
                   jax.ShapeDtypeStruct((B, 1, M_HEADS), F32)],
        compiler_params=_cparams(("parallel", "arbitrary")),
        name="mlstm",
    )(mq, gt, gtt, C0, n0, m0.reshape(B, 1, M_HEADS), g.reshape(1, M_WIDTH))
    return mo, C, n, m.reshape(B, M_HEADS)


def _gelu_tanh(x):
    return x * (0.5 * (1.0 + jnp.tanh(math.sqrt(2.0 / math.pi) * (x + 0.044715 * (x * x * x)))))


def _compress_body(load, pe_ref, w1_ref, w2_ref, rows_total, nh, n_cmp):
    w = KV_WIDTH
    half = CMP_LEN // 2
    pa = jnp.zeros((rows_total, w), F32)
    pb = jnp.zeros((rows_total, w), F32)
    for j in range(half):
        xj = load(j)
        pa = pa + jnp.dot((xj + pe_ref[j:j + 1, :]).astype(BF16), w1_ref[j], preferred_element_type=F32)
        pb = pb + jnp.dot((xj + pe_ref[half + j:half + j + 1, :]).astype(BF16), w1_ref[half + j],
                          preferred_element_type=F32)
    hid = pa + pltpu.roll(pb, rows_total - 1, 0)
    y = jnp.dot(_gelu_tanh(hid).astype(BF16), w2_ref[...], preferred_element_type=F32)
    rows = lax.broadcasted_iota(jnp.int32, (rows_total, w), 0)
    assert nh & (nh - 1) == 0
    return jnp.where((rows & (nh - 1)) < n_cmp, y, 0.0)


def _compress_kernel(xk_ref, xv_ref, pe_ref, w1_ref, w2_ref, o_ref, *, n_cmp):
    nh = xk_ref.shape[1] // CMP_STRIDE
    for kv, x_ref in enumerate((xk_ref, xv_ref)):
        o_ref[0, :, kv * KV_WIDTH:(kv + 1) * KV_WIDTH] = _compress_body(
            lambda j: x_ref[0, pl.ds(j, nh, stride=CMP_STRIDE), :], pe_ref.at[kv], w1_ref.at[kv], w2_ref.at[kv],
            nh, nh, n_cmp)


def _compress_sample_kernel(pt_ref, pool_ref, pe_ref, w1_ref, w2_ref, o_ref, stage, xrow_k, xrow_v, sem, *,
                            seqs, n_pages, n_cmp):
    i = pl.program_id(0)
    slot = i % 2
    page = pool_ref.shape[2]
    nh = n_pages * page // CMP_STRIDE

    def copies(step, sl):
        return [pltpu.make_async_copy(pool_ref.at[pt_ref[step * seqs + s, p], pl.ds(kv * KV_WIDTH, KV_WIDTH), :],
                                      stage.at[sl, kv, s * n_pages + p], sem.at[sl])
                for s in range(seqs) for p in range(n_pages) for kv in range(2)]

    @pl.when(i == 0)
    def _():
        for cp in copies(0, 0):
            cp.start()

    @pl.when(i + 1 < pl.num_programs(0))
    def _():
        for cp in copies(i + 1, 1 - slot):
            cp.start()

    for cp in copies(i, slot):
        cp.wait()
    xrows = (xrow_k, xrow_v)
    for kv in range(2):
        for pi in range(seqs * n_pages):
            xrows[kv][pi * page:(pi + 1) * page, :] = stage[slot, kv, pi].T
    for kv in range(2):
        o_ref[:, kv * KV_WIDTH:(kv + 1) * KV_WIDTH] = _compress_body(
            lambda j: xrows[kv][pl.ds(j, seqs * nh, stride=CMP_STRIDE), :],
            pe_ref.at[kv], w1_ref.at[kv], w2_ref.at[kv], seqs * nh, nh, n_cmp)


def _compress_sample(pool, page_table, cmp_w, n_cmp, seqs):
    n_pool, w, page = pool.shape
    DB, n_pages = page_table.shape
    nh = n_pages * page // CMP_STRIDE
    pe4, w1, w2 = cmp_w
    full = lambda *s: pl.BlockSpec(s, lambda i, pt: (0,) * len(s))
    grid_spec = pltpu.PrefetchScalarGridSpec(
        num_scalar_prefetch=1,
        grid=(DB // seqs,),
        in_specs=[pl.BlockSpec(memory_space=pl.ANY), full(*pe4.shape), full(*w1.shape), full(*w2.shape)],
        out_specs=pl.BlockSpec((seqs * nh, w), lambda i, pt: (i, 0)),
        scratch_shapes=[pltpu.VMEM((2, 2, seqs * n_pages, KV_WIDTH, page), F32),
                        pltpu.VMEM((seqs * n_pages * page, KV_WIDTH), F32),
                        pltpu.VMEM((seqs * n_pages * page, KV_WIDTH), F32),
                        pltpu.SemaphoreType.DMA((2,))],
    )
    return pl.pallas_call(
        functools.partial(_compress_sample_kernel, seqs=seqs, n_pages=n_pages, n_cmp=n_cmp),
        grid_spec=grid_spec,
        out_shape=jax.ShapeDtypeStruct((DB * nh, w), F32),
        compiler_params=_cparams(("arbitrary",)),
        name="compress_sample",
    )(page_table, pool, pe4, w1, w2)


def _block_diag2(a):
    z = jnp.zeros_like(a)
    return jnp.concatenate([jnp.concatenate([a, z], -1), jnp.concatenate([z, a], -1)], -2)


def _compress_weights(cmp_params):
    pe_k, w1_k, w2_k, pe_v, w1_v, w2_v = cmp_params
    pe = jnp.stack([jnp.concatenate([p, p], axis=1) for p in (pe_k, pe_v)])
    w1 = jnp.stack([_block_diag2(w.reshape(CMP_LEN, A_DIM, A_DIM)) for w in (w1_k, w1_v)]).astype(BF16)
    w2 = jnp.stack([_block_diag2(w) for w in (w2_k, w2_v)]).astype(BF16)
    return pe, w1, w2


def _compress_prompt(cmp_rows, cmp_w):
    B, L, w = cmp_rows.shape
    nh = L // CMP_STRIDE
    n_cmp = (L - CMP_LEN) // CMP_STRIDE + 1
    pe4, w1, w2 = cmp_w
    full = lambda *s: pl.BlockSpec(s, lambda b: (0,) * len(s))
    return pl.pallas_call(
        functools.partial(_compress_kernel, n_cmp=n_cmp),
        grid=(B,),
        in_specs=[pl.BlockSpec((1, L, KV_WIDTH), lambda b: (b, 0, 0)),
                  pl.BlockSpec((1, L, KV_WIDTH), lambda b: (b, 0, 1)),
                  full(*pe4.shape), full(*w1.shape), full(*w2.shape)],
        out_specs=pl.BlockSpec((1, nh, w), lambda b: (b, 0, 0)),
        out_shape=jax.ShapeDtypeStruct((B, nh, w), F32),
        compiler_params=_cparams(("parallel",)),
        name="compress_prompt",
    )(cmp_rows, cmp_rows, pe4, w1, w2)


_NEG = -1e30
_NEG_SEL = -1e9
_SEL_CHUNK = 512


def _dot_nt(a, b):
    return lax.dot_general(a, b, (((1,), (1,)), ((), ())), preferred_element_type=F32)


def _softmax_rows(s, valid):
    s = jnp.where(valid[None], s, _NEG)
    m = jnp.max(s, axis=-1, keepdims=True)
    e = jnp.where(valid[None], jnp.exp(s - m), 0.0)
    return e / jnp.maximum(jnp.sum(e, axis=-1, keepdims=True), jnp.finfo(jnp.float32).tiny)


def _heads_to_rows(q):
    tq = q.shape[0]
    lane = lax.broadcasted_iota(jnp.int32, (tq, LANES), 1)
    q = q * (A_DIM ** -0.5)
    rows = []
    for hd in range(A_HEADS):
        g = hd // A_GROUP
        tile = q[:, LANES * (hd // 2):LANES * (hd // 2 + 1)]
        if hd % 2 != g:
            tile = pltpu.roll(tile, A_DIM, 1)
        keep = (lane < A_DIM) if g == 0 else (lane >= A_DIM)
        rows.append(jnp.where(keep, tile, 0.0))
    return jnp.concatenate(rows, axis=0).astype(BF16)


def _gate_rows_to_heads(gt, o_c, o_s, o_w):
    tq = gt.shape[0]
    lane = lax.broadcasted_iota(jnp.int32, (tq, LANES), 1)
    gs = jax.nn.sigmoid(gt)
    tiles = []
    for pair in range(A_HEADS // 2):
        g = (2 * pair) // A_GROUP
        mixed = []
        for hd in (2 * pair, 2 * pair + 1):
            c = 2 * M_HEADS + 3 * hd
            r = slice(hd * tq, (hd + 1) * tq)
            mixed.append(gs[:, c:c + 1] * o_c[r] + gs[:, c + 1:c + 2] * o_s[r] + gs[:, c + 2:c + 3] * o_w[r])
        a, b = mixed
        if g == 0:
            tiles.append(jnp.where(lane < A_DIM, a, pltpu.roll(b, A_DIM, 1)))
        else:
            tiles.append(jnp.where(lane < A_DIM, pltpu.roll(a, A_DIM, 1), b))
    return jnp.concatenate(tiles, axis=1)


def _attend_two(s_a, ok_a, v_a, s_b, ok_b, v_b, v_a_transposed=False):
    nh, tq = s_a.shape[:2]
    if ok_a is not None:
        s_a = jnp.where(ok_a[None], s_a, _NEG)
    s_b = jnp.where(ok_b[None], s_b, _NEG)
    m = jnp.maximum(jnp.max(s_a, axis=-1, keepdims=True), jnp.max(s_b, axis=-1, keepdims=True))
    e_a = jnp.exp(s_a - m)
    e_b = jnp.exp(s_b - m)
    l = jnp.sum(e_a, axis=-1, keepdims=True) + jnp.sum(e_b, axis=-1, keepdims=True)
    e_a = e_a.reshape(nh * tq, -1).astype(BF16)
    o_a = _dot_nt(e_a, v_a) if v_a_transposed else jnp.dot(e_a, v_a, preferred_element_type=F32)
    o = o_a + jnp.dot(e_b.reshape(nh * tq, -1).astype(BF16), v_b, preferred_element_type=F32)
    return o / l.reshape(nh * tq, 1)


def _nsa_sample_kernel(pt_ref, q_ref, gt_ref, kc_ref, vc_ref, seln_ref, wst_ref, wnew_ref, oh_ref, pool_ref,
                       o_ref, nwin_ref, selbuf, sem, *, past):
    b = pl.program_id(0)
    slot = b % 2
    tq = q_ref.shape[0]
    nh = A_HEADS
    n_pages = pt_ref.shape[1]
    page = pool_ref.shape[2]
    wbuf = wst_ref.shape[1]
    n_sel = -(-(past + tq) // SEL_LEN)

    def copies(seq, sl):
        return [pltpu.make_async_copy(pool_ref.at[pt_ref[seq, p]], selbuf.at[sl, :, pl.ds(p * page, page)],
                                      sem.at[sl]) for p in range(n_pages)]

    @pl.when(b == 0)
    def _():
        for cp in copies(0, 0):
            cp.start()

    @pl.when(b + 1 < pl.num_programs(0))
    def _():
        for cp in copies(b + 1, 1 - slot):
            cp.start()

    qz = _heads_to_rows(q_ref[...])
    tpos = past + lax.broadcasted_iota(jnp.int32, (tq, LANES), 0)

    kc = kc_ref[...].astype(BF16)
    vc = vc_ref[...].astype(BF16)
    ncp = kc.shape[0]
    s_c = _dot_nt(qz, kc).reshape(nh, tq, ncp)
    tp_c = past + lax.broadcasted_iota(jnp.int32, (tq, ncp), 0)
    nidx = lax.broadcasted_iota(jnp.int32, (tq, ncp), 1)
    p_c = _softmax_rows(s_c, nidx * CMP_STRIDE + (CMP_LEN - 1) <= tp_c)
    o_c = jnp.dot(p_c.reshape(nh * tq, ncp).astype(BF16), vc, preferred_element_type=F32)

    cn = lax.broadcasted_iota(jnp.int32, (ncp, LANES), 0) * CMP_STRIDE
    jn = lax.broadcasted_iota(jnp.int32, (ncp, LANES), 1) * SEL_LEN
    ov = jnp.where((cn < jn + SEL_LEN) & (cn + CMP_LEN > jn), 1.0, 0.0).astype(BF16)
    jb = lax.broadcasted_iota(jnp.int32, (tq, LANES), 1)
    cur = tpos // SEL_LEN
    forced = (jb == 0) | (jb == cur) | (jb == cur - 1)
    bias = []
    for g in range(A_KV_HEADS):
        ps = p_c[g * A_GROUP]
        for r in range(1, A_GROUP):
            ps = ps + p_c[g * A_GROUP + r]
        hi = ps.astype(BF16)
        lo = (ps - hi.astype(F32)).astype(BF16)
        score = (jnp.dot(hi, ov, preferred_element_type=F32) + jnp.dot(lo, ov, preferred_element_type=F32))
        score = jnp.where(forced, FORCE_SCORE, score)
        score = jnp.where(jb * SEL_LEN <= tpos, score, -1.0)
        cnt = jnp.zeros((tq, LANES), jnp.int32)
        for k in range(n_sel):
            sk = score[:, k:k + 1]
            ahead = (sk > score) | ((sk == score) & (jb > k))
            cnt = cnt + jnp.where(ahead, 1, 0)
        bias.append(jnp.where(cnt < min(SEL_TOP, n_sel), 0.0, _NEG_SEL).astype(BF16))
    q_aug = jnp.concatenate(
        [qz, jnp.concatenate([bias[hd // A_GROUP] for hd in range(nh)], axis=0)], axis=1)

    tw = lax.broadcasted_iota(jnp.int32, (tq, wbuf), 0)
    iw = lax.broadcasted_iota(jnp.int32, (tq, wbuf), 1)
    tn = lax.broadcasted_iota(jnp.int32, (tq, tq), 0)
    un = lax.broadcasted_iota(jnp.int32, (tq, tq), 1)
    wst = wst_ref[0]
    wnew = wnew_ref[...]
    s_wa = _dot_nt(qz, wst[:, :KV_WIDTH].astype(BF16)).reshape(nh, tq, wbuf)
    s_wb = _dot_nt(qz, wnew[:, :KV_WIDTH].astype(BF16)).reshape(nh, tq, tq)
    o_w = _attend_two(s_wa, (wbuf + tw - iw < WINDOW), wst[:, KV_WIDTH:].astype(BF16),
                      s_wb, un <= tn, wnew[:, KV_WIDTH:].astype(BF16))
    nwin_ref[0, :wbuf - tq, :] = wst[tq:, :]
    nwin_ref[0, wbuf - tq:, :] = wnew

    for cp in copies(b, slot):
        cp.wait()
    seln = seln_ref[...]
    k_aug_t = jnp.concatenate([selbuf[slot, :KV_WIDTH, :].astype(BF16), oh_ref[...]], axis=0)
    nblk = (past + lax.broadcasted_iota(jnp.int32, (tq, LANES), 0)) // SEL_LEN
    oh_new = jnp.where(nblk == lax.broadcasted_iota(jnp.int32, (tq, LANES), 1), 1.0, 0.0)
    kn_aug = jnp.concatenate([seln[:, :KV_WIDTH], oh_new], axis=1).astype(BF16)
    s_sa = jnp.dot(q_aug, k_aug_t, preferred_element_type=F32).reshape(nh, tq, past)
    s_sb = _dot_nt(q_aug, kn_aug).reshape(nh, tq, tq)
    o_s = _attend_two(s_sa, None, selbuf[slot, KV_WIDTH:, :].astype(BF16), s_sb, un <= tn,
                      seln[:, KV_WIDTH:].astype(BF16), v_a_transposed=True)

    o_ref[...] = _gate_rows_to_heads(gt_ref[...], o_c, o_s, o_w)


def _nsa_sample_call(qa, gt, comp, sel_new, win_state, win_new, sel_pool, page_table, T):
    DB, n_pages = page_table.shape
    page = sel_pool.shape[2]
    past = n_pages * page
    wbuf = win_state.shape[1]
    ncp = comp.shape[0] // DB
    w2 = 2 * KV_WIDTH
    assert past % SEL_LEN == 0 and T < CMP_STRIDE and T % 8 == 0 and wbuf == WINDOW and past >= WINDOW
    assert page % LANES == 0
    onehot = (jnp.arange(LANES)[:, None] == jnp.arange(past)[None, :] // SEL_LEN).astype(BF16)
    tok = lambda w: pl.BlockSpec((T, w), lambda b, pt: (b, 0))
    grid_spec = pltpu.PrefetchScalarGridSpec(
        num_scalar_prefetch=1,
        grid=(DB,),
        in_specs=[tok(A_WIDTH), tok(LANES),
                  pl.BlockSpec((ncp, KV_WIDTH), lambda b, pt: (b, 0)),
                  pl.BlockSpec((ncp, KV_WIDTH), lambda b, pt: (b, 1)),
                  tok(w2),
                  pl.BlockSpec((1, wbuf, w2), lambda b, pt: (b, 0, 0)),
                  tok(w2),
                  pl.BlockSpec((LANES, past), lambda b, pt: (0, 0)),
                  pl.BlockSpec(memory_space=pl.ANY)],
        out_specs=[tok(A_WIDTH), pl.BlockSpec((1, wbuf, w2), lambda b, pt: (b, 0, 0))],
        scratch_shapes=[pltpu.VMEM((2, w2, past), F32), pltpu.SemaphoreType.DMA((2,))],
    )
    return pl.pallas_call(
        functools.partial(_nsa_sample_kernel, past=past),
        grid_spec=grid_spec,
        out_shape=[jax.ShapeDtypeStruct((DB * T, A_WIDTH), F32), jax.ShapeDtypeStruct((DB, wbuf, w2), F32)],
        compiler_params=_cparams(("arbitrary",)),
        name="nsa_sample",
    )(page_table, qa, gt, comp, comp, sel_new, win_state, win_new, onehot, sel_pool)


def _nsa_prompt_t_kernel(q_ref, gt_ref, kc_ref, vc_ref, selk_ref, selvt_ref, wink_ref, winvt_ref, o_ref, *, seq):
    tq = Q_BLOCK
    nh = A_HEADS
    rows = nh * tq
    n_sel = seq // SEL_LEN
    kt = LANES
    s0 = pl.program_id(1) * tq
    qz = _heads_to_rows(q_ref[0])
    htile = lambda x, hd: x[:, hd * tq:(hd + 1) * tq]

    def qpos(n):
        return s0 + (lax.broadcasted_iota(jnp.int32, (n, rows), 1) & (tq - 1))

    def with_ones(vt):
        return jnp.concatenate([vt, jnp.ones((8, vt.shape[1]), F32)], axis=0).astype(BF16)

    def normalise(acc):
        return acc[:LANES] * (1.0 / acc[LANES:LANES + 1])

    kc = kc_ref[0].astype(BF16)
    ncp = kc.shape[0]
    s_c = _dot_nt(kc, qz)
    ok_c = lax.broadcasted_iota(jnp.int32, (ncp, rows), 0) * CMP_STRIDE + (CMP_LEN - 1) <= qpos(ncp)
    s_c = jnp.where(ok_c, s_c, _NEG)
    e_c = jnp.where(ok_c, jnp.exp(s_c - jnp.max(s_c, axis=0, keepdims=True)), 0.0)
    p_c = e_c * (1.0 / jnp.maximum(jnp.sum(e_c, axis=0, keepdims=True), jnp.finfo(jnp.float32).tiny))
    o_c = jnp.dot(vc_ref[0].T.astype(BF16), p_c.astype(BF16), preferred_element_type=F32)

    jn = lax.broadcasted_iota(jnp.int32, (n_sel, ncp), 0) * SEL_LEN
    cn = lax.broadcasted_iota(jnp.int32, (n_sel, ncp), 1) * CMP_STRIDE
    ov_t = jnp.where((cn < jn + SEL_LEN) & (cn + CMP_LEN > jn), 1.0, 0.0).astype(BF16)
    jb = lax.broadcasted_iota(jnp.int32, (n_sel, tq), 0)
    tp = s0 + lax.broadcasted_iota(jnp.int32, (n_sel, tq), 1)
    cur = tp // SEL_LEN
    forced = (jb == 0) | (jb == cur) | (jb == cur - 1)
    bias = []
    for g in range(A_KV_HEADS):
        mass = htile(p_c, g * A_GROUP)
        for r in range(1, A_GROUP):
            mass = mass + htile(p_c, g * A_GROUP + r)
        hi = mass.astype(BF16)
        lo = (mass - hi.astype(F32)).astype(BF16)
        score = (jnp.dot(ov_t, hi, preferred_element_type=F32) + jnp.dot(ov_t, lo, preferred_element_type=F32))
        score = jnp.where(forced, FORCE_SCORE, score)
        score = jnp.where(jb * SEL_LEN <= tp, score, -1.0)
        cnt = jnp.zeros((n_sel, tq), jnp.int32)
        for k in range(n_sel):
            rk = score[k:k + 1, :]
            ahead = (rk > score) | ((rk == score) & (jb > k))
            cnt = cnt + jnp.where(ahead, 1, 0)
        bias_t = jnp.where(cnt < min(SEL_TOP, n_sel), 0.0, _NEG_SEL)
        if n_sel < LANES:
            bias_t = jnp.concatenate([bias_t, jnp.zeros((LANES - n_sel, tq), F32)], axis=0)
        bias.append(bias_t.T.astype(BF16))
    q_aug = jnp.concatenate(
        [qz, jnp.concatenate([bias[hd // A_GROUP] for hd in range(nh)], axis=0)], axis=1)

    wk = WINDOW + tq
    w0 = pl.multiple_of(jnp.clip(s0 - WINDOW, 0, seq - wk), tq)
    s_w = _dot_nt(wink_ref[0, pl.ds(w0, wk), :].astype(BF16), qz)
    dist = qpos(wk) - (w0 + lax.broadcasted_iota(jnp.int32, (wk, rows), 0))
    s_w = jnp.where((dist >= 0) & (dist < WINDOW), s_w, _NEG)
    e_w = jnp.exp(s_w - jnp.max(s_w, axis=0, keepdims=True)).astype(BF16)
    vw_t = jnp.concatenate([winvt_ref[0, w0 // kt + i] for i in range(wk // kt)], axis=1)
    o_w = normalise(jnp.dot(with_ones(vw_t), e_w, preferred_element_type=F32))

    kc_n = _SEL_CHUNK

    def chunk(c, carry, causal):
        m, acc = carry
        k0 = pl.multiple_of(c * kc_n, kc_n)
        kblk = (k0 + lax.broadcasted_iota(jnp.int32, (kc_n, LANES), 0)) // SEL_LEN
        onehot = jnp.where(kblk == lax.broadcasted_iota(jnp.int32, (kc_n, LANES), 1), 1.0, 0.0)
        k_aug = jnp.concatenate([selk_ref[0, pl.ds(k0, kc_n), :], onehot], axis=1).astype(BF16)
        s = _dot_nt(k_aug, q_aug)
        if causal:
            s = jnp.where(k0 + lax.broadcasted_iota(jnp.int32, (kc_n, rows), 0) <= qpos(kc_n), s, _NEG_SEL)
        m_new = jnp.maximum(m, jnp.max(s, axis=0, keepdims=True))
        p = jnp.exp(s - m_new).astype(BF16)
        v_t = jnp.concatenate([selvt_ref[0, c * (kc_n // kt) + i] for i in range(kc_n // kt)], axis=1)
        acc = jnp.exp(m - m_new) * acc + jnp.dot(with_ones(v_t), p, preferred_element_type=F32)
        return m_new, acc

    n_ch = (s0 + tq - 1) // kc_n + 1
    init = (jnp.full((1, rows), _NEG, F32), jnp.zeros((LANES + 8, rows), F32))
    carry = lax.fori_loop(0, n_ch - 1, lambda c, cr: chunk(c, cr, False), init)
    o_s = normalise(chunk(n_ch - 1, carry, True)[1])

    gs_t = jax.nn.sigmoid(gt_ref[0]).T
    mixed = []
    for hd in range(nh):
        c = 2 * M_HEADS + 3 * hd
        g = hd // A_GROUP
        mix = (gs_t[c:c + 1, :] * htile(o_c, hd) + gs_t[c + 1:c + 2, :] * htile(o_s, hd)
               + gs_t[c + 2:c + 3, :] * htile(o_w, hd))
        mixed.append(mix[g * A_DIM:(g + 1) * A_DIM, :])
    o_ref[0] = jnp.concatenate(
        [jnp.concatenate([mixed[2 * pr], mixed[2 * pr + 1]], axis=0).T for pr in range(nh // 2)], axis=1)


def _nsa_prompt_t_call(qa, gt, comp, sel_k, sel_vt, win_k, win_vt):
    B, S, _ = qa.shape
    assert S % _SEL_CHUNK == 0 and S >= WINDOW + Q_BLOCK and S // SEL_LEN <= LANES
    ncp = comp.shape[1]
    keys = pl.BlockSpec((1, S, KV_WIDTH), lambda b, i: (b, 0, 0))
    vals = pl.BlockSpec((1, S // LANES, KV_WIDTH, LANES), lambda b, i: (b, 0, 0, 0))
    return pl.pallas_call(
        functools.partial(_nsa_prompt_t_kernel, seq=S),
        grid=(B, S // Q_BLOCK),
        in_specs=[pl.BlockSpec((1, Q_BLOCK, A_WIDTH), lambda b, i: (b, i, 0)),
                  pl.BlockSpec((1, Q_BLOCK, LANES), lambda b, i: (b, i, 0)),
                  pl.BlockSpec((1, ncp, KV_WIDTH), lambda b, i: (b, 0, 0)),
                  pl.BlockSpec((1, ncp, KV_WIDTH), lambda b, i: (b, 0, 1)),
                  keys, vals, keys, vals],
        out_specs=pl.BlockSpec((1, Q_BLOCK, A_WIDTH), lambda b, i: (b, i, 0)),
        out_shape=jax.ShapeDtypeStruct((B, S, A_WIDTH), F32),
        compiler_params=_cparams(("parallel", "arbitrary")),
        name="nsa_prompt",
    )(qa, gt, comp, comp, sel_k, sel_vt, win_k, win_vt)


def _moe(h2, logits, moe_w, n_experts):
    w_gu, b_gu, w_down, b_down = moe_w
    n, d = h2.shape
    gate_w, dest, src, blk_e, n_used = _route(logits, n_experts)
    yb = _ffn(h2[src], blk_e, n_used, w_gu, b_gu, w_down, b_down)
    return yb[dest.T], gate_w


def kernel(x_prompt, x_sample, cache_cmp, cache_sel, state_win, state_C, state_n, state_m, page_table,
           c_prompt, c_sample, w_ada, b_ada, w_in, b_in, m_norm_g, cmp_pe_k, cmp_w1_k, cmp_w2_k,
           cmp_pe_v, cmp_w1_v, cmp_w2_v, w_out, ln1_g, ln1_b, w_router, b_router, w_gu, b_gu,
           w_down, b_down, ln2_g, ln2_b):
    B, S, d = x_prompt.shape
    DB, T, _ = x_sample.shape
    depth = w_ada.shape[0]
    n_experts = w_router.shape[-1]
    alpha = (2 * depth) ** 0.25
    n_pool, page = cache_cmp.shape[1:3]
    past_len = page_table.shape[1] * page
    wbuf = state_win.shape[2]
    pos_p = jnp.arange(S)
    pos_s = past_len + jnp.arange(T)
    tt_p = min(S, 256)
    bb_s = min(DB, max(1, 256 // T))
    n_cmp_s = (past_len + T - CMP_LEN) // CMP_STRIDE + 1
    assert n_cmp_s < past_len // CMP_STRIDE
    r3 = lambda a: a.reshape(B, S, a.shape[-1])
    kv5 = lambda a, n, t: a.reshape(n, t, 2, A_KV_HEADS, A_DIM)
    y_prompt, y_sample = x_prompt, x_sample
    outs = [[] for _ in range(12)]
    for l in range(depth):
        cmp_l = (cmp_pe_k[l], cmp_w1_k[l], cmp_w2_k[l], cmp_pe_v[l], cmp_w1_v[l], cmp_w2_v[l])
        moe_w = (w_gu[l], b_gu[l], w_down[l], b_down[l])
        post1_w = (w_out[l], ln1_g[l], ln1_b[l], w_router[l], b_router[l])
        w_r, b_r = _prep_w_in(w_in[l], b_in[l])
        cmp_w = _compress_weights(cmp_l)
        c_all = jnp.concatenate([c_prompt, c_sample], axis=0)
        c_all = jnp.pad(c_all, ((0, -c_all.shape[0] % 8), (0, 0)))
        mod_all = _ada(c_all, w_ada[l], b_ada[l]).reshape(-1, 6, d)
        mod_p, mod_s = mod_all[:B], mod_all[B:B + DB]

        (mq, gt, qa, cmp_p, sel_p, win_p, cmp_t, sel_t, win_t, sel_vt, win_vt) = _pre(
            y_prompt, mod_p, pos_p, w_r, b_r, 1, tt_p, feature_major=True)
        zc = jnp.zeros((B, M_HEADS, M_DIM, M_DIM), F32)
        mo, C_p, n_p, m_p = _mlstm_call(r3(mq), r3(gt), zc, zc[..., 0], zc[..., 0, 0], m_norm_g[l],
                                        math.gcd(B, 4))
        comp = _compress_prompt(r3(cmp_p), cmp_w)
        ma = _nsa_prompt_t_call(r3(qa), r3(gt), comp, r3(sel_p), sel_vt, r3(win_p), win_vt)
        x1_p, h2_p, lg_p = _post1(mo.reshape(B * S, M_WIDTH), ma.reshape(B * S, A_WIDTH), y_prompt, mod_p,
                                  *post1_w, 1, tt_p, alpha)

        mq, gt, qa, cmp_s, sel_s, win_s = _pre(y_sample, mod_s, pos_s, w_r, b_r, bb_s, T)
        mo, C_s, n_s, m_s = _mlstm_call(mq.reshape(DB, T, -1), gt.reshape(DB, T, -1), state_C[l],
                                        state_n[l], state_m[l], m_norm_g[l], math.gcd(DB, 4))
        feature_major = lambda pool: pool.transpose(0, 2, 3, 4, 1).reshape(n_pool, 2 * KV_WIDTH, page)
        comp = _compress_sample(feature_major(cache_cmp[l]), page_table, cmp_w, n_cmp_s, math.gcd(DB, 4))
        ma, new_win = _nsa_sample_call(qa, gt, comp, sel_s, state_win[l].reshape(DB, wbuf, 2 * KV_WIDTH),
                                       win_s, feature_major(cache_sel[l]), page_table, T)
        x1_s, h2_s, lg_s = _post1(mo.reshape(DB * T, M_WIDTH), ma, y_sample, mod_s, *post1_w, bb_s, T, alpha)

        yg, gate_w = _moe(jnp.concatenate([h2_p, h2_s], axis=0), jnp.concatenate([lg_p, lg_s], axis=0),
                          moe_w, n_experts)
        y_prompt = _post2(yg, gate_w, 0, x1_p, mod_p, ln2_g[l], ln2_b[l], 1, tt_p, alpha)
        y_sample = _post2(yg, gate_w, B * S, x1_s, mod_s, ln2_g[l], ln2_b[l], bb_s, T, alpha)

        kv5_t = lambda a: a.reshape(B, 2, A_KV_HEADS, A_DIM, -1).transpose(0, 4, 1, 2, 3)
        new = (kv5_t(cmp_t), kv5_t(sel_t), kv5_t(win_t[:, :, -min(WINDOW, S):]), C_p, n_p, m_p,
               kv5(cmp_s, DB, T), kv5(sel_s, DB, T), new_win.reshape(state_win.shape[1:]), C_s, n_s, m_s)
        for lst, v in zip(outs, new):
            lst.append(v)
    return (y_prompt, y_sample, *[jnp.stack(v) for v in outs])
```

```python
import functools
import math

import numpy as np
import jax
import jax.numpy as jnp
from jax import lax
from jax.experimental import pallas as pl
from jax.experimental.pallas import tpu as pltpu

F32 = jnp.float32
BF16 = jnp.bfloat16

M_HEADS = 4
M_DIM = 128
M_WIDTH = M_HEADS * M_DIM
M_CHUNK = 64
A_HEADS = 8
A_KV_HEADS = 2
A_GROUP = A_HEADS // A_KV_HEADS
A_DIM = 64
A_WIDTH = A_HEADS * A_DIM
KV_WIDTH = A_KV_HEADS * A_DIM
CMP_LEN = 32
CMP_STRIDE = 16
SEL_LEN = 64
SEL_TOP = 16
WINDOW = 512
Q_BLOCK = 256
FORCE_SCORE = 1.0e4
ROPE_THETA = 500000.0
ROPE_DIM = A_DIM // 4
TOP_K = 4
SWIGLU_LIMIT = 7.0
SWIGLU_ALPHA = 1.702
LN_EPS = 1e-5
LANES = 128
MOE_ROWS = 512
VMEM_LIMIT = 56 * 1024 * 1024

_O_MQ = 0
_O_IF = 4 * M_WIDTH
_O_QA = _O_IF + 2 * M_HEADS
_O_KV = _O_QA + A_WIDTH
_O_GA = _O_KV + 6 * KV_WIDTH
_N_IN = _O_GA + 3 * A_HEADS
_R_QA = 4 * M_WIDTH
_R_KV = _R_QA + A_WIDTH
_R_GT = _R_KV + 6 * KV_WIDTH
_R_END = _R_GT + LANES


def _cparams(sem):
    return pltpu.CompilerParams(dimension_semantics=sem, vmem_limit_bytes=VMEM_LIMIT)


def _ln_core(x):
    mu = jnp.mean(x, axis=-1, keepdims=True)
    xc = x - mu
    var = jnp.mean(xc * xc, axis=-1, keepdims=True)
    return xc * lax.rsqrt(var + LN_EPS)


def _ada_kernel(c_ref, w_ref, b_ref, o_ref):
    o_ref[...] = jnp.dot(c_ref[...].astype(BF16), w_ref[...].astype(BF16),
                         preferred_element_type=F32) + b_ref[...]


def _ada(c, w_ada, b_ada):
    n, d = c.shape
    cols = w_ada.shape[1]
    return pl.pallas_call(
        _ada_kernel,
        grid=(cols // d,),
        in_specs=[pl.BlockSpec((n, d), lambda j: (0, 0)),
                  pl.BlockSpec((d, d), lambda j: (0, j)),
                  pl.BlockSpec((1, d), lambda j: (0, j))],
        out_specs=pl.BlockSpec((n, d), lambda j: (0, j)),
        out_shape=jax.ShapeDtypeStruct((n, cols), F32),
        compiler_params=_cparams(("arbitrary",)),
        name="ada",
    )(c, w_ada, b_ada.reshape(1, cols))


def _rope_apply(v, cos, sa, sb):
    reps = v.shape[1] // LANES
    tile = lambda t: t if reps == 1 else jnp.concatenate([t] * reps, axis=1)
    w = v.shape[1]
    return (v * tile(cos) + pltpu.roll(v, w - ROPE_DIM // 2, 1) * tile(sa)
            + pltpu.roll(v, ROPE_DIM // 2, 1) * tile(sb))


def _pre_kernel(x_ref, mod_ref, cos_ref, sa_ref, sb_ref, w_ref, b_ref,
                mq_ref, gt_ref, qa_ref, cmp_ref, sel_ref, win_ref, *t_refs):
    bb, tt, d = x_ref.shape
    mod = mod_ref[...]
    h = _ln_core(x_ref[...]) * (1.0 + mod[:, 1:2, :]) + mod[:, 0:1, :]
    h = h.reshape(bb * tt, d).astype(BF16)
    z = jnp.dot(h, w_ref[...], preferred_element_type=F32) + b_ref[...]
    cos, sa, sb = cos_ref[...], sa_ref[...], sb_ref[...]
    mq_ref[...] = z[:, :_R_QA]
    gt_ref[...] = z[:, _R_GT:_R_END]
    qa_ref[...] = _rope_apply(z[:, _R_QA:_R_KV], cos, sa, sb)
    for n, ref in enumerate((cmp_ref, sel_ref, win_ref)):
        o = _R_KV + 2 * KV_WIDTH * n
        kv = jnp.concatenate([_rope_apply(z[:, o:o + KV_WIDTH], cos, sa, sb),
                              z[:, o + KV_WIDTH:o + 2 * KV_WIDTH]], axis=1)
        ref[...] = kv
        if t_refs:
            kv_t = kv.T
            t_refs[n][0] = kv_t
            if n > 0:
                for c in range(kv_t.shape[1] // LANES):
                    t_refs[2 + n][0, c] = kv_t[KV_WIDTH:, c * LANES:(c + 1) * LANES]


def _rope_tables(pos):
    half = ROPE_DIM // 2
    inv = ROPE_THETA ** (-jnp.arange(0, ROPE_DIM, 2, dtype=F32) / ROPE_DIM)
    ang = pos.astype(F32)[:, None] * inv[None, :]
    cos, sin = jnp.cos(ang), jnp.sin(ang)
    n = pos.shape[0]
    one = jnp.ones((n, A_DIM - ROPE_DIM), F32)
    zero = jnp.zeros((n, A_DIM - ROPE_DIM), F32)
    zh = jnp.zeros((n, half), F32)
    cos_t = jnp.concatenate([cos, cos, one], axis=1)
    sa_t = jnp.concatenate([-sin, zh, zero], axis=1)
    sb_t = jnp.concatenate([zh, sin, zero], axis=1)
    two = lambda t: jnp.concatenate([t, t], axis=1)
    return two(cos_t), two(sa_t), two(sb_t)


def _prep_w_in(w_in, b_in):
    pad = LANES - 2 * M_HEADS - 3 * A_HEADS
    cat = lambda a: jnp.concatenate(
        [a[..., _O_MQ:_O_IF], a[..., _O_QA:_O_GA], a[..., _O_IF:_O_QA], a[..., _O_GA:_N_IN],
         jnp.zeros(a.shape[:-1] + (pad,), a.dtype)], axis=-1)
    return cat(w_in).astype(BF16), cat(b_in[None, :])


def _pre(x, mod, pos, w_r, b_r, bb, tt, feature_major=False):
    B, T, d = x.shape
    nt = T // tt
    rows = bb * tt
    cos, sa, sb = _rope_tables(pos)
    if bb > 1:
        cos, sa, sb = (jnp.tile(t, (bb, 1)) for t in (cos, sa, sb))
    n_tok = B * T
    tab = pl.BlockSpec((rows, LANES), lambda i, j: (j, 0))
    row = lambda w: pl.BlockSpec((rows, w), lambda i, j: (i * nt + j, 0))
    widths = (_R_QA, LANES, A_WIDTH, 2 * KV_WIDTH, 2 * KV_WIDTH, 2 * KV_WIDTH)
    out_specs = [row(w) for w in widths]
    out_shape = [jax.ShapeDtypeStruct((n_tok, w), F32) for w in widths]
    if feature_major:
        assert bb == 1 and tt % LANES == 0
        out_specs += [pl.BlockSpec((1, 2 * KV_WIDTH, tt), lambda i, j: (i, 0, j))] * 3
        out_shape += [jax.ShapeDtypeStruct((B, 2 * KV_WIDTH, T), F32)] * 3
        out_specs += [pl.BlockSpec((1, tt // LANES, KV_WIDTH, LANES), lambda i, j: (i, j, 0, 0))] * 2
        out_shape += [jax.ShapeDtypeStruct((B, T // LANES, KV_WIDTH, LANES), F32)] * 2
    return pl.pallas_call(
        _pre_kernel,
        grid=(B // bb, nt),
        in_specs=[pl.BlockSpec((bb, tt, d), lambda i, j: (i, j, 0)),
                  pl.BlockSpec((bb, 6, d), lambda i, j: (i, 0, 0)),
                  tab, tab, tab,
                  pl.BlockSpec((d, _R_END), lambda i, j: (0, 0)),
                  pl.BlockSpec((1, _R_END), lambda i, j: (0, 0))],
        out_specs=out_specs,
        out_shape=out_shape,
        compiler_params=_cparams(("parallel", "arbitrary")),
        name="pre",
    )(x, mod, cos, sa, sb, w_r, b_r)


def _post1_kernel(mo_ref, ao_ref, x_ref, mod_ref, wo_ref, g_ref, b_ref, wr_ref, br_ref,
                  x1_ref, h2_ref, lg_ref, *, alpha):
    bb, tt, d = x_ref.shape
    mod = mod_ref[...]
    mixin = jnp.concatenate([mo_ref[...], ao_ref[...]], axis=1).astype(BF16)
    mix = jnp.dot(mixin, wo_ref[...], preferred_element_type=F32).reshape(bb, tt, d)
    x1 = _ln_core(alpha * x_ref[...] + mod[:, 2:3, :] * mix) * g_ref[...] + b_ref[...]
    h2 = _ln_core(x1) * (1.0 + mod[:, 4:5, :]) + mod[:, 3:4, :]
    x1_ref[...] = x1
    h2f = h2.reshape(bb * tt, d)
    h2_ref[...] = h2f
    hi = h2f.astype(BF16)
    lo = (h2f - hi.astype(F32)).astype(BF16)
    lg_ref[...] = (jnp.dot(hi, wr_ref[0], preferred_element_type=F32)
                   + jnp.dot(hi, wr_ref[1], preferred_element_type=F32)
                   + jnp.dot(lo, wr_ref[0], preferred_element_type=F32)) + br_ref[...]


def _post1(mo, ao, x, mod, w_out, ln_g, ln_b, w_router, b_router, bb, tt, alpha):
    B, T, d = x.shape
    nt = T // tt
    rows = bb * tt
    ne = w_router.shape[1]
    wr = jnp.pad(w_router, ((0, 0), (0, LANES - ne)))
    wr_hi = wr.astype(BF16)
    wr = jnp.stack([wr_hi, (wr - wr_hi.astype(F32)).astype(BF16)])
    br = jnp.pad(b_router, (0, LANES - ne), constant_values=-jnp.inf).reshape(1, LANES)
    full = lambda *s: pl.BlockSpec(s, lambda i, j: (0,) * len(s))
    row = lambda w: pl.BlockSpec((rows, w), lambda i, j: (i * nt + j, 0))
    return pl.pallas_call(
        functools.partial(_post1_kernel, alpha=alpha),
        grid=(B // bb, nt),
        in_specs=[row(M_WIDTH), row(A_WIDTH),
                  pl.BlockSpec((bb, tt, d), lambda i, j: (i, j, 0)),
                  pl.BlockSpec((bb, 6, d), lambda i, j: (i, 0, 0)),
                  full(M_WIDTH + A_WIDTH, d), full(1, d), full(1, d), full(2, d, LANES), full(1, LANES)],
        out_specs=[pl.BlockSpec((bb, tt, d), lambda i, j: (i, j, 0)), row(d), row(LANES)],
        out_shape=[jax.ShapeDtypeStruct((B, T, d), F32),
                   jax.ShapeDtypeStruct((B * T, d), F32),
                   jax.ShapeDtypeStruct((B * T, LANES), F32)],
        compiler_params=_cparams(("parallel", "arbitrary")),
        name="post1",
    )(mo, ao, x, mod, w_out.astype(BF16), ln_g.reshape(1, d), ln_b.reshape(1, d), wr, br)


def _ffn_kernel(be_ref, nu_ref, x_ref, wgu_ref, bgu_ref, wd_ref, bd_ref, y_ref, wgu_bf, wd_bf):
    i = pl.program_id(0)
    dff = wd_ref.shape[1]

    @pl.when((i == 0) | (be_ref[i] != be_ref[jnp.maximum(i - 1, 0)]))
    def _():
        wgu_bf[...] = wgu_ref[0].astype(BF16)
        wd_bf[...] = wd_ref[0].astype(BF16)

    @pl.when(i < nu_ref[0])
    def _():
        half = x_ref.shape[0] // 2
        halves = [slice(h * half, (h + 1) * half) for h in range(2)]
        gus = [jnp.dot(x_ref[r, :].astype(BF16), wgu_bf[...], preferred_element_type=F32) + bgu_ref[0]
               for r in halves]
        acts = []
        for gu in gus:
            g = jnp.minimum(gu[:, :dff], SWIGLU_LIMIT)
            u = jnp.clip(gu[:, dff:], -SWIGLU_LIMIT, SWIGLU_LIMIT)
            acts.append(((u + 1.0) * (g * jax.nn.sigmoid(SWIGLU_ALPHA * g))).astype(BF16))
        for r, act in zip(halves, acts):
            y_ref[r, :] = jnp.dot(act, wd_bf[...], preferred_element_type=F32) + bd_ref[0]

    @pl.when(i >= nu_ref[0])
    def _():
        y_ref[...] = jnp.zeros_like(y_ref)


def _ffn(xb, blk_e, n_used, w_gu, b_gu, w_down, b_down):
    rows, d = xb.shape
    ne, _, f2 = w_gu.shape
    dff = w_down.shape[1]
    nb = rows // MOE_ROWS
    grid_spec = pltpu.PrefetchScalarGridSpec(
        num_scalar_prefetch=2,
        grid=(nb,),
        in_specs=[pl.BlockSpec((MOE_ROWS, d), lambda i, be, nu: (i, 0)),
                  pl.BlockSpec((1, d, f2), lambda i, be, nu: (be[i], 0, 0)),
                  pl.BlockSpec((1, 1, f2), lambda i, be, nu: (be[i], 0, 0)),
                  pl.BlockSpec((1, dff, d), lambda i, be, nu: (be[i], 0, 0)),
                  pl.BlockSpec((1, 1, d), lambda i, be, nu: (be[i], 0, 0))],
        out_specs=pl.BlockSpec((MOE_ROWS, d), lambda i, be, nu: (i, 0)),
        scratch_shapes=[pltpu.VMEM((d, f2), BF16), pltpu.VMEM((dff, d), BF16)],
    )
    return pl.pallas_call(
        _ffn_kernel,
        grid_spec=grid_spec,
        out_shape=jax.ShapeDtypeStruct((rows, d), F32),
        compiler_params=_cparams(("arbitrary",)),
        name="ffn",
    )(blk_e, n_used, xb, w_gu, b_gu.reshape(ne, 1, f2), w_down, b_down.reshape(ne, 1, d))


def _post2_kernel(yg_ref, gw_ref, x1_ref, mod_ref, g_ref, b_ref, y_ref, *, alpha):
    bb, tt, d = x1_ref.shape
    gw = gw_ref[...]
    f = yg_ref[0] * gw[:, 0:1]
    for k in range(1, TOP_K):
        f = f + yg_ref[k] * gw[:, k:k + 1]
    y = alpha * x1_ref[...] + mod_ref[...][:, 5:6, :] * f.reshape(bb, tt, d)
    y_ref[...] = _ln_core(y) * g_ref[...] + b_ref[...]


def _post2(yg, gw, row0, x1, mod, ln_g, ln_b, bb, tt, alpha):
    B, T, d = x1.shape
    rows = bb * tt
    nt = T // tt
    blk0 = row0 // rows
    assert row0 % rows == 0
    full = lambda *s: pl.BlockSpec(s, lambda i, j: (0,) * len(s))
    return pl.pallas_call(
        functools.partial(_post2_kernel, alpha=alpha),
        grid=(B // bb, nt),
        in_specs=[pl.BlockSpec((TOP_K, rows, d), lambda i, j: (0, blk0 + i * nt + j, 0)),
                  pl.BlockSpec((rows, LANES), lambda i, j: (blk0 + i * nt + j, 0)),
                  pl.BlockSpec((bb, tt, d), lambda i, j: (i, j, 0)),
                  pl.BlockSpec((bb, 6, d), lambda i, j: (i, 0, 0)),
                  full(1, d), full(1, d)],
        out_specs=pl.BlockSpec((bb, tt, d), lambda i, j: (i, j, 0)),
        out_shape=jax.ShapeDtypeStruct((B, T, d), F32),
        compiler_params=_cparams(("parallel", "arbitrary")),
        name="post2",
    )(yg, gw, x1, mod, ln_g.reshape(1, d), ln_b.reshape(1, d))


def _lane_prefix_sum(x):
    lane = lax.broadcasted_iota(jnp.int32, x.shape, 1)
    s = 1
    while s < LANES:
        x = x + jnp.where(lane >= s, pltpu.roll(x, s, 1), 0.0)
        s *= 2
    return x


def _route_kernel(lg_ref, dest_ref, gw_ref, cnt_ref, counts, running):
    phase = pl.program_id(0)
    blk = pl.program_id(1)
    r = lg_ref.shape[0]
    lane = lax.broadcasted_iota(jnp.int32, (r, LANES), 1)

    @pl.when((phase == 0) & (blk == 0))
    def _():
        counts[...] = jnp.zeros_like(counts)

    @pl.when((phase == 1) & (blk == 0))
    def _():
        running[...] = jnp.zeros_like(running)

    vals = lg_ref[...]
    onehots, tops = [], []
    for k in range(TOP_K):
        m = jnp.max(vals, axis=1, keepdims=True)
        idx = jnp.min(jnp.where(vals == m, lane, LANES), axis=1, keepdims=True)
        hit = lane == idx
        onehots.append(hit)
        tops.append(m)
        vals = jnp.where(hit, -jnp.inf, vals)
    ohf = [jnp.where(h, 1.0, 0.0) for h in onehots]
    block_cnt = [jnp.sum(o, axis=0, keepdims=True) for o in ohf]

    @pl.when(phase == 0)
    def _():
        counts[...] += block_cnt[0] + block_cnt[1] + block_cnt[2] + block_cnt[3]

    @pl.when(phase == 1)
    def _():
        cnt = counts[...]
        padded = jnp.ceil(cnt * (1.0 / MOE_ROWS)) * MOE_ROWS
        pad_start = _lane_prefix_sum(padded) - padded
        es = [jnp.exp(t - tops[0]) for t in tops]
        den = es[0] + es[1] + es[2] + es[3]
        ti = lax.broadcasted_iota(jnp.int32, (r, r), 0)
        tj = lax.broadcasted_iota(jnp.int32, (r, r), 1)
        before = jnp.where(tj < ti, 1.0, 0.0).astype(BF16)
        base = pad_start + running[...]
        dest = jnp.zeros((r, LANES), F32)
        gw = jnp.zeros((r, LANES), F32)
        for k in range(TOP_K):
            rank = jnp.dot(before, ohf[k].astype(BF16), preferred_element_type=F32)
            d_k = jnp.sum(ohf[k] * (base + rank), axis=1, keepdims=True)
            dest = jnp.where(lane == k, d_k, dest)
            gw = jnp.where(lane == k, es[k] / den, gw)
            base = base + block_cnt[k]
        running[...] = base - pad_start
        dest_ref[...] = dest.astype(jnp.int32)
        gw_ref[...] = gw
        cnt_ref[...] = jnp.broadcast_to(cnt, cnt_ref.shape)


def _route(logits, n_experts):
    n = logits.shape[0]
    rows = 512 if n % 512 == 0 else 256
    assert n % rows == 0
    nb = n // rows
    tok = pl.BlockSpec((rows, LANES), lambda p, i: (i, 0))
    out = pl.BlockSpec((rows, LANES), lambda p, i: (i * p, 0))
    dest, gw, cnt = pl.pallas_call(
        _route_kernel,
        grid=(2, nb),
        in_specs=[tok],
        out_specs=[out, out, pl.BlockSpec((8, LANES), lambda p, i: (0, 0))],
        out_shape=[jax.ShapeDtypeStruct((n, LANES), jnp.int32), jax.ShapeDtypeStruct((n, LANES), F32),
                   jax.ShapeDtypeStruct((8, LANES), F32)],
        scratch_shapes=[pltpu.VMEM((1, LANES), F32), pltpu.VMEM((1, LANES), F32)],
        compiler_params=_cparams(("arbitrary", "arbitrary")),
        name="route",
    )(logits)
    counts = cnt[0, :n_experts].astype(jnp.int32)
    pad_end = jnp.cumsum((counts + MOE_ROWS - 1) // MOE_ROWS * MOE_ROWS)
    n_slots = n * TOP_K
    n_blocks = -(-n_slots // MOE_ROWS) + n_experts
    n_used = pad_end[-1] // MOE_ROWS
    blk = jnp.minimum(jnp.arange(n_blocks, dtype=jnp.int32), n_used - 1) * MOE_ROWS
    blk_e = jnp.minimum(jnp.sum(pad_end[None, :] <= blk[:, None], axis=1), n_experts - 1).astype(jnp.int32)
    dest4 = dest[:, :TOP_K]
    src = jnp.zeros((n_blocks * MOE_ROWS,), jnp.int32).at[dest4.reshape(-1)].set(
        jnp.arange(n_slots, dtype=jnp.int32) // TOP_K, unique_indices=True)
    return gw, dest4, src, blk_e, n_used.reshape(1).astype(jnp.int32)


def _log_sigmoid(x):
    return jnp.minimum(x, 0.0) - jnp.log(1.0 + jnp.exp(-jnp.abs(x)))


def _mlstm_kernel(mq_ref, gt_ref, gtt_ref, c0_ref, n0_ref, m0_ref, g_ref, mo_ref, c_ref, n_ref, m_ref):
    bb, L, _ = mq_ref.shape
    d = M_DIM

    @pl.when(pl.program_id(1) == 0)
    def _():
        c_ref[...] = c0_ref[...]
        n_ref[...] = n0_ref[...]
        m_ref[...] = m0_ref[...]

    tt = lax.broadcasted_iota(jnp.int32, (L, L), 0)
    ss = lax.broadcasted_iota(jnp.int32, (L, L), 1)
    causal = ss <= tt
    pairs = [(s, h) for s in range(bb) for h in range(M_HEADS)]
    col = lambda s, h, off: mq_ref[s, :, off + h * d:off + (h + 1) * d]
    old = {p: (c_ref[p[0], p[1]], n_ref[p[0], p[1]:p[1] + 1, :], m_ref[p[0], :, p[1]:p[1] + 1]) for p in pairs}

    qb = {p: col(*p, 0).astype(BF16) for p in pairs}
    kb = {p: (col(*p, M_WIDTH) * (d ** -0.5)).astype(BF16) for p in pairs}
    qk = {p: _dot_nt(qb[p], kb[p]) for p in pairs}
    qc = {p: _dot_nt(qb[p], old[p][0].astype(BF16)) for p in pairs}

    i_c = {(s, h): gt_ref[s][:, h:h + 1] for s, h in pairs}
    i_r = {(s, h): gtt_ref[s, 0][h:h + 1, :] for s, h in pairs}
    lf_c = {(s, h): _log_sigmoid(gt_ref[s][:, M_HEADS + h:M_HEADS + h + 1]) for s, h in pairs}
    lf_r = {(s, h): _log_sigmoid(gtt_ref[s, 0][M_HEADS + h:M_HEADS + h + 1, :]) for s, h in pairs}
    b_c = {p: jnp.sum(jnp.where(causal, lf_r[p], 0.0), axis=1, keepdims=True) for p in pairs}
    b_r = {p: jnp.sum(jnp.where(causal, 0.0, lf_c[p]), axis=0, keepdims=True) + lf_r[p] for p in pairs}

    dmat = {p: jnp.where(causal, b_c[p] - b_r[p] + i_r[p], _NEG) for p in pairs}
    dmax = {p: jnp.max(dmat[p], axis=1, keepdims=True) for p in pairs}
    m_t = {p: jnp.maximum(b_c[p] + old[p][2], dmax[p]) for p in pairs}
    inter = {p: jnp.exp(b_c[p] + old[p][2] - m_t[p]) for p in pairs}
    a = {p: jnp.exp(dmat[p] - m_t[p]) * qk[p] for p in pairs}
    av = {p: jnp.dot(a[p].astype(BF16), col(*p, 2 * M_WIDTH).astype(BF16), preferred_element_type=F32)
          for p in pairs}

    qn = {p: jnp.sum(col(*p, 0) * old[p][1], axis=1, keepdims=True) for p in pairs}
    asum = {p: jnp.sum(a[p], axis=1, keepdims=True) for p in pairs}
    hv = {p: (inter[p] * qc[p] + av[p]) / jnp.maximum(jnp.abs(inter[p] * qn[p] + asum[p]), jnp.exp(-m_t[p]))
          for p in pairs}
    mu = {p: jnp.mean(hv[p], axis=1, keepdims=True) for p in pairs}
    hc = {p: hv[p] - mu[p] for p in pairs}
    var = {p: jnp.mean(hc[p] * hc[p], axis=1, keepdims=True) for p in pairs}
    out = {(s, h): (hc[s, h] * lax.rsqrt(var[s, h] + LN_EPS) * g_ref[:, h * d:(h + 1) * d])
           * jax.nn.sigmoid(col(s, h, 3 * M_WIDTH)) for s, h in pairs}

    new = {}
    for p in pairs:
        c_old, n_old, m_old = old[p]
        k = col(*p, M_WIDTH) * (d ** -0.5)
        m_new = m_t[p][L - 1:L, :]
        b_last = b_c[p][L - 1:L, :]
        w_src = jnp.exp(b_last - b_c[p] + i_c[p] - m_new)
        w_old = jnp.exp(b_last + m_old - m_new)
        c_new = w_old * c_old + lax.dot_general(
            (w_src * col(*p, 2 * M_WIDTH)).astype(BF16), kb[p], (((0,), (0,)), ((), ())),
            preferred_element_type=F32)
        n_new = w_old * n_old + jnp.sum(w_src * k, axis=0, keepdims=True)
        new[p] = (c_new, n_new, m_new, out[p])
    for (s, h), (c_new, n_new, m_new, out) in new.items():
        c_ref[s, h] = c_new
        n_ref[s, h:h + 1, :] = n_new
        m_ref[s, :, h:h + 1] = m_new
        mo_ref[s, :, h * d:(h + 1) * d] = out


def _mlstm_call(mq, gt, C0, n0, m0, g, bb):
    B, T, _ = mq.shape
    L = math.gcd(T, M_CHUNK)
    nc = T // L
    gtt = gt[:, :, :2 * M_HEADS].reshape(B, nc, L, 2 * M_HEADS).transpose(0, 1, 3, 2)
    st4 = pl.BlockSpec((bb, M_HEADS, M_DIM, M_DIM), lambda i, c: (i, 0, 0, 0))
    st3 = pl.BlockSpec((bb, M_HEADS, M_DIM), lambda i, c: (i, 0, 0))
    st2 = pl.BlockSpec((bb, 1, M_HEADS), lambda i, c: (i, 0, 0))
    mo, C, n, m = pl.pallas_call(
        _mlstm_kernel,
        grid=(B // bb, nc),
        in_specs=[pl.BlockSpec((bb, L, 4 * M_WIDTH), lambda i, c: (i, c, 0)),
                  pl.BlockSpec((bb, L, LANES), lambda i, c: (i, c, 0)),
                  pl.BlockSpec((bb, 1, 2 * M_HEADS, L), lambda i, c: (i, c, 0, 0)),
                  st4, st3, st2,
                  pl.BlockSpec((1, M_WIDTH), lambda i, c: (0, 0))],
        out_specs=[pl.BlockSpec((bb, L, M_WIDTH), lambda i, c: (i, c, 0)), st4, st3, st2],
        out_shape=[jax.ShapeDtypeStruct((B, T, M_WIDTH), F32),
                   jax.ShapeDtypeStruct(C0.shape, F32), jax.ShapeDtypeStruct(n0.shape, F32),
                   jax.ShapeDtypeStruct((B, 1, M_HEADS), F32)],
        compiler_params=_cparams(("parallel", "arbitrary")),
        name="mlstm",
    )(mq, gt, gtt, C0, n0, m0.reshape(B, 1, M_HEADS), g.reshape(1, M_WIDTH))
    return mo, C, n, m.reshape(B, M_HEADS)


def _gelu_tanh(x):
    return x * (0.5 * (1.0 + jnp.tanh(math.sqrt(2.0 / math.pi) * (x + 0.044715 * (x * x * x)))))


def _compress_body(load, pe_ref, w1_ref, w2_ref, rows_total, nh, n_cmp):
    w = KV_WIDTH
    half = CMP_LEN // 2
    pa = jnp.zeros((rows_total, w), F32)
    pb = jnp.zeros((rows_total, w), F32)
    for j in range(half):
        xj = load(j)
        pa = pa + jnp.dot((xj + pe_ref[j:j + 1, :]).astype(BF16), w1_ref[j], preferred_element_type=F32)
        pb = pb + jnp.dot((xj + pe_ref[half + j:half + j + 1, :]).astype(BF16), w1_ref[half + j],
                          preferred_element_type=F32)
    hid = pa + pltpu.roll(pb, rows_total - 1, 0)
    y = jnp.dot(_gelu_tanh(hid).astype(BF16), w2_ref[...], preferred_element_type=F32)
    rows = lax.broadcasted_iota(jnp.int32, (rows_total, w), 0)
    assert nh & (nh - 1) == 0
    return jnp.where((rows & (nh - 1)) < n_cmp, y, 0.0)


def _compress_kernel(xk_ref, xv_ref, pe_ref, w1_ref, w2_ref, o_ref, *, n_cmp):
    nh = xk_ref.shape[1] // CMP_STRIDE
    for kv, x_ref in enumerate((xk_ref, xv_ref)):
        o_ref[0, :, kv * KV_WIDTH:(kv + 1) * KV_WIDTH] = _compress_body(
            lambda j: x_ref[0, pl.ds(j, nh, stride=CMP_STRIDE), :], pe_ref.at[kv], w1_ref.at[kv], w2_ref.at[kv],
            nh, nh, n_cmp)


def _compress_sample_kernel(pt_ref, pool_ref, pe_ref, w1_ref, w2_ref, o_ref, stage, xrow_k, xrow_v, sem, *,
                            seqs, n_pages, n_cmp):
    i = pl.program_id(0)
    slot = i % 2
    page = pool_ref.shape[2]
    nh = n_pages * page // CMP_STRIDE

    def copies(step, sl):
        return [pltpu.make_async_copy(pool_ref.at[pt_ref[step * seqs + s, p], pl.ds(kv * KV_WIDTH, KV_WIDTH), :],
                                      stage.at[sl, kv, s * n_pages + p], sem.at[sl])
                for s in range(seqs) for p in range(n_pages) for kv in range(2)]

    @pl.when(i == 0)
    def _():
        for cp in copies(0, 0):
            cp.start()

    @pl.when(i + 1 < pl.num_programs(0))
    def _():
        for cp in copies(i + 1, 1 - slot):
            cp.start()

    for cp in copies(i, slot):
        cp.wait()
    xrows = (xrow_k, xrow_v)
    for kv in range(2):
        for pi in range(seqs * n_pages):
            xrows[kv][pi * page:(pi + 1) * page, :] = stage[slot, kv, pi].T
    for kv in range(2):
        o_ref[:, kv * KV_WIDTH:(kv + 1) * KV_WIDTH] = _compress_body(
            lambda j: xrows[kv][pl.ds(j, seqs * nh, stride=CMP_STRIDE), :],
            pe_ref.at[kv], w1_ref.at[kv], w2_ref.at[kv], seqs * nh, nh, n_cmp)


def _compress_sample(pool, page_table, cmp_w, n_cmp, seqs):
    n_pool, w, page = pool.shape
    DB, n_pages = page_table.shape
    nh = n_pages * page // CMP_STRIDE
    pe4, w1, w2 = cmp_w
    full = lambda *s: pl.BlockSpec(s, lambda i, pt: (0,) * len(s))
    grid_spec = pltpu.PrefetchScalarGridSpec(
        num_scalar_prefetch=1,
        grid=(DB // seqs,),
        in_specs=[pl.BlockSpec(memory_space=pl.ANY), full(*pe4.shape), full(*w1.shape), full(*w2.shape)],
        out_specs=pl.BlockSpec((seqs * nh, w), lambda i, pt: (i, 0)),
        scratch_shapes=[pltpu.VMEM((2, 2, seqs * n_pages, KV_WIDTH, page), F32),
                        pltpu.VMEM((seqs * n_pages * page, KV_WIDTH), F32),
                        pltpu.VMEM((seqs * n_pages * page, KV_WIDTH), F32),
                        pltpu.SemaphoreType.DMA((2,))],
    )
    return pl.pallas_call(
        functools.partial(_compress_sample_kernel, seqs=seqs, n_pages=n_pages, n_cmp=n_cmp),
        grid_spec=grid_spec,
        out_shape=jax.ShapeDtypeStruct((DB * nh, w), F32),
        compiler_params=_cparams(("arbitrary",)),
        name="compress_sample",
    )(page_table, pool, pe4, w1, w2)


def _block_diag2(a):
    z = jnp.zeros_like(a)
    return jnp.concatenate([jnp.concatenate([a, z], -1), jnp.concatenate([z, a], -1)], -2)


def _compress_weights(cmp_params):
    pe_k, w1_k, w2_k, pe_v, w1_v, w2_v = cmp_params
    pe = jnp.stack([jnp.concatenate([p, p], axis=1) for p in (pe_k, pe_v)])
    w1 = jnp.stack([_block_diag2(w.reshape(CMP_LEN, A_DIM, A_DIM)) for w in (w1_k, w1_v)]).astype(BF16)
    w2 = jnp.stack([_block_diag2(w) for w in (w2_k, w2_v)]).astype(BF16)
    return pe, w1, w2


def _compress_prompt(cmp_rows, cmp_w):
    B, L, w = cmp_rows.shape
    nh = L // CMP_STRIDE
    n_cmp = (L - CMP_LEN) // CMP_STRIDE + 1
    pe4, w1, w2 = cmp_w
    full = lambda *s: pl.BlockSpec(s, lambda b: (0,) * len(s))
    return pl.pallas_call(
        functools.partial(_compress_kernel, n_cmp=n_cmp),
        grid=(B,),
        in_specs=[pl.BlockSpec((1, L, KV_WIDTH), lambda b: (b, 0, 0)),
                  pl.BlockSpec((1, L, KV_WIDTH), lambda b: (b, 0, 1)),
                  full(*pe4.shape), full(*w1.shape), full(*w2.shape)],
        out_specs=pl.BlockSpec((1, nh, w), lambda b: (b, 0, 0)),
        out_shape=jax.ShapeDtypeStruct((B, nh, w), F32),
        compiler_params=_cparams(("parallel",)),
        name="compress_prompt",
    )(cmp_rows, cmp_rows, pe4, w1, w2)


_NEG = -1e30
_NEG_SEL = -1e9
_SEL_CHUNK = 512


def _dot_nt(a, b):
    return lax.dot_general(a, b, (((1,), (1,)), ((), ())), preferred_element_type=F32)


def _softmax_rows(s, valid):
    s = jnp.where(valid[None], s, _NEG)
    m = jnp.max(s, axis=-1, keepdims=True)
    e = jnp.where(valid[None], jnp.exp(s - m), 0.0)
    return e / jnp.maximum(jnp.sum(e, axis=-1, keepdims=True), jnp.finfo(jnp.float32).tiny)


def _heads_to_rows(q):
    tq = q.shape[0]
    lane = lax.broadcasted_iota(jnp.int32, (tq, LANES), 1)
    q = q * (A_DIM ** -0.5)
    rows = []
    for hd in range(A_HEADS):
        g = hd // A_GROUP
        tile = q[:, LANES * (hd // 2):LANES * (hd // 2 + 1)]
        if hd % 2 != g:
            tile = pltpu.roll(tile, A_DIM, 1)
        keep = (lane < A_DIM) if g == 0 else (lane >= A_DIM)
        rows.append(jnp.where(keep, tile, 0.0))
    return jnp.concatenate(rows, axis=0).astype(BF16)


def _gate_rows_to_heads(gt, o_c, o_s, o_w):
    tq = gt.shape[0]
    lane = lax.broadcasted_iota(jnp.int32, (tq, LANES), 1)
    gs = jax.nn.sigmoid(gt)
    tiles = []
    for pair in range(A_HEADS // 2):
        g = (2 * pair) // A_GROUP
        mixed = []
        for hd in (2 * pair, 2 * pair + 1):
            c = 2 * M_HEADS + 3 * hd
            r = slice(hd * tq, (hd + 1) * tq)
            mixed.append(gs[:, c:c + 1] * o_c[r] + gs[:, c + 1:c + 2] * o_s[r] + gs[:, c + 2:c + 3] * o_w[r])
        a, b = mixed
        if g == 0:
            tiles.append(jnp.where(lane < A_DIM, a, pltpu.roll(b, A_DIM, 1)))
        else:
            tiles.append(jnp.where(lane < A_DIM, pltpu.roll(a, A_DIM, 1), b))
    return jnp.concatenate(tiles, axis=1)


def _attend_two(s_a, ok_a, v_a, s_b, ok_b, v_b, v_a_transposed=False):
    nh, tq = s_a.shape[:2]
    if ok_a is not None:
        s_a = jnp.where(ok_a[None], s_a, _NEG)
    s_b = jnp.where(ok_b[None], s_b, _NEG)
    m = jnp.maximum(jnp.max(s_a, axis=-1, keepdims=True), jnp.max(s_b, axis=-1, keepdims=True))
    e_a = jnp.exp(s_a - m)
    e_b = jnp.exp(s_b - m)
    l = jnp.sum(e_a, axis=-1, keepdims=True) + jnp.sum(e_b, axis=-1, keepdims=True)
    e_a = e_a.reshape(nh * tq, -1).astype(BF16)
    o_a = _dot_nt(e_a, v_a) if v_a_transposed else jnp.dot(e_a, v_a, preferred_element_type=F32)
    o = o_a + jnp.dot(e_b.reshape(nh * tq, -1).astype(BF16), v_b, preferred_element_type=F32)
    return o / l.reshape(nh * tq, 1)


def _nsa_sample_kernel(pt_ref, q_ref, gt_ref, kc_ref, vc_ref, seln_ref, wst_ref, wnew_ref, oh_ref, pool_ref,
                       o_ref, nwin_ref, selbuf, sem, *, past):
    b = pl.program_id(0)
    slot = b % 2
    tq = q_ref.shape[0]
    nh = A_HEADS
    n_pages = pt_ref.shape[1]
    page = pool_ref.shape[2]
    wbuf = wst_ref.shape[1]
    n_sel = -(-(past + tq) // SEL_LEN)

    def copies(seq, sl):
        return [pltpu.make_async_copy(pool_ref.at[pt_ref[seq, p]], selbuf.at[sl, :, pl.ds(p * page, page)],
                                      sem.at[sl]) for p in range(n_pages)]

    @pl.when(b == 0)
    def _():
        for cp in copies(0, 0):
            cp.start()

    @pl.when(b + 1 < pl.num_programs(0))
    def _():
        for cp in copies(b + 1, 1 - slot):
            cp.start()

    qz = _heads_to_rows(q_ref[...])
    tpos = past + lax.broadcasted_iota(jnp.int32, (tq, LANES), 0)

    kc = kc_ref[...].astype(BF16)
    vc = vc_ref[...].astype(BF16)
    ncp = kc.shape[0]
    s_c = _dot_nt(qz, kc).reshape(nh, tq, ncp)
    tp_c = past + lax.broadcasted_iota(jnp.int32, (tq, ncp), 0)
    nidx = lax.broadcasted_iota(jnp.int32, (tq, ncp), 1)
    p_c = _softmax_rows(s_c, nidx * CMP_STRIDE + (CMP_LEN - 1) <= tp_c)
    o_c = jnp.dot(p_c.reshape(nh * tq, ncp).astype(BF16), vc, preferred_element_type=F32)

    cn = lax.broadcasted_iota(jnp.int32, (ncp, LANES), 0) * CMP_STRIDE
    jn = lax.broadcasted_iota(jnp.int32, (ncp, LANES), 1) * SEL_LEN
    ov = jnp.where((cn < jn + SEL_LEN) & (cn + CMP_LEN > jn), 1.0, 0.0).astype(BF16)
    jb = lax.broadcasted_iota(jnp.int32, (tq, LANES), 1)
    cur = tpos // SEL_LEN
    forced = (jb == 0) | (jb == cur) | (jb == cur - 1)
    bias = []
    for g in range(A_KV_HEADS):
        ps = p_c[g * A_GROUP]
        for r in range(1, A_GROUP):
            ps = ps + p_c[g * A_GROUP + r]
        hi = ps.astype(BF16)
        lo = (ps - hi.astype(F32)).astype(BF16)
        score = (jnp.dot(hi, ov, preferred_element_type=F32) + jnp.dot(lo, ov, preferred_element_type=F32))
        score = jnp.where(forced, FORCE_SCORE, score)
        score = jnp.where(jb * SEL_LEN <= tpos, score, -1.0)
        cnt = jnp.zeros((tq, LANES), jnp.int32)
        for k in range(n_sel):
            sk = score[:, k:k + 1]
            ahead = (sk > score) | ((sk == score) & (jb > k))
            cnt = cnt + jnp.where(ahead, 1, 0)
        bias.append(jnp.where(cnt < min(SEL_TOP, n_sel), 0.0, _NEG_SEL).astype(BF16))
    q_aug = jnp.concatenate(
        [qz, jnp.concatenate([bias[hd // A_GROUP] for hd in range(nh)], axis=0)], axis=1)

    tw = lax.broadcasted_iota(jnp.int32, (tq, wbuf), 0)
    iw = lax.broadcasted_iota(jnp.int32, (tq, wbuf), 1)
    tn = lax.broadcasted_iota(jnp.int32, (tq, tq), 0)
    un = lax.broadcasted_iota(jnp.int32, (tq, tq), 1)
    wst = wst_ref[0]
    wnew = wnew_ref[...]
    s_wa = _dot_nt(qz, wst[:, :KV_WIDTH].astype(BF16)).reshape(nh, tq, wbuf)
    s_wb = _dot_nt(qz, wnew[:, :KV_WIDTH].astype(BF16)).reshape(nh, tq, tq)
    o_w = _attend_two(s_wa, (wbuf + tw - iw < WINDOW), wst[:, KV_WIDTH:].astype(BF16),
                      s_wb, un <= tn, wnew[:, KV_WIDTH:].astype(BF16))
    nwin_ref[0, :wbuf - tq, :] = wst[tq:, :]
    nwin_ref[0, wbuf - tq:, :] = wnew

    for cp in copies(b, slot):
        cp.wait()
    seln = seln_ref[...]
    k_aug_t = jnp.concatenate([selbuf[slot, :KV_WIDTH, :].astype(BF16), oh_ref[...]], axis=0)
    nblk = (past + lax.broadcasted_iota(jnp.int32, (tq, LANES), 0)) // SEL_LEN
    oh_new = jnp.where(nblk == lax.broadcasted_iota(jnp.int32, (tq, LANES), 1), 1.0, 0.0)
    kn_aug = jnp.concatenate([seln[:, :KV_WIDTH], oh_new], axis=1).astype(BF16)
    s_sa = jnp.dot(q_aug, k_aug_t, preferred_element_type=F32).reshape(nh, tq, past)
    s_sb = _dot_nt(q_aug, kn_aug).reshape(nh, tq, tq)
    o_s = _attend_two(s_sa, None, selbuf[slot, KV_WIDTH:, :].astype(BF16), s_sb, un <= tn,
                      seln[:, KV_WIDTH:].astype(BF16), v_a_transposed=True)

    o_ref[...] = _gate_rows_to_heads(gt_ref[...], o_c, o_s, o_w)


def _nsa_sample_call(qa, gt, comp, sel_new, win_state, win_new, sel_pool, page_table, T):
    DB, n_pages = page_table.shape
    page = sel_pool.shape[2]
    past = n_pages * page
    wbuf = win_state.shape[1]
    ncp = comp.shape[0] // DB
    w2 = 2 * KV_WIDTH
    assert past % SEL_LEN == 0 and T < CMP_STRIDE and T % 8 == 0 and wbuf == WINDOW and past >= WINDOW
    assert page % LANES == 0
    onehot = (jnp.arange(LANES)[:, None] == jnp.arange(past)[None, :] // SEL_LEN).astype(BF16)
    tok = lambda w: pl.BlockSpec((T, w), lambda b, pt: (b, 0))
    grid_spec = pltpu.PrefetchScalarGridSpec(
        num_scalar_prefetch=1,
        grid=(DB,),
        in_specs=[tok(A_WIDTH), tok(LANES),
                  pl.BlockSpec((ncp, KV_WIDTH), lambda b, pt: (b, 0)),
                  pl.BlockSpec((ncp, KV_WIDTH), lambda b, pt: (b, 1)),
                  tok(w2),
                  pl.BlockSpec((1, wbuf, w2), lambda b, pt: (b, 0, 0)),
                  tok(w2),
                  pl.BlockSpec((LANES, past), lambda b, pt: (0, 0)),
                  pl.BlockSpec(memory_space=pl.ANY)],
        out_specs=[tok(A_WIDTH), pl.BlockSpec((1, wbuf, w2), lambda b, pt: (b, 0, 0))],
        scratch_shapes=[pltpu.VMEM((2, w2, past), F32), pltpu.SemaphoreType.DMA((2,))],
    )
    return pl.pallas_call(
        functools.partial(_nsa_sample_kernel, past=past),
        grid_spec=grid_spec,
        out_shape=[jax.ShapeDtypeStruct((DB * T, A_WIDTH), F32), jax.ShapeDtypeStruct((DB, wbuf, w2), F32)],
        compiler_params=_cparams(("arbitrary",)),
        name="nsa_sample",
    )(page_table, qa, gt, comp, comp, sel_new, win_state, win_new, onehot, sel_pool)


def _nsa_prompt_t_kernel(q_ref, gt_ref, kc_ref, vc_ref, selk_ref, selvt_ref, wink_ref, winvt_ref, o_ref, *, seq):
    tq = Q_BLOCK
    nh = A_HEADS
    rows = nh * tq
    n_sel = seq // SEL_LEN
    kt = LANES
    s0 = pl.program_id(1) * tq
    qz = _heads_to_rows(q_ref[0])
    htile = lambda x, hd: x[:, hd * tq:(hd + 1) * tq]

    def qpos(n):
        return s0 + (lax.broadcasted_iota(jnp.int32, (n, rows), 1) & (tq - 1))

    def with_ones(vt):
        return jnp.concatenate([vt, jnp.ones((8, vt.shape[1]), F32)], axis=0).astype(BF16)

    def normalise(acc):
        return acc[:LANES] * (1.0 / acc[LANES:LANES + 1])

    kc = kc_ref[0].astype(BF16)
    ncp = kc.shape[0]
    s_c = _dot_nt(kc, qz)
    ok_c = lax.broadcasted_iota(jnp.int32, (ncp, rows), 0) * CMP_STRIDE + (CMP_LEN - 1) <= qpos(ncp)
    s_c = jnp.where(ok_c, s_c, _NEG)
    e_c = jnp.where(ok_c, jnp.exp(s_c - jnp.max(s_c, axis=0, keepdims=True)), 0.0)
    p_c = e_c * (1.0 / jnp.maximum(jnp.sum(e_c, axis=0, keepdims=True), jnp.finfo(jnp.float32).tiny))
    o_c = jnp.dot(vc_ref[0].T.astype(BF16), p_c.astype(BF16), preferred_element_type=F32)

    jn = lax.broadcasted_iota(jnp.int32, (n_sel, ncp), 0) * SEL_LEN
    cn = lax.broadcasted_iota(jnp.int32, (n_sel, ncp), 1) * CMP_STRIDE
    ov_t = jnp.where((cn < jn + SEL_LEN) & (cn + CMP_LEN > jn), 1.0, 0.0).astype(BF16)
    jb = lax.broadcasted_iota(jnp.int32, (n_sel, tq), 0)
    tp = s0 + lax.broadcasted_iota(jnp.int32, (n_sel, tq), 1)
    cur = tp // SEL_LEN
    forced = (jb == 0) | (jb == cur) | (jb == cur - 1)
    bias = []
    for g in range(A_KV_HEADS):
        mass = htile(p_c, g * A_GROUP)
        for r in range(1, A_GROUP):
            mass = mass + htile(p_c, g * A_GROUP + r)
        hi = mass.astype(BF16)
        lo = (mass - hi.astype(F32)).astype(BF16)
        score = (jnp.dot(ov_t, hi, preferred_element_type=F32) + jnp.dot(ov_t, lo, preferred_element_type=F32))
        score = jnp.where(forced, FORCE_SCORE, score)
        score = jnp.where(jb * SEL_LEN <= tp, score, -1.0)
        cnt = jnp.zeros((n_sel, tq), jnp.int32)
        for k in range(n_sel):
            rk = score[k:k + 1, :]
            ahead = (rk > score) | ((rk == score) & (jb > k))
            cnt = cnt + jnp.where(ahead, 1, 0)
        bias_t = jnp.where(cnt < min(SEL_TOP, n_sel), 0.0, _NEG_SEL)
        if n_sel < LANES:
            bias_t = jnp.concatenate([bias_t, jnp.zeros((LANES - n_sel, tq), F32)], axis=0)
        bias.append(bias_t.T.astype(BF16))
    q_aug = jnp.concatenate(
        [qz, jnp.concatenate([bias[hd // A_GROUP] for hd in range(nh)], axis=0)], axis=1)

    wk = WINDOW + tq
    w0 = pl.multiple_of(jnp.clip(s0 - WINDOW, 0, seq - wk), tq)
    s_w = _dot_nt(wink_ref[0, pl.ds(w0, wk), :].astype(BF16), qz)
    dist = qpos(wk) - (w0 + lax.broadcasted_iota(jnp.int32, (wk, rows), 0))
    s_w = jnp.where((dist >= 0) & (dist < WINDOW), s_w, _NEG)
    e_w = jnp.exp(s_w - jnp.max(s_w, axis=0, keepdims=True)).astype(BF16)
    vw_t = jnp.concatenate([winvt_ref[0, w0 // kt + i] for i in range(wk // kt)], axis=1)
    o_w = normalise(jnp.dot(with_ones(vw_t), e_w, preferred_element_type=F32))

    kc_n = _SEL_CHUNK

    def chunk(c, carry, causal):
        m, acc = carry
        k0 = pl.multiple_of(c * kc_n, kc_n)
        kblk = (k0 + lax.broadcasted_iota(jnp.int32, (kc_n, LANES), 0)) // SEL_LEN
        onehot = jnp.where(kblk == lax.broadcasted_iota(jnp.int32, (kc_n, LANES), 1), 1.0, 0.0)
        k_aug = jnp.concatenate([selk_ref[0, pl.ds(k0, kc_n), :], onehot], axis=1).astype(BF16)
        s = _dot_nt(k_aug, q_aug)
        if causal:
            s = jnp.where(k0 + lax.broadcasted_iota(jnp.int32, (kc_n, rows), 0) <= qpos(kc_n), s, _NEG_SEL)
        m_new = jnp.maximum(m, jnp.max(s, axis=0, keepdims=True))
        p = jnp.exp(s - m_new).astype(BF16)
        v_t = jnp.concatenate([selvt_ref[0, c * (kc_n // kt) + i] for i in range(kc_n // kt)], axis=1)
        acc = jnp.exp(m - m_new) * acc + jnp.dot(with_ones(v_t), p, preferred_element_type=F32)
        return m_new, acc

    n_ch = (s0 + tq - 1) // kc_n + 1
    init = (jnp.full((1, rows), _NEG, F32), jnp.zeros((LANES + 8, rows), F32))
    carry = lax.fori_loop(0, n_ch - 1, lambda c, cr: chunk(c, cr, False), init)
    o_s = normalise(chunk(n_ch - 1, carry, True)[1])

    gs_t = jax.nn.sigmoid(gt_ref[0]).T
    mixed = []
    for hd in range(nh):
        c = 2 * M_HEADS + 3 * hd
        g = hd // A_GROUP
        mix = (gs_t[c:c + 1, :] * htile(o_c, hd) + gs_t[c + 1:c + 2, :] * htile(o_s, hd)
               + gs_t[c + 2:c + 3, :] * htile(o_w, hd))
        mixed.append(mix[g * A_DIM:(g + 1) * A_DIM, :])
    o_ref[0] = jnp.concatenate(
        [jnp.concatenate([mixed[2 * pr], mixed[2 * pr + 1]], axis=0).T for pr in range(nh // 2)], axis=1)


def _nsa_prompt_t_call(qa, gt, comp, sel_k, sel_vt, win_k, win_vt):
    B, S, _ = qa.shape
    assert S % _SEL_CHUNK == 0 and S >= WINDOW + Q_BLOCK and S // SEL_LEN <= LANES
    ncp = comp.shape[1]
    keys = pl.BlockSpec((1, S, KV_WIDTH), lambda b, i: (b, 0, 0))
    vals = pl.BlockSpec((1, S // LANES, KV_WIDTH, LANES), lambda b, i: (b, 0, 0, 0))
    return pl.pallas_call(
        functools.partial(_nsa_prompt_t_kernel, seq=S),
        grid=(B, S // Q_BLOCK),
        in_specs=[pl.BlockSpec((1, Q_BLOCK, A_WIDTH), lambda b, i: (b, i, 0)),
                  pl.BlockSpec((1, Q_BLOCK, LANES), lambda b, i: (b, i, 0)),
                  pl.BlockSpec((1, ncp, KV_WIDTH), lambda b, i: (b, 0, 0)),
                  pl.BlockSpec((1, ncp, KV_WIDTH), lambda b, i: (b, 0, 1)),
                  keys, vals, keys, vals],
        out_specs=pl.BlockSpec((1, Q_BLOCK, A_WIDTH), lambda b, i: (b, i, 0)),
        out_shape=jax.ShapeDtypeStruct((B, S, A_WIDTH), F32),
        compiler_params=_cparams(("parallel", "arbitrary")),
        name="nsa_prompt",
    )(qa, gt, comp, comp, sel_k, sel_vt, win_k, win_vt)


def _moe(h2, logits, moe_w, n_experts):
    w_gu, b_gu, w_down, b_down = moe_w
    n, d = h2.shape
    gate_w, dest, src, blk_e, n_used = _route(logits, n_experts)
    yb = _ffn(h2[src], blk_e, n_used, w_gu, b_gu, w_down, b_down)
    return yb[dest.T], gate_w


def kernel(x_prompt, x_sample, cache_cmp, cache_sel, state_win, state_C, state_n, state_m, page_table,
           c_prompt, c_sample, w_ada, b_ada, w_in, b_in, m_norm_g, cmp_pe_k, cmp_w1_k, cmp_w2_k,
           cmp_pe_v, cmp_w1_v, cmp_w2_v, w_out, ln1_g, ln1_b, w_router, b_router, w_gu, b_gu,
           w_down, b_down, ln2_g, ln2_b):
    B, S, d = x_prompt.shape
    DB, T, _ = x_sample.shape
    depth = w_ada.shape[0]
    n_experts = w_router.shape[-1]
    alpha = (2 * depth) ** 0.25
    n_pool, page = cache_cmp.shape[1:3]
    past_len = page_table.shape[1] * page
    wbuf = state_win.shape[2]
    pos_p = jnp.arange(S)
    pos_s = past_len + jnp.arange(T)
    tt_p = min(S, 256)
    bb_s = min(DB, max(1, 256 // T))
    n_cmp_s = (past_len + T - CMP_LEN) // CMP_STRIDE + 1
    assert n_cmp_s < past_len // CMP_STRIDE
    r3 = lambda a: a.reshape(B, S, a.shape[-1])
    kv5 = lambda a, n, t: a.reshape(n, t, 2, A_KV_HEADS, A_DIM)
    y_prompt, y_sample = x_prompt, x_sample
    outs = [[] for _ in range(12)]
    for l in range(depth):
        cmp_l = (cmp_pe_k[l], cmp_w1_k[l], cmp_w2_k[l], cmp_pe_v[l], cmp_w1_v[l], cmp_w2_v[l])
        moe_w = (w_gu[l], b_gu[l], w_down[l], b_down[l])
        post1_w = (w_out[l], ln1_g[l], ln1_b[l], w_router[l], b_router[l])
        w_r, b_r = _prep_w_in(w_in[l], b_in[l])
        cmp_w = _compress_weights(cmp_l)
        c_all = jnp.concatenate([c_prompt, c_sample], axis=0)
        c_all = jnp.pad(c_all, ((0, -c_all.shape[0] % 8), (0, 0)))
        mod_all = _ada(c_all, w_ada[l], b_ada[l]).reshape(-1, 6, d)
        mod_p, mod_s = mod_all[:B], mod_all[B:B + DB]

        (mq, gt, qa, cmp_p, sel_p, win_p, cmp_t, sel_t, win_t, sel_vt, win_vt) = _pre(
            y_prompt, mod_p, pos_p, w_r, b_r, 1, tt_p, feature_major=True)
        zc = jnp.zeros((B, M_HEADS, M_DIM, M_DIM), F32)
        mo, C_p, n_p, m_p = _mlstm_call(r3(mq), r3(gt), zc, zc[..., 0], zc[..., 0, 0], m_norm_g[l],
                                        math.gcd(B, 4))
        comp = _compress_prompt(r3(cmp_p), cmp_w)
        ma = _nsa_prompt_t_call(r3(qa), r3(gt), comp, r3(sel_p), sel_vt, r3(win_p), win_vt)
        x1_p, h2_p, lg_p = _post1(mo.reshape(B * S, M_WIDTH), ma.reshape(B * S, A_WIDTH), y_prompt, mod_p,
                                  *post1_w, 1, tt_p, alpha)

        mq, gt, qa, cmp_s, sel_s, win_s = _pre(y_sample, mod_s, pos_s, w_r, b_r, bb_s, T)
        mo, C_s, n_s, m_s = _mlstm_call(mq.reshape(DB, T, -1), gt.reshape(DB, T, -1), state_C[l],
                                        state_n[l], state_m[l], m_norm_g[l], math.gcd(DB, 4))
        feature_major = lambda pool: pool.transpose(0, 2, 3, 4, 1).reshape(n_pool, 2 * KV_WIDTH, page)
        comp = _compress_sample(feature_major(cache_cmp[l]), page_table, cmp_w, n_cmp_s, math.gcd(DB, 4))
        ma, new_win = _nsa_sample_call(qa, gt, comp, sel_s, state_win[l].reshape(DB, wbuf, 2 * KV_WIDTH),
                                       win_s, feature_major(cache_sel[l]), page_table, T)
        x1_s, h2_s, lg_s = _post1(mo.reshape(DB * T, M_WIDTH), ma, y_sample, mod_s, *post1_w, bb_s, T, alpha)

        yg, gate_w = _moe(jnp.concatenate([h2_p, h2_s], axis=0), jnp.concatenate([lg_p, lg_s], axis=0),
                          moe_w, n_experts)
        y_prompt = _post2(yg, gate_w, 0, x1_p, mod_p, ln2_g[l], ln2_b[l], 1, tt_p, alpha)
        y_sample = _post2(yg, gate_w, B * S, x1_s, mod_s, ln2_g[l], ln2_b[l], bb_s, T, alpha)

        kv5_t = lambda a: a.reshape(B, 2, A_KV_HEADS, A_DIM, -1).transpose(0, 4, 1, 2, 3)
        new = (kv5_t(cmp_t), kv5_t(sel_t), kv5_t(win_t[:, :, -min(WINDOW, S):]), C_p, n_p, m_p,
               kv5(cmp_s, DB, T), kv5(sel_s, DB, T), new_win.reshape(state_win.shape[1:]), C_s, n_s, m_s)
        for lst, v in zip(outs, new):
            lst.append(v)
    return (y_prompt, y_sample, *[jnp.stack(v) for v in outs])
```

```python
import functools
import math

import numpy as np
import jax
import jax.numpy as jnp
from jax import lax
from jax.experimental import pallas as pl
from jax.experimental.pallas import tpu as pltpu

F32 = jnp.float32
BF16 = jnp.bfloat16

M_HEADS = 4
M_DIM = 128
M_WIDTH = M_HEADS * M_DIM
M_CHUNK = 64
A_HEADS = 8
A_KV_HEADS = 2
A_GROUP = A_HEADS // A_KV_HEADS
A_DIM = 64
A_WIDTH = A_HEADS * A_DIM
KV_WIDTH = A_KV_HEADS * A_DIM
CMP_LEN = 32
CMP_STRIDE = 16
SEL_LEN = 64
SEL_TOP = 16
WINDOW = 512
Q_BLOCK = 256
FORCE_SCORE = 1.0e4
ROPE_THETA = 500000.0
ROPE_DIM = A_DIM // 4
TOP_K = 4
SWIGLU_LIMIT = 7.0
SWIGLU_ALPHA = 1.702
LN_EPS = 1e-5
LANES = 128
MOE_ROWS = 512
VMEM_LIMIT = 56 * 1024 * 1024

_O_MQ = 0
_O_IF = 4 * M_WIDTH
_O_QA = _O_IF + 2 * M_HEADS
_O_KV = _O_QA + A_WIDTH
_O_GA = _O_KV + 6 * KV_WIDTH
_N_IN = _O_GA + 3 * A_HEADS
_R_QA = 4 * M_WIDTH
_R_KV = _R_QA + A_WIDTH
_R_GT = _R_KV + 6 * KV_WIDTH
_R_END = _R_GT + LANES


def _cparams(sem):
    return pltpu.CompilerParams(dimension_semantics=sem, vmem_limit_bytes=VMEM_LIMIT)


def _ln_core(x):
    mu = jnp.mean(x, axis=-1, keepdims=True)
    xc = x - mu
    var = jnp.mean(xc * xc, axis=-1, keepdims=True)
    return xc * lax.rsqrt(var + LN_EPS)


def _ada_kernel(c_ref, w_ref, b_ref, o_ref):
    o_ref[...] = jnp.dot(c_ref[...].astype(BF16), w_ref[...].astype(BF16),
                         preferred_element_type=F32) + b_ref[...]


def _ada(c, w_ada, b_ada):
    n, d = c.shape
    cols = w_ada.shape[1]
    return pl.pallas_call(
        _ada_kernel,
        grid=(cols // d,),
        in_specs=[pl.BlockSpec((n, d), lambda j: (0, 0)),
                  pl.BlockSpec((d, d), lambda j: (0, j)),
                  pl.BlockSpec((1, d), lambda j: (0, j))],
        out_specs=pl.BlockSpec((n, d), lambda j: (0, j)),
        out_shape=jax.ShapeDtypeStruct((n, cols), F32),
        compiler_params=_cparams(("arbitrary",)),
        name="ada",
    )(c, w_ada, b_ada.reshape(1, cols))


def _rope_apply(v, cos, sa, sb):
    reps = v.shape[1] // LANES
    tile = lambda t: t if reps == 1 else jnp.concatenate([t] * reps, axis=1)
    w = v.shape[1]
    return (v * tile(cos) + pltpu.roll(v, w - ROPE_DIM // 2, 1) * tile(sa)
            + pltpu.roll(v, ROPE_DIM // 2, 1) * tile(sb))


def _pre_kernel(x_ref, mod_ref, cos_ref, sa_ref, sb_ref, w_ref, b_ref,
                mq_ref, gt_ref, qa_ref, cmp_ref, sel_ref, win_ref, *t_refs):
    bb, tt, d = x_ref.shape
    mod = mod_ref[...]
    h = _ln_core(x_ref[...]) * (1.0 + mod[:, 1:2, :]) + mod[:, 0:1, :]
    h = h.reshape(bb * tt, d).astype(BF16)
    z = jnp.dot(h, w_ref[...], preferred_element_type=F32) + b_ref[...]
    cos, sa, sb = cos_ref[...], sa_ref[...], sb_ref[...]
    mq_ref[...] = z[:, :_R_QA]
    gt_ref[...] = z[:, _R_GT:_R_END]
    qa_ref[...] = _rope_apply(z[:, _R_QA:_R_KV], cos, sa, sb)
    for n, ref in enumerate((cmp_ref, sel_ref, win_ref)):
        o = _R_KV + 2 * KV_WIDTH * n
        kv = jnp.concatenate([_rope_apply(z[:, o:o + KV_WIDTH], cos, sa, sb),
                              z[:, o + KV_WIDTH:o + 2 * KV_WIDTH]], axis=1)
        ref[...] = kv
        if t_refs:
            kv_t = kv.T
            t_refs[n][0] = kv_t
            if n > 0:
                for c in range(kv_t.shape[1] // LANES):
                    t_refs[2 + n][0, c] = kv_t[KV_WIDTH:, c * LANES:(c + 1) * LANES]


def _rope_tables(pos):
    half = ROPE_DIM // 2
    inv = ROPE_THETA ** (-jnp.arange(0, ROPE_DIM, 2, dtype=F32) / ROPE_DIM)
    ang = pos.astype(F32)[:, None] * inv[None, :]
    cos, sin = jnp.cos(ang), jnp.sin(ang)
    n = pos.shape[0]
    one = jnp.ones((n, A_DIM - ROPE_DIM), F32)
    zero = jnp.zeros((n, A_DIM - ROPE_DIM), F32)
    zh = jnp.zeros((n, half), F32)
    cos_t = jnp.concatenate([cos, cos, one], axis=1)
    sa_t = jnp.concatenate([-sin, zh, zero], axis=1)
    sb_t = jnp.concatenate([zh, sin, zero], axis=1)
    two = lambda t: jnp.concatenate([t, t], axis=1)
    return two(cos_t), two(sa_t), two(sb_t)


def _prep_w_in(w_in, b_in):
    pad = LANES - 2 * M_HEADS - 3 * A_HEADS
    cat = lambda a: jnp.concatenate(
        [a[..., _O_MQ:_O_IF], a[..., _O_QA:_O_GA], a[..., _O_IF:_O_QA], a[..., _O_GA:_N_IN],
         jnp.zeros(a.shape[:-1] + (pad,), a.dtype)], axis=-1)
    return cat(w_in).astype(BF16), cat(b_in[None, :])


def _pre(x, mod, pos, w_r, b_r, bb, tt, feature_major=False):
    B, T, d = x.shape
    nt = T // tt
    rows = bb * tt
    cos, sa, sb = _rope_tables(pos)
    if bb > 1:
        cos, sa, sb = (jnp.tile(t, (bb, 1)) for t in (cos, sa, sb))
    n_tok = B * T
    tab = pl.BlockSpec((rows, LANES), lambda i, j: (j, 0))
    row = lambda w: pl.BlockSpec((rows, w), lambda i, j: (i * nt + j, 0))
    widths = (_R_QA, LANES, A_WIDTH, 2 * KV_WIDTH, 2 * KV_WIDTH, 2 * KV_WIDTH)
    out_specs = [row(w) for w in widths]
    out_shape = [jax.ShapeDtypeStruct((n_tok, w), F32) for w in widths]
    if feature_major:
        assert bb == 1 and tt % LANES == 0
        out_specs += [pl.BlockSpec((1, 2 * KV_WIDTH, tt), lambda i, j: (i, 0, j))] * 3
        out_shape += [jax.ShapeDtypeStruct((B, 2 * KV_WIDTH, T), F32)] * 3
        out_specs += [pl.BlockSpec((1, tt // LANES, KV_WIDTH, LANES), lambda i, j: (i, j, 0, 0))] * 2
        out_shape += [jax.ShapeDtypeStruct((B, T // LANES, KV_WIDTH, LANES), F32)] * 2
    return pl.pallas_call(
        _pre_kernel,
        grid=(B // bb, nt),
        in_specs=[pl.BlockSpec((bb, tt, d), lambda i, j: (i, j, 0)),
                  pl.BlockSpec((bb, 6, d), lambda i, j: (i, 0, 0)),
                  tab, tab, tab,
                  pl.BlockSpec((d, _R_END), lambda i, j: (0, 0)),
                  pl.BlockSpec((1, _R_END), lambda i, j: (0, 0))],
        out_specs=out_specs,
        out_shape=out_shape,
        compiler_params=_cparams(("parallel", "arbitrary")),
        name="pre",
    )(x, mod, cos, sa, sb, w_r, b_r)


def _post1_kernel(mo_ref, ao_ref, x_ref, mod_ref, wo_ref, g_ref, b_ref, wr_ref, br_ref,
                  x1_ref, h2_ref, lg_ref, *, alpha):
    bb, tt, d = x_ref.shape
    mod = mod_ref[...]
    mixin = jnp.concatenate([mo_ref[...], ao_ref[...]], axis=1).astype(BF16)
    mix = jnp.dot(mixin, wo_ref[...], preferred_element_type=F32).reshape(bb, tt, d)
    x1 = _ln_core(alpha * x_ref[...] + mod[:, 2:3, :] * mix) * g_ref[...] + b_ref[...]
    h2 = _ln_core(x1) * (1.0 + mod[:, 4:5, :]) + mod[:, 3:4, :]
    x1_ref[...] = x1
    h2f = h2.reshape(bb * tt, d)
    h2_ref[...] = h2f
    hi = h2f.astype(BF16)
    lo = (h2f - hi.astype(F32)).astype(BF16)
    lg_ref[...] = (jnp.dot(hi, wr_ref[0], preferred_element_type=F32)
                   + jnp.dot(hi, wr_ref[1], preferred_element_type=F32)
                   + jnp.dot(lo, wr_ref[0], preferred_element_type=F32)) + br_ref[...]


def _post1(mo, ao, x, mod, w_out, ln_g, ln_b, w_router, b_router, bb, tt, alpha):
    B, T, d = x.shape
    nt = T // tt
    rows = bb * tt
    ne = w_router.shape[1]
    wr = jnp.pad(w_router, ((0, 0), (0, LANES - ne)))
    wr_hi = wr.astype(BF16)
    wr = jnp.stack([wr_hi, (wr - wr_hi.astype(F32)).astype(BF16)])
    br = jnp.pad(b_router, (0, LANES - ne), constant_values=-jnp.inf).reshape(1, LANES)
    full = lambda *s: pl.BlockSpec(s, lambda i, j: (0,) * len(s))
    row = lambda w: pl.BlockSpec((rows, w), lambda i, j: (i * nt + j, 0))
    return pl.pallas_call(
        functools.partial(_post1_kernel, alpha=alpha),
        grid=(B // bb, nt),
        in_specs=[row(M_WIDTH), row(A_WIDTH),
                  pl.BlockSpec((bb, tt, d), lambda i, j: (i, j, 0)),
                  pl.BlockSpec((bb, 6, d), lambda i, j: (i, 0, 0)),
                  full(M_WIDTH + A_WIDTH, d), full(1, d), full(1, d), full(2, d, LANES), full(1, LANES)],
        out_specs=[pl.BlockSpec((bb, tt, d), lambda i, j: (i, j, 0)), row(d), row(LANES)],
        out_shape=[jax.ShapeDtypeStruct((B, T, d), F32),
                   jax.ShapeDtypeStruct((B * T, d), F32),
                   jax.ShapeDtypeStruct((B * T, LANES), F32)],
        compiler_params=_cparams(("parallel", "arbitrary")),
        name="post1",
    )(mo, ao, x, mod, w_out.astype(BF16), ln_g.reshape(1, d), ln_b.reshape(1, d), wr, br)


def _ffn_kernel(be_ref, nu_ref, x_ref, wgu_ref, bgu_ref, wd_ref, bd_ref, y_ref, wgu_bf, wd_bf):
    i = pl.program_id(0)
    dff = wd_ref.shape[1]

    @pl.when((i == 0) | (be_ref[i] != be_ref[jnp.maximum(i - 1, 0)]))
    def _():
        wgu_bf[...] = wgu_ref[0].astype(BF16)
        wd_bf[...] = wd_ref[0].astype(BF16)

    @pl.when(i < nu_ref[0])
    def _():
        half = x_ref.shape[0] // 2
        halves = [slice(h * half, (h + 1) * half) for h in range(2)]
        gus = [jnp.dot(x_ref[r, :].astype(BF16), wgu_bf[...], preferred_element_type=F32) + bgu_ref[0]
               for r in halves]
        acts = []
        for gu in gus:
            g = jnp.minimum(gu[:, :dff], SWIGLU_LIMIT)
            u = jnp.clip(gu[:, dff:], -SWIGLU_LIMIT, SWIGLU_LIMIT)
            acts.append(((u + 1.0) * (g * jax.nn.sigmoid(SWIGLU_ALPHA * g))).astype(BF16))
        for r, act in zip(halves, acts):
            y_ref[r, :] = (jnp.dot(act, wd_bf[...], preferred_element_type=F32) + bd_ref[0]).astype(BF16)

    @pl.when(i >= nu_ref[0])
    def _():
        y_ref[...] = jnp.zeros_like(y_ref)


def _ffn(xb, blk_e, n_used, w_gu, b_gu, w_down, b_down):
    rows, d = xb.shape
    ne, _, f2 = w_gu.shape
    dff = w_down.shape[1]
    nb = rows // MOE_ROWS
    grid_spec = pltpu.PrefetchScalarGridSpec(
        num_scalar_prefetch=2,
        grid=(nb,),
        in_specs=[pl.BlockSpec((MOE_ROWS, d), lambda i, be, nu: (i, 0)),
                  pl.BlockSpec((1, d, f2), lambda i, be, nu: (be[i], 0, 0)),
                  pl.BlockSpec((1, 1, f2), lambda i, be, nu: (be[i], 0, 0)),
                  pl.BlockSpec((1, dff, d), lambda i, be, nu: (be[i], 0, 0)),
                  pl.BlockSpec((1, 1, d), lambda i, be, nu: (be[i], 0, 0))],
        out_specs=pl.BlockSpec((MOE_ROWS, d), lambda i, be, nu: (i, 0)),
        scratch_shapes=[pltpu.VMEM((d, f2), BF16), pltpu.VMEM((dff, d), BF16)],
    )
    return pl.pallas_call(
        _ffn_kernel,
        grid_spec=grid_spec,
        out_shape=jax.ShapeDtypeStruct((rows, d), BF16),
        compiler_params=_cparams(("arbitrary",)),
        name="ffn",
    )(blk_e, n_used, xb, w_gu, b_gu.reshape(ne, 1, f2), w_down, b_down.reshape(ne, 1, d))


def _post2_kernel(yg_ref, gw_ref, x1_ref, mod_ref, g_ref, b_ref, y_ref, *, alpha):
    bb, tt, d = x1_ref.shape
    gw = gw_ref[...]
    f = yg_ref[0].astype(F32) * gw[:, 0:1]
    for k in range(1, TOP_K):
        f = f + yg_ref[k].astype(F32) * gw[:, k:k + 1]
    y = alpha * x1_ref[...] + mod_ref[...][:, 5:6, :] * f.reshape(bb, tt, d)
    y_ref[...] = _ln_core(y) * g_ref[...] + b_ref[...]


def _post2(yg, gw, row0, x1, mod, ln_g, ln_b, bb, tt, alpha):
    B, T, d = x1.shape
    rows = bb * tt
    nt = T // tt
    blk0 = row0 // rows
    assert row0 % rows == 0
    full = lambda *s: pl.BlockSpec(s, lambda i, j: (0,) * len(s))
    return pl.pallas_call(
        functools.partial(_post2_kernel, alpha=alpha),
        grid=(B // bb, nt),
        in_specs=[pl.BlockSpec((TOP_K, rows, d), lambda i, j: (0, blk0 + i * nt + j, 0)),
                  pl.BlockSpec((rows, LANES), lambda i, j: (blk0 + i * nt + j, 0)),
                  pl.BlockSpec((bb, tt, d), lambda i, j: (i, j, 0)),
                  pl.BlockSpec((bb, 6, d), lambda i, j: (i, 0, 0)),
                  full(1, d), full(1, d)],
        out_specs=pl.BlockSpec((bb, tt, d), lambda i, j: (i, j, 0)),
        out_shape=jax.ShapeDtypeStruct((B, T, d), F32),
        compiler_params=_cparams(("parallel", "arbitrary")),
        name="post2",
    )(yg, gw, x1, mod, ln_g.reshape(1, d), ln_b.reshape(1, d))


def _lane_prefix_sum(x):
    lane = lax.broadcasted_iota(jnp.int32, x.shape, 1)
    s = 1
    while s < LANES:
        x = x + jnp.where(lane >= s, pltpu.roll(x, s, 1), 0.0)
        s *= 2
    return x


def _route_kernel(lg_ref, dest_ref, gw_ref, cnt_ref, counts, running):
    phase = pl.program_id(0)
    blk = pl.program_id(1)
    r = lg_ref.shape[0]
    lane = lax.broadcasted_iota(jnp.int32, (r, LANES), 1)

    @pl.when((phase == 0) & (blk == 0))
    def _():
        counts[...] = jnp.zeros_like(counts)

    @pl.when((phase == 1) & (blk == 0))
    def _():
        running[...] = jnp.zeros_like(running)

    vals = lg_ref[...]
    onehots, tops = [], []
    for k in range(TOP_K):
        m = jnp.max(vals, axis=1, keepdims=True)
        idx = jnp.min(jnp.where(vals == m, lane, LANES), axis=1, keepdims=True)
        hit = lane == idx
        onehots.append(hit)
        tops.append(m)
        vals = jnp.where(hit, -jnp.inf, vals)
    ohf = [jnp.where(h, 1.0, 0.0) for h in onehots]
    block_cnt = [jnp.sum(o, axis=0, keepdims=True) for o in ohf]

    @pl.when(phase == 0)
    def _():
        counts[...] += block_cnt[0] + block_cnt[1] + block_cnt[2] + block_cnt[3]

    @pl.when(phase == 1)
    def _():
        cnt = counts[...]
        padded = jnp.ceil(cnt * (1.0 / MOE_ROWS)) * MOE_ROWS
        pad_start = _lane_prefix_sum(padded) - padded
        es = [jnp.exp(t - tops[0]) for t in tops]
        den = es[0] + es[1] + es[2] + es[3]
        ti = lax.broadcasted_iota(jnp.int32, (r, r), 0)
        tj = lax.broadcasted_iota(jnp.int32, (r, r), 1)
        before = jnp.where(tj < ti, 1.0, 0.0).astype(BF16)
        base = pad_start + running[...]
        dest = jnp.zeros((r, LANES), F32)
        gw = jnp.zeros((r, LANES), F32)
        for k in range(TOP_K):
            rank = jnp.dot(before, ohf[k].astype(BF16), preferred_element_type=F32)
            d_k = jnp.sum(ohf[k] * (base + rank), axis=1, keepdims=True)
            dest = jnp.where(lane == k, d_k, dest)
            gw = jnp.where(lane == k, es[k] / den, gw)
            base = base + block_cnt[k]
        running[...] = base - pad_start
        dest_ref[...] = dest.astype(jnp.int32)
        gw_ref[...] = gw
        cnt_ref[...] = jnp.broadcast_to(cnt, cnt_ref.shape)


def _route(logits, n_experts):
    n = logits.shape[0]
    rows = 512 if n % 512 == 0 else 256
    assert n % rows == 0
    nb = n // rows
    tok = pl.BlockSpec((rows, LANES), lambda p, i: (i, 0))
    out = pl.BlockSpec((rows, LANES), lambda p, i: (i * p, 0))
    dest, gw, cnt = pl.pallas_call(
        _route_kernel,
        grid=(2, nb),
        in_specs=[tok],
        out_specs=[out, out, pl.BlockSpec((8, LANES), lambda p, i: (0, 0))],
        out_shape=[jax.ShapeDtypeStruct((n, LANES), jnp.int32), jax.ShapeDtypeStruct((n, LANES), F32),
                   jax.ShapeDtypeStruct((8, LANES), F32)],
        scratch_shapes=[pltpu.VMEM((1, LANES), F32), pltpu.VMEM((1, LANES), F32)],
        compiler_params=_cparams(("arbitrary", "arbitrary")),
        name="route",
    )(logits)
    counts = cnt[0, :n_experts].astype(jnp.int32)
    pad_end = jnp.cumsum((counts + MOE_ROWS - 1) // MOE_ROWS * MOE_ROWS)
    n_slots = n * TOP_K
    n_blocks = -(-n_slots // MOE_ROWS) + n_experts
    n_used = pad_end[-1] // MOE_ROWS
    blk = jnp.minimum(jnp.arange(n_blocks, dtype=jnp.int32), n_used - 1) * MOE_ROWS
    blk_e = jnp.minimum(jnp.sum(pad_end[None, :] <= blk[:, None], axis=1), n_experts - 1).astype(jnp.int32)
    dest4 = dest[:, :TOP_K]
    src = jnp.zeros((n_blocks * MOE_ROWS,), jnp.int32).at[dest4.reshape(-1)].set(
        jnp.arange(n_slots, dtype=jnp.int32) // TOP_K, unique_indices=True)
    return gw, dest4, src, blk_e, n_used.reshape(1).astype(jnp.int32)


def _log_sigmoid(x):
    return jnp.minimum(x, 0.0) - jnp.log(1.0 + jnp.exp(-jnp.abs(x)))


def _mlstm_kernel(mq_ref, gt_ref, gtt_ref, c0_ref, n0_ref, m0_ref, g_ref, mo_ref, c_ref, n_ref, m_ref):
    bb, L, _ = mq_ref.shape
    d = M_DIM

    @pl.when(pl.program_id(1) == 0)
    def _():
        c_ref[...] = c0_ref[...]
        n_ref[...] = n0_ref[...]
        m_ref[...] = m0_ref[...]

    tt = lax.broadcasted_iota(jnp.int32, (L, L), 0)
    ss = lax.broadcasted_iota(jnp.int32, (L, L), 1)
    causal = ss <= tt
    pairs = [(s, h) for s in range(bb) for h in range(M_HEADS)]
    col = lambda s, h, off: mq_ref[s, :, off + h * d:off + (h + 1) * d]
    old = {p: (c_ref[p[0], p[1]], n_ref[p[0], p[1]:p[1] + 1, :], m_ref[p[0], :, p[1]:p[1] + 1]) for p in pairs}

    qb = {p: col(*p, 0).astype(BF16) for p in pairs}
    kb = {p: (col(*p, M_WIDTH) * (d ** -0.5)).astype(BF16) for p in pairs}
    qk = {p: _dot_nt(qb[p], kb[p]) for p in pairs}
    qc = {p: _dot_nt(qb[p], old[p][0].astype(BF16)) for p in pairs}

    i_c = {(s, h): gt_ref[s][:, h:h + 1] for s, h in pairs}
    i_r = {(s, h): gtt_ref[s, 0][h:h + 1, :] for s, h in pairs}
    lf_c = {(s, h): _log_sigmoid(gt_ref[s][:, M_HEADS + h:M_HEADS + h + 1]) for s, h in pairs}
    lf_r = {(s, h): _log_sigmoid(gtt_ref[s, 0][M_HEADS + h:M_HEADS + h + 1, :]) for s, h in pairs}
    b_c = {p: jnp.sum(jnp.where(causal, lf_r[p], 0.0), axis=1, keepdims=True) for p in pairs}
    b_r = {p: jnp.sum(jnp.where(causal, 0.0, lf_c[p]), axis=0, keepdims=True) + lf_r[p] for p in pairs}

    dmat = {p: jnp.where(causal, b_c[p] - b_r[p] + i_r[p], _NEG) for p in pairs}
    dmax = {p: jnp.max(dmat[p], axis=1, keepdims=True) for p in pairs}
    m_t = {p: jnp.maximum(b_c[p] + old[p][2], dmax[p]) for p in pairs}
    inter = {p: jnp.exp(b_c[p] + old[p][2] - m_t[p]) for p in pairs}
    a = {p: jnp.exp(dmat[p] - m_t[p]) * qk[p] for p in pairs}
    av = {p: jnp.dot(a[p].astype(BF16), col(*p, 2 * M_WIDTH).astype(BF16), preferred_element_type=F32)
          for p in pairs}

    qn = {p: jnp.sum(col(*p, 0) * old[p][1], axis=1, keepdims=True) for p in pairs}
    asum = {p: jnp.sum(a[p], axis=1, keepdims=True) for p in pairs}
    hv = {p: (inter[p] * qc[p] + av[p]) / jnp.maximum(jnp.abs(inter[p] * qn[p] + asum[p]), jnp.exp(-m_t[p]))
          for p in pairs}
    mu = {p: jnp.mean(hv[p], axis=1, keepdims=True) for p in pairs}
    hc = {p: hv[p] - mu[p] for p in pairs}
    var = {p: jnp.mean(hc[p] * hc[p], axis=1, keepdims=True) for p in pairs}
    out = {(s, h): (hc[s, h] * lax.rsqrt(var[s, h] + LN_EPS) * g_ref[:, h * d:(h + 1) * d])
           * jax.nn.sigmoid(col(s, h, 3 * M_WIDTH)) for s, h in pairs}

    new = {}
    for p in pairs:
        c_old, n_old, m_old = old[p]
        k = col(*p, M_WIDTH) * (d ** -0.5)
        m_new = m_t[p][L - 1:L, :]
        b_last = b_c[p][L - 1:L, :]
        w_src = jnp.exp(b_last - b_c[p] + i_c[p] - m_new)
        w_old = jnp.exp(b_last + m_old - m_new)
        c_new = w_old * c_old + lax.dot_general(
            (w_src * col(*p, 2 * M_WIDTH)).astype(BF16), kb[p], (((0,), (0,)), ((), ())),
            preferred_element_type=F32)
        n_new = w_old * n_old + jnp.sum(w_src * k, axis=0, keepdims=True)
        new[p] = (c_new, n_new, m_new, out[p])
    for (s, h), (c_new, n_new, m_new, out) in new.items():
        c_ref[s, h] = c_new
        n_ref[s, h:h + 1, :] = n_new
        m_ref[s, :, h:h + 1] = m_new
        mo_ref[s, :, h * d:(h + 1) * d] = out


def _mlstm_call(mq, gt, C0, n0, m0, g, bb):
    B, T, _ = mq.shape
    L = math.gcd(T, M_CHUNK)
    nc = T // L
    gtt = gt[:, :, :2 * M_HEADS].reshape(B, nc, L, 2 * M_HEADS).transpose(0, 1, 3, 2)
    st4 = pl.BlockSpec((bb, M_HEADS, M_DIM, M_DIM), lambda i, c: (i, 0, 0, 0))
    st3 = pl.BlockSpec((bb, M_HEADS, M_DIM), lambda i, c: (i, 0, 0))
    st2 = pl.BlockSpec((bb, 1, M_HEADS), lambda i, c: (i, 0, 0))
    mo, C, n, m = pl.pallas_call(
        _mlstm_kernel,
        grid=(B // bb, nc),
        in_specs=[pl.BlockSpec((bb, L, 4 * M_WIDTH), lambda i, c: (i, c, 0)),
                  pl.BlockSpec((bb, L, LANES), lambda i, c: (i, c, 0)),
                  pl.BlockSpec((bb, 1, 2 * M_HEADS, L), lambda i, c: (i, c, 0, 0)),
                  st4, st3, st2,
                  pl.BlockSpec((1, M_WIDTH), lambda i, c: (0, 0))],
        out_specs=[pl.BlockSpec((bb, L, M_WIDTH), lambda i, c: (i, c, 0)), st4, st3, st2],
        out_shape=[jax.ShapeDtypeStruct((B, T, M_WIDTH), F32),
                   jax.ShapeDtypeStruct(C0.shape, F32), jax.ShapeDtypeStruct(n0.shape, F32),
                   jax.ShapeDtypeStruct((B, 1, M_HEADS), F32)],
        compiler_params=_cparams(("parallel", "arbitrary")),
        name="mlstm",
    )(mq, gt, gtt, C0, n0, m0.reshape(B, 1, M_HEADS), g.reshape(1, M_WIDTH))
    return mo, C, n, m.reshape(B, M_HEADS)


def _gelu_tanh(x):
    return x * (0.5 * (1.0 + jnp.tanh(math.sqrt(2.0 / math.pi) * (x + 0.044715 * (x * x * x)))))


def _compress_body(load, pe_ref, w1_ref, w2_ref, rows_total, nh, n_cmp):
    w = KV_WIDTH
    half = CMP_LEN // 2
    pa = jnp.zeros((rows_total, w), F32)
    pb = jnp.zeros((rows_total, w), F32)
    for j in range(half):
        xj = load(j)
        pa = pa + jnp.dot((xj + pe_ref[j:j + 1, :]).astype(BF16), w1_ref[j], preferred_element_type=F32)
        pb = pb + jnp.dot((xj + pe_ref[half + j:half + j + 1, :]).astype(BF16), w1_ref[half + j],
                          preferred_element_type=F32)
    hid = pa + pltpu.roll(pb, rows_total - 1, 0)
    y = jnp.dot(_gelu_tanh(hid).astype(BF16), w2_ref[...], preferred_element_type=F32)
    rows = lax.broadcasted_iota(jnp.int32, (rows_total, w), 0)
    assert nh & (nh - 1) == 0
    return jnp.where((rows & (nh - 1)) < n_cmp, y, 0.0)


def _compress_kernel(xk_ref, xv_ref, pe_ref, w1_ref, w2_ref, o_ref, *, n_cmp):
    nh = xk_ref.shape[1] // CMP_STRIDE
    for kv, x_ref in enumerate((xk_ref, xv_ref)):
        o_ref[0, :, kv * KV_WIDTH:(kv + 1) * KV_WIDTH] = _compress_body(
            lambda j: x_ref[0, pl.ds(j, nh, stride=CMP_STRIDE), :], pe_ref.at[kv], w1_ref.at[kv], w2_ref.at[kv],
            nh, nh, n_cmp)


def _compress_sample_kernel(pt_ref, pool_ref, pe_ref, w1_ref, w2_ref, o_ref, stage, xrow_k, xrow_v, sem, *,
                            seqs, n_pages, n_cmp):
    i = pl.program_id(0)
    slot = i % 2
    page = pool_ref.shape[2]
    nh = n_pages * page // CMP_STRIDE

    def copies(step, sl):
        return [pltpu.make_async_copy(pool_ref.at[pt_ref[step * seqs + s, p], pl.ds(kv * KV_WIDTH, KV_WIDTH), :],
                                      stage.at[sl, kv, s * n_pages + p], sem.at[sl])
                for s in range(seqs) for p in range(n_pages) for kv in range(2)]

    @pl.when(i == 0)
    def _():
        for cp in copies(0, 0):
            cp.start()

    @pl.when(i + 1 < pl.num_programs(0))
    def _():
        for cp in copies(i + 1, 1 - slot):
            cp.start()

    for cp in copies(i, slot):
        cp.wait()
    xrows = (xrow_k, xrow_v)
    for kv in range(2):
        for pi in range(seqs * n_pages):
            xrows[kv][pi * page:(pi + 1) * page, :] = stage[slot, kv, pi].T
    for kv in range(2):
        o_ref[:, kv * KV_WIDTH:(kv + 1) * KV_WIDTH] = _compress_body(
            lambda j: xrows[kv][pl.ds(j, seqs * nh, stride=CMP_STRIDE), :],
            pe_ref.at[kv], w1_ref.at[kv], w2_ref.at[kv], seqs * nh, nh, n_cmp)


def _compress_sample(pool, page_table, cmp_w, n_cmp, seqs):
    n_pool, w, page = pool.shape
    DB, n_pages = page_table.shape
    nh = n_pages * page // CMP_STRIDE
    pe4, w1, w2 = cmp_w
    full = lambda *s: pl.BlockSpec(s, lambda i, pt: (0,) * len(s))
    grid_spec = pltpu.PrefetchScalarGridSpec(
        num_scalar_prefetch=1,
        grid=(DB // seqs,),
        in_specs=[pl.BlockSpec(memory_space=pl.ANY), full(*pe4.shape), full(*w1.shape), full(*w2.shape)],
        out_specs=pl.BlockSpec((seqs * nh, w), lambda i, pt: (i, 0)),
        scratch_shapes=[pltpu.VMEM((2, 2, seqs * n_pages, KV_WIDTH, page), F32),
                        pltpu.VMEM((seqs * n_pages * page, KV_WIDTH), F32),
                        pltpu.VMEM((seqs * n_pages * page, KV_WIDTH), F32),
                        pltpu.SemaphoreType.DMA((2,))],
    )
    return pl.pallas_call(
        functools.partial(_compress_sample_kernel, seqs=seqs, n_pages=n_pages, n_cmp=n_cmp),
        grid_spec=grid_spec,
        out_shape=jax.ShapeDtypeStruct((DB * nh, w), F32),
        compiler_params=_cparams(("arbitrary",)),
        name="compress_sample",
    )(page_table, pool, pe4, w1, w2)


def _block_diag2(a):
    z = jnp.zeros_like(a)
    return jnp.concatenate([jnp.concatenate([a, z], -1), jnp.concatenate([z, a], -1)], -2)


def _compress_weights(cmp_params):
    pe_k, w1_k, w2_k, pe_v, w1_v, w2_v = cmp_params
    pe = jnp.stack([jnp.concatenate([p, p], axis=1) for p in (pe_k, pe_v)])
    w1 = jnp.stack([_block_diag2(w.reshape(CMP_LEN, A_DIM, A_DIM)) for w in (w1_k, w1_v)]).astype(BF16)
    w2 = jnp.stack([_block_diag2(w) for w in (w2_k, w2_v)]).astype(BF16)
    return pe, w1, w2


def _compress_prompt(cmp_rows, cmp_w):
    B, L, w = cmp_rows.shape
    nh = L // CMP_STRIDE
    n_cmp = (L - CMP_LEN) // CMP_STRIDE + 1
    pe4, w1, w2 = cmp_w
    full = lambda *s: pl.BlockSpec(s, lambda b: (0,) * len(s))
    return pl.pallas_call(
        functools.partial(_compress_kernel, n_cmp=n_cmp),
        grid=(B,),
        in_specs=[pl.BlockSpec((1, L, KV_WIDTH), lambda b: (b, 0, 0)),
                  pl.BlockSpec((1, L, KV_WIDTH), lambda b: (b, 0, 1)),
                  full(*pe4.shape), full(*w1.shape), full(*w2.shape)],
        out_specs=pl.BlockSpec((1, nh, w), lambda b: (b, 0, 0)),
        out_shape=jax.ShapeDtypeStruct((B, nh, w), F32),
        compiler_params=_cparams(("parallel",)),
        name="compress_prompt",
    )(cmp_rows, cmp_rows, pe4, w1, w2)


_NEG = -1e30
_NEG_SEL = -1e9
_SEL_CHUNK = 512


def _dot_nt(a, b):
    return lax.dot_general(a, b, (((1,), (1,)), ((), ())), preferred_element_type=F32)


def _softmax_rows(s, valid):
    s = jnp.where(valid[None], s, _NEG)
    m = jnp.max(s, axis=-1, keepdims=True)
    e = jnp.where(valid[None], jnp.exp(s - m), 0.0)
    return e / jnp.maximum(jnp.sum(e, axis=-1, keepdims=True), jnp.finfo(jnp.float32).tiny)


def _heads_to_rows(q):
    tq = q.shape[0]
    lane = lax.broadcasted_iota(jnp.int32, (tq, LANES), 1)
    q = q * (A_DIM ** -0.5)
    rows = []
    for hd in range(A_HEADS):
        g = hd // A_GROUP
        tile = q[:, LANES * (hd // 2):LANES * (hd // 2 + 1)]
        if hd % 2 != g:
            tile = pltpu.roll(tile, A_DIM, 1)
        keep = (lane < A_DIM) if g == 0 else (lane >= A_DIM)
        rows.append(jnp.where(keep, tile, 0.0))
    return jnp.concatenate(rows, axis=0).astype(BF16)


def _gate_rows_to_heads(gt, o_c, o_s, o_w):
    tq = gt.shape[0]
    lane = lax.broadcasted_iota(jnp.int32, (tq, LANES), 1)
    gs = jax.nn.sigmoid(gt)
    tiles = []
    for pair in range(A_HEADS // 2):
        g = (2 * pair) // A_GROUP
        mixed = []
        for hd in (2 * pair, 2 * pair + 1):
            c = 2 * M_HEADS + 3 * hd
            r = slice(hd * tq, (hd + 1) * tq)
            mixed.append(gs[:, c:c + 1] * o_c[r] + gs[:, c + 1:c + 2] * o_s[r] + gs[:, c + 2:c + 3] * o_w[r])
        a, b = mixed
        if g == 0:
            tiles.append(jnp.where(lane < A_DIM, a, pltpu.roll(b, A_DIM, 1)))
        else:
            tiles.append(jnp.where(lane < A_DIM, pltpu.roll(a, A_DIM, 1), b))
    return jnp.concatenate(tiles, axis=1)


def _attend_two(s_a, ok_a, v_a, s_b, ok_b, v_b, v_a_transposed=False):
    nh, tq = s_a.shape[:2]
    if ok_a is not None:
        s_a = jnp.where(ok_a[None], s_a, _NEG)
    s_b = jnp.where(ok_b[None], s_b, _NEG)
    m = jnp.maximum(jnp.max(s_a, axis=-1, keepdims=True), jnp.max(s_b, axis=-1, keepdims=True))
    e_a = jnp.exp(s_a - m)
    e_b = jnp.exp(s_b - m)
    l = jnp.sum(e_a, axis=-1, keepdims=True) + jnp.sum(e_b, axis=-1, keepdims=True)
    e_a = e_a.reshape(nh * tq, -1).astype(BF16)
    o_a = _dot_nt(e_a, v_a) if v_a_transposed else jnp.dot(e_a, v_a, preferred_element_type=F32)
    o = o_a + jnp.dot(e_b.reshape(nh * tq, -1).astype(BF16), v_b, preferred_element_type=F32)
    return o / l.reshape(nh * tq, 1)


def _nsa_sample_kernel(pt_ref, q_ref, gt_ref, kc_ref, vc_ref, seln_ref, wst_ref, wnew_ref, oh_ref, pool_ref,
                       o_ref, nwin_ref, selbuf, sem, *, past):
    b = pl.program_id(0)
    slot = b % 2
    tq = q_ref.shape[0]
    nh = A_HEADS
    n_pages = pt_ref.shape[1]
    page = pool_ref.shape[2]
    wbuf = wst_ref.shape[1]
    n_sel = -(-(past + tq) // SEL_LEN)

    def copies(seq, sl):
        return [pltpu.make_async_copy(pool_ref.at[pt_ref[seq, p]], selbuf.at[sl, :, pl.ds(p * page, page)],
                                      sem.at[sl]) for p in range(n_pages)]

    @pl.when(b == 0)
    def _():
        for cp in copies(0, 0):
            cp.start()

    @pl.when(b + 1 < pl.num_programs(0))
    def _():
        for cp in copies(b + 1, 1 - slot):
            cp.start()

    qz = _heads_to_rows(q_ref[...])
    tpos = past + lax.broadcasted_iota(jnp.int32, (tq, LANES), 0)

    kc = kc_ref[...].astype(BF16)
    vc = vc_ref[...].astype(BF16)
    ncp = kc.shape[0]
    s_c = _dot_nt(qz, kc).reshape(nh, tq, ncp)
    tp_c = past + lax.broadcasted_iota(jnp.int32, (tq, ncp), 0)
    nidx = lax.broadcasted_iota(jnp.int32, (tq, ncp), 1)
    p_c = _softmax_rows(s_c, nidx * CMP_STRIDE + (CMP_LEN - 1) <= tp_c)
    o_c = jnp.dot(p_c.reshape(nh * tq, ncp).astype(BF16), vc, preferred_element_type=F32)

    cn = lax.broadcasted_iota(jnp.int32, (ncp, LANES), 0) * CMP_STRIDE
    jn = lax.broadcasted_iota(jnp.int32, (ncp, LANES), 1) * SEL_LEN
    ov = jnp.where((cn < jn + SEL_LEN) & (cn + CMP_LEN > jn), 1.0, 0.0).astype(BF16)
    jb = lax.broadcasted_iota(jnp.int32, (tq, LANES), 1)
    cur = tpos // SEL_LEN
    forced = (jb == 0) | (jb == cur) | (jb == cur - 1)
    bias = []
    for g in range(A_KV_HEADS):
        ps = p_c[g * A_GROUP]
        for r in range(1, A_GROUP):
            ps = ps + p_c[g * A_GROUP + r]
        hi = ps.astype(BF16)
        lo = (ps - hi.astype(F32)).astype(BF16)
        score = (jnp.dot(hi, ov, preferred_element_type=F32) + jnp.dot(lo, ov, preferred_element_type=F32))
        score = jnp.where(forced, FORCE_SCORE, score)
        score = jnp.where(jb * SEL_LEN <= tpos, score, -1.0)
        cnt = jnp.zeros((tq, LANES), jnp.int32)
        for k in range(n_sel):
            sk = score[:, k:k + 1]
            ahead = (sk > score) | ((sk == score) & (jb > k))
            cnt = cnt + jnp.where(ahead, 1, 0)
        bias.append(jnp.where(cnt < min(SEL_TOP, n_sel), 0.0, _NEG_SEL).astype(BF16))
    q_aug = jnp.concatenate(
        [qz, jnp.concatenate([bias[hd // A_GROUP] for hd in range(nh)], axis=0)], axis=1)

    tw = lax.broadcasted_iota(jnp.int32, (tq, wbuf), 0)
    iw = lax.broadcasted_iota(jnp.int32, (tq, wbuf), 1)
    tn = lax.broadcasted_iota(jnp.int32, (tq, tq), 0)
    un = lax.broadcasted_iota(jnp.int32, (tq, tq), 1)
    wst = wst_ref[0]
    wnew = wnew_ref[...]
    s_wa = _dot_nt(qz, wst[:, :KV_WIDTH].astype(BF16)).reshape(nh, tq, wbuf)
    s_wb = _dot_nt(qz, wnew[:, :KV_WIDTH].astype(BF16)).reshape(nh, tq, tq)
    o_w = _attend_two(s_wa, (wbuf + tw - iw < WINDOW), wst[:, KV_WIDTH:].astype(BF16),
                      s_wb, un <= tn, wnew[:, KV_WIDTH:].astype(BF16))
    nwin_ref[0, :wbuf - tq, :] = wst[tq:, :]
    nwin_ref[0, wbuf - tq:, :] = wnew

    for cp in copies(b, slot):
        cp.wait()
    seln = seln_ref[...]
    k_aug_t = jnp.concatenate([selbuf[slot, :KV_WIDTH, :].astype(BF16), oh_ref[...]], axis=0)
    nblk = (past + lax.broadcasted_iota(jnp.int32, (tq, LANES), 0)) // SEL_LEN
    oh_new = jnp.where(nblk == lax.broadcasted_iota(jnp.int32, (tq, LANES), 1), 1.0, 0.0)
    kn_aug = jnp.concatenate([seln[:, :KV_WIDTH], oh_new], axis=1).astype(BF16)
    s_sa = jnp.dot(q_aug, k_aug_t, preferred_element_type=F32).reshape(nh, tq, past)
    s_sb = _dot_nt(q_aug, kn_aug).reshape(nh, tq, tq)
    o_s = _attend_two(s_sa, None, selbuf[slot, KV_WIDTH:, :].astype(BF16), s_sb, un <= tn,
                      seln[:, KV_WIDTH:].astype(BF16), v_a_transposed=True)

    o_ref[...] = _gate_rows_to_heads(gt_ref[...], o_c, o_s, o_w)


def _nsa_sample_call(qa, gt, comp, sel_new, win_state, win_new, sel_pool, page_table, T):
    DB, n_pages = page_table.shape
    page = sel_pool.shape[2]
    past = n_pages * page
    wbuf = win_state.shape[1]
    ncp = comp.shape[0] // DB
    w2 = 2 * KV_WIDTH
    assert past % SEL_LEN == 0 and T < CMP_STRIDE and T % 8 == 0 and wbuf == WINDOW and past >= WINDOW
    assert page % LANES == 0
    onehot = (jnp.arange(LANES)[:, None] == jnp.arange(past)[None, :] // SEL_LEN).astype(BF16)
    tok = lambda w: pl.BlockSpec((T, w), lambda b, pt: (b, 0))
    grid_spec = pltpu.PrefetchScalarGridSpec(
        num_scalar_prefetch=1,
        grid=(DB,),
        in_specs=[tok(A_WIDTH), tok(LANES),
                  pl.BlockSpec((ncp, KV_WIDTH), lambda b, pt: (b, 0)),
                  pl.BlockSpec((ncp, KV_WIDTH), lambda b, pt: (b, 1)),
                  tok(w2),
                  pl.BlockSpec((1, wbuf, w2), lambda b, pt: (b, 0, 0)),
                  tok(w2),
                  pl.BlockSpec((LANES, past), lambda b, pt: (0, 0)),
                  pl.BlockSpec(memory_space=pl.ANY)],
        out_specs=[tok(A_WIDTH), pl.BlockSpec((1, wbuf, w2), lambda b, pt: (b, 0, 0))],
        scratch_shapes=[pltpu.VMEM((2, w2, past), F32), pltpu.SemaphoreType.DMA((2,))],
    )
    return pl.pallas_call(
        functools.partial(_nsa_sample_kernel, past=past),
        grid_spec=grid_spec,
        out_shape=[jax.ShapeDtypeStruct((DB * T, A_WIDTH), F32), jax.ShapeDtypeStruct((DB, wbuf, w2), F32)],
        compiler_params=_cparams(("arbitrary",)),
        name="nsa_sample",
    )(page_table, qa, gt, comp, comp, sel_new, win_state, win_new, onehot, sel_pool)


def _nsa_prompt_t_kernel(q_ref, gt_ref, kc_ref, vc_ref, selk_ref, selvt_ref, wink_ref, winvt_ref, o_ref, *, seq):
    tq = Q_BLOCK
    nh = A_HEADS
    rows = nh * tq
    n_sel = seq // SEL_LEN
    kt = LANES
    s0 = pl.program_id(1) * tq
    qz = _heads_to_rows(q_ref[0])
    htile = lambda x, hd: x[:, hd * tq:(hd + 1) * tq]

    def qpos(n):
        return s0 + (lax.broadcasted_iota(jnp.int32, (n, rows), 1) & (tq - 1))

    def with_ones(vt):
        return jnp.concatenate([vt, jnp.ones((8, vt.shape[1]), F32)], axis=0).astype(BF16)

    def normalise(acc):
        return acc[:LANES] * (1.0 / acc[LANES:LANES + 1])

    kc = kc_ref[0].astype(BF16)
    ncp = kc.shape[0]
    s_c = _dot_nt(kc, qz)
    ok_c = lax.broadcasted_iota(jnp.int32, (ncp, rows), 0) * CMP_STRIDE + (CMP_LEN - 1) <= qpos(ncp)
    s_c = jnp.where(ok_c, s_c, _NEG)
    e_c = jnp.where(ok_c, jnp.exp(s_c - jnp.max(s_c, axis=0, keepdims=True)), 0.0)
    p_c = e_c * (1.0 / jnp.maximum(jnp.sum(e_c, axis=0, keepdims=True), jnp.finfo(jnp.float32).tiny))
    o_c = jnp.dot(vc_ref[0].T.astype(BF16), p_c.astype(BF16), preferred_element_type=F32)

    jn = lax.broadcasted_iota(jnp.int32, (n_sel, ncp), 0) * SEL_LEN
    cn = lax.broadcasted_iota(jnp.int32, (n_sel, ncp), 1) * CMP_STRIDE
    ov_t = jnp.where((cn < jn + SEL_LEN) & (cn + CMP_LEN > jn), 1.0, 0.0).astype(BF16)
    jb = lax.broadcasted_iota(jnp.int32, (n_sel, tq), 0)
    tp = s0 + lax.broadcasted_iota(jnp.int32, (n_sel, tq), 1)
    cur = tp // SEL_LEN
    forced = (jb == 0) | (jb == cur) | (jb == cur - 1)
    bias = []
    for g in range(A_KV_HEADS):
        mass = htile(p_c, g * A_GROUP)
        for r in range(1, A_GROUP):
            mass = mass + htile(p_c, g * A_GROUP + r)
        hi = mass.astype(BF16)
        lo = (mass - hi.astype(F32)).astype(BF16)
        score = (jnp.dot(ov_t, hi, preferred_element_type=F32) + jnp.dot(ov_t, lo, preferred_element_type=F32))
        score = jnp.where(forced, FORCE_SCORE, score)
        score = jnp.where(jb * SEL_LEN <= tp, score, -1.0)
        cnt = jnp.zeros((n_sel, tq), jnp.int32)
        for k in range(n_sel):
            rk = score[k:k + 1, :]
            ahead = (rk > score) | ((rk == score) & (jb > k))
            cnt = cnt + jnp.where(ahead, 1, 0)
        bias_t = jnp.where(cnt < min(SEL_TOP, n_sel), 0.0, _NEG_SEL)
        if n_sel < LANES:
            bias_t = jnp.concatenate([bias_t, jnp.zeros((LANES - n_sel, tq), F32)], axis=0)
        bias.append(bias_t.T.astype(BF16))
    q_aug = jnp.concatenate(
        [qz, jnp.concatenate([bias[hd // A_GROUP] for hd in range(nh)], axis=0)], axis=1)

    wk = WINDOW + tq
    w0 = pl.multiple_of(jnp.clip(s0 - WINDOW, 0, seq - wk), tq)
    s_w = _dot_nt(wink_ref[0, pl.ds(w0, wk), :].astype(BF16), qz)
    dist = qpos(wk) - (w0 + lax.broadcasted_iota(jnp.int32, (wk, rows), 0))
    s_w = jnp.where((dist >= 0) & (dist < WINDOW), s_w, _NEG)
    e_w = jnp.exp(s_w - jnp.max(s_w, axis=0, keepdims=True)).astype(BF16)
    vw_t = jnp.concatenate([winvt_ref[0, w0 // kt + i] for i in range(wk // kt)], axis=1)
    o_w = normalise(jnp.dot(with_ones(vw_t), e_w, preferred_element_type=F32))

    kc_n = _SEL_CHUNK

    def chunk(c, carry, causal):
        m, acc = carry
        k0 = pl.multiple_of(c * kc_n, kc_n)
        kblk = (k0 + lax.broadcasted_iota(jnp.int32, (kc_n, LANES), 0)) // SEL_LEN
        onehot = jnp.where(kblk == lax.broadcasted_iota(jnp.int32, (kc_n, LANES), 1), 1.0, 0.0)
        k_aug = jnp.concatenate([selk_ref[0, pl.ds(k0, kc_n), :], onehot], axis=1).astype(BF16)
        s = _dot_nt(k_aug, q_aug)
        if causal:
            s = jnp.where(k0 + lax.broadcasted_iota(jnp.int32, (kc_n, rows), 0) <= qpos(kc_n), s, _NEG_SEL)
        m_new = jnp.maximum(m, jnp.max(s, axis=0, keepdims=True))
        p = jnp.exp(s - m_new).astype(BF16)
        v_t = jnp.concatenate([selvt_ref[0, c * (kc_n // kt) + i] for i in range(kc_n // kt)], axis=1)
        acc = jnp.exp(m - m_new) * acc + jnp.dot(with_ones(v_t), p, preferred_element_type=F32)
        return m_new, acc

    n_ch = (s0 + tq - 1) // kc_n + 1
    init = (jnp.full((1, rows), _NEG, F32), jnp.zeros((LANES + 8, rows), F32))
    carry = lax.fori_loop(0, n_ch - 1, lambda c, cr: chunk(c, cr, False), init)
    o_s = normalise(chunk(n_ch - 1, carry, True)[1])

    gs_t = jax.nn.sigmoid(gt_ref[0]).T
    mixed = []
    for hd in range(nh):
        c = 2 * M_HEADS + 3 * hd
        g = hd // A_GROUP
        mix = (gs_t[c:c + 1, :] * htile(o_c, hd) + gs_t[c + 1:c + 2, :] * htile(o_s, hd)
               + gs_t[c + 2:c + 3, :] * htile(o_w, hd))
        mixed.append(mix[g * A_DIM:(g + 1) * A_DIM, :])
    o_ref[0] = jnp.concatenate(
        [jnp.concatenate([mixed[2 * pr], mixed[2 * pr + 1]], axis=0).T for pr in range(nh // 2)], axis=1)


def _nsa_prompt_t_call(qa, gt, comp, sel_k, sel_vt, win_k, win_vt):
    B, S, _ = qa.shape
    assert S % _SEL_CHUNK == 0 and S >= WINDOW + Q_BLOCK and S // SEL_LEN <= LANES
    ncp = comp.shape[1]
    keys = pl.BlockSpec((1, S, KV_WIDTH), lambda b, i: (b, 0, 0))
    vals = pl.BlockSpec((1, S // LANES, KV_WIDTH, LANES), lambda b, i: (b, 0, 0, 0))
    return pl.pallas_call(
        functools.partial(_nsa_prompt_t_kernel, seq=S),
        grid=(B, S // Q_BLOCK),
        in_specs=[pl.BlockSpec((1, Q_BLOCK, A_WIDTH), lambda b, i: (b, i, 0)),
                  pl.BlockSpec((1, Q_BLOCK, LANES), lambda b, i: (b, i, 0)),
                  pl.BlockSpec((1, ncp, KV_WIDTH), lambda b, i: (b, 0, 0)),
                  pl.BlockSpec((1, ncp, KV_WIDTH), lambda b, i: (b, 0, 1)),
                  keys, vals, keys, vals],
        out_specs=pl.BlockSpec((1, Q_BLOCK, A_WIDTH), lambda b, i: (b, i, 0)),
        out_shape=jax.ShapeDtypeStruct((B, S, A_WIDTH), F32),
        compiler_params=_cparams(("parallel", "arbitrary")),
        name="nsa_prompt",
    )(qa, gt, comp, comp, sel_k, sel_vt, win_k, win_vt)


def _moe(h2, logits, moe_w, n_experts):
    w_gu, b_gu, w_down, b_down = moe_w
    n, d = h2.shape
    gate_w, dest, src, blk_e, n_used = _route(logits, n_experts)
    yb = _ffn(h2[src], blk_e, n_used, w_gu, b_gu, w_down, b_down)
    return yb[dest.T], gate_w


def kernel(x_prompt, x_sample, cache_cmp, cache_sel, state_win, state_C, state_n, state_m, page_table,
           c_prompt, c_sample, w_ada, b_ada, w_in, b_in, m_norm_g, cmp_pe_k, cmp_w1_k, cmp_w2_k,
           cmp_pe_v, cmp_w1_v, cmp_w2_v, w_out, ln1_g, ln1_b, w_router, b_router, w_gu, b_gu,
           w_down, b_down, ln2_g, ln2_b):
    B, S, d = x_prompt.shape
    DB, T, _ = x_sample.shape
    depth = w_ada.shape[0]
    n_experts = w_router.shape[-1]
    alpha = (2 * depth) ** 0.25
    n_pool, page = cache_cmp.shape[1:3]
    past_len = page_table.shape[1] * page
    wbuf = state_win.shape[2]
    pos_p = jnp.arange(S)
    pos_s = past_len + jnp.arange(T)
    tt_p = min(S, 256)
    bb_s = min(DB, max(1, 256 // T))
    n_cmp_s = (past_len + T - CMP_LEN) // CMP_STRIDE + 1
    assert n_cmp_s < past_len // CMP_STRIDE
    r3 = lambda a: a.reshape(B, S, a.shape[-1])
    kv5 = lambda a, n, t: a.reshape(n, t, 2, A_KV_HEADS, A_DIM)
    y_prompt, y_sample = x_prompt, x_sample
    outs = [[] for _ in range(12)]
    for l in range(depth):
        cmp_l = (cmp_pe_k[l], cmp_w1_k[l], cmp_w2_k[l], cmp_pe_v[l], cmp_w1_v[l], cmp_w2_v[l])
        moe_w = (w_gu[l], b_gu[l], w_down[l], b_down[l])
        post1_w = (w_out[l], ln1_g[l], ln1_b[l], w_router[l], b_router[l])
        w_r, b_r = _prep_w_in(w_in[l], b_in[l])
        cmp_w = _compress_weights(cmp_l)
        c_all = jnp.concatenate([c_prompt, c_sample], axis=0)
        c_all = jnp.pad(c_all, ((0, -c_all.shape[0] % 8), (0, 0)))
        mod_all = _ada(c_all, w_ada[l], b_ada[l]).reshape(-1, 6, d)
        mod_p, mod_s = mod_all[:B], mod_all[B:B + DB]

        (mq, gt, qa, cmp_p, sel_p, win_p, cmp_t, sel_t, win_t, sel_vt, win_vt) = _pre(
            y_prompt, mod_p, pos_p, w_r, b_r, 1, tt_p, feature_major=True)
        zc = jnp.zeros((B, M_HEADS, M_DIM, M_DIM), F32)
        mo, C_p, n_p, m_p = _mlstm_call(r3(mq), r3(gt), zc, zc[..., 0], zc[..., 0, 0], m_norm_g[l],
                                        math.gcd(B, 4))
        comp = _compress_prompt(r3(cmp_p), cmp_w)
        ma = _nsa_prompt_t_call(r3(qa), r3(gt), comp, r3(sel_p), sel_vt, r3(win_p), win_vt)
        x1_p, h2_p, lg_p = _post1(mo.reshape(B * S, M_WIDTH), ma.reshape(B * S, A_WIDTH), y_prompt, mod_p,
                                  *post1_w, 1, tt_p, alpha)

        mq, gt, qa, cmp_s, sel_s, win_s = _pre(y_sample, mod_s, pos_s, w_r, b_r, bb_s, T)
        mo, C_s, n_s, m_s = _mlstm_call(mq.reshape(DB, T, -1), gt.reshape(DB, T, -1), state_C[l],
                                        state_n[l], state_m[l], m_norm_g[l], math.gcd(DB, 4))
        feature_major = lambda pool: pool.transpose(0, 2, 3, 4, 1).reshape(n_pool, 2 * KV_WIDTH, page)
        comp = _compress_sample(feature_major(cache_cmp[l]), page_table, cmp_w, n_cmp_s, math.gcd(DB, 4))
        ma, new_win = _nsa_sample_call(qa, gt, comp, sel_s, state_win[l].reshape(DB, wbuf, 2 * KV_WIDTH),
                                       win_s, feature_major(cache_sel[l]), page_table, T)
        x1_s, h2_s, lg_s = _post1(mo.reshape(DB * T, M_WIDTH), ma, y_sample, mod_s, *post1_w, bb_s, T, alpha)

        yg, gate_w = _moe(jnp.concatenate([h2_p, h2_s], axis=0), jnp.concatenate([lg_p, lg_s], axis=0),
                          moe_w, n_experts)
        y_prompt = _post2(yg, gate_w, 0, x1_p, mod_p, ln2_g[l], ln2_b[l], 1, tt_p, alpha)
        y_sample = _post2(yg, gate_w, B * S, x1_s, mod_s, ln2_g[l], ln2_b[l], bb_s, T, alpha)

        kv5_t = lambda a: a.reshape(B, 2, A_KV_HEADS, A_DIM, -1).transpose(0, 4, 1, 2, 3)
        new = (kv5_t(cmp_t), kv5_t(sel_t), kv5_t(win_t[:, :, -min(WINDOW, S):]), C_p, n_p, m_p,
               kv5(cmp_s, DB, T), kv5(sel_s, DB, T), new_win.reshape(state_win.shape[1:]), C_s, n_s, m_s)
        for lst, v in zip(outs, new):
            lst.append(v)
    return (y_prompt, y_sample, *[jnp.stack(v) for v in outs])
```

```python
import functools
import math

import numpy as np
import jax
import jax.numpy as jnp
from jax import lax
from jax.experimental import pallas as pl
from jax.experimental.pallas import tpu as pltpu

F32 = jnp.float32
BF16 = jnp.bfloat16

M_HEADS = 4
M_DIM = 128
M_WIDTH = M_HEADS * M_DIM
M_CHUNK = 64
A_HEADS = 8
A_KV_HEADS = 2
A_GROUP = A_HEADS // A_KV_HEADS
A_DIM = 64
A_WIDTH = A_HEADS * A_DIM
KV_WIDTH = A_KV_HEADS * A_DIM
CMP_LEN = 32
CMP_STRIDE = 16
SEL_LEN = 64
SEL_TOP = 16
WINDOW = 512
Q_BLOCK = 256
FORCE_SCORE = 1.0e4
ROPE_THETA = 500000.0
ROPE_DIM = A_DIM // 4
TOP_K = 4
SWIGLU_LIMIT = 7.0
SWIGLU_ALPHA = 1.702
LN_EPS = 1e-5
LANES = 128
MOE_ROWS = 512
VMEM_LIMIT = 56 * 1024 * 1024

_O_MQ = 0
_O_IF = 4 * M_WIDTH
_O_QA = _O_IF + 2 * M_HEADS
_O_KV = _O_QA + A_WIDTH
_O_GA = _O_KV + 6 * KV_WIDTH
_N_IN = _O_GA + 3 * A_HEADS
_R_QA = 4 * M_WIDTH
_R_KV = _R_QA + A_WIDTH
_R_GT = _R_KV + 6 * KV_WIDTH
_R_END = _R_GT + LANES


def _cparams(sem):
    return pltpu.CompilerParams(dimension_semantics=sem, vmem_limit_bytes=VMEM_LIMIT)


def _ln_core(x):
    mu = jnp.mean(x, axis=-1, keepdims=True)
    xc = x - mu
    var = jnp.mean(xc * xc, axis=-1, keepdims=True)
    return xc * lax.rsqrt(var + LN_EPS)


def _ada_kernel(c_ref, w_ref, b_ref, o_ref):
    o_ref[...] = jnp.dot(c_ref[...].astype(BF16), w_ref[...].astype(BF16),
                         preferred_element_type=F32) + b_ref[...]


def _ada(c, w_ada, b_ada):
    n, d = c.shape
    cols = w_ada.shape[1]
    return pl.pallas_call(
        _ada_kernel,
        grid=(cols // d,),
        in_specs=[pl.BlockSpec((n, d), lambda j: (0, 0)),
                  pl.BlockSpec((d, d), lambda j: (0, j)),
                  pl.BlockSpec((1, d), lambda j: (0, j))],
        out_specs=pl.BlockSpec((n, d), lambda j: (0, j)),
        out_shape=jax.ShapeDtypeStruct((n, cols), F32),
        compiler_params=_cparams(("arbitrary",)),
        name="ada",
    )(c, w_ada, b_ada.reshape(1, cols))


def _rope_apply(v, cos, sa, sb):
    reps = v.shape[1] // LANES
    tile = lambda t: t if reps == 1 else jnp.concatenate([t] * reps, axis=1)
    w = v.shape[1]
    return (v * tile(cos) + pltpu.roll(v, w - ROPE_DIM // 2, 1) * tile(sa)
            + pltpu.roll(v, ROPE_DIM // 2, 1) * tile(sb))


def _pre_kernel(x_ref, mod_ref, cos_ref, sa_ref, sb_ref, w_ref, b_ref,
                mq_ref, gt_ref, qa_ref, cmp_ref, sel_ref, win_ref, *t_refs):
    bb, tt, d = x_ref.shape
    mod = mod_ref[...]
    h = _ln_core(x_ref[...]) * (1.0 + mod[:, 1:2, :]) + mod[:, 0:1, :]
    h = h.reshape(bb * tt, d).astype(BF16)
    z = jnp.dot(h, w_ref[...], preferred_element_type=F32) + b_ref[...]
    cos, sa, sb = cos_ref[...], sa_ref[...], sb_ref[...]
    mq_ref[...] = z[:, :_R_QA]
    gt_ref[...] = z[:, _R_GT:_R_END]
    qa_ref[...] = _rope_apply(z[:, _R_QA:_R_KV], cos, sa, sb)
    for n, ref in enumerate((cmp_ref, sel_ref, win_ref)):
        o = _R_KV + 2 * KV_WIDTH * n
        kv = jnp.concatenate([_rope_apply(z[:, o:o + KV_WIDTH], cos, sa, sb),
                              z[:, o + KV_WIDTH:o + 2 * KV_WIDTH]], axis=1)
        ref[...] = kv
        if t_refs:
            kv_t = kv.T
            t_refs[n][0] = kv_t
            if n > 0:
                for c in range(kv_t.shape[1] // LANES):
                    t_refs[2 + n][0, c] = kv_t[KV_WIDTH:, c * LANES:(c + 1) * LANES]


def _rope_tables(pos):
    half = ROPE_DIM // 2
    inv = ROPE_THETA ** (-jnp.arange(0, ROPE_DIM, 2, dtype=F32) / ROPE_DIM)
    ang = pos.astype(F32)[:, None] * inv[None, :]
    cos, sin = jnp.cos(ang), jnp.sin(ang)
    n = pos.shape[0]
    one = jnp.ones((n, A_DIM - ROPE_DIM), F32)
    zero = jnp.zeros((n, A_DIM - ROPE_DIM), F32)
    zh = jnp.zeros((n, half), F32)
    cos_t = jnp.concatenate([cos, cos, one], axis=1)
    sa_t = jnp.concatenate([-sin, zh, zero], axis=1)
    sb_t = jnp.concatenate([zh, sin, zero], axis=1)
    two = lambda t: jnp.concatenate([t, t], axis=1)
    return two(cos_t), two(sa_t), two(sb_t)


def _prep_w_in(w_in, b_in):
    pad = LANES - 2 * M_HEADS - 3 * A_HEADS
    cat = lambda a: jnp.concatenate(
        [a[..., _O_MQ:_O_IF], a[..., _O_QA:_O_GA], a[..., _O_IF:_O_QA], a[..., _O_GA:_N_IN],
         jnp.zeros(a.shape[:-1] + (pad,), a.dtype)], axis=-1)
    return cat(w_in).astype(BF16), cat(b_in[None, :])


def _pre(x, mod, pos, w_r, b_r, bb, tt, feature_major=False):
    B, T, d = x.shape
    nt = T // tt
    rows = bb * tt
    cos, sa, sb = _rope_tables(pos)
    if bb > 1:
        cos, sa, sb = (jnp.tile(t, (bb, 1)) for t in (cos, sa, sb))
    n_tok = B * T
    tab = pl.BlockSpec((rows, LANES), lambda i, j: (j, 0))
    row = lambda w: pl.BlockSpec((rows, w), lambda i, j: (i * nt + j, 0))
    widths = (_R_QA, LANES, A_WIDTH, 2 * KV_WIDTH, 2 * KV_WIDTH, 2 * KV_WIDTH)
    out_specs = [row(w) for w in widths]
    out_shape = [jax.ShapeDtypeStruct((n_tok, w), F32) for w in widths]
    if feature_major:
        assert bb == 1 and tt % LANES == 0
        out_specs += [pl.BlockSpec((1, 2 * KV_WIDTH, tt), lambda i, j: (i, 0, j))] * 3
        out_shape += [jax.ShapeDtypeStruct((B, 2 * KV_WIDTH, T), F32)] * 3
        out_specs += [pl.BlockSpec((1, tt // LANES, KV_WIDTH, LANES), lambda i, j: (i, j, 0, 0))] * 2
        out_shape += [jax.ShapeDtypeStruct((B, T // LANES, KV_WIDTH, LANES), F32)] * 2
    return pl.pallas_call(
        _pre_kernel,
        grid=(B // bb, nt),
        in_specs=[pl.BlockSpec((bb, tt, d), lambda i, j: (i, j, 0)),
                  pl.BlockSpec((bb, 6, d), lambda i, j: (i, 0, 0)),
                  tab, tab, tab,
                  pl.BlockSpec((d, _R_END), lambda i, j: (0, 0)),
                  pl.BlockSpec((1, _R_END), lambda i, j: (0, 0))],
        out_specs=out_specs,
        out_shape=out_shape,
        compiler_params=_cparams(("parallel", "arbitrary")),
        name="pre",
    )(x, mod, cos, sa, sb, w_r, b_r)


def _post1_kernel(mo_ref, ao_ref, x_ref, mod_ref, wo_ref, g_ref, b_ref, wr_ref, br_ref,
                  x1_ref, h2_ref, lg_ref, *, alpha):
    bb, tt, d = x_ref.shape
    mod = mod_ref[...]
    mixin = jnp.concatenate([mo_ref[...], ao_ref[...]], axis=1).astype(BF16)
    mix = jnp.dot(mixin, wo_ref[...], preferred_element_type=F32).reshape(bb, tt, d)
    x1 = _ln_core(alpha * x_ref[...] + mod[:, 2:3, :] * mix) * g_ref[...] + b_ref[...]
    h2 = _ln_core(x1) * (1.0 + mod[:, 4:5, :]) + mod[:, 3:4, :]
    x1_ref[...] = x1
    h2f = h2.reshape(bb * tt, d)
    h2_ref[...] = h2f
    hi = h2f.astype(BF16)
    lo = (h2f - hi.astype(F32)).astype(BF16)
    lg_ref[...] = (jnp.dot(hi, wr_ref[0], preferred_element_type=F32)
                   + jnp.dot(hi, wr_ref[1], preferred_element_type=F32)
                   + jnp.dot(lo, wr_ref[0], preferred_element_type=F32)) + br_ref[...]


def _post1(mo, ao, x, mod, w_out, ln_g, ln_b, w_router, b_router, bb, tt, alpha):
    B, T, d = x.shape
    nt = T // tt
    rows = bb * tt
    ne = w_router.shape[1]
    wr = jnp.pad(w_router, ((0, 0), (0, LANES - ne)))
    wr_hi = wr.astype(BF16)
    wr = jnp.stack([wr_hi, (wr - wr_hi.astype(F32)).astype(BF16)])
    br = jnp.pad(b_router, (0, LANES - ne), constant_values=-jnp.inf).reshape(1, LANES)
    full = lambda *s: pl.BlockSpec(s, lambda i, j: (0,) * len(s))
    row = lambda w: pl.BlockSpec((rows, w), lambda i, j: (i * nt + j, 0))
    return pl.pallas_call(
        functools.partial(_post1_kernel, alpha=alpha),
        grid=(B // bb, nt),
        in_specs=[row(M_WIDTH), row(A_WIDTH),
                  pl.BlockSpec((bb, tt, d), lambda i, j: (i, j, 0)),
                  pl.BlockSpec((bb, 6, d), lambda i, j: (i, 0, 0)),
                  full(M_WIDTH + A_WIDTH, d), full(1, d), full(1, d), full(2, d, LANES), full(1, LANES)],
        out_specs=[pl.BlockSpec((bb, tt, d), lambda i, j: (i, j, 0)), row(d), row(LANES)],
        out_shape=[jax.ShapeDtypeStruct((B, T, d), F32),
                   jax.ShapeDtypeStruct((B * T, d), F32),
                   jax.ShapeDtypeStruct((B * T, LANES), F32)],
        compiler_params=_cparams(("parallel", "arbitrary")),
        name="post1",
    )(mo, ao, x, mod, w_out.astype(BF16), ln_g.reshape(1, d), ln_b.reshape(1, d), wr, br)


def _ffn_kernel(be_ref, nu_ref, x_ref, wgu_ref, bgu_ref, wd_ref, bd_ref, y_ref, wgu_bf, wd_bf):
    i = pl.program_id(0)
    dff = wd_ref.shape[1]

    @pl.when((i == 0) | (be_ref[i] != be_ref[jnp.maximum(i - 1, 0)]))
    def _():
        wgu_bf[...] = wgu_ref[0].astype(BF16)
        wd_bf[...] = wd_ref[0].astype(BF16)

    @pl.when(i < nu_ref[0])
    def _():
        half = x_ref.shape[0] // 2
        halves = [slice(h * half, (h + 1) * half) for h in range(2)]
        gus = [jnp.dot(x_ref[r, :].astype(BF16), wgu_bf[...], preferred_element_type=F32) + bgu_ref[0]
               for r in halves]
        acts = []
        for gu in gus:
            g = jnp.minimum(gu[:, :dff], SWIGLU_LIMIT)
            u = jnp.clip(gu[:, dff:], -SWIGLU_LIMIT, SWIGLU_LIMIT)
            acts.append(((u + 1.0) * (g * jax.nn.sigmoid(SWIGLU_ALPHA * g))).astype(BF16))
        for r, act in zip(halves, acts):
            y_ref[r, :] = (jnp.dot(act, wd_bf[...], preferred_element_type=F32) + bd_ref[0]).astype(BF16)

    @pl.when(i >= nu_ref[0])
    def _():
        y_ref[...] = jnp.zeros_like(y_ref)


def _ffn(xb, blk_e, n_used, w_gu, b_gu, w_down, b_down):
    rows, d = xb.shape
    ne, _, f2 = w_gu.shape
    dff = w_down.shape[1]
    nb = rows // MOE_ROWS
    grid_spec = pltpu.PrefetchScalarGridSpec(
        num_scalar_prefetch=2,
        grid=(nb,),
        in_specs=[pl.BlockSpec((MOE_ROWS, d), lambda i, be, nu: (i, 0)),
                  pl.BlockSpec((1, d, f2), lambda i, be, nu: (be[i], 0, 0)),
                  pl.BlockSpec((1, 1, f2), lambda i, be, nu: (be[i], 0, 0)),
                  pl.BlockSpec((1, dff, d), lambda i, be, nu: (be[i], 0, 0)),
                  pl.BlockSpec((1, 1, d), lambda i, be, nu: (be[i], 0, 0))],
        out_specs=pl.BlockSpec((MOE_ROWS, d), lambda i, be, nu: (i, 0)),
        scratch_shapes=[pltpu.VMEM((d, f2), BF16), pltpu.VMEM((dff, d), BF16)],
    )
    return pl.pallas_call(
        _ffn_kernel,
        grid_spec=grid_spec,
        out_shape=jax.ShapeDtypeStruct((rows, d), BF16),
        compiler_params=_cparams(("arbitrary",)),
        name="ffn",
    )(blk_e, n_used, xb, w_gu, b_gu.reshape(ne, 1, f2), w_down, b_down.reshape(ne, 1, d))


def _post2_kernel(yg_ref, gw_ref, x1_ref, mod_ref, g_ref, b_ref, y_ref, *, alpha):
    bb, tt, d = x1_ref.shape
    gw = gw_ref[...]
    f = yg_ref[0].astype(F32) * gw[:, 0:1]
    for k in range(1, TOP_K):
        f = f + yg_ref[k].astype(F32) * gw[:, k:k + 1]
    y = alpha * x1_ref[...] + mod_ref[...][:, 5:6, :] * f.reshape(bb, tt, d)
    y_ref[...] = _ln_core(y) * g_ref[...] + b_ref[...]


def _post2(yg, gw, row0, x1, mod, ln_g, ln_b, bb, tt, alpha):
    B, T, d = x1.shape
    rows = bb * tt
    nt = T // tt
    blk0 = row0 // rows
    assert row0 % rows == 0
    full = lambda *s: pl.BlockSpec(s, lambda i, j: (0,) * len(s))
    return pl.pallas_call(
        functools.partial(_post2_kernel, alpha=alpha),
        grid=(B // bb, nt),
        in_specs=[pl.BlockSpec((TOP_K, rows, d), lambda i, j: (0, blk0 + i * nt + j, 0)),
                  pl.BlockSpec((rows, LANES), lambda i, j: (blk0 + i * nt + j, 0)),
                  pl.BlockSpec((bb, tt, d), lambda i, j: (i, j, 0)),
                  pl.BlockSpec((bb, 6, d), lambda i, j: (i, 0, 0)),
                  full(1, d), full(1, d)],
        out_specs=pl.BlockSpec((bb, tt, d), lambda i, j: (i, j, 0)),
        out_shape=jax.ShapeDtypeStruct((B, T, d), F32),
        compiler_params=_cparams(("parallel", "arbitrary")),
        name="post2",
    )(yg, gw, x1, mod, ln_g.reshape(1, d), ln_b.reshape(1, d))


def _lane_prefix_sum(x):
    lane = lax.broadcasted_iota(jnp.int32, x.shape, 1)
    s = 1
    while s < LANES:
        x = x + jnp.where(lane >= s, pltpu.roll(x, s, 1), 0.0)
        s *= 2
    return x


def _route_kernel(lg_ref, dest_ref, gw_ref, cnt_ref, counts, running):
    phase = pl.program_id(0)
    blk = pl.program_id(1)
    r = lg_ref.shape[0]
    lane = lax.broadcasted_iota(jnp.int32, (r, LANES), 1)

    @pl.when((phase == 0) & (blk == 0))
    def _():
        counts[...] = jnp.zeros_like(counts)

    @pl.when((phase == 1) & (blk == 0))
    def _():
        running[...] = jnp.zeros_like(running)

    vals = lg_ref[...]
    onehots, tops = [], []
    for k in range(TOP_K):
        m = jnp.max(vals, axis=1, keepdims=True)
        idx = jnp.min(jnp.where(vals == m, lane, LANES), axis=1, keepdims=True)
        hit = lane == idx
        onehots.append(hit)
        tops.append(m)
        vals = jnp.where(hit, -jnp.inf, vals)
    ohf = [jnp.where(h, 1.0, 0.0) for h in onehots]
    block_cnt = [jnp.sum(o, axis=0, keepdims=True) for o in ohf]

    @pl.when(phase == 0)
    def _():
        counts[...] += block_cnt[0] + block_cnt[1] + block_cnt[2] + block_cnt[3]

    @pl.when(phase == 1)
    def _():
        cnt = counts[...]
        padded = jnp.ceil(cnt * (1.0 / MOE_ROWS)) * MOE_ROWS
        pad_start = _lane_prefix_sum(padded) - padded
        es = [jnp.exp(t - tops[0]) for t in tops]
        den = es[0] + es[1] + es[2] + es[3]
        ti = lax.broadcasted_iota(jnp.int32, (r, r), 0)
        tj = lax.broadcasted_iota(jnp.int32, (r, r), 1)
        before = jnp.where(tj < ti, 1.0, 0.0).astype(BF16)
        base = pad_start + running[...]
        dest = jnp.zeros((r, LANES), F32)
        gw = jnp.zeros((r, LANES), F32)
        for k in range(TOP_K):
            rank = jnp.dot(before, ohf[k].astype(BF16), preferred_element_type=F32)
            d_k = jnp.sum(ohf[k] * (base + rank), axis=1, keepdims=True)
            dest = jnp.where(lane == k, d_k, dest)
            gw = jnp.where(lane == k, es[k] / den, gw)
            base = base + block_cnt[k]
        running[...] = base - pad_start
        dest_ref[...] = dest.astype(jnp.int32)
        gw_ref[...] = gw
        cnt_ref[...] = jnp.broadcast_to(cnt, cnt_ref.shape)


def _route(logits, n_experts):
    n = logits.shape[0]
    rows = 512 if n % 512 == 0 else 256
    assert n % rows == 0
    nb = n // rows
    tok = pl.BlockSpec((rows, LANES), lambda p, i: (i, 0))
    out = pl.BlockSpec((rows, LANES), lambda p, i: (i * p, 0))
    dest, gw, cnt = pl.pallas_call(
        _route_kernel,
        grid=(2, nb),
        in_specs=[tok],
        out_specs=[out, out, pl.BlockSpec((8, LANES), lambda p, i: (0, 0))],
        out_shape=[jax.ShapeDtypeStruct((n, LANES), jnp.int32), jax.ShapeDtypeStruct((n, LANES), F32),
                   jax.ShapeDtypeStruct((8, LANES), F32)],
        scratch_shapes=[pltpu.VMEM((1, LANES), F32), pltpu.VMEM((1, LANES), F32)],
        compiler_params=_cparams(("arbitrary", "arbitrary")),
        name="route",
    )(logits)
    counts = cnt[0, :n_experts].astype(jnp.int32)
    pad_end = jnp.cumsum((counts + MOE_ROWS - 1) // MOE_ROWS * MOE_ROWS)
    n_slots = n * TOP_K
    n_blocks = -(-n_slots // MOE_ROWS) + n_experts
    n_used = pad_end[-1] // MOE_ROWS
    blk = jnp.minimum(jnp.arange(n_blocks, dtype=jnp.int32), n_used - 1) * MOE_ROWS
    blk_e = jnp.minimum(jnp.sum(pad_end[None, :] <= blk[:, None], axis=1), n_experts - 1).astype(jnp.int32)
    dest4 = dest[:, :TOP_K]
    src = jnp.zeros((n_blocks * MOE_ROWS,), jnp.int32).at[dest4.reshape(-1)].set(
        jnp.arange(n_slots, dtype=jnp.int32) // TOP_K, unique_indices=True)
    return gw, dest4, src, blk_e, n_used.reshape(1).astype(jnp.int32)


def _log_sigmoid(x):
    return jnp.minimum(x, 0.0) - jnp.log(1.0 + jnp.exp(-jnp.abs(x)))


def _mlstm_kernel(mq_ref, gt_ref, gtt_ref, c0_ref, n0_ref, m0_ref, g_ref, mo_ref, c_ref, n_ref, m_ref):
    bb, L, _ = mq_ref.shape
    d = M_DIM

    @pl.when(pl.program_id(1) == 0)
    def _():
        c_ref[...] = c0_ref[...]
        n_ref[...] = n0_ref[...]
        m_ref[...] = m0_ref[...]

    tt = lax.broadcasted_iota(jnp.int32, (L, L), 0)
    ss = lax.broadcasted_iota(jnp.int32, (L, L), 1)
    causal = ss <= tt
    pairs = [(s, h) for s in range(bb) for h in range(M_HEADS)]
    col = lambda s, h, off: mq_ref[s, :, off + h * d:off + (h + 1) * d]
    old = {p: (c_ref[p[0], p[1]], n_ref[p[0], p[1]:p[1] + 1, :], m_ref[p[0], :, p[1]:p[1] + 1]) for p in pairs}

    qb = {p: col(*p, 0).astype(BF16) for p in pairs}
    kb = {p: (col(*p, M_WIDTH) * (d ** -0.5)).astype(BF16) for p in pairs}
    qk = {p: _dot_nt(qb[p], kb[p]) for p in pairs}
    qc = {p: _dot_nt(qb[p], old[p][0].astype(BF16)) for p in pairs}

    i_c = {(s, h): gt_ref[s][:, h:h + 1] for s, h in pairs}
    i_r = {(s, h): gtt_ref[s, 0][h:h + 1, :] for s, h in pairs}
    lf_c = {(s, h): _log_sigmoid(gt_ref[s][:, M_HEADS + h:M_HEADS + h + 1]) for s, h in pairs}
    lf_r = {(s, h): _log_sigmoid(gtt_ref[s, 0][M_HEADS + h:M_HEADS + h + 1, :]) for s, h in pairs}
    b_c = {p: jnp.sum(jnp.where(causal, lf_r[p], 0.0), axis=1, keepdims=True) for p in pairs}
    b_r = {p: jnp.sum(jnp.where(causal, 0.0, lf_c[p]), axis=0, keepdims=True) + lf_r[p] for p in pairs}

    dmat = {p: jnp.where(causal, b_c[p] - b_r[p] + i_r[p], _NEG) for p in pairs}
    dmax = {p: jnp.max(dmat[p], axis=1, keepdims=True) for p in pairs}
    m_t = {p: jnp.maximum(b_c[p] + old[p][2], dmax[p]) for p in pairs}
    inter = {p: jnp.exp(b_c[p] + old[p][2] - m_t[p]) for p in pairs}
    a = {p: jnp.exp(dmat[p] - m_t[p]) * qk[p] for p in pairs}
    av = {p: jnp.dot(a[p].astype(BF16), col(*p, 2 * M_WIDTH).astype(BF16), preferred_element_type=F32)
          for p in pairs}

    qn = {p: jnp.sum(col(*p, 0) * old[p][1], axis=1, keepdims=True) for p in pairs}
    asum = {p: jnp.sum(a[p], axis=1, keepdims=True) for p in pairs}
    hv = {p: (inter[p] * qc[p] + av[p]) / jnp.maximum(jnp.abs(inter[p] * qn[p] + asum[p]), jnp.exp(-m_t[p]))
          for p in pairs}
    mu = {p: jnp.mean(hv[p], axis=1, keepdims=True) for p in pairs}
    hc = {p: hv[p] - mu[p] for p in pairs}
    var = {p: jnp.mean(hc[p] * hc[p], axis=1, keepdims=True) for p in pairs}
    out = {(s, h): (hc[s, h] * lax.rsqrt(var[s, h] + LN_EPS) * g_ref[:, h * d:(h + 1) * d])
           * jax.nn.sigmoid(col(s, h, 3 * M_WIDTH)) for s, h in pairs}

    new = {}
    for p in pairs:
        c_old, n_old, m_old = old[p]
        k = col(*p, M_WIDTH) * (d ** -0.5)
        m_new = m_t[p][L - 1:L, :]
        b_last = b_c[p][L - 1:L, :]
        w_src = jnp.exp(b_last - b_c[p] + i_c[p] - m_new)
        w_old = jnp.exp(b_last + m_old - m_new)
        c_new = w_old * c_old + lax.dot_general(
            (w_src * col(*p, 2 * M_WIDTH)).astype(BF16), kb[p], (((0,), (0,)), ((), ())),
            preferred_element_type=F32)
        n_new = w_old * n_old + jnp.sum(w_src * k, axis=0, keepdims=True)
        new[p] = (c_new, n_new, m_new, out[p])
    for (s, h), (c_new, n_new, m_new, out) in new.items():
        c_ref[s, h] = c_new
        n_ref[s, h:h + 1, :] = n_new
        m_ref[s, :, h:h + 1] = m_new
        mo_ref[s, :, h * d:(h + 1) * d] = out


def _mlstm_call(mq, gt, C0, n0, m0, g, bb):
    B, T, _ = mq.shape
    L = math.gcd(T, M_CHUNK)
    nc = T // L
    gtt = gt[:, :, :2 * M_HEADS].reshape(B, nc, L, 2 * M_HEADS).transpose(0, 1, 3, 2)
    st4 = pl.BlockSpec((bb, M_HEADS, M_DIM, M_DIM), lambda i, c: (i, 0, 0, 0))
    st3 = pl.BlockSpec((bb, M_HEADS, M_DIM), lambda i, c: (i, 0, 0))
    st2 = pl.BlockSpec((bb, 1, M_HEADS), lambda i, c: (i, 0, 0))
    mo, C, n, m = pl.pallas_call(
        _mlstm_kernel,
        grid=(B // bb, nc),
        in_specs=[pl.BlockSpec((bb, L, 4 * M_WIDTH), lambda i, c: (i, c, 0)),
                  pl.BlockSpec((bb, L, LANES), lambda i, c: (i, c, 0)),
                  pl.BlockSpec((bb, 1, 2 * M_HEADS, L), lambda i, c: (i, c, 0, 0)),
                  st4, st3, st2,
                  pl.BlockSpec((1, M_WIDTH), lambda i, c: (0, 0))],
        out_specs=[pl.BlockSpec((bb, L, M_WIDTH), lambda i, c: (i, c, 0)), st4, st3, st2],
        out_shape=[jax.ShapeDtypeStruct((B, T, M_WIDTH), F32),
                   jax.ShapeDtypeStruct(C0.shape, F32), jax.ShapeDtypeStruct(n0.shape, F32),
                   jax.ShapeDtypeStruct((B, 1, M_HEADS), F32)],
        compiler_params=_cparams(("parallel", "arbitrary")),
        name="mlstm",
    )(mq, gt, gtt, C0, n0, m0.reshape(B, 1, M_HEADS), g.reshape(1, M_WIDTH))
    return mo, C, n, m.reshape(B, M_HEADS)


def _gelu_tanh(x):
    return x * (0.5 * (1.0 + jnp.tanh(math.sqrt(2.0 / math.pi) * (x + 0.044715 * (x * x * x)))))


def _compress_body(load, pe_ref, w1_ref, w2_ref, rows_total, nh, n_cmp):
    w = KV_WIDTH
    half = CMP_LEN // 2
    pa = jnp.zeros((rows_total, w), F32)
    pb = jnp.zeros((rows_total, w), F32)
    for j in range(half):
        xj = load(j)
        pa = pa + jnp.dot((xj + pe_ref[j:j + 1, :]).astype(BF16), w1_ref[j], preferred_element_type=F32)
        pb = pb + jnp.dot((xj + pe_ref[half + j:half + j + 1, :]).astype(BF16), w1_ref[half + j],
                          preferred_element_type=F32)
    hid = pa + pltpu.roll(pb, rows_total - 1, 0)
    y = jnp.dot(_gelu_tanh(hid).astype(BF16), w2_ref[...], preferred_element_type=F32)
    rows = lax.broadcasted_iota(jnp.int32, (rows_total, w), 0)
    assert nh & (nh - 1) == 0
    return jnp.where((rows & (nh - 1)) < n_cmp, y, 0.0)


def _compress_kernel(xk_ref, xv_ref, pe_ref, w1_ref, w2_ref, o_ref, *, n_cmp):
    nh = xk_ref.shape[1] // CMP_STRIDE
    for kv, x_ref in enumerate((xk_ref, xv_ref)):
        o_ref[0, :, kv * KV_WIDTH:(kv + 1) * KV_WIDTH] = _compress_body(
            lambda j: x_ref[0, pl.ds(j, nh, stride=CMP_STRIDE), :], pe_ref.at[kv], w1_ref.at[kv], w2_ref.at[kv],
            nh, nh, n_cmp)


def _compress_sample_kernel(pt_ref, pool_ref, pe_ref, w1_ref, w2_ref, o_ref, stage, xrow_k, xrow_v, sem, *,
                            seqs, n_pages, n_cmp):
    i = pl.program_id(0)
    slot = i % 2
    page = pool_ref.shape[2]
    nh = n_pages * page // CMP_STRIDE

    def copies(step, sl):
        return [pltpu.make_async_copy(pool_ref.at[pt_ref[step * seqs + s, p], pl.ds(kv * KV_WIDTH, KV_WIDTH), :],
                                      stage.at[sl, kv, s * n_pages + p], sem.at[sl])
                for s in range(seqs) for p in range(n_pages) for kv in range(2)]

    @pl.when(i == 0)
    def _():
        for cp in copies(0, 0):
            cp.start()

    @pl.when(i + 1 < pl.num_programs(0))
    def _():
        for cp in copies(i + 1, 1 - slot):
            cp.start()

    for cp in copies(i, slot):
        cp.wait()
    xrows = (xrow_k, xrow_v)
    for kv in range(2):
        for pi in range(seqs * n_pages):
            xrows[kv][pi * page:(pi + 1) * page, :] = stage[slot, kv, pi].T
    for kv in range(2):
        o_ref[:, kv * KV_WIDTH:(kv + 1) * KV_WIDTH] = _compress_body(
            lambda j: xrows[kv][pl.ds(j, seqs * nh, stride=CMP_STRIDE), :],
            pe_ref.at[kv], w1_ref.at[kv], w2_ref.at[kv], seqs * nh, nh, n_cmp)


def _compress_sample(pool, page_table, cmp_w, n_cmp, seqs):
    n_pool, w, page = pool.shape
    DB, n_pages = page_table.shape
    nh = n_pages * page // CMP_STRIDE
    pe4, w1, w2 = cmp_w
    full = lambda *s: pl.BlockSpec(s, lambda i, pt: (0,) * len(s))
    grid_spec = pltpu.PrefetchScalarGridSpec(
        num_scalar_prefetch=1,
        grid=(DB // seqs,),
        in_specs=[pl.BlockSpec(memory_space=pl.ANY), full(*pe4.shape), full(*w1.shape), full(*w2.shape)],
        out_specs=pl.BlockSpec((seqs * nh, w), lambda i, pt: (i, 0)),
        scratch_shapes=[pltpu.VMEM((2, 2, seqs * n_pages, KV_WIDTH, page), F32),
                        pltpu.VMEM((seqs * n_pages * page, KV_WIDTH), F32),
                        pltpu.VMEM((seqs * n_pages * page, KV_WIDTH), F32),
                        pltpu.SemaphoreType.DMA((2,))],
    )
    return pl.pallas_call(
        functools.partial(_compress_sample_kernel, seqs=seqs, n_pages=n_pages, n_cmp=n_cmp),
        grid_spec=grid_spec,
        out_shape=jax.ShapeDtypeStruct((DB * nh, w), F32),
        compiler_params=_cparams(("arbitrary",)),
        name="compress_sample",
    )(page_table, pool, pe4, w1, w2)


def _block_diag2(a):
    z = jnp.zeros_like(a)
    return jnp.concatenate([jnp.concatenate([a, z], -1), jnp.concatenate([z, a], -1)], -2)


def _compress_weights(cmp_params):
    pe_k, w1_k, w2_k, pe_v, w1_v, w2_v = cmp_params
    pe = jnp.stack([jnp.concatenate([p, p], axis=1) for p in (pe_k, pe_v)])
    w1 = jnp.stack([_block_diag2(w.reshape(CMP_LEN, A_DIM, A_DIM)) for w in (w1_k, w1_v)]).astype(BF16)
    w2 = jnp.stack([_block_diag2(w) for w in (w2_k, w2_v)]).astype(BF16)
    return pe, w1, w2


def _compress_prompt(cmp_rows, cmp_w):
    B, L, w = cmp_rows.shape
    nh = L // CMP_STRIDE
    n_cmp = (L - CMP_LEN) // CMP_STRIDE + 1
    pe4, w1, w2 = cmp_w
    full = lambda *s: pl.BlockSpec(s, lambda b: (0,) * len(s))
    return pl.pallas_call(
        functools.partial(_compress_kernel, n_cmp=n_cmp),
        grid=(B,),
        in_specs=[pl.BlockSpec((1, L, KV_WIDTH), lambda b: (b, 0, 0)),
                  pl.BlockSpec((1, L, KV_WIDTH), lambda b: (b, 0, 1)),
                  full(*pe4.shape), full(*w1.shape), full(*w2.shape)],
        out_specs=pl.BlockSpec((1, nh, w), lambda b: (b, 0, 0)),
        out_shape=jax.ShapeDtypeStruct((B, nh, w), F32),
        compiler_params=_cparams(("parallel",)),
        name="compress_prompt",
    )(cmp_rows, cmp_rows, pe4, w1, w2)


_NEG = -1e30
_NEG_SEL = -1e9
_SEL_CHUNK = 512


def _dot_nt(a, b):
    return lax.dot_general(a, b, (((1,), (1,)), ((), ())), preferred_element_type=F32)


def _softmax_rows(s, valid):
    s = jnp.where(valid[None], s, _NEG)
    m = jnp.max(s, axis=-1, keepdims=True)
    e = jnp.where(valid[None], jnp.exp(s - m), 0.0)
    return e / jnp.maximum(jnp.sum(e, axis=-1, keepdims=True), jnp.finfo(jnp.float32).tiny)


def _heads_to_rows(q):
    tq = q.shape[0]
    lane = lax.broadcasted_iota(jnp.int32, (tq, LANES), 1)
    q = q * (A_DIM ** -0.5)
    rows = []
    for hd in range(A_HEADS):
        g = hd // A_GROUP
        tile = q[:, LANES * (hd // 2):LANES * (hd // 2 + 1)]
        if hd % 2 != g:
            tile = pltpu.roll(tile, A_DIM, 1)
        keep = (lane < A_DIM) if g == 0 else (lane >= A_DIM)
        rows.append(jnp.where(keep, tile, 0.0))
    return jnp.concatenate(rows, axis=0).astype(BF16)


def _gate_rows_to_heads(gt, o_c, o_s, o_w):
    tq = gt.shape[0]
    lane = lax.broadcasted_iota(jnp.int32, (tq, LANES), 1)
    gs = jax.nn.sigmoid(gt)
    tiles = []
    for pair in range(A_HEADS // 2):
        g = (2 * pair) // A_GROUP
        mixed = []
        for hd in (2 * pair, 2 * pair + 1):
            c = 2 * M_HEADS + 3 * hd
            r = slice(hd * tq, (hd + 1) * tq)
            mixed.append(gs[:, c:c + 1] * o_c[r] + gs[:, c + 1:c + 2] * o_s[r] + gs[:, c + 2:c + 3] * o_w[r])
        a, b = mixed
        if g == 0:
            tiles.append(jnp.where(lane < A_DIM, a, pltpu.roll(b, A_DIM, 1)))
        else:
            tiles.append(jnp.where(lane < A_DIM, pltpu.roll(a, A_DIM, 1), b))
    return jnp.concatenate(tiles, axis=1)


def _attend_two(s_a, ok_a, v_a, s_b, ok_b, v_b, v_a_transposed=False):
    nh, tq = s_a.shape[:2]
    if ok_a is not None:
        s_a = jnp.where(ok_a[None], s_a, _NEG)
    s_b = jnp.where(ok_b[None], s_b, _NEG)
    m = jnp.maximum(jnp.max(s_a, axis=-1, keepdims=True), jnp.max(s_b, axis=-1, keepdims=True))
    e_a = jnp.exp(s_a - m)
    e_b = jnp.exp(s_b - m)
    l = jnp.sum(e_a, axis=-1, keepdims=True) + jnp.sum(e_b, axis=-1, keepdims=True)
    e_a = e_a.reshape(nh * tq, -1).astype(BF16)
    o_a = _dot_nt(e_a, v_a) if v_a_transposed else jnp.dot(e_a, v_a, preferred_element_type=F32)
    o = o_a + jnp.dot(e_b.reshape(nh * tq, -1).astype(BF16), v_b, preferred_element_type=F32)
    return o / l.reshape(nh * tq, 1)


def _nsa_sample_kernel(pt_ref, q_ref, gt_ref, kc_ref, vc_ref, seln_ref, wst_ref, wnew_ref, oh_ref, pool_ref,
                       o_ref, nwin_ref, selbuf, sem, *, past):
    b = pl.program_id(0)
    slot = b % 2
    tq = q_ref.shape[0]
    nh = A_HEADS
    n_pages = pt_ref.shape[1]
    page = pool_ref.shape[2]
    wbuf = wst_ref.shape[1]
    n_sel = -(-(past + tq) // SEL_LEN)

    def copies(seq, sl):
        return [pltpu.make_async_copy(pool_ref.at[pt_ref[seq, p]], selbuf.at[sl, :, pl.ds(p * page, page)],
                                      sem.at[sl]) for p in range(n_pages)]

    @pl.when(b == 0)
    def _():
        for cp in copies(0, 0):
            cp.start()

    @pl.when(b + 1 < pl.num_programs(0))
    def _():
        for cp in copies(b + 1, 1 - slot):
            cp.start()

    qz = _heads_to_rows(q_ref[...])
    tpos = past + lax.broadcasted_iota(jnp.int32, (tq, LANES), 0)

    tw = lax.broadcasted_iota(jnp.int32, (tq, wbuf), 0)
    iw = lax.broadcasted_iota(jnp.int32, (tq, wbuf), 1)
    tn = lax.broadcasted_iota(jnp.int32, (tq, tq), 0)
    un = lax.broadcasted_iota(jnp.int32, (tq, tq), 1)
    wst = wst_ref[0]
    wnew = wnew_ref[...]
    s_wa = _dot_nt(qz, wst[:, :KV_WIDTH].astype(BF16)).reshape(nh, tq, wbuf)
    s_wb = _dot_nt(qz, wnew[:, :KV_WIDTH].astype(BF16)).reshape(nh, tq, tq)
    o_w = _attend_two(s_wa, (wbuf + tw - iw < WINDOW), wst[:, KV_WIDTH:].astype(BF16),
                      s_wb, un <= tn, wnew[:, KV_WIDTH:].astype(BF16))
    nwin_ref[0, :wbuf - tq, :] = wst[tq:, :]
    nwin_ref[0, wbuf - tq:, :] = wnew

    kc = kc_ref[...].astype(BF16)
    vc = vc_ref[...].astype(BF16)
    ncp = kc.shape[0]
    s_c = _dot_nt(qz, kc).reshape(nh, tq, ncp)
    tp_c = past + lax.broadcasted_iota(jnp.int32, (tq, ncp), 0)
    nidx = lax.broadcasted_iota(jnp.int32, (tq, ncp), 1)
    p_c = _softmax_rows(s_c, nidx * CMP_STRIDE + (CMP_LEN - 1) <= tp_c)
    o_c = jnp.dot(p_c.reshape(nh * tq, ncp).astype(BF16), vc, preferred_element_type=F32)

    cn = lax.broadcasted_iota(jnp.int32, (ncp, LANES), 0) * CMP_STRIDE
    jn = lax.broadcasted_iota(jnp.int32, (ncp, LANES), 1) * SEL_LEN
    ov = jnp.where((cn < jn + SEL_LEN) & (cn + CMP_LEN > jn), 1.0, 0.0).astype(BF16)
    jb = lax.broadcasted_iota(jnp.int32, (tq, LANES), 1)
    cur = tpos // SEL_LEN
    forced = (jb == 0) | (jb == cur) | (jb == cur - 1)
    bias = []
    for g in range(A_KV_HEADS):
        ps = p_c[g * A_GROUP]
        for r in range(1, A_GROUP):
            ps = ps + p_c[g * A_GROUP + r]
        hi = ps.astype(BF16)
        lo = (ps - hi.astype(F32)).astype(BF16)
        score = (jnp.dot(hi, ov, preferred_element_type=F32) + jnp.dot(lo, ov, preferred_element_type=F32))
        score = jnp.where(forced, FORCE_SCORE, score)
        score = jnp.where(jb * SEL_LEN <= tpos, score, -1.0)
        cnt = jnp.zeros((tq, LANES), jnp.int32)
        for k in range(n_sel):
            sk = score[:, k:k + 1]
            ahead = (sk > score) | ((sk == score) & (jb > k))
            cnt = cnt + jnp.where(ahead, 1, 0)
        bias.append(jnp.where(cnt < min(SEL_TOP, n_sel), 0.0, _NEG_SEL).astype(BF16))
    q_aug = jnp.concatenate(
        [qz, jnp.concatenate([bias[hd // A_GROUP] for hd in range(nh)], axis=0)], axis=1)

    for cp in copies(b, slot):
        cp.wait()
    seln = seln_ref[...]
    k_aug_t = jnp.concatenate([selbuf[slot, :KV_WIDTH, :].astype(BF16), oh_ref[...]], axis=0)
    nblk = (past + lax.broadcasted_iota(jnp.int32, (tq, LANES), 0)) // SEL_LEN
    oh_new = jnp.where(nblk == lax.broadcasted_iota(jnp.int32, (tq, LANES), 1), 1.0, 0.0)
    kn_aug = jnp.concatenate([seln[:, :KV_WIDTH], oh_new], axis=1).astype(BF16)
    s_sa = jnp.dot(q_aug, k_aug_t, preferred_element_type=F32).reshape(nh, tq, past)
    s_sb = _dot_nt(q_aug, kn_aug).reshape(nh, tq, tq)
    o_s = _attend_two(s_sa, None, selbuf[slot, KV_WIDTH:, :].astype(BF16), s_sb, un <= tn,
                      seln[:, KV_WIDTH:].astype(BF16), v_a_transposed=True)

    o_ref[...] = _gate_rows_to_heads(gt_ref[...], o_c, o_s, o_w)


def _nsa_sample_call(qa, gt, comp, sel_new, win_state, win_new, sel_pool, page_table, T):
    DB, n_pages = page_table.shape
    page = sel_pool.shape[2]
    past = n_pages * page
    wbuf = win_state.shape[1]
    ncp = comp.shape[0] // DB
    w2 = 2 * KV_WIDTH
    assert past % SEL_LEN == 0 and T < CMP_STRIDE and T % 8 == 0 and wbuf == WINDOW and past >= WINDOW
    assert page % LANES == 0
    onehot = (jnp.arange(LANES)[:, None] == jnp.arange(past)[None, :] // SEL_LEN).astype(BF16)
    tok = lambda w: pl.BlockSpec((T, w), lambda b, pt: (b, 0))
    grid_spec = pltpu.PrefetchScalarGridSpec(
        num_scalar_prefetch=1,
        grid=(DB,),
        in_specs=[tok(A_WIDTH), tok(LANES),
                  pl.BlockSpec((ncp, KV_WIDTH), lambda b, pt: (b, 0)),
                  pl.BlockSpec((ncp, KV_WIDTH), lambda b, pt: (b, 1)),
                  tok(w2),
                  pl.BlockSpec((1, wbuf, w2), lambda b, pt: (b, 0, 0)),
                  tok(w2),
                  pl.BlockSpec((LANES, past), lambda b, pt: (0, 0)),
                  pl.BlockSpec(memory_space=pl.ANY)],
        out_specs=[tok(A_WIDTH), pl.BlockSpec((1, wbuf, w2), lambda b, pt: (b, 0, 0))],
        scratch_shapes=[pltpu.VMEM((2, w2, past), F32), pltpu.SemaphoreType.DMA((2,))],
    )
    return pl.pallas_call(
        functools.partial(_nsa_sample_kernel, past=past),
        grid_spec=grid_spec,
        out_shape=[jax.ShapeDtypeStruct((DB * T, A_WIDTH), F32), jax.ShapeDtypeStruct((DB, wbuf, w2), F32)],
        compiler_params=_cparams(("arbitrary",)),
        name="nsa_sample",
    )(page_table, qa, gt, comp, comp, sel_new, win_state, win_new, onehot, sel_pool)


def _nsa_prompt_t_kernel(q_ref, gt_ref, kc_ref, vc_ref, selk_ref, selvt_ref, wink_ref, winvt_ref, o_ref, *, seq):
    tq = Q_BLOCK
    nh = A_HEADS
    rows = nh * tq
    n_sel = seq // SEL_LEN
    kt = LANES
    s0 = pl.program_id(1) * tq
    qz = _heads_to_rows(q_ref[0])
    htile = lambda x, hd: x[:, hd * tq:(hd + 1) * tq]

    def qpos(n):
        return s0 + (lax.broadcasted_iota(jnp.int32, (n, rows), 1) & (tq - 1))

    def with_ones(vt):
        return jnp.concatenate([vt, jnp.ones((8, vt.shape[1]), F32)], axis=0).astype(BF16)

    def normalise(acc):
        return acc[:LANES] * (1.0 / acc[LANES:LANES + 1])

    kc = kc_ref[0].astype(BF16)
    ncp = kc.shape[0]
    s_c = _dot_nt(kc, qz)
    ok_c = lax.broadcasted_iota(jnp.int32, (ncp, rows), 0) * CMP_STRIDE + (CMP_LEN - 1) <= qpos(ncp)
    s_c = jnp.where(ok_c, s_c, _NEG)
    e_c = jnp.where(ok_c, jnp.exp(s_c - jnp.max(s_c, axis=0, keepdims=True)), 0.0)
    p_c = e_c * (1.0 / jnp.maximum(jnp.sum(e_c, axis=0, keepdims=True), jnp.finfo(jnp.float32).tiny))
    o_c = jnp.dot(vc_ref[0].T.astype(BF16), p_c.astype(BF16), preferred_element_type=F32)

    jn = lax.broadcasted_iota(jnp.int32, (n_sel, ncp), 0) * SEL_LEN
    cn = lax.broadcasted_iota(jnp.int32, (n_sel, ncp), 1) * CMP_STRIDE
    ov_t = jnp.where((cn < jn + SEL_LEN) & (cn + CMP_LEN > jn), 1.0, 0.0).astype(BF16)
    jb = lax.broadcasted_iota(jnp.int32, (n_sel, tq), 0)
    tp = s0 + lax.broadcasted_iota(jnp.int32, (n_sel, tq), 1)
    cur = tp // SEL_LEN
    forced = (jb == 0) | (jb == cur) | (jb == cur - 1)
    bias = []
    for g in range(A_KV_HEADS):
        mass = htile(p_c, g * A_GROUP)
        for r in range(1, A_GROUP):
            mass = mass + htile(p_c, g * A_GROUP + r)
        hi = mass.astype(BF16)
        lo = (mass - hi.astype(F32)).astype(BF16)
        score = (jnp.dot(ov_t, hi, preferred_element_type=F32) + jnp.dot(ov_t, lo, preferred_element_type=F32))
        score = jnp.where(forced, FORCE_SCORE, score)
        score = jnp.where(jb * SEL_LEN <= tp, score, -1.0)
        cnt = jnp.zeros((n_sel, tq), jnp.int32)
        for k in range(n_sel):
            rk = score[k:k + 1, :]
            ahead = (rk > score) | ((rk == score) & (jb > k))
            cnt = cnt + jnp.where(ahead, 1, 0)
        bias_t = jnp.where(cnt < min(SEL_TOP, n_sel), 0.0, _NEG_SEL)
        if n_sel < LANES:
            bias_t = jnp.concatenate([bias_t, jnp.zeros((LANES - n_sel, tq), F32)], axis=0)
        bias.append(bias_t.T.astype(BF16))
    q_aug = jnp.concatenate(
        [qz, jnp.concatenate([bias[hd // A_GROUP] for hd in range(nh)], axis=0)], axis=1)

    wk = WINDOW + tq
    w0 = pl.multiple_of(jnp.clip(s0 - WINDOW, 0, seq - wk), tq)
    s_w = _dot_nt(wink_ref[0, pl.ds(w0, wk), :].astype(BF16), qz)
    dist = qpos(wk) - (w0 + lax.broadcasted_iota(jnp.int32, (wk, rows), 0))
    s_w = jnp.where((dist >= 0) & (dist < WINDOW), s_w, _NEG)
    e_w = jnp.exp(s_w - jnp.max(s_w, axis=0, keepdims=True)).astype(BF16)
    vw_t = jnp.concatenate([winvt_ref[0, w0 // kt + i] for i in range(wk // kt)], axis=1)
    o_w = normalise(jnp.dot(with_ones(vw_t), e_w, preferred_element_type=F32))

    kc_n = _SEL_CHUNK

    def chunk(c, carry, causal):
        m, acc = carry
        k0 = pl.multiple_of(c * kc_n, kc_n)
        kblk = (k0 + lax.broadcasted_iota(jnp.int32, (kc_n, LANES), 0)) // SEL_LEN
        onehot = jnp.where(kblk == lax.broadcasted_iota(jnp.int32, (kc_n, LANES), 1), 1.0, 0.0)
        k_aug = jnp.concatenate([selk_ref[0, pl.ds(k0, kc_n), :], onehot], axis=1).astype(BF16)
        s = _dot_nt(k_aug, q_aug)
        if causal:
            s = jnp.where(k0 + lax.broadcasted_iota(jnp.int32, (kc_n, rows), 0) <= qpos(kc_n), s, _NEG_SEL)
        m_new = jnp.maximum(m, jnp.max(s, axis=0, keepdims=True))
        p = jnp.exp(s - m_new).astype(BF16)
        v_t = jnp.concatenate([selvt_ref[0, c * (kc_n // kt) + i] for i in range(kc_n // kt)], axis=1)
        acc = jnp.exp(m - m_new) * acc + jnp.dot(with_ones(v_t), p, preferred_element_type=F32)
        return m_new, acc

    n_ch = (s0 + tq - 1) // kc_n + 1
    init = (jnp.full((1, rows), _NEG, F32), jnp.zeros((LANES + 8, rows), F32))
    carry = lax.fori_loop(0, n_ch - 1, lambda c, cr: chunk(c, cr, False), init)
    o_s = normalise(chunk(n_ch - 1, carry, True)[1])

    gs_t = jax.nn.sigmoid(gt_ref[0]).T
    mixed = []
    for hd in range(nh):
        c = 2 * M_HEADS + 3 * hd
        g = hd // A_GROUP
        mix = (gs_t[c:c + 1, :] * htile(o_c, hd) + gs_t[c + 1:c + 2, :] * htile(o_s, hd)
               + gs_t[c + 2:c + 3, :] * htile(o_w, hd))
        mixed.append(mix[g * A_DIM:(g + 1) * A_DIM, :])
    o_ref[0] = jnp.concatenate(
        [jnp.concatenate([mixed[2 * pr], mixed[2 * pr + 1]], axis=0).T for pr in range(nh // 2)], axis=1)


def _nsa_prompt_t_call(qa, gt, comp, sel_k, sel_vt, win_k, win_vt):
    B, S, _ = qa.shape
    assert S % _SEL_CHUNK == 0 and S >= WINDOW + Q_BLOCK and S // SEL_LEN <= LANES
    ncp = comp.shape[1]
    keys = pl.BlockSpec((1, S, KV_WIDTH), lambda b, i: (b, 0, 0))
    vals = pl.BlockSpec((1, S // LANES, KV_WIDTH, LANES), lambda b, i: (b, 0, 0, 0))
    return pl.pallas_call(
        functools.partial(_nsa_prompt_t_kernel, seq=S),
        grid=(B, S // Q_BLOCK),
        in_specs=[pl.BlockSpec((1, Q_BLOCK, A_WIDTH), lambda b, i: (b, i, 0)),
                  pl.BlockSpec((1, Q_BLOCK, LANES), lambda b, i: (b, i, 0)),
                  pl.BlockSpec((1, ncp, KV_WIDTH), lambda b, i: (b, 0, 0)),
                  pl.BlockSpec((1, ncp, KV_WIDTH), lambda b, i: (b, 0, 1)),
                  keys, vals, keys, vals],
        out_specs=pl.BlockSpec((1, Q_BLOCK, A_WIDTH), lambda b, i: (b, i, 0)),
        out_shape=jax.ShapeDtypeStruct((B, S, A_WIDTH), F32),
        compiler_params=_cparams(("parallel", "arbitrary")),
        name="nsa_prompt",
    )(qa, gt, comp, comp, sel_k, sel_vt, win_k, win_vt)


def _moe(h2, logits, moe_w, n_experts):
    w_gu, b_gu, w_down, b_down = moe_w
    n, d = h2.shape
    gate_w, dest, src, blk_e, n_used = _route(logits, n_experts)
    yb = _ffn(h2[src], blk_e, n_used, w_gu, b_gu, w_down, b_down)
    return yb[dest.T], gate_w


def kernel(x_prompt, x_sample, cache_cmp, cache_sel, state_win, state_C, state_n, state_m, page_table,
           c_prompt, c_sample, w_ada, b_ada, w_in, b_in, m_norm_g, cmp_pe_k, cmp_w1_k, cmp_w2_k,
           cmp_pe_v, cmp_w1_v, cmp_w2_v, w_out, ln1_g, ln1_b, w_router, b_router, w_gu, b_gu,
           w_down, b_down, ln2_g, ln2_b):
    B, S, d = x_prompt.shape
    DB, T, _ = x_sample.shape
    depth = w_ada.shape[0]
    n_experts = w_router.shape[-1]
    alpha = (2 * depth) ** 0.25
    n_pool, page = cache_cmp.shape[1:3]
    past_len = page_table.shape[1] * page
    wbuf = state_win.shape[2]
    pos_p = jnp.arange(S)
    pos_s = past_len + jnp.arange(T)
    tt_p = min(S, 256)
    bb_s = min(DB, max(1, 256 // T))
    n_cmp_s = (past_len + T - CMP_LEN) // CMP_STRIDE + 1
    assert n_cmp_s < past_len // CMP_STRIDE
    r3 = lambda a: a.reshape(B, S, a.shape[-1])
    kv5 = lambda a, n, t: a.reshape(n, t, 2, A_KV_HEADS, A_DIM)
    y_prompt, y_sample = x_prompt, x_sample
    outs = [[] for _ in range(12)]
    for l in range(depth):
        cmp_l = (cmp_pe_k[l], cmp_w1_k[l], cmp_w2_k[l], cmp_pe_v[l], cmp_w1_v[l], cmp_w2_v[l])
        moe_w = (w_gu[l], b_gu[l], w_down[l], b_down[l])
        post1_w = (w_out[l], ln1_g[l], ln1_b[l], w_router[l], b_router[l])
        w_r, b_r = _prep_w_in(w_in[l], b_in[l])
        cmp_w = _compress_weights(cmp_l)
        c_all = jnp.concatenate([c_prompt, c_sample], axis=0)
        c_all = jnp.pad(c_all, ((0, -c_all.shape[0] % 8), (0, 0)))
        mod_all = _ada(c_all, w_ada[l], b_ada[l]).reshape(-1, 6, d)
        mod_p, mod_s = mod_all[:B], mod_all[B:B + DB]

        (mq, gt, qa, cmp_p, sel_p, win_p, cmp_t, sel_t, win_t, sel_vt, win_vt) = _pre(
            y_prompt, mod_p, pos_p, w_r, b_r, 1, tt_p, feature_major=True)
        zc = jnp.zeros((B, M_HEADS, M_DIM, M_DIM), F32)
        mo, C_p, n_p, m_p = _mlstm_call(r3(mq), r3(gt), zc, zc[..., 0], zc[..., 0, 0], m_norm_g[l],
                                        math.gcd(B, 4))
        comp = _compress_prompt(r3(cmp_p), cmp_w)
        ma = _nsa_prompt_t_call(r3(qa), r3(gt), comp, r3(sel_p), sel_vt, r3(win_p), win_vt)
        x1_p, h2_p, lg_p = _post1(mo.reshape(B * S, M_WIDTH), ma.reshape(B * S, A_WIDTH), y_prompt, mod_p,
                                  *post1_w, 1, tt_p, alpha)

        mq, gt, qa, cmp_s, sel_s, win_s = _pre(y_sample, mod_s, pos_s, w_r, b_r, bb_s, T)
        mo, C_s, n_s, m_s = _mlstm_call(mq.reshape(DB, T, -1), gt.reshape(DB, T, -1), state_C[l],
                                        state_n[l], state_m[l], m_norm_g[l], math.gcd(DB, 4))
        feature_major = lambda pool: pool.transpose(0, 2, 3, 4, 1).reshape(n_pool, 2 * KV_WIDTH, page)
        comp = _compress_sample(feature_major(cache_cmp[l]), page_table, cmp_w, n_cmp_s, math.gcd(DB, 4))
        ma, new_win = _nsa_sample_call(qa, gt, comp, sel_s, state_win[l].reshape(DB, wbuf, 2 * KV_WIDTH),
                                       win_s, feature_major(cache_sel[l]), page_table, T)
        x1_s, h2_s, lg_s = _post1(mo.reshape(DB * T, M_WIDTH), ma, y_sample, mod_s, *post1_w, bb_s, T, alpha)

        yg, gate_w = _moe(jnp.concatenate([h2_p, h2_s], axis=0), jnp.concatenate([lg_p, lg_s], axis=0),
                          moe_w, n_experts)
        y_prompt = _post2(yg, gate_w, 0, x1_p, mod_p, ln2_g[l], ln2_b[l], 1, tt_p, alpha)
        y_sample = _post2(yg, gate_w, B * S, x1_s, mod_s, ln2_g[l], ln2_b[l], bb_s, T, alpha)

        kv5_t = lambda a: a.reshape(B, 2, A_KV_HEADS, A_DIM, -1).transpose(0, 4, 1, 2, 3)
        new = (kv5_t(cmp_t), kv5_t(sel_t), kv5_t(win_t[:, :, -min(WINDOW, S):]), C_p, n_p, m_p,
               kv5(cmp_s, DB, T), kv5(sel_s, DB, T), new_win.reshape(state_win.shape[1:]), C_s, n_s, m_s)
        for lst, v in zip(outs, new):
            lst.append(v)
    return (y_prompt, y_sample, *[jnp.stack(v) for v in outs])
```
